```python
import jax, jax.numpy as jnp
from jax import lax
import numpy as np

D_MODEL = 1024
BATCH = 8
SEQ = 8192
DEPTH = 1

GRID_W = 64
CTX_LEN = 256
D_MIX = D_MODEL
NA_HEADS = 8
NA_HEAD_DIM = 64
NA_WIDTH = NA_HEADS * NA_HEAD_DIM
NA_WIN_ROWS = 8
NA_WIN_COLS = 16
ML_HEADS = 4
ML_HEAD_DIM = 128
ML_WIDTH = ML_HEADS * ML_HEAD_DIM
ML_CHUNK = 128
CONV_K = 5
N_GROUPS = 8
EXPERTS_PER_GROUP = 8
N_EXPERTS = N_GROUPS * EXPERTS_PER_GROUP
TOP_K_IN_GROUP = 2
EXPERT_HIDDEN = 512
MOE_BLOCK = 128
ROPE_BASE = 10000.0
LN_EPS = 1e-5
DEEPNORM_ALPHA = (2.0 * DEPTH) ** 0.25
DEEPNORM_BETA = (8.0 * DEPTH) ** -0.25

COL_NA_Q = 0
COL_NA_K = COL_NA_Q + NA_WIDTH
COL_NA_V = COL_NA_K + NA_WIDTH
COL_ML_Q = COL_NA_V + NA_WIDTH
COL_ML_K = COL_ML_Q + ML_WIDTH
COL_ML_V = COL_ML_K + ML_WIDTH
COL_ML_O = COL_ML_V + ML_WIDTH
COL_GATES = COL_ML_O + ML_WIDTH
IN_COLS = COL_GATES + 4 * ML_HEADS

kernel_name = 'hybrid_na_mlstm_hmoe_layer'


def _layer_norm(x, gain=None, bias=None):
    x32 = x.astype(jnp.float32)
    mu = jnp.mean(x32, axis=-1, keepdims=True)
    var = jnp.mean(jnp.square(x32 - mu), axis=-1, keepdims=True)
    y = (x32 - mu) * lax.rsqrt(var + LN_EPS)
    if gain is not None:
        y = y * gain.astype(jnp.float32) + bias.astype(jnp.float32)
    return y.astype(x.dtype)


def _modulate(xn, shift, scale):
    return xn * (1.0 + scale) + shift


def _heads(z, col, n_heads, head_dim):
    B, T, _ = z.shape
    return z[..., col:col + n_heads * head_dim].reshape(B, T, n_heads, head_dim)


def _centred_depthwise_conv(u, w, b):
    pad = CONV_K // 2
    out = lax.conv_general_dilated(u, w[:, None, :].astype(u.dtype), window_strides=(1,),
                                   padding=[(pad, pad)], dimension_numbers=('NWC', 'WIO', 'NWC'),
                                   feature_group_count=u.shape[-1])
    return out + b


def _rope_axis(x, pos):
    half = x.shape[-1] // 2
    inv = ROPE_BASE ** (-jnp.arange(half, dtype=jnp.float32) / half)
    ang = pos.astype(jnp.float32)[:, None] * inv[None, :]
    cos = jnp.cos(ang)[None, :, None, :]
    sin = jnp.sin(ang)[None, :, None, :]
    x32 = x.astype(jnp.float32)
    x1, x2 = x32[..., :half], x32[..., half:]
    return jnp.concatenate([x1 * cos - x2 * sin, x1 * sin + x2 * cos], axis=-1).astype(x.dtype)


def _rope_2d(x, row_pos, col_pos):
    d2 = x.shape[-1] // 2
    return jnp.concatenate([_rope_axis(x[..., :d2], row_pos), _rope_axis(x[..., d2:], col_pos)], axis=-1)


def _neighbourhood_attention(q, k, v, k_ctx, v_ctx, rpb):
    B, N, H, d = q.shape
    rows = N // GRID_W
    kr = min(NA_WIN_ROWS, rows)
    n_win = kr * NA_WIN_COLS
    qg = (q * d ** -0.5).reshape(B, rows, GRID_W, H, d)
    kg = k.reshape(B, rows, GRID_W, H, d)
    vg = v.reshape(B, rows, GRID_W, H, d)
    cols = jnp.arange(GRID_W)
    col_start = jnp.clip(cols - NA_WIN_COLS // 2, 0, GRID_W - NA_WIN_COLS)
    col_idx = col_start[:, None] + jnp.arange(NA_WIN_COLS)[None, :]
    col_off = col_idx - cols[:, None] + (NA_WIN_COLS - 1)

    def row_block(r):
        r0 = jnp.clip(r - kr // 2, 0, rows - kr)
        q_r = lax.dynamic_index_in_dim(qg, r, axis=1, keepdims=False)
        k_band = lax.dynamic_slice_in_dim(kg, r0, kr, axis=1)
        v_band = lax.dynamic_slice_in_dim(vg, r0, kr, axis=1)
        k_win = k_band[:, :, col_idx]
        v_win = v_band[:, :, col_idx]
        row_off = r0 + jnp.arange(kr) - r + (NA_WIN_ROWS - 1)
        bias = rpb[:, row_off[:, None, None], col_off[None, :, :]]
        bias = bias.transpose(0, 2, 1, 3).reshape(H, GRID_W, n_win)
        s_win = jnp.einsum('bqhd,brqjhd->bhqrj', q_r, k_win).reshape(B, H, GRID_W, n_win)
        s_ctx = jnp.einsum('bqhd,bchd->bhqc', q_r, k_ctx)
        s = jnp.concatenate([s_win.astype(jnp.float32) + bias.astype(jnp.float32),
                             s_ctx.astype(jnp.float32)], axis=-1)
        p = jax.nn.softmax(s, axis=-1).astype(v.dtype)
        p_win = p[..., :n_win].reshape(B, H, GRID_W, kr, NA_WIN_COLS)
        p_ctx = p[..., n_win:]
        return (jnp.einsum('bhqrj,brqjhd->bqhd', p_win, v_win)
                + jnp.einsum('bhqc,bchd->bqhd', p_ctx, v_ctx))

    out = lax.map(row_block, jnp.arange(rows))
    return out.transpose(1, 0, 2, 3, 4).reshape(B, N, H * d)


def _context_attention(q, k, v):
    B, T, H, d = q.shape
    s = jnp.einsum('bqhd,bkhd->bhqk', q * d ** -0.5, k).astype(jnp.float32)
    p = jax.nn.softmax(s, axis=-1).astype(v.dtype)
    return jnp.einsum('bhqk,bkhd->bqhd', p, v).reshape(B, T, H * d)


def _mlstm_streams(z, conv_w, conv_b, gate_b, row_pos, col_pos):
    B, T, _ = z.shape
    qk = jax.nn.silu(_centred_depthwise_conv(z[..., COL_ML_Q:COL_ML_V], conv_w, conv_b))
    q = qk[..., :ML_WIDTH].reshape(B, T, ML_HEADS, ML_HEAD_DIM)
    k = qk[..., ML_WIDTH:].reshape(B, T, ML_HEADS, ML_HEAD_DIM)
    if row_pos is not None:
        q = _rope_2d(q, row_pos, col_pos)
        k = _rope_2d(k, row_pos, col_pos)
    v = _heads(z, COL_ML_V, ML_HEADS, ML_HEAD_DIM)
    gates = (z[..., COL_GATES:IN_COLS] + gate_b).astype(jnp.float32)
    gates = gates.reshape(B, T, 4, ML_HEADS).transpose(2, 0, 3, 1)
    to_bhtd = lambda a: a.transpose(0, 2, 1, 3)
    return to_bhtd(q), to_bhtd(k) * ML_HEAD_DIM ** -0.5, to_bhtd(v), gates


def _mlstm_chunkwise(q, k, v, i_pre, f_pre, state):
    B, H, T, dk = q.shape
    dv = v.shape[-1]
    L = ML_CHUNK
    nc = T // L

    def to_chunks(a):
        a = a.astype(jnp.float32)
        return jnp.moveaxis(a.reshape((B, H, nc, L) + a.shape[3:]), 2, 0)

    causal = jnp.tril(jnp.ones((L, L), dtype=bool))

    def step(carry, inp):
        C, n, m = carry
        q_c, k_c, v_c, i_c, lf_c = inp
        b = jnp.cumsum(lf_c, axis=-1)
        dlog = jnp.where(causal, b[..., :, None] - b[..., None, :] + i_c[..., None, :], -jnp.inf)
        m_t = jnp.maximum(b + m[..., None], jnp.max(dlog, axis=-1))
        dw = jnp.exp(dlog - m_t[..., None])
        inter = jnp.exp(b + m[..., None] - m_t)
        s = jnp.einsum('bhtd,bhsd->bhts', q_c, k_c) * dw
        num = jnp.einsum('bhts,bhsv->bhtv', s, v_c) + inter[..., None] * jnp.einsum('bhtd,bhdv->bhtv', q_c, C)
        den = jnp.sum(s, axis=-1) + inter * jnp.einsum('bhtd,bhd->bht', q_c, n)
        h = num / jnp.maximum(jnp.abs(den), jnp.exp(-m_t))[..., None]
        b_end = b[..., -1]
        g = b_end[..., None] - b + i_c
        m_new = jnp.maximum(b_end + m, jnp.max(g, axis=-1))
        decay = jnp.exp(b_end + m - m_new)
        wgt = jnp.exp(g - m_new[..., None])
        C_new = decay[..., None, None] * C + jnp.einsum('bhs,bhsd,bhsv->bhdv', wgt, k_c, v_c)
        n_new = decay[..., None] * n + jnp.einsum('bhs,bhsd->bhd', wgt, k_c)
        return (C_new, n_new, m_new), h

    xs = (to_chunks(q), to_chunks(k), to_chunks(v), to_chunks(i_pre), to_chunks(jax.nn.log_sigmoid(f_pre)))
    state, hs = lax.scan(step, state, xs)
    h = jnp.moveaxis(hs, 0, 2).reshape(B, H, T, dv)
    return h, state


def _mlstm_bidirectional(q, k, v, gates, state_fwd, state_bwd):
    i_f, f_f, i_b, f_b = gates
    h_f, st_f = _mlstm_chunkwise(q, k, v, i_f, f_f, state_fwd)
    rev = lambda a: jnp.flip(a, axis=2)
    h_b, st_b = _mlstm_chunkwise(rev(q), rev(k), rev(v), rev(i_b), rev(f_b), state_bwd)
    return h_f + rev(h_b), st_f, st_b


def _mlstm_readout(h, o_pre, gain):
    B, H, T, dv = h.shape
    hn = h * lax.rsqrt(jnp.mean(jnp.square(h), axis=-1, keepdims=True) + LN_EPS)
    hn = hn.transpose(0, 2, 1, 3).reshape(B, T, H * dv)
    return (hn * gain.astype(jnp.float32) * jax.nn.sigmoid(o_pre.astype(jnp.float32))).astype(o_pre.dtype)


def _zero_mlstm_state(B):
    return (jnp.zeros((B, ML_HEADS, ML_HEAD_DIM, ML_HEAD_DIM), jnp.float32),
            jnp.zeros((B, ML_HEADS, ML_HEAD_DIM), jnp.float32),
            jnp.zeros((B, ML_HEADS), jnp.float32))


def _hier_moe(h, w_rg, b_rg, w_re, b_re, w1, w3, w2):
    B, T, D = h.shape
    n_tok = B * T
    xt = h.reshape(n_tok, D)
    group_logits = (xt @ w_rg + b_rg).astype(jnp.float32)
    group_prob = jax.nn.softmax(group_logits, axis=-1)
    grp = jnp.argmax(group_logits, axis=-1)
    grp_w = jnp.take_along_axis(group_prob, grp[:, None], axis=1)
    exp_logits = (xt @ w_re + b_re).astype(jnp.float32).reshape(n_tok, N_GROUPS, EXPERTS_PER_GROUP)
    exp_logits = jnp.take_along_axis(exp_logits, grp[:, None, None], axis=1)[:, 0]
    top_val, top_idx = lax.top_k(exp_logits, TOP_K_IN_GROUP)
    weights = grp_w * jax.nn.softmax(top_val, axis=-1)
    expert = grp[:, None] * EXPERTS_PER_GROUP + top_idx

    n_assign = n_tok * TOP_K_IN_GROUP
    flat_e = expert.reshape(-1).astype(jnp.int32)
    flat_tok = jnp.repeat(jnp.arange(n_tok, dtype=jnp.int32), TOP_K_IN_GROUP)
    flat_w = weights.reshape(-1)
    order = jnp.argsort(flat_e)
    e_sorted, tok_sorted, w_sorted = flat_e[order], flat_tok[order], flat_w[order]
    sizes = jnp.bincount(flat_e, length=N_EXPERTS)
    padded = (sizes + MOE_BLOCK - 1) // MOE_BLOCK * MOE_BLOCK
    start = jnp.cumsum(sizes) - sizes
    pstart = jnp.cumsum(padded) - padded
    dest = pstart[e_sorted] + jnp.arange(n_assign, dtype=jnp.int32) - start[e_sorted]
    cap = -(-n_assign // MOE_BLOCK) * MOE_BLOCK + N_EXPERTS * MOE_BLOCK
    n_blocks = cap // MOE_BLOCK
    buf_tok = jnp.full((cap,), n_tok, jnp.int32).at[dest].set(tok_sorted)
    buf_w = jnp.zeros((cap,), jnp.float32).at[dest].set(w_sorted)
    block_e = jnp.minimum(jnp.searchsorted(jnp.cumsum(padded), jnp.arange(n_blocks) * MOE_BLOCK, side='right'),
                          N_EXPERTS - 1)

    def expert_block(args):
        tok, e = args
        xb = xt[jnp.minimum(tok, n_tok - 1)]
        return (jax.nn.silu(xb @ w1[e]) * (xb @ w3[e])) @ w2[e]

    ys = lax.map(expert_block, (buf_tok.reshape(n_blocks, MOE_BLOCK), block_e)).reshape(cap, D)
    out = jax.ops.segment_sum(ys * buf_w[:, None].astype(ys.dtype), buf_tok, num_segments=n_tok + 1)[:n_tok]
    return out.reshape(B, T, D)


def setup_inputs(seed: int = 0) -> dict:
    key = jax.random.key(seed)
    ks = jax.random.split(key, 25)
    f32 = jnp.float32
    nrm = lambda k, shape, s: jax.random.normal(k, shape, f32) * s
    x = nrm(ks[0], (BATCH, SEQ, D_MODEL), 1.0)
    c = nrm(ks[1], (BATCH, D_MODEL), 1.0)
    ctx = nrm(ks[2], (BATCH, CTX_LEN, D_MODEL), 1.0)
    c_ctx = nrm(ks[3], (D_MODEL,), 1.0)
    w_ada = nrm(ks[4], (DEPTH, D_MODEL, 6 * D_MODEL), D_MODEL ** -0.5)
    b_ada = nrm(ks[5], (DEPTH, 6 * D_MODEL), 0.02)
    w_in = nrm(ks[6], (DEPTH, D_MODEL, IN_COLS), D_MODEL ** -0.5)
    conv_w = nrm(ks[7], (DEPTH, CONV_K, 2 * ML_WIDTH), CONV_K ** -0.5)
    conv_b = nrm(ks[8], (DEPTH, 2 * ML_WIDTH), 0.02)
    i_bias = nrm(ks[9], (DEPTH, 2, ML_HEADS), 0.1)
    f_bias = jnp.linspace(3.0, 6.0, ML_HEADS, dtype=f32) + nrm(ks[10], (DEPTH, 2, ML_HEADS), 0.1)
    gate_b = jnp.stack([i_bias[:, 0], f_bias[:, 0], i_bias[:, 1], f_bias[:, 1]], axis=1).reshape(DEPTH, 4 * ML_HEADS)
    rpb = nrm(ks[11], (DEPTH, NA_HEADS, 2 * NA_WIN_ROWS - 1, 2 * NA_WIN_COLS - 1), 0.1)
    ml_norm_g = 1.0 + nrm(ks[12], (DEPTH, ML_WIDTH), 0.02)
    w_out = nrm(ks[13], (DEPTH, D_MIX, D_MODEL), D_MIX ** -0.5 * DEEPNORM_BETA)
    ln1_g = 1.0 + nrm(ks[14], (DEPTH, D_MODEL), 0.02)
    ln1_b = nrm(ks[15], (DEPTH, D_MODEL), 0.02)
    w_router_g = nrm(ks[16], (DEPTH, D_MODEL, N_GROUPS), D_MODEL ** -0.5)
    b_router_g = nrm(ks[17], (DEPTH, N_GROUPS), 0.01)
    w_router_e = nrm(ks[18], (DEPTH, D_MODEL, N_EXPERTS), D_MODEL ** -0.5)
    b_router_e = nrm(ks[19], (DEPTH, N_EXPERTS), 0.01)
    w1 = nrm(ks[20], (DEPTH, N_EXPERTS, D_MODEL, EXPERT_HIDDEN), D_MODEL ** -0.5)
    w3 = nrm(ks[21], (DEPTH, N_EXPERTS, D_MODEL, EXPERT_HIDDEN), D_MODEL ** -0.5)
    w2 = nrm(ks[22], (DEPTH, N_EXPERTS, EXPERT_HIDDEN, D_MODEL), EXPERT_HIDDEN ** -0.5 * DEEPNORM_BETA)
    ln2_g = 1.0 + nrm(ks[23], (DEPTH, D_MODEL), 0.02)
    ln2_b = nrm(ks[24], (DEPTH, D_MODEL), 0.02)
    return {'x': x, 'c': c, 'ctx': ctx, 'c_ctx': c_ctx, 'w_ada': w_ada, 'b_ada': b_ada, 'w_in': w_in,
            'conv_w': conv_w, 'conv_b': conv_b, 'gate_b': gate_b, 'rpb': rpb, 'ml_norm_g': ml_norm_g,
            'w_out': w_out, 'ln1_g': ln1_g, 'ln1_b': ln1_b, 'w_router_g': w_router_g, 'b_router_g': b_router_g,
            'w_router_e': w_router_e, 'b_router_e': b_router_e, 'w1': w1, 'w3': w3, 'w2': w2,
            'ln2_g': ln2_g, 'ln2_b': ln2_b}


def reference(x, c, ctx, c_ctx, w_ada, b_ada, w_in, conv_w, conv_b, gate_b, rpb, ml_norm_g, w_out,
              ln1_g, ln1_b, w_router_g, b_router_g, w_router_e, b_router_e, w1, w3, w2, ln2_g, ln2_b):
    B, N, _ = x.shape
    pos = jnp.arange(N)
    row_pos = pos // GRID_W
    col_pos = pos % GRID_W
    for l in range(DEPTH):
        last = l == DEPTH - 1
        ada = (jax.nn.silu(c) @ w_ada[l] + b_ada[l])[:, None, :]
        ada_c = jax.nn.silu(c_ctx) @ w_ada[l] + b_ada[l]
        sh1, sc1, g1, sh2, sc2, g2 = jnp.split(ada, 6, axis=-1)
        csh1, csc1, cg1, csh2, csc2, cg2 = jnp.split(ada_c, 6, axis=-1)

        z = _modulate(_layer_norm(x), sh1, sc1) @ w_in[l]
        zc = _modulate(_layer_norm(ctx), csh1, csc1) @ w_in[l]

        k_na_c = _heads(zc, COL_NA_K, NA_HEADS, NA_HEAD_DIM)
        v_na_c = _heads(zc, COL_NA_V, NA_HEADS, NA_HEAD_DIM)
        na = _neighbourhood_attention(_heads(z, COL_NA_Q, NA_HEADS, NA_HEAD_DIM),
                                      _heads(z, COL_NA_K, NA_HEADS, NA_HEAD_DIM),
                                      _heads(z, COL_NA_V, NA_HEADS, NA_HEAD_DIM),
                                      k_na_c, v_na_c, rpb[l])

        zero = _zero_mlstm_state(B)
        q_c, k_c, v_c, gates_c = _mlstm_streams(zc, conv_w[l], conv_b[l], gate_b[l], None, None)
        h_ml_c, st_f, st_b = _mlstm_bidirectional(q_c, k_c, v_c, gates_c, zero, zero)
        q_l, k_l, v_l, gates_l = _mlstm_streams(z, conv_w[l], conv_b[l], gate_b[l], row_pos, col_pos)
        h_ml, _, _ = _mlstm_bidirectional(q_l, k_l, v_l, gates_l, st_f, st_b)
        ml = _mlstm_readout(h_ml, z[..., COL_ML_O:COL_GATES], ml_norm_g[l])

        mix = jnp.concatenate([na, ml], axis=-1) @ w_out[l]
        x_mid = _layer_norm(DEEPNORM_ALPHA * x + g1 * mix, ln1_g[l], ln1_b[l])

        moe = _hier_moe(_modulate(_layer_norm(x_mid), sh2, sc2), w_router_g[l], b_router_g[l],
                        w_router_e[l], b_router_e[l], w1[l], w3[l], w2[l])
        x_next = _layer_norm(DEEPNORM_ALPHA * x_mid + g2 * moe, ln2_g[l], ln2_b[l])

        if not last:
            na_c = _context_attention(_heads(zc, COL_NA_Q, NA_HEADS, NA_HEAD_DIM), k_na_c, v_na_c)
            ml_cout = _mlstm_readout(h_ml_c, zc[..., COL_ML_O:COL_GATES], ml_norm_g[l])
            mix_c = jnp.concatenate([na_c, ml_cout], axis=-1) @ w_out[l]
            ctx_mid = _layer_norm(DEEPNORM_ALPHA * ctx + cg1 * mix_c, ln1_g[l], ln1_b[l])
            moe_c = _hier_moe(_modulate(_layer_norm(ctx_mid), csh2, csc2), w_router_g[l], b_router_g[l],
                              w_router_e[l], b_router_e[l], w1[l], w3[l], w2[l])
            ctx = _layer_norm(DEEPNORM_ALPHA * ctx_mid + cg2 * moe_c, ln2_g[l], ln2_b[l])
        x = x_next
    return x
```

```python
import functools

import numpy as np
import jax
import jax.numpy as jnp
from jax import lax
from jax.experimental import pallas as pl
from jax.experimental.pallas import tpu as pltpu

F32 = jnp.float32
BF16 = jnp.bfloat16
HIGHEST = lax.Precision.HIGHEST

GRID_W = 64
NA_HEADS = 8
NA_HEAD_DIM = 64
NA_WIDTH = NA_HEADS * NA_HEAD_DIM
NA_WIN_ROWS = 8
NA_WIN_COLS = 16
ML_HEADS = 4
ML_HEAD_DIM = 128
ML_WIDTH = ML_HEADS * ML_HEAD_DIM
ML_CHUNK = 128
CONV_K = 5
N_GROUPS = 8
EXPERTS_PER_GROUP = 8
N_EXPERTS = N_GROUPS * EXPERTS_PER_GROUP
ROPE_BASE = 10000.0
LN_EPS = 1e-5

LANES = 128
SUBLANES = 8
VMEM_LIMIT = 56 * 1024 * 1024

NA_ROWS_PER_STEP = 4
EXPERT_ROWS = 512
NEG = -1e30


def _cparams(*sem):
    return pltpu.CompilerParams(dimension_semantics=sem, vmem_limit_bytes=VMEM_LIMIT)


def _silu(v):
    return v * jax.nn.sigmoid(v)


def _ln_rows(v):
    mu = jnp.mean(v, axis=-1, keepdims=True)
    vc = v - mu
    var = jnp.mean(vc * vc, axis=-1, keepdims=True)
    return vc * lax.rsqrt(var + LN_EPS)


def _ada_kernel(c_ref, w_ref, b_ref, o_ref):
    o_ref[...] = jnp.dot(_silu(c_ref[...]), w_ref[...], preferred_element_type=F32,
                         precision=HIGHEST) + b_ref[...]


def _ada(cvec, w, b):
    rows, d = cvec.shape
    cols = w.shape[1]
    tn = 1024
    return pl.pallas_call(
        _ada_kernel,
        grid=(cols // tn,),
        in_specs=[pl.BlockSpec((rows, d), lambda j: (0, 0)),
                  pl.BlockSpec((d, tn), lambda j: (0, j)),
                  pl.BlockSpec((1, tn), lambda j: (0, j))],
        out_specs=pl.BlockSpec((rows, tn), lambda j: (0, j)),
        out_shape=jax.ShapeDtypeStruct((rows, cols), F32),
        compiler_params=_cparams("arbitrary"),
        name="ada",
    )(cvec, w, b.reshape(1, cols))


def _inproj_kernel(x_ref, sh_ref, sc_ref, wna_ref, wml_ref, wg_ref, gb_ref, zna_ref, zml_ref, g_ref):
    y = _ln_rows(x_ref[0]) * (1.0 + sc_ref[0]) + sh_ref[0]
    yb = y.astype(BF16)
    zna_ref[0] = jnp.dot(yb, wna_ref[...], preferred_element_type=F32).astype(BF16)
    zml_ref[0] = jnp.dot(yb, wml_ref[...], preferred_element_type=F32).astype(BF16)
    g_ref[0] = jnp.dot(yb, wg_ref[...], preferred_element_type=F32) + gb_ref[...]


def _inproj(x, shift, scale, wna, wml, wg, gb, tm):
    b, t, d = x.shape
    full = lambda a: pl.BlockSpec(a.shape, lambda i, j: (0,) * a.ndim)
    row = lambda w: pl.BlockSpec((1, tm, w), lambda i, j: (i, j, 0))
    vec = pl.BlockSpec((1, 1, d), lambda i, j: (i, 0, 0))
    return pl.pallas_call(
        _inproj_kernel,
        grid=(b, t // tm),
        in_specs=[row(d), vec, vec, full(wna), full(wml), full(wg), full(gb)],
        out_specs=[row(wna.shape[1]), row(wml.shape[1]), row(LANES)],
        out_shape=[jax.ShapeDtypeStruct((b, t, wna.shape[1]), BF16),
                   jax.ShapeDtypeStruct((b, t, wml.shape[1]), BF16),
                   jax.ShapeDtypeStruct((b, t, LANES), F32)],
        compiler_params=_cparams("parallel", "parallel"),
        name="inproj",
    )(x, shift, scale, wna, wml, wg, gb)


HALO = 16


def _qk_kernel(cur_ref, prev_ref, next_ref, cw_ref, cb_ref, cos_ref, sin_ref, o_ref, pad_ref, *, rope, tr):
    i = pl.program_id(1)
    last = pl.num_programs(1) - 1
    pad = CONV_K // 2
    pad_ref[HALO:HALO + tr, :] = cur_ref[0].astype(F32)
    pad_ref[0:HALO, :] = jnp.where(i > 0, prev_ref[0].astype(F32), 0.0)
    pad_ref[HALO + tr:2 * HALO + tr, :] = jnp.where(i < last, next_ref[0].astype(F32), 0.0)
    acc = cw_ref[0:1, :] * pad_ref[HALO - pad:HALO - pad + tr, :] + cb_ref[...]
    for j in range(1, CONV_K):
        acc = acc + cw_ref[j:j + 1, :] * pad_ref[HALO - pad + j:HALO - pad + j + tr, :]
    u = _silu(acc)
    width = u.shape[1]
    lane = lax.broadcasted_iota(jnp.int32, (tr, LANES), 1)
    first_half = (lane % (ML_HEAD_DIM // 2)) < (ML_HEAD_DIM // 4)
    kscale = ML_HEAD_DIM ** -0.5
    for g in range(width // LANES):
        ug = u[:, g * LANES:(g + 1) * LANES]
        if rope:
            partner = jnp.where(first_half, pltpu.roll(ug, LANES - ML_HEAD_DIM // 4, 1),
                                pltpu.roll(ug, ML_HEAD_DIM // 4, 1))
            ug = ug * cos_ref[...] + partner * sin_ref[...]
        if g >= ML_HEADS:
            ug = ug * kscale
        o_ref[0, :, g * LANES:(g + 1) * LANES] = ug.astype(BF16)


def _qk_streams(zml, conv_w, conv_b, cos_t, sin_t, rope, tr):
    b, t, _ = zml.shape
    width = 2 * ML_WIDTH
    nh = tr // HALO
    nblk = t // HALO
    return pl.pallas_call(
        functools.partial(_qk_kernel, rope=rope, tr=tr),
        grid=(b, t // tr),
        in_specs=[pl.BlockSpec((1, tr, width), lambda i, j: (i, j, 0)),
                  pl.BlockSpec((1, HALO, width), lambda i, j: (i, jnp.maximum(j * nh - 1, 0), 0)),
                  pl.BlockSpec((1, HALO, width), lambda i, j: (i, jnp.minimum((j + 1) * nh, nblk - 1), 0)),
                  pl.BlockSpec((CONV_K, width), lambda i, j: (0, 0)),
                  pl.BlockSpec((1, width), lambda i, j: (0, 0)),
                  pl.BlockSpec((tr, LANES), lambda i, j: (j, 0)),
                  pl.BlockSpec((tr, LANES), lambda i, j: (j, 0))],
        out_specs=pl.BlockSpec((1, tr, width), lambda i, j: (i, j, 0)),
        out_shape=jax.ShapeDtypeStruct((b, t, width), BF16),
        scratch_shapes=[pltpu.VMEM((tr + 2 * HALO, width), F32)],
        compiler_params=_cparams("parallel", "parallel"),
        name="qk_rope" if rope else "qk_ctx",
    )(zml, zml, zml, conv_w, conv_b.reshape(1, width), cos_t, sin_t)


def _rope_tables(n):
    pos = jnp.arange(n)
    half = ML_HEAD_DIM // 4
    inv = ROPE_BASE ** (-jnp.arange(half, dtype=F32) / half)

    def axis_tables(p):
        ang = p.astype(F32)[:, None] * inv[None, :]
        c, s = jnp.cos(ang), jnp.sin(ang)
        return jnp.concatenate([c, c], axis=-1), jnp.concatenate([-s, s], axis=-1)

    cr, sr = axis_tables(pos // GRID_W)
    cc, sc = axis_tables(pos % GRID_W)
    return jnp.concatenate([cr, cc], axis=-1), jnp.concatenate([sr, sc], axis=-1)


def _log_sigmoid(v):
    return jnp.minimum(v, 0.0) - jnp.log1p(jnp.exp(-jnp.abs(v)))


def _mlstm_direction(d, qk, v, g, cn_s, m_s, h_ref):
    L = ML_CHUNK
    row = lax.broadcasted_iota(jnp.int32, (L, L), 0)
    col = lax.broadcasted_iota(jnp.int32, (L, L), 1)
    fwd = d == 0
    keep = (row >= col) if fwd else (row <= col)
    tri = keep.astype(F32)
    lf = _log_sigmoid(g)
    bcol_all = jnp.dot(tri, lf, preferred_element_type=F32, precision=HIGHEST)
    brow_all = bcol_all.T
    g_t = g.T
    ones = jnp.ones((L, ML_HEAD_DIM), BF16)
    end = L - 1 if fwd else 0
    outs = []
    for h in range(ML_HEADS):
        ci = (2 * d) * ML_HEADS + h
        cf = (2 * d + 1) * ML_HEADS + h
        bc = bcol_all[:, cf:cf + 1]
        br = brow_all[cf:cf + 1, :]
        ir = g_t[ci:ci + 1, :]
        ic = g[:, ci:ci + 1]
        m_prev = m_s[d, h][0:1, 0:1]
        dlog = jnp.where(keep, bc - br + ir, NEG)
        m_t = jnp.maximum(bc + m_prev, jnp.max(dlog, axis=1, keepdims=True))
        dw = jnp.exp(dlog - m_t)
        inter = jnp.exp(bc + m_prev - m_t)
        qh = qk[:, h * LANES:(h + 1) * LANES]
        kh = qk[:, ML_WIDTH + h * LANES:ML_WIDTH + (h + 1) * LANES]
        vp = jnp.concatenate([v[:, h * LANES:(h + 1) * LANES], ones], axis=1)
        s = lax.dot_general(qh, kh, (((1,), (1,)), ((), ())), preferred_element_type=F32) * dw
        cn = cn_s[d, h]
        nd = (jnp.dot(s.astype(BF16), vp, preferred_element_type=F32)
              + inter * jnp.dot(qh, cn.astype(BF16), preferred_element_type=F32))
        num = nd[:, :ML_HEAD_DIM]
        den = nd[:, ML_HEAD_DIM:]
        outs.append(num / jnp.maximum(jnp.abs(den), jnp.exp(-m_t)))
        b_end = br[:, end:end + 1]
        g_row = b_end - br + ir
        m_new = jnp.maximum(b_end + m_prev, jnp.max(g_row, axis=1, keepdims=True))
        decay = jnp.exp(b_end + m_prev - m_new)
        wgt = jnp.exp(b_end - bc + ic - m_new)
        kw = (kh.astype(F32) * wgt).astype(BF16)
        upd = lax.dot_general(kw, vp, (((0,), (0,)), ((), ())), preferred_element_type=F32)
        cn_s[d, h] = decay * cn + upd
        m_s[d, h] = jnp.broadcast_to(m_new, m_s.shape[2:])
    if h_ref is not None:
        h_ref[0] = jnp.concatenate(outs, axis=1).astype(h_ref.dtype)


def _mlstm_kernel(qkf_ref, vf_ref, gf_ref, qkb_ref, vb_ref, gb_ref, c0_ref, m0_ref, *rest, emit_h):
    if emit_h:
        hf_ref, hb_ref, ct_ref, mt_ref, cn_s, m_s = rest
    else:
        ct_ref, mt_ref, cn_s, m_s = rest
        hf_ref = hb_ref = None
    c = pl.program_id(1)

    @pl.when(c == 0)
    def _():
        cn_s[...] = c0_ref[0]
        m_s[...] = m0_ref[0]

    _mlstm_direction(0, qkf_ref[0], vf_ref[0], gf_ref[0], cn_s, m_s, hf_ref)
    _mlstm_direction(1, qkb_ref[0], vb_ref[0], gb_ref[0], cn_s, m_s, hb_ref)

    @pl.when(c == pl.num_programs(1) - 1)
    def _():
        ct_ref[0] = cn_s[...]
        mt_ref[0] = m_s[...]


def _mlstm(qk, zml, gates, c0, m0, emit_h):
    b, t, _ = qk.shape
    L = ML_CHUNK
    nc = t // L
    vcol = 2
    f_idx = lambda i, c: (i, c, 0)
    b_idx = lambda i, c: (i, nc - 1 - c, 0)
    st_c = pl.BlockSpec((1, 2, ML_HEADS, ML_HEAD_DIM, 2 * ML_HEAD_DIM), lambda i, c: (i, 0, 0, 0, 0))
    st_m = pl.BlockSpec((1, 2, ML_HEADS, 8, LANES), lambda i, c: (i, 0, 0, 0, 0))
    out_specs = [st_c, st_m]
    out_shape = [jax.ShapeDtypeStruct(c0.shape, F32), jax.ShapeDtypeStruct(m0.shape, F32)]
    if emit_h:
        out_specs = [pl.BlockSpec((1, L, ML_WIDTH), f_idx), pl.BlockSpec((1, L, ML_WIDTH), b_idx)] + out_specs
        out_shape = [jax.ShapeDtypeStruct((b, t, ML_WIDTH), BF16)] * 2 + out_shape
    return pl.pallas_call(
        functools.partial(_mlstm_kernel, emit_h=emit_h),
        grid=(b, nc),
        in_specs=[pl.BlockSpec((1, L, 2 * ML_WIDTH), f_idx),
                  pl.BlockSpec((1, L, ML_WIDTH), lambda i, c: (i, c, vcol)),
                  pl.BlockSpec((1, L, LANES), f_idx),
                  pl.BlockSpec((1, L, 2 * ML_WIDTH), b_idx),
                  pl.BlockSpec((1, L, ML_WIDTH), lambda i, c: (i, nc - 1 - c, vcol)),
                  pl.BlockSpec((1, L, LANES), b_idx),
                  st_c, st_m],
        out_specs=out_specs,
        out_shape=out_shape,
        scratch_shapes=[pltpu.VMEM((2, ML_HEADS, ML_HEAD_DIM, 2 * ML_HEAD_DIM), F32),
                        pltpu.VMEM((2, ML_HEADS, 8, LANES), F32)],
        compiler_params=_cparams("parallel", "arbitrary"),
        name="mlstm" if emit_h else "mlstm_ctx",
    )(qk, zml, gates, qk, zml, gates, c0, m0)


def _na_kernel(q_ref, k0_ref, k1_ref, k2_ref, v0_ref, v1_ref, v2_ref, kc_ref, vc_ref, bias_ref, o_ref):
    nq = q_ref.shape[1]
    lane = lax.broadcasted_iota(jnp.int32, (nq, LANES), 1)
    nt = (((1,), (1,)), ((), ()))
    for p in range(NA_HEADS // 2):
        sl = slice(p * LANES, (p + 1) * LANES)
        q2 = q_ref[0, :, sl] * (NA_HEAD_DIM ** -0.5)
        kwin = jnp.concatenate([k0_ref[0, :, sl], k1_ref[0, :, sl], k2_ref[0, :, sl]], axis=0)
        vwin = jnp.concatenate([v0_ref[0, :, sl], v1_ref[0, :, sl], v2_ref[0, :, sl]], axis=0)
        kc = kc_ref[0, :, sl]
        vc = vc_ref[0, :, sl]
        halves = []
        for a in range(2):
            in_head = (lane >= a * NA_HEAD_DIM) & (lane < (a + 1) * NA_HEAD_DIM)
            qm = jnp.where(in_head, q2, jnp.zeros_like(q2))
            s_win = lax.dot_general(qm, kwin, nt, preferred_element_type=F32) + bias_ref[0, 2 * p + a]
            s_ctx = lax.dot_general(qm, kc, nt, preferred_element_type=F32)
            m = jnp.maximum(jnp.max(s_win, axis=1, keepdims=True), jnp.max(s_ctx, axis=1, keepdims=True))
            p_win = jnp.exp(s_win - m)
            p_ctx = jnp.exp(s_ctx - m)
            den = jnp.sum(p_win, axis=1, keepdims=True) + jnp.sum(p_ctx, axis=1, keepdims=True)
            o = (jnp.dot(p_win.astype(BF16), vwin, preferred_element_type=F32)
                 + jnp.dot(p_ctx.astype(BF16), vc, preferred_element_type=F32))
            halves.append(o / den)
        o_ref[0, :, sl] = jnp.where(lane < NA_HEAD_DIM, halves[0], halves[1]).astype(o_ref.dtype)


def _na_bias_tables(rpb, rows):
    R = NA_ROWS_PER_STEP
    nblk = rows // R
    kr = NA_WIN_ROWS
    cq = np.arange(GRID_W)
    cstart = np.clip(cq - NA_WIN_COLS // 2, 0, GRID_W - NA_WIN_COLS)
    ck = np.arange(GRID_W)
    col_ok = (ck[None, :] >= cstart[:, None]) & (ck[None, :] < cstart[:, None] + NA_WIN_COLS)
    col_off = np.where(col_ok, ck[None, :] - cq[:, None] + NA_WIN_COLS - 1, 0)
    row_ok = np.zeros((3, R, 3 * R), bool)
    row_off = np.zeros((3, R, 3 * R), np.int64)
    for vi, j in enumerate((0, 1, nblk - 1)):
        for i in range(R):
            r = j * R + i
            r0 = min(max(r - kr // 2, 0), rows - kr)
            for t in range(3):
                jb = j - 1 + t
                if jb < 0 or jb >= nblk:
                    continue
                for rr in range(R):
                    krow = jb * R + rr
                    if r0 <= krow < r0 + kr:
                        row_ok[vi, i, t * R + rr] = True
                        row_off[vi, i, t * R + rr] = krow - r + NA_WIN_ROWS - 1
    ok = row_ok[:, :, None, :, None] & col_ok[None, None, :, None, :]
    ro = np.broadcast_to(row_off[:, :, None, :, None], ok.shape)
    co = np.broadcast_to(col_off[None, None, :, None, :], ok.shape)
    flat = (ro * rpb.shape[2] + co).reshape(-1)
    vals = jnp.take(rpb.reshape(NA_HEADS, -1), jnp.asarray(flat, jnp.int32), axis=1)
    vals = vals.reshape((NA_HEADS,) + ok.shape)
    vals = jnp.where(jnp.asarray(ok)[None], vals, NEG)
    return vals.transpose(1, 0, 2, 3, 4, 5).reshape(3, NA_HEADS, R * GRID_W, 3 * R * GRID_W).astype(F32)


def _na(zna, zcna, bias, rows):
    b, n, _ = zna.shape
    ctx = zcna.shape[1]
    R = NA_ROWS_PER_STEP
    nq = R * GRID_W
    nblk = rows // R
    kb = lambda col, off: pl.BlockSpec(
        (1, nq, NA_WIDTH), lambda i, j: (i, jnp.clip(j + off, 0, nblk - 1), col))
    variant = lambda i, j: (jnp.where(j == 0, 0, jnp.where(j == nblk - 1, 2, 1)), 0, 0, 0)
    return pl.pallas_call(
        _na_kernel,
        grid=(b, nblk),
        in_specs=[pl.BlockSpec((1, nq, NA_WIDTH), lambda i, j: (i, j, 0)),
                  kb(1, -1), kb(1, 0), kb(1, 1), kb(2, -1), kb(2, 0), kb(2, 1),
                  pl.BlockSpec((1, ctx, NA_WIDTH), lambda i, j: (i, 0, 1)),
                  pl.BlockSpec((1, ctx, NA_WIDTH), lambda i, j: (i, 0, 2)),
                  pl.BlockSpec((1, NA_HEADS, nq, 3 * nq), variant)],
        out_specs=pl.BlockSpec((1, nq, NA_WIDTH), lambda i, j: (i, j, 0)),
        out_shape=jax.ShapeDtypeStruct((b, n, NA_WIDTH), BF16),
        compiler_params=_cparams("parallel", "parallel"),
        name="na",
    )(zna, zna, zna, zna, zna, zna, zna, zcna, zcna, bias)


def _mix_kernel(na_ref, hf_ref, hb_ref, o_ref, x_ref, g1_ref, sh2_ref, sc2_ref, wo_ref, mg_ref, l1g_ref, l1b_ref,
                wr_ref, br_ref, xmid_ref, ri_ref, rw_ref, cnt_ref, carry_s, *, alpha):
    first = (pl.program_id(0) == 0) & (pl.program_id(1) == 0)

    @pl.when(first)
    def _():
        carry_s[...] = jnp.zeros_like(carry_s)

    tm = x_ref.shape[1]
    h = hf_ref[0].astype(F32) + hb_ref[0].astype(F32)
    parts = []
    for hd in range(ML_HEADS):
        hh = h[:, hd * LANES:(hd + 1) * LANES]
        parts.append(hh * lax.rsqrt(jnp.mean(hh * hh, axis=-1, keepdims=True) + LN_EPS))
    hn = jnp.concatenate(parts, axis=1)
    ml = (hn * mg_ref[...] * jax.nn.sigmoid(o_ref[0].astype(F32))).astype(BF16)
    mix = (jnp.dot(na_ref[0], wo_ref[0:NA_WIDTH, :], preferred_element_type=F32)
           + jnp.dot(ml, wo_ref[NA_WIDTH:, :], preferred_element_type=F32))
    xmid = _ln_rows(alpha * x_ref[0] + g1_ref[0] * mix) * l1g_ref[...] + l1b_ref[...]
    xmid_ref[0] = xmid

    xt = (_ln_rows(xmid) * (1.0 + sc2_ref[0]) + sh2_ref[0]).astype(BF16)
    logits = jnp.dot(xt, wr_ref[...], preferred_element_type=F32) + br_ref[...]
    lane = lax.broadcasted_iota(jnp.int32, (tm, LANES), 1)
    is_g = lane < N_GROUPS
    gl = jnp.where(is_g, logits, NEG)
    gmax = jnp.max(gl, axis=1, keepdims=True)
    grp = jnp.min(jnp.where(gl == gmax, lane, LANES), axis=1, keepdims=True)
    gsum = jnp.sum(jnp.where(is_g, jnp.exp(gl - gmax), 0.0), axis=1, keepdims=True)
    grp_w = 1.0 / gsum
    lo = N_GROUPS + EXPERTS_PER_GROUP * grp
    el = jnp.where((lane >= lo) & (lane < lo + EXPERTS_PER_GROUP), logits, NEG)
    t1 = jnp.max(el, axis=1, keepdims=True)
    i1 = jnp.min(jnp.where(el == t1, lane, LANES), axis=1, keepdims=True)
    el2 = jnp.where(lane == i1, NEG, el)
    t2 = jnp.max(el2, axis=1, keepdims=True)
    i2 = jnp.min(jnp.where(el2 == t2, lane, LANES), axis=1, keepdims=True)
    e21 = jnp.exp(t2 - t1)
    w0 = grp_w / (1.0 + e21)
    w1 = grp_w * e21 / (1.0 + e21)

    hit1 = lane == i1
    hit2 = lane == i2
    onehot = (hit1 | hit2).astype(BF16)
    r_i = lax.broadcasted_iota(jnp.int32, (tm, tm), 0)
    c_i = lax.broadcasted_iota(jnp.int32, (tm, tm), 1)
    before = (r_i > c_i).astype(BF16)
    prefix = jnp.dot(before, onehot, preferred_element_type=F32) + carry_s[0:1, :]
    rank0 = jnp.sum(jnp.where(hit1, prefix, 0.0), axis=1, keepdims=True)
    rank1 = jnp.sum(jnp.where(hit2, prefix, 0.0), axis=1, keepdims=True)
    total = carry_s[0:1, :] + jnp.sum(onehot.astype(F32), axis=0, keepdims=True)
    carry_s[...] = jnp.broadcast_to(total, carry_s.shape)
    cnt_ref[...] = jnp.broadcast_to(total, cnt_ref.shape)

    ri = jnp.where(lane == 0, i1 - N_GROUPS,
                   jnp.where(lane == 1, i2 - N_GROUPS,
                             jnp.where(lane == 2, rank0.astype(jnp.int32),
                                       jnp.where(lane == 3, rank1.astype(jnp.int32), 0))))
    ri_ref[0] = ri
    rw_ref[0] = jnp.where(lane == 0, w0, jnp.where(lane == 1, w1, 0.0))


def _mix(na, hf, hb, zml, x, g1, sh2, sc2, wo, mg, l1g, l1b, wr, br, alpha, tm):
    b, n, d = x.shape
    row = lambda w: pl.BlockSpec((1, tm, w), lambda i, j: (i, j, 0))
    vec = pl.BlockSpec((1, 1, d), lambda i, j: (i, 0, 0))
    full = lambda a: pl.BlockSpec(a.shape, lambda i, j: (0,) * a.ndim)
    ocol = 3
    return pl.pallas_call(
        functools.partial(_mix_kernel, alpha=alpha),
        grid=(b, n // tm),
        in_specs=[row(NA_WIDTH), row(ML_WIDTH), row(ML_WIDTH),
                  pl.BlockSpec((1, tm, ML_WIDTH), lambda i, j: (i, j, ocol)),
                  row(d), vec, vec, vec, full(wo), full(mg), full(l1g), full(l1b), full(wr), full(br)],
        out_specs=[row(d), row(LANES), row(LANES), pl.BlockSpec((8, LANES), lambda i, j: (0, 0))],
        out_shape=[jax.ShapeDtypeStruct((b, n, d), F32),
                   jax.ShapeDtypeStruct((b, n, LANES), jnp.int32),
                   jax.ShapeDtypeStruct((b, n, LANES), F32),
                   jax.ShapeDtypeStruct((8, LANES), F32)],
        scratch_shapes=[pltpu.VMEM((8, LANES), F32)],
        compiler_params=_cparams("arbitrary", "arbitrary"),
        name="mix",
    )(na, hf, hb, zml, x, g1, sh2, sc2, wo, mg, l1g, l1b, wr, br)


def _zero_fill_padding(pad_base_ref, pad_len_ref, nused_ref, xs_ref, zero_s, sem, wait):
    tb = zero_s.shape[0]

    def run(copy):
        copy.wait() if wait else copy.start()

    def per_expert(e, _):
        plen = pad_len_ref[e]
        base = pad_base_ref[e]
        end = base + plen
        bit = tb // 2
        while bit >= SUBLANES:
            off = pl.multiple_of(end - (plen & ~(bit - 1)), SUBLANES)

            @pl.when((plen & bit) != 0)
            def _(bit=bit, off=off):
                run(pltpu.make_async_copy(zero_s.at[pl.ds(0, bit)], xs_ref.at[pl.ds(off, bit)], sem))

            bit //= 2

        def single(r, _):
            run(pltpu.make_async_copy(zero_s.at[pl.ds(0, 1)], xs_ref.at[pl.ds(base + r, 1)], sem))
            return 0

        lax.fori_loop(0, plen & (SUBLANES - 1), single, 0)
        return 0

    lax.fori_loop(0, N_EXPERTS, per_expert, 0)

    def per_block(i, _):
        run(pltpu.make_async_copy(zero_s, xs_ref.at[pl.ds(pl.multiple_of(i * tb, tb), tb)], sem))
        return 0

    lax.fori_loop(nused_ref[0], xs_ref.shape[0] // tb, per_block, 0)


def _dispatch_kernel(pad_base_ref, pad_len_ref, nused_ref, dest_ref, xmid_ref, sh2_ref, sc2_ref, xs_ref,
                     xt_s, zero_s, sem, zsem):
    tm = xmid_ref.shape[1]
    first = (pl.program_id(0) == 0) & (pl.program_id(1) == 0)

    @pl.when(first)
    def _():
        zero_s[...] = jnp.zeros_like(zero_s)
        _zero_fill_padding(pad_base_ref, pad_len_ref, nused_ref, xs_ref, zero_s, zsem, False)

    xt_s[...] = _ln_rows(xmid_ref[0]) * (1.0 + sc2_ref[0]) + sh2_ref[0]

    def row_copy(r, k):
        return pltpu.make_async_copy(xt_s.at[pl.ds(r, 1)], xs_ref.at[pl.ds(dest_ref[0, k, r], 1)], sem)

    def start(r, _):
        row_copy(r, 0).start()
        row_copy(r, 1).start()
        return 0

    def wait(r, _):
        row_copy(r, 0).wait()
        row_copy(r, 1).wait()
        return 0

    lax.fori_loop(0, tm, start, 0)
    lax.fori_loop(0, tm, wait, 0)

    @pl.when(first)
    def _():
        _zero_fill_padding(pad_base_ref, pad_len_ref, nused_ref, xs_ref, zero_s, zsem, True)


def _dispatch(xmid, sh2, sc2, dest, pad_base, pad_len, nused, cap, tm):
    b, n, d = xmid.shape
    nt = n // tm
    dest = dest.reshape(b * nt, tm, 2).transpose(0, 2, 1)
    vec = pl.BlockSpec((1, 1, d), lambda i, j, *_: (i, 0, 0))
    return pl.pallas_call(
        _dispatch_kernel,
        grid_spec=pltpu.PrefetchScalarGridSpec(
            num_scalar_prefetch=3,
            grid=(b, nt),
            in_specs=[pl.BlockSpec((1, 2, tm), lambda i, j, *_: (i * nt + j, 0, 0), memory_space=pltpu.SMEM),
                      pl.BlockSpec((1, tm, d), lambda i, j, *_: (i, j, 0)), vec, vec],
            out_specs=pl.BlockSpec(memory_space=pl.ANY),
            scratch_shapes=[pltpu.VMEM((tm, d), F32), pltpu.VMEM((EXPERT_ROWS, d), F32),
                            pltpu.SemaphoreType.DMA, pltpu.SemaphoreType.DMA]),
        out_shape=jax.ShapeDtypeStruct((cap, d), F32),
        compiler_params=_cparams("arbitrary", "arbitrary"),
        name="dispatch",
    )(pad_base, pad_len, nused, dest, xmid, sh2, sc2)


def _expert_kernel(be_ref, nv_ref, xs_ref, w1_ref, w3_ref, w2_ref, ys_ref, w1b, w3b, w2b):
    i = pl.program_id(0)
    e = be_ref[i]
    changed = (i == 0) | (be_ref[jnp.maximum(i - 1, 0)] != e)

    @pl.when(changed)
    def _():
        w1b[...] = w1_ref[0].astype(BF16)
        w3b[...] = w3_ref[0].astype(BF16)
        w2b[...] = w2_ref[0].astype(BF16)

    nv = nv_ref[i]

    @pl.when(nv > 0)
    def _():
        xb = xs_ref[...].astype(BF16)
        h1 = jnp.dot(xb, w1b[...], preferred_element_type=F32)
        h3 = jnp.dot(xb, w3b[...], preferred_element_type=F32)
        a = (_silu(h1) * h3).astype(BF16)
        ys_ref[...] = jnp.dot(a, w2b[...], preferred_element_type=F32)

    @pl.when(nv == 0)
    def _():
        ys_ref[...] = jnp.zeros_like(ys_ref)


def _experts(xs, block_e, block_nv, w1, w3, w2):
    cap, d = xs.shape
    hid = w1.shape[2]
    tb = EXPERT_ROWS
    return pl.pallas_call(
        _expert_kernel,
        grid_spec=pltpu.PrefetchScalarGridSpec(
            num_scalar_prefetch=2,
            grid=(cap // tb,),
            in_specs=[pl.BlockSpec((tb, d), lambda i, be, nv: (i, 0)),
                      pl.BlockSpec((1, d, hid), lambda i, be, nv: (be[i], 0, 0)),
                      pl.BlockSpec((1, d, hid), lambda i, be, nv: (be[i], 0, 0)),
                      pl.BlockSpec((1, hid, d), lambda i, be, nv: (be[i], 0, 0))],
            out_specs=pl.BlockSpec((tb, d), lambda i, be, nv: (i, 0)),
            scratch_shapes=[pltpu.VMEM((d, hid), BF16), pltpu.VMEM((d, hid), BF16), pltpu.VMEM((hid, d), BF16)]),
        out_shape=jax.ShapeDtypeStruct((cap, d), F32),
        compiler_params=_cparams("arbitrary"),
        name="experts",
    )(block_e, block_nv, xs, w1, w3, w2)


def _combine_kernel(dest_ref, xmid_ref, rw_ref, g2_ref, l2g_ref, l2b_ref, ys_ref, o_ref, y0_s, y1_s, sem, *, alpha):
    tm = xmid_ref.shape[1]

    def row_copy(r, k, buf):
        return pltpu.make_async_copy(ys_ref.at[pl.ds(dest_ref[0, k, r], 1)], buf.at[pl.ds(r, 1)], sem)

    def start(r, _):
        row_copy(r, 0, y0_s).start()
        row_copy(r, 1, y1_s).start()
        return 0

    def wait(r, _):
        row_copy(r, 0, y0_s).wait()
        row_copy(r, 1, y1_s).wait()
        return 0

    lax.fori_loop(0, tm, start, 0)
    lax.fori_loop(0, tm, wait, 0)
    rw = rw_ref[0]
    moe = rw[:, 0:1] * y0_s[...] + rw[:, 1:2] * y1_s[...]
    o_ref[0] = _ln_rows(alpha * xmid_ref[0] + g2_ref[0] * moe) * l2g_ref[...] + l2b_ref[...]


def _combine(xmid, rw, g2, l2g, l2b, ys, dest, alpha, tm):
    b, n, d = xmid.shape
    nt = n // tm
    dest = dest.reshape(b * nt, tm, 2).transpose(0, 2, 1)
    full = lambda a: pl.BlockSpec(a.shape, lambda i, j: (0,) * a.ndim)
    return pl.pallas_call(
        functools.partial(_combine_kernel, alpha=alpha),
        grid=(b, nt),
        in_specs=[pl.BlockSpec((1, 2, tm), lambda i, j: (i * nt + j, 0, 0), memory_space=pltpu.SMEM),
                  pl.BlockSpec((1, tm, d), lambda i, j: (i, j, 0)),
                  pl.BlockSpec((1, tm, LANES), lambda i, j: (i, j, 0)),
                  pl.BlockSpec((1, 1, d), lambda i, j: (i, 0, 0)),
                  full(l2g), full(l2b),
                  pl.BlockSpec(memory_space=pl.ANY)],
        out_specs=pl.BlockSpec((1, tm, d), lambda i, j: (i, j, 0)),
        out_shape=jax.ShapeDtypeStruct((b, n, d), F32),
        scratch_shapes=[pltpu.VMEM((tm, d), F32), pltpu.VMEM((tm, d), F32), pltpu.SemaphoreType.DMA],
        compiler_params=_cparams("arbitrary", "arbitrary"),
        name="combine",
    )(dest, xmid, rw, g2, l2g, l2b, ys)


def _tile(n, want):
    t = min(n, want)
    assert n % t == 0, (n, t)
    return t


def kernel(x, c, ctx, c_ctx, w_ada, b_ada, w_in, conv_w, conv_b, gate_b, rpb, ml_norm_g, w_out, ln1_g, ln1_b,
           w_router_g, b_router_g, w_router_e, b_router_e, w1, w3, w2, ln2_g, ln2_b):
    B, N, D = x.shape
    T_CTX = ctx.shape[1]
    depth = w_ada.shape[0]
    rows = N // GRID_W
    assert depth == 1 and N % GRID_W == 0 and rows % NA_ROWS_PER_STEP == 0 and rows >= 3 * NA_ROWS_PER_STEP
    assert N % ML_CHUNK == 0 and T_CTX % ML_CHUNK == 0
    alpha = (2.0 * depth) ** 0.25
    l = 0

    pad_rows = -(B + 1) % 8
    cvec = jnp.concatenate([c, c_ctx[None], jnp.zeros((pad_rows, D), F32)], axis=0)
    ada = _ada(cvec, w_ada[l], b_ada[l])
    sh1, sc1, g1, sh2, sc2, g2 = [a[:, None, :] for a in jnp.split(ada[:B], 6, axis=-1)]
    csh1, csc1 = [jnp.broadcast_to(a[None], (B, 1, D)) for a in jnp.split(ada[B:B + 1], 6, axis=-1)[:2]]

    col_ml = 3 * NA_WIDTH
    col_g = col_ml + 4 * ML_WIDTH
    wb = w_in[l].astype(BF16)
    wna, wml = wb[:, :col_ml], wb[:, col_ml:col_g]
    n_gate = 4 * ML_HEADS
    wg = jnp.pad(wb[:, col_g:], ((0, 0), (0, LANES - n_gate)))
    gb = jnp.pad(gate_b[l], (0, LANES - n_gate)).reshape(1, LANES)
    zna, zml, gates = _inproj(x, sh1, sc1, wna, wml, wg, gb, _tile(N, 512))
    zcna, zcml, gates_c = _inproj(ctx, csh1, csc1, wna, wml, wg, gb, _tile(T_CTX, 256))

    cos_t, sin_t = _rope_tables(N)
    qk_c = _qk_streams(zcml, conv_w[l], conv_b[l], cos_t[:T_CTX], sin_t[:T_CTX], False, _tile(T_CTX, 512))
    qk_l = _qk_streams(zml, conv_w[l], conv_b[l], cos_t, sin_t, True, _tile(N, 512))
    c0 = jnp.zeros((B, 2, ML_HEADS, ML_HEAD_DIM, 2 * ML_HEAD_DIM), F32)
    m0 = jnp.zeros((B, 2, ML_HEADS, 8, LANES), F32)
    c_ctx_end, m_ctx_end = _mlstm(qk_c, zcml, gates_c, c0, m0, False)
    hf, hb, _, _ = _mlstm(qk_l, zml, gates, c_ctx_end, m_ctx_end, True)

    na = _na(zna, zcna, _na_bias_tables(rpb[l], rows), rows)

    wr = jnp.pad(jnp.concatenate([w_router_g[l], w_router_e[l]], axis=1),
                 ((0, 0), (0, LANES - N_GROUPS - N_EXPERTS))).astype(BF16)
    br = jnp.pad(jnp.concatenate([b_router_g[l], b_router_e[l]]), (0, LANES - N_GROUPS - N_EXPERTS)).reshape(1, LANES)
    tm = _tile(N, 256)
    xmid, ri, rw, counts = _mix(na, hf, hb, zml, x, g1, sh2, sc2, w_out[l].astype(BF16),
                                ml_norm_g[l].reshape(1, ML_WIDTH), ln1_g[l].reshape(1, D), ln1_b[l].reshape(1, D),
                                wr, br, alpha, tm)

    tb = EXPERT_ROWS
    n_assign = 2 * B * N
    cap = -(-n_assign // tb) * tb + N_EXPERTS * tb
    sizes = counts[0, N_GROUPS:N_GROUPS + N_EXPERTS].astype(jnp.int32)
    padded = (sizes + tb - 1) // tb * tb
    pend = jnp.cumsum(padded)
    pstart = pend - padded
    dest = jnp.take(pstart, ri[..., 0:2], axis=0) + ri[..., 2:4]
    blk0 = jnp.arange(cap // tb, dtype=jnp.int32) * tb
    block_e = jnp.minimum(jnp.searchsorted(pend, blk0, side='right'), N_EXPERTS - 1).astype(jnp.int32)
    block_nv = jnp.clip(pstart[block_e] + sizes[block_e] - blk0, 0, tb).astype(jnp.int32)

    nused = (pend[-1:] // tb).astype(jnp.int32)
    xs = _dispatch(xmid, sh2, sc2, dest, pstart + sizes, padded - sizes, nused, cap, tm)
    ys = _experts(xs, block_e, block_nv, w1[l], w3[l], w2[l])
    return _combine(xmid, rw, g2, ln2_g[l].reshape(1, D), ln2_b[l].reshape(1, D), ys, dest, alpha, tm)
```

```python
import functools

import numpy as np
import jax
import jax.numpy as jnp
from jax import lax
from jax.experimental import pallas as pl
from jax.experimental.pallas import tpu as pltpu

F32 = jnp.float32
BF16 = jnp.bfloat16
HIGHEST = lax.Precision.HIGHEST

GRID_W = 64
NA_HEADS = 8
NA_HEAD_DIM = 64
NA_WIDTH = NA_HEADS * NA_HEAD_DIM
NA_WIN_ROWS = 8
NA_WIN_COLS = 16
ML_HEADS = 4
ML_HEAD_DIM = 128
ML_WIDTH = ML_HEADS * ML_HEAD_DIM
ML_CHUNK = 128
CONV_K = 5
N_GROUPS = 8
EXPERTS_PER_GROUP = 8
N_EXPERTS = N_GROUPS * EXPERTS_PER_GROUP
ROPE_BASE = 10000.0
LN_EPS = 1e-5

LANES = 128
SUBLANES = 8
VMEM_LIMIT = 56 * 1024 * 1024

NA_ROWS_PER_STEP = 4
EXPERT_ROWS = 512
NEG = -1e30


def _cparams(*sem):
    return pltpu.CompilerParams(dimension_semantics=sem, vmem_limit_bytes=VMEM_LIMIT)


def _silu(v):
    return v * jax.nn.sigmoid(v)


def _ln_rows(v):
    mu = jnp.mean(v, axis=-1, keepdims=True)
    vc = v - mu
    var = jnp.mean(vc * vc, axis=-1, keepdims=True)
    return vc * lax.rsqrt(var + LN_EPS)


def _ada_kernel(c_ref, w_ref, b_ref, o_ref):
    o_ref[...] = jnp.dot(_silu(c_ref[...]), w_ref[...], preferred_element_type=F32,
                         precision=HIGHEST) + b_ref[...]


def _ada(cvec, w, b):
    rows, d = cvec.shape
    cols = w.shape[1]
    tn = 1024
    return pl.pallas_call(
        _ada_kernel,
        grid=(cols // tn,),
        in_specs=[pl.BlockSpec((rows, d), lambda j: (0, 0)),
                  pl.BlockSpec((d, tn), lambda j: (0, j)),
                  pl.BlockSpec((1, tn), lambda j: (0, j))],
        out_specs=pl.BlockSpec((rows, tn), lambda j: (0, j)),
        out_shape=jax.ShapeDtypeStruct((rows, cols), F32),
        compiler_params=_cparams("arbitrary"),
        name="ada",
    )(cvec, w, b.reshape(1, cols))


def _inproj_kernel(x_ref, sh_ref, sc_ref, wna_ref, wml_ref, wg_ref, gb_ref, zna_ref, zml_ref, g_ref):
    y = _ln_rows(x_ref[0]) * (1.0 + sc_ref[0]) + sh_ref[0]
    yb = y.astype(BF16)
    zna_ref[0] = jnp.dot(yb, wna_ref[...], preferred_element_type=F32).astype(BF16)
    zml_ref[0] = jnp.dot(yb, wml_ref[...], preferred_element_type=F32).astype(BF16)
    g_ref[0] = jnp.dot(yb, wg_ref[...], preferred_element_type=F32) + gb_ref[...]


def _inproj(x, shift, scale, wna, wml, wg, gb, tm):
    b, t, d = x.shape
    full = lambda a: pl.BlockSpec(a.shape, lambda i, j: (0,) * a.ndim)
    row = lambda w: pl.BlockSpec((1, tm, w), lambda i, j: (i, j, 0))
    vec = pl.BlockSpec((1, 1, d), lambda i, j: (i, 0, 0))
    return pl.pallas_call(
        _inproj_kernel,
        grid=(b, t // tm),
        in_specs=[row(d), vec, vec, full(wna), full(wml), full(wg), full(gb)],
        out_specs=[row(wna.shape[1]), row(wml.shape[1]), row(LANES)],
        out_shape=[jax.ShapeDtypeStruct((b, t, wna.shape[1]), BF16),
                   jax.ShapeDtypeStruct((b, t, wml.shape[1]), BF16),
                   jax.ShapeDtypeStruct((b, t, LANES), F32)],
        compiler_params=_cparams("parallel", "parallel"),
        name="inproj",
    )(x, shift, scale, wna, wml, wg, gb)


HALO = 16


def _qk_kernel(cur_ref, prev_ref, next_ref, cw_ref, cb_ref, cos_ref, sin_ref, o_ref, pad_ref, *, rope, tr):
    i = pl.program_id(1)
    last = pl.num_programs(1) - 1
    pad = CONV_K // 2
    pad_ref[HALO:HALO + tr, :] = cur_ref[0].astype(F32)
    pad_ref[0:HALO, :] = jnp.where(i > 0, prev_ref[0].astype(F32), 0.0)
    pad_ref[HALO + tr:2 * HALO + tr, :] = jnp.where(i < last, next_ref[0].astype(F32), 0.0)
    acc = cw_ref[0:1, :] * pad_ref[HALO - pad:HALO - pad + tr, :] + cb_ref[...]
    for j in range(1, CONV_K):
        acc = acc + cw_ref[j:j + 1, :] * pad_ref[HALO - pad + j:HALO - pad + j + tr, :]
    u = _silu(acc)
    width = u.shape[1]
    lane = lax.broadcasted_iota(jnp.int32, (tr, LANES), 1)
    first_half = (lane % (ML_HEAD_DIM // 2)) < (ML_HEAD_DIM // 4)
    kscale = ML_HEAD_DIM ** -0.5
    for g in range(width // LANES):
        ug = u[:, g * LANES:(g + 1) * LANES]
        if rope:
            partner = jnp.where(first_half, pltpu.roll(ug, LANES - ML_HEAD_DIM // 4, 1),
                                pltpu.roll(ug, ML_HEAD_DIM // 4, 1))
            ug = ug * cos_ref[...] + partner * sin_ref[...]
        if g >= ML_HEADS:
            ug = ug * kscale
        o_ref[0, :, g * LANES:(g + 1) * LANES] = ug.astype(BF16)


def _qk_streams(zml, conv_w, conv_b, cos_t, sin_t, rope, tr):
    b, t, _ = zml.shape
    width = 2 * ML_WIDTH
    nh = tr // HALO
    nblk = t // HALO
    return pl.pallas_call(
        functools.partial(_qk_kernel, rope=rope, tr=tr),
        grid=(b, t // tr),
        in_specs=[pl.BlockSpec((1, tr, width), lambda i, j: (i, j, 0)),
                  pl.BlockSpec((1, HALO, width), lambda i, j: (i, jnp.maximum(j * nh - 1, 0), 0)),
                  pl.BlockSpec((1, HALO, width), lambda i, j: (i, jnp.minimum((j + 1) * nh, nblk - 1), 0)),
                  pl.BlockSpec((CONV_K, width), lambda i, j: (0, 0)),
                  pl.BlockSpec((1, width), lambda i, j: (0, 0)),
                  pl.BlockSpec((tr, LANES), lambda i, j: (j, 0)),
                  pl.BlockSpec((tr, LANES), lambda i, j: (j, 0))],
        out_specs=pl.BlockSpec((1, tr, width), lambda i, j: (i, j, 0)),
        out_shape=jax.ShapeDtypeStruct((b, t, width), BF16),
        scratch_shapes=[pltpu.VMEM((tr + 2 * HALO, width), F32)],
        compiler_params=_cparams("parallel", "parallel"),
        name="qk_rope" if rope else "qk_ctx",
    )(zml, zml, zml, conv_w, conv_b.reshape(1, width), cos_t, sin_t)


def _rope_tables(n):
    pos = jnp.arange(n)
    half = ML_HEAD_DIM // 4
    inv = ROPE_BASE ** (-jnp.arange(half, dtype=F32) / half)

    def axis_tables(p):
        ang = p.astype(F32)[:, None] * inv[None, :]
        c, s = jnp.cos(ang), jnp.sin(ang)
        return jnp.concatenate([c, c], axis=-1), jnp.concatenate([-s, s], axis=-1)

    cr, sr = axis_tables(pos // GRID_W)
    cc, sc = axis_tables(pos % GRID_W)
    return jnp.concatenate([cr, cc], axis=-1), jnp.concatenate([sr, sc], axis=-1)


def _log_sigmoid(v):
    return jnp.minimum(v, 0.0) - jnp.log1p(jnp.exp(-jnp.abs(v)))


def _mlstm_direction(d, qk, v, g, cn_s, m_s, h_ref):
    L = ML_CHUNK
    row = lax.broadcasted_iota(jnp.int32, (L, L), 0)
    col = lax.broadcasted_iota(jnp.int32, (L, L), 1)
    fwd = d == 0
    keep = (row >= col) if fwd else (row <= col)
    tri = keep.astype(F32)
    lf = _log_sigmoid(g)
    bcol_all = jnp.dot(tri, lf, preferred_element_type=F32, precision=HIGHEST)
    brow_all = bcol_all.T
    g_t = g.T
    ones = jnp.ones((L, ML_HEAD_DIM), BF16)
    end = L - 1 if fwd else 0
    outs = []
    for h in range(ML_HEADS):
        ci = (2 * d) * ML_HEADS + h
        cf = (2 * d + 1) * ML_HEADS + h
        bc = jnp.broadcast_to(bcol_all[:, cf:cf + 1], (L, L))
        ic = jnp.broadcast_to(g[:, ci:ci + 1], (L, L))
        br = brow_all[cf:cf + 1, :]
        ir = g_t[ci:ci + 1, :]
        m_prev = m_s[d, h][0:1, :]
        dlog = jnp.where(keep, bc - br + ir, NEG)
        m_t = jnp.maximum(bc + m_prev, jnp.max(dlog, axis=1, keepdims=True))
        dw = jnp.exp(dlog - m_t)
        inter = jnp.exp(bc + m_prev - m_t)
        qh = qk[:, h * LANES:(h + 1) * LANES]
        kh = qk[:, ML_WIDTH + h * LANES:ML_WIDTH + (h + 1) * LANES]
        vp = jnp.concatenate([v[:, h * LANES:(h + 1) * LANES], ones], axis=1)
        s = lax.dot_general(qh, kh, (((1,), (1,)), ((), ())), preferred_element_type=F32) * dw
        cn = cn_s[d, h]
        a1 = jnp.dot(s.astype(BF16), vp, preferred_element_type=F32)
        a2 = jnp.dot(qh, cn.astype(BF16), preferred_element_type=F32)
        num = a1[:, :ML_HEAD_DIM] + inter * a2[:, :ML_HEAD_DIM]
        den = a1[:, ML_HEAD_DIM:] + inter * a2[:, ML_HEAD_DIM:]
        outs.append(num / jnp.maximum(jnp.abs(den), jnp.exp(-m_t)))
        b_end = jnp.broadcast_to(br[:, end:end + 1], (1, L))
        g_row = b_end - br + ir
        m_new = jnp.maximum(b_end + m_prev, jnp.max(g_row, axis=1, keepdims=True))
        decay = jnp.exp(b_end + m_prev - m_new)
        wgt = jnp.exp(b_end - bc + ic - m_new)
        kw = (kh.astype(F32) * wgt).astype(BF16)
        upd = lax.dot_general(kw, vp, (((0,), (0,)), ((), ())), preferred_element_type=F32)
        cn_s[d, h] = jnp.concatenate([decay, decay], axis=1) * cn + upd
        m_s[d, h] = jnp.broadcast_to(m_new, m_s.shape[2:])
    if h_ref is not None:
        h_ref[0] = jnp.concatenate(outs, axis=1).astype(h_ref.dtype)


def _mlstm_kernel(qkf_ref, vf_ref, gf_ref, qkb_ref, vb_ref, gb_ref, c0_ref, m0_ref, *rest, emit_h):
    if emit_h:
        hf_ref, hb_ref, ct_ref, mt_ref, cn_s, m_s = rest
    else:
        ct_ref, mt_ref, cn_s, m_s = rest
        hf_ref = hb_ref = None
    c = pl.program_id(1)

    @pl.when(c == 0)
    def _():
        cn_s[...] = c0_ref[0]
        m_s[...] = m0_ref[0]

    _mlstm_direction(0, qkf_ref[0], vf_ref[0], gf_ref[0], cn_s, m_s, hf_ref)
    _mlstm_direction(1, qkb_ref[0], vb_ref[0], gb_ref[0], cn_s, m_s, hb_ref)

    @pl.when(c == pl.num_programs(1) - 1)
    def _():
        ct_ref[0] = cn_s[...]
        mt_ref[0] = m_s[...]


def _mlstm(qk, zml, gates, c0, m0, emit_h):
    b, t, _ = qk.shape
    L = ML_CHUNK
    nc = t // L
    vcol = 2
    f_idx = lambda i, c: (i, c, 0)
    b_idx = lambda i, c: (i, nc - 1 - c, 0)
    st_c = pl.BlockSpec((1, 2, ML_HEADS, ML_HEAD_DIM, 2 * ML_HEAD_DIM), lambda i, c: (i, 0, 0, 0, 0))
    st_m = pl.BlockSpec((1, 2, ML_HEADS, 8, LANES), lambda i, c: (i, 0, 0, 0, 0))
    out_specs = [st_c, st_m]
    out_shape = [jax.ShapeDtypeStruct(c0.shape, F32), jax.ShapeDtypeStruct(m0.shape, F32)]
    if emit_h:
        out_specs = [pl.BlockSpec((1, L, ML_WIDTH), f_idx), pl.BlockSpec((1, L, ML_WIDTH), b_idx)] + out_specs
        out_shape = [jax.ShapeDtypeStruct((b, t, ML_WIDTH), BF16)] * 2 + out_shape
    return pl.pallas_call(
        functools.partial(_mlstm_kernel, emit_h=emit_h),
        grid=(b, nc),
        in_specs=[pl.BlockSpec((1, L, 2 * ML_WIDTH), f_idx),
                  pl.BlockSpec((1, L, ML_WIDTH), lambda i, c: (i, c, vcol)),
                  pl.BlockSpec((1, L, LANES), f_idx),
                  pl.BlockSpec((1, L, 2 * ML_WIDTH), b_idx),
                  pl.BlockSpec((1, L, ML_WIDTH), lambda i, c: (i, nc - 1 - c, vcol)),
                  pl.BlockSpec((1, L, LANES), b_idx),
                  st_c, st_m],
        out_specs=out_specs,
        out_shape=out_shape,
        scratch_shapes=[pltpu.VMEM((2, ML_HEADS, ML_HEAD_DIM, 2 * ML_HEAD_DIM), F32),
                        pltpu.VMEM((2, ML_HEADS, 8, LANES), F32)],
        compiler_params=_cparams("parallel", "arbitrary"),
        name="mlstm" if emit_h else "mlstm_ctx",
    )(qk, zml, gates, qk, zml, gates, c0, m0)


def _lane_in(shape, start, width):
    lane = lax.broadcasted_iota(jnp.int32, shape, 1)
    return (lane >= start) & (lane < start + width)


def _na_kernel(q_ref, k0_ref, k1_ref, k2_ref, v0_ref, v1_ref, v2_ref, kc_ref, vc_ref, bias_ref, o_ref):
    nq = q_ref.shape[1]
    lane = lax.broadcasted_iota(jnp.int32, (nq, LANES), 1)
    nt = (((1,), (1,)), ((), ()))
    for p in range(NA_HEADS // 2):
        sl = slice(p * LANES, (p + 1) * LANES)
        q2 = q_ref[0, :, sl] * (NA_HEAD_DIM ** -0.5)
        kwin = jnp.concatenate([k0_ref[0, :, sl], k1_ref[0, :, sl], k2_ref[0, :, sl]], axis=0)
        vwin = jnp.concatenate([v0_ref[0, :, sl], v1_ref[0, :, sl], v2_ref[0, :, sl]], axis=0)
        kc = kc_ref[0, :, sl]
        vc = vc_ref[0, :, sl]
        halves = []
        for a in range(2):
            in_head = (lane >= a * NA_HEAD_DIM) & (lane < (a + 1) * NA_HEAD_DIM)
            qm = jnp.where(in_head, q2, jnp.zeros_like(q2))
            s_win = lax.dot_general(qm, kwin, nt, preferred_element_type=F32) + bias_ref[0, 2 * p + a]
            s_ctx = lax.dot_general(qm, kc, nt, preferred_element_type=F32)
            m = jnp.maximum(jnp.max(s_win, axis=1, keepdims=True), jnp.max(s_ctx, axis=1, keepdims=True))
            p_win = jnp.exp(s_win - m).astype(BF16)
            p_ctx = jnp.exp(s_ctx - m).astype(BF16)
            vw = jnp.where(_lane_in(vwin.shape, a * NA_HEAD_DIM, NA_HEAD_DIM), vwin, jnp.ones_like(vwin))
            vx = jnp.where(_lane_in(vc.shape, a * NA_HEAD_DIM, NA_HEAD_DIM), vc, jnp.ones_like(vc))
            o = jnp.dot(p_win, vw, preferred_element_type=F32) + jnp.dot(p_ctx, vx, preferred_element_type=F32)
            halves.append(o / pltpu.roll(o, NA_HEAD_DIM, 1))
        o_ref[0, :, sl] = jnp.where(lane < NA_HEAD_DIM, halves[0], halves[1]).astype(o_ref.dtype)


def _na_bias_tables(rpb, rows):
    R = NA_ROWS_PER_STEP
    nblk = rows // R
    kr = NA_WIN_ROWS
    cq = np.arange(GRID_W)
    cstart = np.clip(cq - NA_WIN_COLS // 2, 0, GRID_W - NA_WIN_COLS)
    ck = np.arange(GRID_W)
    col_ok = (ck[None, :] >= cstart[:, None]) & (ck[None, :] < cstart[:, None] + NA_WIN_COLS)
    col_off = np.where(col_ok, ck[None, :] - cq[:, None] + NA_WIN_COLS - 1, 0)
    row_ok = np.zeros((3, R, 3 * R), bool)
    row_off = np.zeros((3, R, 3 * R), np.int64)
    for vi, j in enumerate((0, 1, nblk - 1)):
        for i in range(R):
            r = j * R + i
            r0 = min(max(r - kr // 2, 0), rows - kr)
            for t in range(3):
                jb = j - 1 + t
                if jb < 0 or jb >= nblk:
                    continue
                for rr in range(R):
                    krow = jb * R + rr
                    if r0 <= krow < r0 + kr:
                        row_ok[vi, i, t * R + rr] = True
                        row_off[vi, i, t * R + rr] = krow - r + NA_WIN_ROWS - 1
    col_sel = (np.arange(rpb.shape[2])[None, None, :] == col_off[:, :, None]) & col_ok[:, :, None]
    row_sel = (np.arange(rpb.shape[1])[None, None, None, :] == row_off[..., None]) & row_ok[..., None]
    by_col = jnp.einsum('hrc,qkc->hrqk', rpb, jnp.asarray(col_sel, F32), precision=HIGHEST)
    vals = jnp.einsum('vixr,hrqk->vhiqxk', jnp.asarray(row_sel, F32), by_col, precision=HIGHEST)
    ok = row_ok[:, None, :, None, :, None] & col_ok[None, None, None, :, None, :]
    vals = jnp.where(jnp.asarray(ok), vals, NEG)
    return vals.reshape(3, NA_HEADS, R * GRID_W, 3 * R * GRID_W)


def _na(zna, zcna, bias, rows):
    b, n, _ = zna.shape
    ctx = zcna.shape[1]
    R = NA_ROWS_PER_STEP
    nq = R * GRID_W
    nblk = rows // R
    kb = lambda col, off: pl.BlockSpec(
        (1, nq, NA_WIDTH), lambda i, j: (i, jnp.clip(j + off, 0, nblk - 1), col))
    variant = lambda i, j: (jnp.where(j == 0, 0, jnp.where(j == nblk - 1, 2, 1)), 0, 0, 0)
    return pl.pallas_call(
        _na_kernel,
        grid=(b, nblk),
        in_specs=[pl.BlockSpec((1, nq, NA_WIDTH), lambda i, j: (i, j, 0)),
                  kb(1, -1), kb(1, 0), kb(1, 1), kb(2, -1), kb(2, 0), kb(2, 1),
                  pl.BlockSpec((1, ctx, NA_WIDTH), lambda i, j: (i, 0, 1)),
                  pl.BlockSpec((1, ctx, NA_WIDTH), lambda i, j: (i, 0, 2)),
                  pl.BlockSpec((1, NA_HEADS, nq, 3 * nq), variant)],
        out_specs=pl.BlockSpec((1, nq, NA_WIDTH), lambda i, j: (i, j, 0)),
        out_shape=jax.ShapeDtypeStruct((b, n, NA_WIDTH), BF16),
        compiler_params=_cparams("parallel", "parallel"),
        name="na",
    )(zna, zna, zna, zna, zna, zna, zna, zcna, zcna, bias)


def _mix_kernel(na_ref, hf_ref, hb_ref, o_ref, x_ref, g1_ref, sh2_ref, sc2_ref, wo_ref, mg_ref, l1g_ref, l1b_ref,
                wr_ref, br_ref, xmid_ref, rt_ref, rw_ref, cnt_ref, carry_s, *, alpha):
    first = (pl.program_id(0) == 0) & (pl.program_id(1) == 0)

    @pl.when(first)
    def _():
        carry_s[...] = jnp.zeros_like(carry_s)

    tm = x_ref.shape[1]
    h = hf_ref[0].astype(F32) + hb_ref[0].astype(F32)
    parts = []
    for hd in range(ML_HEADS):
        hh = h[:, hd * LANES:(hd + 1) * LANES]
        parts.append(hh * lax.rsqrt(jnp.mean(hh * hh, axis=-1, keepdims=True) + LN_EPS))
    hn = jnp.concatenate(parts, axis=1)
    ml = (hn * mg_ref[...] * jax.nn.sigmoid(o_ref[0].astype(F32))).astype(BF16)
    mix = (jnp.dot(na_ref[0], wo_ref[0:NA_WIDTH, :], preferred_element_type=F32)
           + jnp.dot(ml, wo_ref[NA_WIDTH:, :], preferred_element_type=F32))
    xmid = _ln_rows(alpha * x_ref[0] + g1_ref[0] * mix) * l1g_ref[...] + l1b_ref[...]
    xmid_ref[0] = xmid

    xt = (_ln_rows(xmid) * (1.0 + sc2_ref[0]) + sh2_ref[0]).astype(BF16)
    logits = jnp.dot(xt, wr_ref[...], preferred_element_type=F32) + br_ref[...]
    lane = lax.broadcasted_iota(jnp.int32, (tm, LANES), 1)
    is_g = lane < N_GROUPS
    gl = jnp.where(is_g, logits, NEG)
    gmax = jnp.max(gl, axis=1, keepdims=True)
    grp = jnp.min(jnp.where(gl == gmax, lane, LANES), axis=1, keepdims=True)
    gsum = jnp.sum(jnp.where(is_g, jnp.exp(gl - gmax), 0.0), axis=1, keepdims=True)
    grp_w = 1.0 / gsum
    lo = N_GROUPS + EXPERTS_PER_GROUP * grp
    el = jnp.where((lane >= lo) & (lane < lo + EXPERTS_PER_GROUP), logits, NEG)
    t1 = jnp.max(el, axis=1, keepdims=True)
    i1 = jnp.min(jnp.where(el == t1, lane, LANES), axis=1, keepdims=True)
    el2 = jnp.where(lane == i1, NEG, el)
    t2 = jnp.max(el2, axis=1, keepdims=True)
    i2 = jnp.min(jnp.where(el2 == t2, lane, LANES), axis=1, keepdims=True)
    e21 = jnp.exp(t2 - t1)
    w0 = grp_w / (1.0 + e21)
    w1 = grp_w * e21 / (1.0 + e21)

    hit1 = lane == i1
    hit2 = lane == i2
    onehot = (hit1 | hit2).astype(BF16)
    r_i = lax.broadcasted_iota(jnp.int32, (tm, tm), 0)
    c_i = lax.broadcasted_iota(jnp.int32, (tm, tm), 1)
    before = (r_i > c_i).astype(BF16)
    prefix = jnp.dot(before, onehot, preferred_element_type=F32) + carry_s[0:1, :]
    rank0 = jnp.sum(jnp.where(hit1, prefix, 0.0), axis=1, keepdims=True)
    rank1 = jnp.sum(jnp.where(hit2, prefix, 0.0), axis=1, keepdims=True)
    total = carry_s[0:1, :] + jnp.sum(onehot.astype(F32), axis=0, keepdims=True)
    carry_s[...] = jnp.broadcast_to(total, carry_s.shape)
    cnt_ref[...] = jnp.broadcast_to(total, cnt_ref.shape)

    rf = jnp.where(lane == 0, (i1 - N_GROUPS).astype(F32),
                   jnp.where(lane == 1, (i2 - N_GROUPS).astype(F32),
                             jnp.where(lane == 2, rank0, jnp.where(lane == 3, rank1, 0.0))))
    rt_ref[0] = rf.T[0:SUBLANES, :].astype(jnp.int32)
    rw_ref[0] = jnp.where(lane == 0, w0, jnp.where(lane == 1, w1, 0.0))


def _mix(na, hf, hb, zml, x, g1, sh2, sc2, wo, mg, l1g, l1b, wr, br, alpha, tm):
    b, n, d = x.shape
    row = lambda w: pl.BlockSpec((1, tm, w), lambda i, j: (i, j, 0))
    vec = pl.BlockSpec((1, 1, d), lambda i, j: (i, 0, 0))
    full = lambda a: pl.BlockSpec(a.shape, lambda i, j: (0,) * a.ndim)
    ocol = 3
    return pl.pallas_call(
        functools.partial(_mix_kernel, alpha=alpha),
        grid=(b, n // tm),
        in_specs=[row(NA_WIDTH), row(ML_WIDTH), row(ML_WIDTH),
                  pl.BlockSpec((1, tm, ML_WIDTH), lambda i, j: (i, j, ocol)),
                  row(d), vec, vec, vec, full(wo), full(mg), full(l1g), full(l1b), full(wr), full(br)],
        out_specs=[row(d), pl.BlockSpec((1, SUBLANES, tm), lambda i, j: (i * (n // tm) + j, 0, 0)), row(LANES),
                   pl.BlockSpec((SUBLANES, LANES), lambda i, j: (0, 0))],
        out_shape=[jax.ShapeDtypeStruct((b, n, d), F32),
                   jax.ShapeDtypeStruct((b * (n // tm), SUBLANES, tm), jnp.int32),
                   jax.ShapeDtypeStruct((b, n, LANES), F32),
                   jax.ShapeDtypeStruct((SUBLANES, LANES), F32)],
        scratch_shapes=[pltpu.VMEM((8, LANES), F32)],
        compiler_params=_cparams("arbitrary", "arbitrary"),
        name="mix",
    )(na, hf, hb, zml, x, g1, sh2, sc2, wo, mg, l1g, l1b, wr, br)


def _zero_fill_padding(pad_base_ref, pad_len_ref, nused_ref, xs_ref, zero_s, sem, wait):
    tb = zero_s.shape[0] // SUBLANES

    def run(copy):
        copy.wait() if wait else copy.start()

    def fill(off, nrows):
        run(pltpu.make_async_copy(zero_s.at[pl.ds(0, nrows * SUBLANES)], _token_rows(xs_ref, off, nrows), sem))

    def per_expert(e, _):
        plen = pad_len_ref[e]
        base = pad_base_ref[e]
        bit = tb // 2
        while bit >= 1:
            off = base + (plen & ~(2 * bit - 1))

            @pl.when((plen & bit) != 0)
            def _(bit=bit, off=off):
                fill(off, bit)

            bit //= 2
        return 0

    lax.fori_loop(0, N_EXPERTS, per_expert, 0)

    def per_block(i, _):
        fill(i * tb, tb)
        return 0

    lax.fori_loop(nused_ref[0], xs_ref.shape[0] // (tb * SUBLANES), per_block, 0)


def _token_rows(ref, tok, n):
    return ref.at[pl.ds(pl.multiple_of(tok * SUBLANES, SUBLANES), n * SUBLANES)]


def _to_token_tiles(dst_ref, base, val):
    tm = val.shape[0]
    for s in range(val.shape[1] // LANES):
        dst_ref[pl.ds(base + s, tm, stride=SUBLANES), :] = val[:, s * LANES:(s + 1) * LANES]


def _from_token_tiles(src_ref, base, tm):
    return jnp.concatenate([src_ref[pl.ds(base + s, tm, stride=SUBLANES), :] for s in range(SUBLANES)], axis=1)


def _dispatch_kernel(pad_base_ref, pad_len_ref, nused_ref, dest_ref, xmid_ref, sh2_ref, sc2_ref, xs_ref,
                     xt_s, zero_s, sem, zsem):
    tm = xmid_ref.shape[1]
    step = pl.program_id(0) * pl.num_programs(1) + pl.program_id(1)
    nsteps = pl.num_programs(0) * pl.num_programs(1)

    def wait_step_copies():
        for _ in range(2):
            pltpu.make_async_copy(xt_s, _token_rows(xs_ref, 0, tm), sem).wait()

    @pl.when(step == 0)
    def _():
        zero_s[...] = jnp.zeros_like(zero_s)
        _zero_fill_padding(pad_base_ref, pad_len_ref, nused_ref, xs_ref, zero_s, zsem, False)

    xt = _ln_rows(xmid_ref[0]) * (1.0 + sc2_ref[0]) + sh2_ref[0]

    @pl.when(step > 0)
    def _():
        wait_step_copies()

    _to_token_tiles(xt_s, 0, xt)

    def start(r, _):
        for k in range(2):
            pltpu.make_async_copy(_token_rows(xt_s, r, 1), _token_rows(xs_ref, dest_ref[0, k, r], 1), sem).start()
        return 0

    lax.fori_loop(0, tm, start, 0, unroll=8)

    @pl.when(step == 0)
    def _():
        _zero_fill_padding(pad_base_ref, pad_len_ref, nused_ref, xs_ref, zero_s, zsem, True)

    @pl.when(step == nsteps - 1)
    def _():
        wait_step_copies()


def _dispatch(xmid, sh2, sc2, dest, pad_base, pad_len, nused, cap, tm):
    b, n, d = xmid.shape
    assert d == SUBLANES * LANES
    nt = n // tm
    vec = pl.BlockSpec((1, 1, d), lambda i, j, *_: (i, 0, 0))
    return pl.pallas_call(
        _dispatch_kernel,
        grid_spec=pltpu.PrefetchScalarGridSpec(
            num_scalar_prefetch=3,
            grid=(b, nt),
            in_specs=[pl.BlockSpec((1, 2, tm), lambda i, j, *_: (i * nt + j, 0, 0), memory_space=pltpu.SMEM),
                      pl.BlockSpec((1, tm, d), lambda i, j, *_: (i, j, 0)), vec, vec],
            out_specs=pl.BlockSpec(memory_space=pl.ANY),
            scratch_shapes=[pltpu.VMEM((tm * SUBLANES, LANES), F32), pltpu.VMEM((EXPERT_ROWS * SUBLANES, LANES), F32),
                            pltpu.SemaphoreType.DMA, pltpu.SemaphoreType.DMA]),
        out_shape=jax.ShapeDtypeStruct((cap * SUBLANES, LANES), F32),
        compiler_params=_cparams("arbitrary", "arbitrary"),
        name="dispatch",
    )(pad_base, pad_len, nused, dest, xmid, sh2, sc2)


def _expert_kernel(be_ref, nv_ref, xs_ref, w1_ref, w3_ref, w2_ref, ys_ref, w1b, w3b, w2b, xb_s):
    tb = xb_s.shape[0]
    i = pl.program_id(0)
    e = be_ref[i]
    changed = (i == 0) | (be_ref[jnp.maximum(i - 1, 0)] != e)

    @pl.when(changed)
    def _():
        w1b[...] = w1_ref[0].astype(BF16)
        w3b[...] = w3_ref[0].astype(BF16)
        w2b[...] = w2_ref[0].astype(BF16)

    nv = nv_ref[i]

    @pl.when(nv > 0)
    def _():
        for s in range(SUBLANES):
            xb_s[:, s * LANES:(s + 1) * LANES] = xs_ref[pl.ds(s, tb, stride=SUBLANES), :].astype(BF16)
        xb = xb_s[...]
        h1 = jnp.dot(xb, w1b[...], preferred_element_type=F32)
        h3 = jnp.dot(xb, w3b[...], preferred_element_type=F32)
        a = (_silu(h1) * h3).astype(BF16)
        _to_token_tiles(ys_ref, 0, jnp.dot(a, w2b[...], preferred_element_type=F32))

    @pl.when(nv == 0)
    def _():
        ys_ref[...] = jnp.zeros_like(ys_ref)


def _experts(xs, block_e, block_nv, w1, w3, w2):
    d, hid = w1.shape[1], w1.shape[2]
    tb = EXPERT_ROWS
    tile_rows = tb * SUBLANES
    return pl.pallas_call(
        _expert_kernel,
        grid_spec=pltpu.PrefetchScalarGridSpec(
            num_scalar_prefetch=2,
            grid=(xs.shape[0] // tile_rows,),
            in_specs=[pl.BlockSpec((tile_rows, LANES), lambda i, be, nv: (i, 0)),
                      pl.BlockSpec((1, d, hid), lambda i, be, nv: (be[i], 0, 0)),
                      pl.BlockSpec((1, d, hid), lambda i, be, nv: (be[i], 0, 0)),
                      pl.BlockSpec((1, hid, d), lambda i, be, nv: (be[i], 0, 0))],
            out_specs=pl.BlockSpec((tile_rows, LANES), lambda i, be, nv: (i, 0)),
            scratch_shapes=[pltpu.VMEM((d, hid), BF16), pltpu.VMEM((d, hid), BF16), pltpu.VMEM((hid, d), BF16),
                            pltpu.VMEM((tb, d), BF16)]),
        out_shape=jax.ShapeDtypeStruct(xs.shape, F32),
        compiler_params=_cparams("arbitrary"),
        name="experts",
    )(block_e, block_nv, xs, w1, w3, w2)


def _combine_kernel(dcur_ref, dnext_ref, xmid_ref, rw_ref, g2_ref, l2g_ref, l2b_ref, ys_ref, o_ref,
                    y0_s, y1_s, sem, *, alpha):
    tm = xmid_ref.shape[1]
    step = pl.program_id(0) * pl.num_programs(1) + pl.program_id(1)
    nsteps = pl.num_programs(0) * pl.num_programs(1)
    slot = step % 2

    def gather(dest_ref, into):
        def start(r, _):
            for k, buf in ((0, y0_s), (1, y1_s)):
                pltpu.make_async_copy(_token_rows(ys_ref, dest_ref[0, k, r], 1),
                                      _token_rows(buf, into * tm + r, 1), sem.at[into]).start()
            return 0

        lax.fori_loop(0, tm, start, 0, unroll=8)

    @pl.when(step == 0)
    def _():
        gather(dcur_ref, 0)

    @pl.when(step + 1 < nsteps)
    def _():
        gather(dnext_ref, 1 - slot)

    for buf in (y0_s, y1_s):
        pltpu.make_async_copy(_token_rows(ys_ref, 0, tm), _token_rows(buf, slot * tm, tm), sem.at[slot]).wait()
    base = pl.multiple_of(slot * tm * SUBLANES, SUBLANES)
    rw = rw_ref[0]
    moe = rw[:, 0:1] * _from_token_tiles(y0_s, base, tm) + rw[:, 1:2] * _from_token_tiles(y1_s, base, tm)
    o_ref[0] = _ln_rows(alpha * xmid_ref[0] + g2_ref[0] * moe) * l2g_ref[...] + l2b_ref[...]


def _combine(xmid, rw, g2, l2g, l2b, ys, dest, alpha, tm):
    b, n, d = xmid.shape
    nt = n // tm
    full = lambda a: pl.BlockSpec(a.shape, lambda i, j: (0,) * a.ndim)
    return pl.pallas_call(
        functools.partial(_combine_kernel, alpha=alpha),
        grid=(b, nt),
        in_specs=[pl.BlockSpec((1, 2, tm), lambda i, j: (i * nt + j, 0, 0), memory_space=pltpu.SMEM),
                  pl.BlockSpec((1, 2, tm), lambda i, j: (jnp.minimum(i * nt + j + 1, b * nt - 1), 0, 0),
                               memory_space=pltpu.SMEM),
                  pl.BlockSpec((1, tm, d), lambda i, j: (i, j, 0)),
                  pl.BlockSpec((1, tm, LANES), lambda i, j: (i, j, 0)),
                  pl.BlockSpec((1, 1, d), lambda i, j: (i, 0, 0)),
                  full(l2g), full(l2b),
                  pl.BlockSpec(memory_space=pl.ANY)],
        out_specs=pl.BlockSpec((1, tm, d), lambda i, j: (i, j, 0)),
        out_shape=jax.ShapeDtypeStruct((b, n, d), F32),
        scratch_shapes=[pltpu.VMEM((2 * tm * SUBLANES, LANES), F32), pltpu.VMEM((2 * tm * SUBLANES, LANES), F32),
                        pltpu.SemaphoreType.DMA((2,))],
        compiler_params=_cparams("arbitrary", "arbitrary"),
        name="combine",
    )(dest, dest, xmid, rw, g2, l2g, l2b, ys)


def _tile(n, want):
    t = min(n, want)
    assert n % t == 0, (n, t)
    return t


def kernel(x, c, ctx, c_ctx, w_ada, b_ada, w_in, conv_w, conv_b, gate_b, rpb, ml_norm_g, w_out, ln1_g, ln1_b,
           w_router_g, b_router_g, w_router_e, b_router_e, w1, w3, w2, ln2_g, ln2_b):
    B, N, D = x.shape
    T_CTX = ctx.shape[1]
    depth = w_ada.shape[0]
    rows = N // GRID_W
    assert depth == 1 and N % GRID_W == 0 and rows % NA_ROWS_PER_STEP == 0 and rows >= 3 * NA_ROWS_PER_STEP
    assert N % ML_CHUNK == 0 and T_CTX % ML_CHUNK == 0
    alpha = (2.0 * depth) ** 0.25
    l = 0

    pad_rows = -(B + 1) % 8
    cvec = jnp.concatenate([c, c_ctx[None], jnp.zeros((pad_rows, D), F32)], axis=0)
    ada = _ada(cvec, w_ada[l], b_ada[l])
    sh1, sc1, g1, sh2, sc2, g2 = [a[:, None, :] for a in jnp.split(ada[:B], 6, axis=-1)]
    csh1, csc1 = [jnp.broadcast_to(a[None], (B, 1, D)) for a in jnp.split(ada[B:B + 1], 6, axis=-1)[:2]]

    col_ml = 3 * NA_WIDTH
    col_g = col_ml + 4 * ML_WIDTH
    wb = w_in[l].astype(BF16)
    wna, wml = wb[:, :col_ml], wb[:, col_ml:col_g]
    n_gate = 4 * ML_HEADS
    wg = jnp.pad(wb[:, col_g:], ((0, 0), (0, LANES - n_gate)))
    gb = jnp.pad(gate_b[l], (0, LANES - n_gate)).reshape(1, LANES)
    zna, zml, gates = _inproj(x, sh1, sc1, wna, wml, wg, gb, _tile(N, 512))
    zcna, zcml, gates_c = _inproj(ctx, csh1, csc1, wna, wml, wg, gb, _tile(T_CTX, 256))

    cos_t, sin_t = _rope_tables(N)
    qk_c = _qk_streams(zcml, conv_w[l], conv_b[l], cos_t[:T_CTX], sin_t[:T_CTX], False, _tile(T_CTX, 512))
    qk_l = _qk_streams(zml, conv_w[l], conv_b[l], cos_t, sin_t, True, _tile(N, 512))
    c0 = jnp.zeros((B, 2, ML_HEADS, ML_HEAD_DIM, 2 * ML_HEAD_DIM), F32)
    m0 = jnp.zeros((B, 2, ML_HEADS, 8, LANES), F32)
    c_ctx_end, m_ctx_end = _mlstm(qk_c, zcml, gates_c, c0, m0, False)
    hf, hb, _, _ = _mlstm(qk_l, zml, gates, c_ctx_end, m_ctx_end, True)

    na = _na(zna, zcna, _na_bias_tables(rpb[l], rows), rows)

    wr = jnp.pad(jnp.concatenate([w_router_g[l], w_router_e[l]], axis=1),
                 ((0, 0), (0, LANES - N_GROUPS - N_EXPERTS))).astype(BF16)
    br = jnp.pad(jnp.concatenate([b_router_g[l], b_router_e[l]]), (0, LANES - N_GROUPS - N_EXPERTS)).reshape(1, LANES)
    tm = _tile(N, 256)
    xmid, rt, rw, counts = _mix(na, hf, hb, zml, x, g1, sh2, sc2, w_out[l].astype(BF16),
                                ml_norm_g[l].reshape(1, ML_WIDTH), ln1_g[l].reshape(1, D), ln1_b[l].reshape(1, D),
                                wr, br, alpha, tm)

    tb = EXPERT_ROWS
    n_assign = 2 * B * N
    cap = -(-n_assign // tb) * tb + N_EXPERTS * tb
    sizes = counts[0, N_GROUPS:N_GROUPS + N_EXPERTS].astype(jnp.int32)
    padded = (sizes + tb - 1) // tb * tb
    pend = jnp.cumsum(padded)
    pstart = pend - padded
    experts = jnp.arange(N_EXPERTS, dtype=jnp.int32)
    first_row = jnp.sum(jnp.where(rt[:, 0:2, :, None] == experts, pstart, 0), axis=-1)
    dest = first_row + rt[:, 2:4, :]
    blk0 = jnp.arange(cap // tb, dtype=jnp.int32) * tb
    block_e = jnp.minimum(jnp.sum(pend[None, :] <= blk0[:, None], axis=1), N_EXPERTS - 1).astype(jnp.int32)
    is_e = block_e[:, None] == experts
    block_nv = jnp.clip(jnp.sum(jnp.where(is_e, pstart + sizes, 0), axis=1) - blk0, 0, tb).astype(jnp.int32)

    nused = (pend[-1:] // tb).astype(jnp.int32)
    xs = _dispatch(xmid, sh2, sc2, dest, pstart + sizes, padded - sizes, nused, cap, tm)
    ys = _experts(xs, block_e, block_nv, w1[l], w3[l], w2[l])
    return _combine(xmid, rw, g2, ln2_g[l].reshape(1, D), ln2_b[l].reshape(1, D), ys, dest, alpha, tm)
```

```python
import functools

import numpy as np
import jax
import jax.numpy as jnp
from jax import lax
from jax.experimental import pallas as pl
from jax.experimental.pallas import tpu as pltpu

F32 = jnp.float32
BF16 = jnp.bfloat16
HIGHEST = lax.Precision.HIGHEST

GRID_W = 64
NA_HEADS = 8
NA_HEAD_DIM = 64
NA_WIDTH = NA_HEADS * NA_HEAD_DIM
NA_WIN_ROWS = 8
NA_WIN_COLS = 16
ML_HEADS = 4
ML_HEAD_DIM = 128
ML_WIDTH = ML_HEADS * ML_HEAD_DIM
ML_CHUNK = 128
CONV_K = 5
N_GROUPS = 8
EXPERTS_PER_GROUP = 8
N_EXPERTS = N_GROUPS * EXPERTS_PER_GROUP
ROPE_BASE = 10000.0
LN_EPS = 1e-5

LANES = 128
SUBLANES = 8
VMEM_LIMIT = 56 * 1024 * 1024

NA_ROWS_PER_STEP = 4
EXPERT_ROWS = 512
MLSTM_CHUNKS_PER_STEP = 4
NEG = -1e30


def _cparams(*sem):
    return pltpu.CompilerParams(dimension_semantics=sem, vmem_limit_bytes=VMEM_LIMIT)


def _silu(v):
    return v * jax.nn.sigmoid(v)


def _ln_rows(v):
    mu = jnp.mean(v, axis=-1, keepdims=True)
    vc = v - mu
    var = jnp.mean(vc * vc, axis=-1, keepdims=True)
    return vc * lax.rsqrt(var + LN_EPS)


def _ada_kernel(c_ref, w_ref, b_ref, o_ref):
    o_ref[...] = jnp.dot(_silu(c_ref[...]), w_ref[...], preferred_element_type=F32,
                         precision=HIGHEST) + b_ref[...]


def _ada(cvec, w, b):
    rows, d = cvec.shape
    cols = w.shape[1]
    tn = 1024
    return pl.pallas_call(
        _ada_kernel,
        grid=(cols // tn,),
        in_specs=[pl.BlockSpec((rows, d), lambda j: (0, 0)),
                  pl.BlockSpec((d, tn), lambda j: (0, j)),
                  pl.BlockSpec((1, tn), lambda j: (0, j))],
        out_specs=pl.BlockSpec((rows, tn), lambda j: (0, j)),
        out_shape=jax.ShapeDtypeStruct((rows, cols), F32),
        compiler_params=_cparams("arbitrary"),
        name="ada",
    )(cvec, w, b.reshape(1, cols))


def _inproj_kernel(x_ref, sh_ref, sc_ref, wna_ref, wml_ref, wg_ref, gb_ref, zna_ref, zml_ref, g_ref):
    y = _ln_rows(x_ref[0]) * (1.0 + sc_ref[0]) + sh_ref[0]
    yb = y.astype(BF16)
    zna_ref[0] = jnp.dot(yb, wna_ref[...], preferred_element_type=F32).astype(BF16)
    zml_ref[0] = jnp.dot(yb, wml_ref[...], preferred_element_type=F32).astype(BF16)
    g_ref[0] = jnp.dot(yb, wg_ref[...], preferred_element_type=F32) + gb_ref[...]


def _inproj(x, shift, scale, wna, wml, wg, gb, tm):
    b, t, d = x.shape
    full = lambda a: pl.BlockSpec(a.shape, lambda i, j: (0,) * a.ndim)
    row = lambda w: pl.BlockSpec((1, tm, w), lambda i, j: (i, j, 0))
    vec = pl.BlockSpec((1, 1, d), lambda i, j: (i, 0, 0))
    return pl.pallas_call(
        _inproj_kernel,
        grid=(b, t // tm),
        in_specs=[row(d), vec, vec, full(wna), full(wml), full(wg), full(gb)],
        out_specs=[row(wna.shape[1]), row(wml.shape[1]), row(LANES)],
        out_shape=[jax.ShapeDtypeStruct((b, t, wna.shape[1]), BF16),
                   jax.ShapeDtypeStruct((b, t, wml.shape[1]), BF16),
                   jax.ShapeDtypeStruct((b, t, LANES), F32)],
        compiler_params=_cparams("parallel", "parallel"),
        name="inproj",
    )(x, shift, scale, wna, wml, wg, gb)


HALO = 16


def _log_sigmoid(v):
    return jnp.minimum(v, 0.0) - jnp.log1p(jnp.exp(-jnp.abs(v)))


GATE_RAW, GATE_PREFIX, GATE_SUFFIX = 0, 16, 32


def _pack_gates(g):
    L = g.shape[0]
    row = lax.broadcasted_iota(jnp.int32, (L, L), 0)
    col = lax.broadcasted_iota(jnp.int32, (L, L), 1)
    lane = lax.broadcasted_iota(jnp.int32, g.shape, 1)
    lf = _log_sigmoid(g)
    prefix = jnp.dot((row >= col).astype(F32), lf, preferred_element_type=F32, precision=HIGHEST)
    suffix = jnp.dot((row <= col).astype(F32), lf, preferred_element_type=F32, precision=HIGHEST)
    return jnp.where(lane < GATE_PREFIX, g,
                     jnp.where(lane < GATE_SUFFIX, pltpu.roll(prefix, GATE_PREFIX, 1),
                               jnp.where(lane < GATE_SUFFIX + GATE_PREFIX, pltpu.roll(suffix, GATE_SUFFIX, 1), 0.0)))


def _qk_kernel(cur_ref, prev_ref, next_ref, cw_ref, cb_ref, cos_ref, sin_ref, g_ref, o_ref, gcol_ref, grow_ref,
               pad_ref, *, rope, tr):
    for c in range(tr // ML_CHUNK):
        packed = _pack_gates(g_ref[0, c * ML_CHUNK:(c + 1) * ML_CHUNK, :])
        gcol_ref[0, c * ML_CHUNK:(c + 1) * ML_CHUNK, :] = packed
        grow_ref[0, c] = packed.T
    i = pl.program_id(1)
    last = pl.num_programs(1) - 1
    pad = CONV_K // 2
    pad_ref[HALO:HALO + tr, :] = cur_ref[0].astype(F32)
    pad_ref[0:HALO, :] = jnp.where(i > 0, prev_ref[0].astype(F32), 0.0)
    pad_ref[HALO + tr:2 * HALO + tr, :] = jnp.where(i < last, next_ref[0].astype(F32), 0.0)
    acc = cw_ref[0:1, :] * pad_ref[HALO - pad:HALO - pad + tr, :] + cb_ref[...]
    for j in range(1, CONV_K):
        acc = acc + cw_ref[j:j + 1, :] * pad_ref[HALO - pad + j:HALO - pad + j + tr, :]
    u = _silu(acc)
    width = u.shape[1]
    lane = lax.broadcasted_iota(jnp.int32, (tr, LANES), 1)
    first_half = (lane % (ML_HEAD_DIM // 2)) < (ML_HEAD_DIM // 4)
    kscale = ML_HEAD_DIM ** -0.5
    for g in range(width // LANES):
        ug = u[:, g * LANES:(g + 1) * LANES]
        if rope:
            partner = jnp.where(first_half, pltpu.roll(ug, LANES - ML_HEAD_DIM // 4, 1),
                                pltpu.roll(ug, ML_HEAD_DIM // 4, 1))
            ug = ug * cos_ref[...] + partner * sin_ref[...]
        if g >= ML_HEADS:
            ug = ug * kscale
        o_ref[0, :, g * LANES:(g + 1) * LANES] = ug.astype(BF16)


def _qk_streams(zml, gates, conv_w, conv_b, cos_t, sin_t, rope, tr):
    b, t, _ = zml.shape
    width = 2 * ML_WIDTH
    nh = tr // HALO
    nblk = t // HALO
    L = ML_CHUNK
    return pl.pallas_call(
        functools.partial(_qk_kernel, rope=rope, tr=tr),
        grid=(b, t // tr),
        in_specs=[pl.BlockSpec((1, tr, width), lambda i, j: (i, j, 0)),
                  pl.BlockSpec((1, HALO, width), lambda i, j: (i, jnp.maximum(j * nh - 1, 0), 0)),
                  pl.BlockSpec((1, HALO, width), lambda i, j: (i, jnp.minimum((j + 1) * nh, nblk - 1), 0)),
                  pl.BlockSpec((CONV_K, width), lambda i, j: (0, 0)),
                  pl.BlockSpec((1, width), lambda i, j: (0, 0)),
                  pl.BlockSpec((tr, LANES), lambda i, j: (j, 0)),
                  pl.BlockSpec((tr, LANES), lambda i, j: (j, 0)),
                  pl.BlockSpec((1, tr, LANES), lambda i, j: (i, j, 0))],
        out_specs=[pl.BlockSpec((1, tr, width), lambda i, j: (i, j, 0)),
                   pl.BlockSpec((1, tr, LANES), lambda i, j: (i, j, 0)),
                   pl.BlockSpec((1, tr // L, LANES, L), lambda i, j: (i, j, 0, 0))],
        out_shape=[jax.ShapeDtypeStruct((b, t, width), BF16),
                   jax.ShapeDtypeStruct((b, t, LANES), F32),
                   jax.ShapeDtypeStruct((b, t // L, LANES, L), F32)],
        scratch_shapes=[pltpu.VMEM((tr + 2 * HALO, width), F32)],
        compiler_params=_cparams("parallel", "parallel"),
        name="qk_rope" if rope else "qk_ctx",
    )(zml, zml, zml, conv_w, conv_b.reshape(1, width), cos_t, sin_t, gates)


def _rope_tables(n):
    pos = jnp.arange(n)
    half = ML_HEAD_DIM // 4
    inv = ROPE_BASE ** (-jnp.arange(half, dtype=F32) / half)

    def axis_tables(p):
        ang = p.astype(F32)[:, None] * inv[None, :]
        c, s = jnp.cos(ang), jnp.sin(ang)
        return jnp.concatenate([c, c], axis=-1), jnp.concatenate([-s, s], axis=-1)

    cr, sr = axis_tables(pos // GRID_W)
    cc, sc = axis_tables(pos % GRID_W)
    return jnp.concatenate([cr, cc], axis=-1), jnp.concatenate([sr, sc], axis=-1)


def _mlstm_direction(d, qk, v, gcol, grow, cn_s, m_s):
    L = ML_CHUNK
    row = lax.broadcasted_iota(jnp.int32, (L, L), 0)
    col = lax.broadcasted_iota(jnp.int32, (L, L), 1)
    fwd = d == 0
    keep = (row >= col) if fwd else (row <= col)
    ones = jnp.ones((L, ML_HEAD_DIM), BF16)
    end = L - 1 if fwd else 0
    outs = []
    for h in range(ML_HEADS):
        ci = GATE_RAW + (2 * d) * ML_HEADS + h
        cf = (GATE_PREFIX if fwd else GATE_SUFFIX) + (2 * d + 1) * ML_HEADS + h
        bc = jnp.broadcast_to(gcol[:, cf:cf + 1], (L, L))
        ic = jnp.broadcast_to(gcol[:, ci:ci + 1], (L, L))
        br = grow[cf:cf + 1, :]
        ir = grow[ci:ci + 1, :]
        m_prev = m_s[d, h][0:1, :]
        dlog = jnp.where(keep, bc - br + ir, NEG)
        m_t = jnp.maximum(bc + m_prev, jnp.max(dlog, axis=1, keepdims=True))
        dw = jnp.exp(dlog - m_t)
        inter = jnp.exp(bc + m_prev - m_t)
        qh = qk[:, h * LANES:(h + 1) * LANES]
        kh = qk[:, ML_WIDTH + h * LANES:ML_WIDTH + (h + 1) * LANES]
        vp = jnp.concatenate([v[:, h * LANES:(h + 1) * LANES], ones], axis=1)
        s = lax.dot_general(qh, kh, (((1,), (1,)), ((), ())), preferred_element_type=F32) * dw
        cn = cn_s[d, h]
        a1 = jnp.dot(s.astype(BF16), vp, preferred_element_type=F32)
        a2 = jnp.dot(qh, cn.astype(BF16), preferred_element_type=F32)
        num = a1[:, :ML_HEAD_DIM] + inter * a2[:, :ML_HEAD_DIM]
        den = a1[:, ML_HEAD_DIM:] + inter * a2[:, ML_HEAD_DIM:]
        outs.append(num / jnp.maximum(jnp.abs(den), jnp.exp(-m_t)))
        b_end = jnp.broadcast_to(br[:, end:end + 1], (1, L))
        g_row = b_end - br + ir
        m_new = jnp.maximum(b_end + m_prev, jnp.max(g_row, axis=1, keepdims=True))
        decay = jnp.exp(b_end + m_prev - m_new)
        wgt = jnp.exp(b_end - bc + ic - m_new)
        kw = (kh.astype(F32) * wgt).astype(BF16)
        upd = lax.dot_general(kw, vp, (((0,), (0,)), ((), ())), preferred_element_type=F32)
        cn_s[d, h] = jnp.concatenate([decay, decay], axis=1) * cn + upd
        m_s[d, h] = jnp.broadcast_to(m_new, m_s.shape[2:])
    return jnp.concatenate(outs, axis=1)


def _mlstm_kernel(qkf_ref, vf_ref, gcf_ref, grf_ref, qkb_ref, vb_ref, gcb_ref, grb_ref, c0_ref, m0_ref, *rest, emit_h):
    if emit_h:
        hf_ref, hb_ref, ct_ref, mt_ref, cn_s, m_s = rest
    else:
        ct_ref, mt_ref, cn_s, m_s = rest
        hf_ref = hb_ref = None
    c = pl.program_id(1)
    L = ML_CHUNK

    @pl.when(c == 0)
    def _():
        cn_s[...] = c0_ref[0]
        m_s[...] = m0_ref[0]

    for d, refs, h_ref in ((0, (qkf_ref, vf_ref, gcf_ref, grf_ref), hf_ref), (1, (qkb_ref, vb_ref, gcb_ref, grb_ref), hb_ref)):
        qk_ref, v_ref, gc_ref, gr_ref = refs
        n_sub = qk_ref.shape[1] // L
        for s in (range(n_sub) if d == 0 else reversed(range(n_sub))):
            rows = slice(s * L, (s + 1) * L)
            h = _mlstm_direction(d, qk_ref[0, rows, :], v_ref[0, rows, :], gc_ref[0, rows, :], gr_ref[0, s], cn_s, m_s)
            if h_ref is not None:
                h_ref[0, rows, :] = h.astype(h_ref.dtype)

    @pl.when(c == pl.num_programs(1) - 1)
    def _():
        ct_ref[0] = cn_s[...]
        mt_ref[0] = m_s[...]


def _mlstm(qk, zml, gcol, grow, c0, m0, emit_h):
    b, t, _ = qk.shape
    n_sub = min(MLSTM_CHUNKS_PER_STEP, t // ML_CHUNK)
    R = n_sub * ML_CHUNK
    assert t % R == 0
    nc = t // R
    vcol = 2
    f_idx = lambda i, c: (i, c, 0)
    b_idx = lambda i, c: (i, nc - 1 - c, 0)
    st_c = pl.BlockSpec((1, 2, ML_HEADS, ML_HEAD_DIM, 2 * ML_HEAD_DIM), lambda i, c: (i, 0, 0, 0, 0))
    st_m = pl.BlockSpec((1, 2, ML_HEADS, 8, LANES), lambda i, c: (i, 0, 0, 0, 0))
    out_specs = [st_c, st_m]
    out_shape = [jax.ShapeDtypeStruct(c0.shape, F32), jax.ShapeDtypeStruct(m0.shape, F32)]
    if emit_h:
        out_specs = [pl.BlockSpec((1, R, ML_WIDTH), f_idx), pl.BlockSpec((1, R, ML_WIDTH), b_idx)] + out_specs
        out_shape = [jax.ShapeDtypeStruct((b, t, ML_WIDTH), BF16)] * 2 + out_shape
    return pl.pallas_call(
        functools.partial(_mlstm_kernel, emit_h=emit_h),
        grid=(b, nc),
        in_specs=[pl.BlockSpec((1, R, 2 * ML_WIDTH), f_idx),
                  pl.BlockSpec((1, R, ML_WIDTH), lambda i, c: (i, c, vcol)),
                  pl.BlockSpec((1, R, LANES), f_idx),
                  pl.BlockSpec((1, n_sub, LANES, ML_CHUNK), lambda i, c: (i, c, 0, 0)),
                  pl.BlockSpec((1, R, 2 * ML_WIDTH), b_idx),
                  pl.BlockSpec((1, R, ML_WIDTH), lambda i, c: (i, nc - 1 - c, vcol)),
                  pl.BlockSpec((1, R, LANES), b_idx),
                  pl.BlockSpec((1, n_sub, LANES, ML_CHUNK), lambda i, c: (i, nc - 1 - c, 0, 0)),
                  st_c, st_m],
        out_specs=out_specs,
        out_shape=out_shape,
        scratch_shapes=[pltpu.VMEM((2, ML_HEADS, ML_HEAD_DIM, 2 * ML_HEAD_DIM), F32),
                        pltpu.VMEM((2, ML_HEADS, 8, LANES), F32)],
        compiler_params=_cparams("parallel", "arbitrary"),
        name="mlstm" if emit_h else "mlstm_ctx",
    )(qk, zml, gcol, grow, qk, zml, gcol, grow, c0, m0)


def _lane_in(shape, start, width):
    lane = lax.broadcasted_iota(jnp.int32, shape, 1)
    return (lane >= start) & (lane < start + width)


def _na_kernel(q_ref, k0_ref, k1_ref, k2_ref, v0_ref, v1_ref, v2_ref, kc_ref, vc_ref, bias_ref, o_ref):
    nq = q_ref.shape[1]
    lane = lax.broadcasted_iota(jnp.int32, (nq, LANES), 1)
    nt = (((1,), (1,)), ((), ()))
    for p in range(NA_HEADS // 2):
        sl = slice(p * LANES, (p + 1) * LANES)
        q2 = q_ref[0, :, sl] * (NA_HEAD_DIM ** -0.5)
        kwin = jnp.concatenate([k0_ref[0, :, sl], k1_ref[0, :, sl], k2_ref[0, :, sl]], axis=0)
        vwin = jnp.concatenate([v0_ref[0, :, sl], v1_ref[0, :, sl], v2_ref[0, :, sl]], axis=0)
        kc = kc_ref[0, :, sl]
        vc = vc_ref[0, :, sl]
        halves = []
        for a in range(2):
            in_head = (lane >= a * NA_HEAD_DIM) & (lane < (a + 1) * NA_HEAD_DIM)
            qm = jnp.where(in_head, q2, jnp.zeros_like(q2))
            s_win = lax.dot_general(qm, kwin, nt, preferred_element_type=F32) + bias_ref[0, 2 * p + a]
            s_ctx = lax.dot_general(qm, kc, nt, preferred_element_type=F32)
            m = jnp.maximum(jnp.max(s_win, axis=1, keepdims=True), jnp.max(s_ctx, axis=1, keepdims=True))
            p_win = jnp.exp(s_win - m).astype(BF16)
            p_ctx = jnp.exp(s_ctx - m).astype(BF16)
            vw = jnp.where(_lane_in(vwin.shape, a * NA_HEAD_DIM, NA_HEAD_DIM), vwin, jnp.ones_like(vwin))
            vx = jnp.where(_lane_in(vc.shape, a * NA_HEAD_DIM, NA_HEAD_DIM), vc, jnp.ones_like(vc))
            o = jnp.dot(p_win, vw, preferred_element_type=F32) + jnp.dot(p_ctx, vx, preferred_element_type=F32)
            halves.append(o / pltpu.roll(o, NA_HEAD_DIM, 1))
        o_ref[0, :, sl] = jnp.where(lane < NA_HEAD_DIM, halves[0], halves[1]).astype(o_ref.dtype)


def _na_bias_tables(rpb, rows):
    R = NA_ROWS_PER_STEP
    nblk = rows // R
    kr = NA_WIN_ROWS
    cq = np.arange(GRID_W)
    cstart = np.clip(cq - NA_WIN_COLS // 2, 0, GRID_W - NA_WIN_COLS)
    ck = np.arange(GRID_W)
    col_ok = (ck[None, :] >= cstart[:, None]) & (ck[None, :] < cstart[:, None] + NA_WIN_COLS)
    col_off = np.where(col_ok, ck[None, :] - cq[:, None] + NA_WIN_COLS - 1, 0)
    row_ok = np.zeros((3, R, 3 * R), bool)
    row_off = np.zeros((3, R, 3 * R), np.int64)
    for vi, j in enumerate((0, 1, nblk - 1)):
        for i in range(R):
            r = j * R + i
            r0 = min(max(r - kr // 2, 0), rows - kr)
            for t in range(3):
                jb = j - 1 + t
                if jb < 0 or jb >= nblk:
                    continue
                for rr in range(R):
                    krow = jb * R + rr
                    if r0 <= krow < r0 + kr:
                        row_ok[vi, i, t * R + rr] = True
                        row_off[vi, i, t * R + rr] = krow - r + NA_WIN_ROWS - 1
    col_sel = (np.arange(rpb.shape[2])[None, None, :] == col_off[:, :, None]) & col_ok[:, :, None]
    row_sel = (np.arange(rpb.shape[1])[None, None, None, :] == row_off[..., None]) & row_ok[..., None]
    by_col = jnp.einsum('hrc,qkc->hrqk', rpb, jnp.asarray(col_sel, F32), precision=HIGHEST)
    vals = jnp.einsum('vixr,hrqk->vhiqxk', jnp.asarray(row_sel, F32), by_col, precision=HIGHEST)
    ok = row_ok[:, None, :, None, :, None] & col_ok[None, None, None, :, None, :]
    vals = jnp.where(jnp.asarray(ok), vals, NEG)
    return vals.reshape(3, NA_HEADS, R * GRID_W, 3 * R * GRID_W)


def _na(zna, zcna, bias, rows):
    b, n, _ = zna.shape
    ctx = zcna.shape[1]
    R = NA_ROWS_PER_STEP
    nq = R * GRID_W
    nblk = rows // R
    kb = lambda col, off: pl.BlockSpec(
        (1, nq, NA_WIDTH), lambda i, j: (i, jnp.clip(j + off, 0, nblk - 1), col))
    variant = lambda i, j: (jnp.where(j == 0, 0, jnp.where(j == nblk - 1, 2, 1)), 0, 0, 0)
    return pl.pallas_call(
        _na_kernel,
        grid=(b, nblk),
        in_specs=[pl.BlockSpec((1, nq, NA_WIDTH), lambda i, j: (i, j, 0)),
                  kb(1, -1), kb(1, 0), kb(1, 1), kb(2, -1), kb(2, 0), kb(2, 1),
                  pl.BlockSpec((1, ctx, NA_WIDTH), lambda i, j: (i, 0, 1)),
                  pl.BlockSpec((1, ctx, NA_WIDTH), lambda i, j: (i, 0, 2)),
                  pl.BlockSpec((1, NA_HEADS, nq, 3 * nq), variant)],
        out_specs=pl.BlockSpec((1, nq, NA_WIDTH), lambda i, j: (i, j, 0)),
        out_shape=jax.ShapeDtypeStruct((b, n, NA_WIDTH), BF16),
        compiler_params=_cparams("parallel", "parallel"),
        name="na",
    )(zna, zna, zna, zna, zna, zna, zna, zcna, zcna, bias)


def _mix_kernel(na_ref, hf_ref, hb_ref, o_ref, x_ref, g1_ref, sh2_ref, sc2_ref, wo_ref, mg_ref, l1g_ref, l1b_ref,
                wr_ref, br_ref, xmid_ref, rt_ref, rw_ref, cnt_ref, carry_s, *, alpha):
    first = (pl.program_id(0) == 0) & (pl.program_id(1) == 0)

    @pl.when(first)
    def _():
        carry_s[...] = jnp.zeros_like(carry_s)

    tm = x_ref.shape[1]
    h = hf_ref[0].astype(F32) + hb_ref[0].astype(F32)
    parts = []
    for hd in range(ML_HEADS):
        hh = h[:, hd * LANES:(hd + 1) * LANES]
        parts.append(hh * lax.rsqrt(jnp.mean(hh * hh, axis=-1, keepdims=True) + LN_EPS))
    hn = jnp.concatenate(parts, axis=1)
    ml = (hn * mg_ref[...] * jax.nn.sigmoid(o_ref[0].astype(F32))).astype(BF16)
    mix = (jnp.dot(na_ref[0], wo_ref[0:NA_WIDTH, :], preferred_element_type=F32)
           + jnp.dot(ml, wo_ref[NA_WIDTH:, :], preferred_element_type=F32))
    xmid = _ln_rows(alpha * x_ref[0] + g1_ref[0] * mix) * l1g_ref[...] + l1b_ref[...]
    xmid_ref[0] = xmid

    xt = (_ln_rows(xmid) * (1.0 + sc2_ref[0]) + sh2_ref[0]).astype(BF16)
    logits = jnp.dot(xt, wr_ref[...], preferred_element_type=F32) + br_ref[...]
    lane = lax.broadcasted_iota(jnp.int32, (tm, LANES), 1)
    is_g = lane < N_GROUPS
    gl = jnp.where(is_g, logits, NEG)
    gmax = jnp.max(gl, axis=1, keepdims=True)
    grp = jnp.min(jnp.where(gl == gmax, lane, LANES), axis=1, keepdims=True)
    gsum = jnp.sum(jnp.where(is_g, jnp.exp(gl - gmax), 0.0), axis=1, keepdims=True)
    grp_w = 1.0 / gsum
    lo = N_GROUPS + EXPERTS_PER_GROUP * grp
    el = jnp.where((lane >= lo) & (lane < lo + EXPERTS_PER_GROUP), logits, NEG)
    t1 = jnp.max(el, axis=1, keepdims=True)
    i1 = jnp.min(jnp.where(el == t1, lane, LANES), axis=1, keepdims=True)
    el2 = jnp.where(lane == i1, NEG, el)
    t2 = jnp.max(el2, axis=1, keepdims=True)
    i2 = jnp.min(jnp.where(el2 == t2, lane, LANES), axis=1, keepdims=True)
    e21 = jnp.exp(t2 - t1)
    w0 = grp_w / (1.0 + e21)
    w1 = grp_w * e21 / (1.0 + e21)

    hit1 = lane == i1
    hit2 = lane == i2
    onehot = (hit1 | hit2).astype(BF16)
    r_i = lax.broadcasted_iota(jnp.int32, (tm, tm), 0)
    c_i = lax.broadcasted_iota(jnp.int32, (tm, tm), 1)
    before = (r_i > c_i).astype(BF16)
    prefix = jnp.dot(before, onehot, preferred_element_type=F32) + carry_s[0:1, :]
    rank0 = jnp.sum(jnp.where(hit1, prefix, 0.0), axis=1, keepdims=True)
    rank1 = jnp.sum(jnp.where(hit2, prefix, 0.0), axis=1, keepdims=True)
    total = carry_s[0:1, :] + jnp.sum(onehot.astype(F32), axis=0, keepdims=True)
    carry_s[...] = jnp.broadcast_to(total, carry_s.shape)
    cnt_ref[...] = jnp.broadcast_to(total, cnt_ref.shape)

    rf = jnp.where(lane == 0, (i1 - N_GROUPS).astype(F32),
                   jnp.where(lane == 1, (i2 - N_GROUPS).astype(F32),
                             jnp.where(lane == 2, rank0, jnp.where(lane == 3, rank1, 0.0))))
    rt_ref[0] = rf.T[0:SUBLANES, :].astype(jnp.int32)
    rw_ref[0] = jnp.where(lane == 0, w0, jnp.where(lane == 1, w1, 0.0))


def _mix(na, hf, hb, zml, x, g1, sh2, sc2, wo, mg, l1g, l1b, wr, br, alpha, tm):
    b, n, d = x.shape
    row = lambda w: pl.BlockSpec((1, tm, w), lambda i, j: (i, j, 0))
    vec = pl.BlockSpec((1, 1, d), lambda i, j: (i, 0, 0))
    full = lambda a: pl.BlockSpec(a.shape, lambda i, j: (0,) * a.ndim)
    ocol = 3
    return pl.pallas_call(
        functools.partial(_mix_kernel, alpha=alpha),
        grid=(b, n // tm),
        in_specs=[row(NA_WIDTH), row(ML_WIDTH), row(ML_WIDTH),
                  pl.BlockSpec((1, tm, ML_WIDTH), lambda i, j: (i, j, ocol)),
                  row(d), vec, vec, vec, full(wo), full(mg), full(l1g), full(l1b), full(wr), full(br)],
        out_specs=[row(d), pl.BlockSpec((1, SUBLANES, tm), lambda i, j: (i * (n // tm) + j, 0, 0)), row(LANES),
                   pl.BlockSpec((SUBLANES, LANES), lambda i, j: (0, 0))],
        out_shape=[jax.ShapeDtypeStruct((b, n, d), F32),
                   jax.ShapeDtypeStruct((b * (n // tm), SUBLANES, tm), jnp.int32),
                   jax.ShapeDtypeStruct((b, n, LANES), F32),
                   jax.ShapeDtypeStruct((SUBLANES, LANES), F32)],
        scratch_shapes=[pltpu.VMEM((8, LANES), F32)],
        compiler_params=_cparams("arbitrary", "arbitrary"),
        name="mix",
    )(na, hf, hb, zml, x, g1, sh2, sc2, wo, mg, l1g, l1b, wr, br)


def _zero_fill_padding(pad_base_ref, pad_len_ref, nused_ref, xs_ref, zero_s, sem, wait):
    tb = zero_s.shape[0] // SUBLANES

    def run(copy):
        copy.wait() if wait else copy.start()

    def fill(off, nrows):
        run(pltpu.make_async_copy(zero_s.at[pl.ds(0, nrows * SUBLANES)], _token_rows(xs_ref, off, nrows), sem))

    def per_expert(e, _):
        plen = pad_len_ref[e]
        base = pad_base_ref[e]
        bit = tb // 2
        while bit >= 1:
            off = base + (plen & ~(2 * bit - 1))

            @pl.when((plen & bit) != 0)
            def _(bit=bit, off=off):
                fill(off, bit)

            bit //= 2
        return 0

    lax.fori_loop(0, N_EXPERTS, per_expert, 0)

    def per_block(i, _):
        fill(i * tb, tb)
        return 0

    lax.fori_loop(nused_ref[0], xs_ref.shape[0] // (tb * SUBLANES), per_block, 0)


def _token_rows(ref, tok, n):
    return ref.at[pl.ds(pl.multiple_of(tok * SUBLANES, SUBLANES), n * SUBLANES)]


def _to_token_tiles(dst_ref, base, val):
    tm = val.shape[0]
    for s in range(val.shape[1] // LANES):
        dst_ref[pl.ds(base + s, tm, stride=SUBLANES), :] = val[:, s * LANES:(s + 1) * LANES]


def _from_token_tiles(src_ref, base, tm):
    return jnp.concatenate([src_ref[pl.ds(base + s, tm, stride=SUBLANES), :] for s in range(SUBLANES)], axis=1)


def _dispatch_kernel(pad_base_ref, pad_len_ref, nused_ref, dest_ref, xmid_ref, sh2_ref, sc2_ref, xs_ref,
                     xt_s, zero_s, sem, zsem):
    tm = xmid_ref.shape[1]
    step = pl.program_id(0) * pl.num_programs(1) + pl.program_id(1)
    nsteps = pl.num_programs(0) * pl.num_programs(1)

    def wait_step_copies():
        for _ in range(2):
            pltpu.make_async_copy(xt_s, _token_rows(xs_ref, 0, tm), sem).wait()

    @pl.when(step == 0)
    def _():
        zero_s[...] = jnp.zeros_like(zero_s)
        _zero_fill_padding(pad_base_ref, pad_len_ref, nused_ref, xs_ref, zero_s, zsem, False)

    xt = _ln_rows(xmid_ref[0]) * (1.0 + sc2_ref[0]) + sh2_ref[0]

    @pl.when(step > 0)
    def _():
        wait_step_copies()

    _to_token_tiles(xt_s, 0, xt)

    def start(r, _):
        for k in range(2):
            pltpu.make_async_copy(_token_rows(xt_s, r, 1), _token_rows(xs_ref, dest_ref[0, k, r], 1), sem).start()
        return 0

    lax.fori_loop(0, tm, start, 0, unroll=8)

    @pl.when(step == 0)
    def _():
        _zero_fill_padding(pad_base_ref, pad_len_ref, nused_ref, xs_ref, zero_s, zsem, True)

    @pl.when(step == nsteps - 1)
    def _():
        wait_step_copies()


def _dispatch(xmid, sh2, sc2, dest, pad_base, pad_len, nused, cap, tm):
    b, n, d = xmid.shape
    assert d == SUBLANES * LANES
    nt = n // tm
    vec = pl.BlockSpec((1, 1, d), lambda i, j, *_: (i, 0, 0))
    return pl.pallas_call(
        _dispatch_kernel,
        grid_spec=pltpu.PrefetchScalarGridSpec(
            num_scalar_prefetch=3,
            grid=(b, nt),
            in_specs=[pl.BlockSpec((1, 2, tm), lambda i, j, *_: (i * nt + j, 0, 0), memory_space=pltpu.SMEM),
                      pl.BlockSpec((1, tm, d), lambda i, j, *_: (i, j, 0)), vec, vec],
            out_specs=pl.BlockSpec(memory_space=pl.ANY),
            scratch_shapes=[pltpu.VMEM((tm * SUBLANES, LANES), F32), pltpu.VMEM((EXPERT_ROWS * SUBLANES, LANES), F32),
                            pltpu.SemaphoreType.DMA, pltpu.SemaphoreType.DMA]),
        out_shape=jax.ShapeDtypeStruct((cap * SUBLANES, LANES), F32),
        compiler_params=_cparams("arbitrary", "arbitrary"),
        name="dispatch",
    )(pad_base, pad_len, nused, dest, xmid, sh2, sc2)


def _expert_kernel(be_ref, nv_ref, xs_ref, w1_ref, w3_ref, w2_ref, ys_ref, w1b, w3b, w2b, xb_s):
    tb = xb_s.shape[0]
    i = pl.program_id(0)
    e = be_ref[i]
    changed = (i == 0) | (be_ref[jnp.maximum(i - 1, 0)] != e)

    @pl.when(changed)
    def _():
        w1b[...] = w1_ref[0].astype(BF16)
        w3b[...] = w3_ref[0].astype(BF16)
        w2b[...] = w2_ref[0].astype(BF16)

    nv = nv_ref[i]

    @pl.when(nv > 0)
    def _():
        for s in range(SUBLANES):
            xb_s[:, s * LANES:(s + 1) * LANES] = xs_ref[pl.ds(s, tb, stride=SUBLANES), :].astype(BF16)
        xb = xb_s[...]
        h1 = jnp.dot(xb, w1b[...], preferred_element_type=F32)
        h3 = jnp.dot(xb, w3b[...], preferred_element_type=F32)
        a = (_silu(h1) * h3).astype(BF16)
        _to_token_tiles(ys_ref, 0, jnp.dot(a, w2b[...], preferred_element_type=F32))

    @pl.when(nv == 0)
    def _():
        ys_ref[...] = jnp.zeros_like(ys_ref)


def _experts(xs, block_e, block_nv, w1, w3, w2):
    d, hid = w1.shape[1], w1.shape[2]
    tb = EXPERT_ROWS
    tile_rows = tb * SUBLANES
    return pl.pallas_call(
        _expert_kernel,
        grid_spec=pltpu.PrefetchScalarGridSpec(
            num_scalar_prefetch=2,
            grid=(xs.shape[0] // tile_rows,),
            in_specs=[pl.BlockSpec((tile_rows, LANES), lambda i, be, nv: (i, 0)),
                      pl.BlockSpec((1, d, hid), lambda i, be, nv: (be[i], 0, 0)),
                      pl.BlockSpec((1, d, hid), lambda i, be, nv: (be[i], 0, 0)),
                      pl.BlockSpec((1, hid, d), lambda i, be, nv: (be[i], 0, 0))],
            out_specs=pl.BlockSpec((tile_rows, LANES), lambda i, be, nv: (i, 0)),
            scratch_shapes=[pltpu.VMEM((d, hid), BF16), pltpu.VMEM((d, hid), BF16), pltpu.VMEM((hid, d), BF16),
                            pltpu.VMEM((tb, d), BF16)]),
        out_shape=jax.ShapeDtypeStruct(xs.shape, F32),
        compiler_params=_cparams("arbitrary"),
        name="experts",
    )(block_e, block_nv, xs, w1, w3, w2)


def _combine_kernel(dcur_ref, dnext_ref, xmid_ref, rw_ref, g2_ref, l2g_ref, l2b_ref, ys_ref, o_ref,
                    y0_s, y1_s, sem, *, alpha):
    tm = xmid_ref.shape[1]
    step = pl.program_id(0) * pl.num_programs(1) + pl.program_id(1)
    nsteps = pl.num_programs(0) * pl.num_programs(1)
    slot = step % 2

    def gather(dest_ref, into):
        def start(r, _):
            for k, buf in ((0, y0_s), (1, y1_s)):
                pltpu.make_async_copy(_token_rows(ys_ref, dest_ref[0, k, r], 1),
                                      _token_rows(buf, into * tm + r, 1), sem.at[into]).start()
            return 0

        lax.fori_loop(0, tm, start, 0, unroll=8)

    @pl.when(step == 0)
    def _():
        gather(dcur_ref, 0)

    @pl.when(step + 1 < nsteps)
    def _():
        gather(dnext_ref, 1 - slot)

    for buf in (y0_s, y1_s):
        pltpu.make_async_copy(_token_rows(ys_ref, 0, tm), _token_rows(buf, slot * tm, tm), sem.at[slot]).wait()
    base = pl.multiple_of(slot * tm * SUBLANES, SUBLANES)
    rw = rw_ref[0]
    moe = rw[:, 0:1] * _from_token_tiles(y0_s, base, tm) + rw[:, 1:2] * _from_token_tiles(y1_s, base, tm)
    o_ref[0] = _ln_rows(alpha * xmid_ref[0] + g2_ref[0] * moe) * l2g_ref[...] + l2b_ref[...]


def _combine(xmid, rw, g2, l2g, l2b, ys, dest, alpha, tm):
    b, n, d = xmid.shape
    nt = n // tm
    full = lambda a: pl.BlockSpec(a.shape, lambda i, j: (0,) * a.ndim)
    return pl.pallas_call(
        functools.partial(_combine_kernel, alpha=alpha),
        grid=(b, nt),
        in_specs=[pl.BlockSpec((1, 2, tm), lambda i, j: (i * nt + j, 0, 0), memory_space=pltpu.SMEM),
                  pl.BlockSpec((1, 2, tm), lambda i, j: (jnp.minimum(i * nt + j + 1, b * nt - 1), 0, 0),
                               memory_space=pltpu.SMEM),
                  pl.BlockSpec((1, tm, d), lambda i, j: (i, j, 0)),
                  pl.BlockSpec((1, tm, LANES), lambda i, j: (i, j, 0)),
                  pl.BlockSpec((1, 1, d), lambda i, j: (i, 0, 0)),
                  full(l2g), full(l2b),
                  pl.BlockSpec(memory_space=pl.ANY)],
        out_specs=pl.BlockSpec((1, tm, d), lambda i, j: (i, j, 0)),
        out_shape=jax.ShapeDtypeStruct((b, n, d), F32),
        scratch_shapes=[pltpu.VMEM((2 * tm * SUBLANES, LANES), F32), pltpu.VMEM((2 * tm * SUBLANES, LANES), F32),
                        pltpu.SemaphoreType.DMA((2,))],
        compiler_params=_cparams("arbitrary", "arbitrary"),
        name="combine",
    )(dest, dest, xmid, rw, g2, l2g, l2b, ys)


def _tile(n, want):
    t = min(n, want)
    assert n % t == 0, (n, t)
    return t


def kernel(x, c, ctx, c_ctx, w_ada, b_ada, w_in, conv_w, conv_b, gate_b, rpb, ml_norm_g, w_out, ln1_g, ln1_b,
           w_router_g, b_router_g, w_router_e, b_router_e, w1, w3, w2, ln2_g, ln2_b):
    B, N, D = x.shape
    T_CTX = ctx.shape[1]
    depth = w_ada.shape[0]
    rows = N // GRID_W
    assert depth == 1 and N % GRID_W == 0 and rows % NA_ROWS_PER_STEP == 0 and rows >= 3 * NA_ROWS_PER_STEP
    assert N % ML_CHUNK == 0 and T_CTX % ML_CHUNK == 0
    alpha = (2.0 * depth) ** 0.25
    l = 0

    pad_rows = -(B + 1) % 8
    cvec = jnp.concatenate([c, c_ctx[None], jnp.zeros((pad_rows, D), F32)], axis=0)
    ada = _ada(cvec, w_ada[l], b_ada[l])
    sh1, sc1, g1, sh2, sc2, g2 = [a[:, None, :] for a in jnp.split(ada[:B], 6, axis=-1)]
    csh1, csc1 = [jnp.broadcast_to(a[None], (B, 1, D)) for a in jnp.split(ada[B:B + 1], 6, axis=-1)[:2]]

    col_ml = 3 * NA_WIDTH
    col_g = col_ml + 4 * ML_WIDTH
    wb = w_in[l].astype(BF16)
    wna, wml = wb[:, :col_ml], wb[:, col_ml:col_g]
    n_gate = 4 * ML_HEADS
    wg = jnp.pad(wb[:, col_g:], ((0, 0), (0, LANES - n_gate)))
    gb = jnp.pad(gate_b[l], (0, LANES - n_gate)).reshape(1, LANES)
    zna, zml, gates = _inproj(x, sh1, sc1, wna, wml, wg, gb, _tile(N, 512))
    zcna, zcml, gates_c = _inproj(ctx, csh1, csc1, wna, wml, wg, gb, _tile(T_CTX, 256))

    cos_t, sin_t = _rope_tables(N)
    qk_c, gcol_c, grow_c = _qk_streams(zcml, gates_c, conv_w[l], conv_b[l], cos_t[:T_CTX], sin_t[:T_CTX], False,
                                       _tile(T_CTX, 512))
    qk_l, gcol_l, grow_l = _qk_streams(zml, gates, conv_w[l], conv_b[l], cos_t, sin_t, True, _tile(N, 512))
    c0 = jnp.zeros((B, 2, ML_HEADS, ML_HEAD_DIM, 2 * ML_HEAD_DIM), F32)
    m0 = jnp.zeros((B, 2, ML_HEADS, 8, LANES), F32)
    c_ctx_end, m_ctx_end = _mlstm(qk_c, zcml, gcol_c, grow_c, c0, m0, False)
    hf, hb, _, _ = _mlstm(qk_l, zml, gcol_l, grow_l, c_ctx_end, m_ctx_end, True)

    na = _na(zna, zcna, _na_bias_tables(rpb[l], rows), rows)

    wr = jnp.pad(jnp.concatenate([w_router_g[l], w_router_e[l]], axis=1),
                 ((0, 0), (0, LANES - N_GROUPS - N_EXPERTS))).astype(BF16)
    br = jnp.pad(jnp.concatenate([b_router_g[l], b_router_e[l]]), (0, LANES - N_GROUPS - N_EXPERTS)).reshape(1, LANES)
    tm = _tile(N, 512)
    xmid, rt, rw, counts = _mix(na, hf, hb, zml, x, g1, sh2, sc2, w_out[l].astype(BF16),
                                ml_norm_g[l].reshape(1, ML_WIDTH), ln1_g[l].reshape(1, D), ln1_b[l].reshape(1, D),
                                wr, br, alpha, tm)

    tb = EXPERT_ROWS
    n_assign = 2 * B * N
    cap = -(-n_assign // tb) * tb + N_EXPERTS * tb
    sizes = counts[0, N_GROUPS:N_GROUPS + N_EXPERTS].astype(jnp.int32)
    padded = (sizes + tb - 1) // tb * tb
    pend = jnp.cumsum(padded)
    pstart = pend - padded
    experts = jnp.arange(N_EXPERTS, dtype=jnp.int32)
    first_row = jnp.sum(jnp.where(rt[:, 0:2, :, None] == experts, pstart, 0), axis=-1)
    dest = first_row + rt[:, 2:4, :]
    blk0 = jnp.arange(cap // tb, dtype=jnp.int32) * tb
    block_e = jnp.minimum(jnp.sum(pend[None, :] <= blk0[:, None], axis=1), N_EXPERTS - 1).astype(jnp.int32)
    is_e = block_e[:, None] == experts
    block_nv = jnp.clip(jnp.sum(jnp.where(is_e, pstart + sizes, 0), axis=1) - blk0, 0, tb).astype(jnp.int32)

    nused = (pend[-1:] // tb).astype(jnp.int32)
    xs = _dispatch(xmid, sh2, sc2, dest, pstart + sizes, padded - sizes, nused, cap, tm)
    ys = _experts(xs, block_e, block_nv, w1[l], w3[l], w2[l])
    return _combine(xmid, rw, g2, ln2_g[l].reshape(1, D), ln2_b[l].reshape(1, D), ys, dest, alpha, tm)
```

```python
import functools

import numpy as np
import jax
import jax.numpy as jnp
from jax import lax
from jax.experimental import pallas as pl
from jax.experimental.pallas import tpu as pltpu

F32 = jnp.float32
BF16 = jnp.bfloat16
ROW_DTYPE = jnp.int32
HIGHEST = lax.Precision.HIGHEST

GRID_W = 64
NA_HEADS = 8
NA_HEAD_DIM = 64
NA_WIDTH = NA_HEADS * NA_HEAD_DIM
NA_WIN_ROWS = 8
NA_WIN_COLS = 16
ML_HEADS = 4
ML_HEAD_DIM = 128
ML_WIDTH = ML_HEADS * ML_HEAD_DIM
ML_CHUNK = 128
CONV_K = 5
N_GROUPS = 8
EXPERTS_PER_GROUP = 8
N_EXPERTS = N_GROUPS * EXPERTS_PER_GROUP
ROPE_BASE = 10000.0
LN_EPS = 1e-5

LANES = 128
SUBLANES = 8
VMEM_LIMIT = 56 * 1024 * 1024

NA_ROWS_PER_STEP = 4
EXPERT_ROWS = 512
MLSTM_CHUNKS_PER_STEP = 4
NEG = -1e30


def _cparams(*sem):
    return pltpu.CompilerParams(dimension_semantics=sem, vmem_limit_bytes=VMEM_LIMIT)


def _silu(v):
    return v * jax.nn.sigmoid(v)


def _ln_rows(v):
    mu = jnp.mean(v, axis=-1, keepdims=True)
    vc = v - mu
    var = jnp.mean(vc * vc, axis=-1, keepdims=True)
    return vc * lax.rsqrt(var + LN_EPS)


def _ada_kernel(c_ref, w_ref, b_ref, o_ref):
    o_ref[...] = jnp.dot(_silu(c_ref[...]), w_ref[...], preferred_element_type=F32,
                         precision=HIGHEST) + b_ref[...]


def _ada(cvec, w, b):
    rows, d = cvec.shape
    cols = w.shape[1]
    tn = 1024
    return pl.pallas_call(
        _ada_kernel,
        grid=(cols // tn,),
        in_specs=[pl.BlockSpec((rows, d), lambda j: (0, 0)),
                  pl.BlockSpec((d, tn), lambda j: (0, j)),
                  pl.BlockSpec((1, tn), lambda j: (0, j))],
        out_specs=pl.BlockSpec((rows, tn), lambda j: (0, j)),
        out_shape=jax.ShapeDtypeStruct((rows, cols), F32),
        compiler_params=_cparams("arbitrary"),
        name="ada",
    )(cvec, w, b.reshape(1, cols))


def _inproj_kernel(x_ref, sh_ref, sc_ref, wna_ref, wml_ref, wg_ref, gb_ref, zna_ref, zml_ref, g_ref):
    y = _ln_rows(x_ref[0]) * (1.0 + sc_ref[0]) + sh_ref[0]
    yb = y.astype(BF16)
    zna_ref[0] = jnp.dot(yb, wna_ref[...], preferred_element_type=F32).astype(BF16)
    zml_ref[0] = jnp.dot(yb, wml_ref[...], preferred_element_type=F32).astype(BF16)
    g_ref[0] = jnp.dot(yb, wg_ref[...], preferred_element_type=F32) + gb_ref[...]


def _inproj(x, shift, scale, wna, wml, wg, gb, tm):
    b, t, d = x.shape
    full = lambda a: pl.BlockSpec(a.shape, lambda i, j: (0,) * a.ndim)
    row = lambda w: pl.BlockSpec((1, tm, w), lambda i, j: (i, j, 0))
    vec = pl.BlockSpec((1, 1, d), lambda i, j: (i, 0, 0))
    return pl.pallas_call(
        _inproj_kernel,
        grid=(b, t // tm),
        in_specs=[row(d), vec, vec, full(wna), full(wml), full(wg), full(gb)],
        out_specs=[row(wna.shape[1]), row(wml.shape[1]), row(LANES)],
        out_shape=[jax.ShapeDtypeStruct((b, t, wna.shape[1]), BF16),
                   jax.ShapeDtypeStruct((b, t, wml.shape[1]), BF16),
                   jax.ShapeDtypeStruct((b, t, LANES), F32)],
        compiler_params=_cparams("parallel", "parallel"),
        name="inproj",
    )(x, shift, scale, wna, wml, wg, gb)


HALO = 16


def _log_sigmoid(v):
    return jnp.minimum(v, 0.0) - jnp.log1p(jnp.exp(-jnp.abs(v)))


GATE_RAW, GATE_PREFIX, GATE_SUFFIX = 0, 16, 32


def _pack_gates(g):
    L = g.shape[0]
    row = lax.broadcasted_iota(jnp.int32, (L, L), 0)
    col = lax.broadcasted_iota(jnp.int32, (L, L), 1)
    lane = lax.broadcasted_iota(jnp.int32, g.shape, 1)
    lf = _log_sigmoid(g)
    prefix = jnp.dot((row >= col).astype(F32), lf, preferred_element_type=F32, precision=HIGHEST)
    suffix = jnp.dot((row <= col).astype(F32), lf, preferred_element_type=F32, precision=HIGHEST)
    return jnp.where(lane < GATE_PREFIX, g,
                     jnp.where(lane < GATE_SUFFIX, pltpu.roll(prefix, GATE_PREFIX, 1),
                               jnp.where(lane < GATE_SUFFIX + GATE_PREFIX, pltpu.roll(suffix, GATE_SUFFIX, 1), 0.0)))


def _qk_kernel(cur_ref, prev_ref, next_ref, cw_ref, cb_ref, cos_ref, sin_ref, g_ref, o_ref, gcol_ref, grow_ref,
               pad_ref, *, rope, tr):
    for c in range(tr // ML_CHUNK):
        packed = _pack_gates(g_ref[0, c * ML_CHUNK:(c + 1) * ML_CHUNK, :])
        gcol_ref[0, c * ML_CHUNK:(c + 1) * ML_CHUNK, :] = packed
        grow_ref[0, c] = packed.T
    i = pl.program_id(1)
    last = pl.num_programs(1) - 1
    pad = CONV_K // 2
    pad_ref[HALO:HALO + tr, :] = cur_ref[0].astype(F32)
    pad_ref[0:HALO, :] = jnp.where(i > 0, prev_ref[0].astype(F32), 0.0)
    pad_ref[HALO + tr:2 * HALO + tr, :] = jnp.where(i < last, next_ref[0].astype(F32), 0.0)
    acc = cw_ref[0:1, :] * pad_ref[HALO - pad:HALO - pad + tr, :] + cb_ref[...]
    for j in range(1, CONV_K):
        acc = acc + cw_ref[j:j + 1, :] * pad_ref[HALO - pad + j:HALO - pad + j + tr, :]
    u = _silu(acc)
    width = u.shape[1]
    lane = lax.broadcasted_iota(jnp.int32, (tr, LANES), 1)
    first_half = (lane % (ML_HEAD_DIM // 2)) < (ML_HEAD_DIM // 4)
    kscale = ML_HEAD_DIM ** -0.5
    for g in range(width // LANES):
        ug = u[:, g * LANES:(g + 1) * LANES]
        if rope:
            partner = jnp.where(first_half, pltpu.roll(ug, LANES - ML_HEAD_DIM // 4, 1),
                                pltpu.roll(ug, ML_HEAD_DIM // 4, 1))
            ug = ug * cos_ref[...] + partner * sin_ref[...]
        if g >= ML_HEADS:
            ug = ug * kscale
        o_ref[0, :, g * LANES:(g + 1) * LANES] = ug.astype(BF16)


def _qk_streams(zml, gates, conv_w, conv_b, cos_t, sin_t, rope, tr):
    b, t, _ = zml.shape
    width = 2 * ML_WIDTH
    nh = tr // HALO
    nblk = t // HALO
    L = ML_CHUNK
    return pl.pallas_call(
        functools.partial(_qk_kernel, rope=rope, tr=tr),
        grid=(b, t // tr),
        in_specs=[pl.BlockSpec((1, tr, width), lambda i, j: (i, j, 0)),
                  pl.BlockSpec((1, HALO, width), lambda i, j: (i, jnp.maximum(j * nh - 1, 0), 0)),
                  pl.BlockSpec((1, HALO, width), lambda i, j: (i, jnp.minimum((j + 1) * nh, nblk - 1), 0)),
                  pl.BlockSpec((CONV_K, width), lambda i, j: (0, 0)),
                  pl.BlockSpec((1, width), lambda i, j: (0, 0)),
                  pl.BlockSpec((tr, LANES), lambda i, j: (j, 0)),
                  pl.BlockSpec((tr, LANES), lambda i, j: (j, 0)),
                  pl.BlockSpec((1, tr, LANES), lambda i, j: (i, j, 0))],
        out_specs=[pl.BlockSpec((1, tr, width), lambda i, j: (i, j, 0)),
                   pl.BlockSpec((1, tr, LANES), lambda i, j: (i, j, 0)),
                   pl.BlockSpec((1, tr // L, LANES, L), lambda i, j: (i, j, 0, 0))],
        out_shape=[jax.ShapeDtypeStruct((b, t, width), BF16),
                   jax.ShapeDtypeStruct((b, t, LANES), F32),
                   jax.ShapeDtypeStruct((b, t // L, LANES, L), F32)],
        scratch_shapes=[pltpu.VMEM((tr + 2 * HALO, width), F32)],
        compiler_params=_cparams("parallel", "parallel"),
        name="qk_rope" if rope else "qk_ctx",
    )(zml, zml, zml, conv_w, conv_b.reshape(1, width), cos_t, sin_t, gates)


def _rope_tables(n):
    pos = jnp.arange(n)
    half = ML_HEAD_DIM // 4
    inv = ROPE_BASE ** (-jnp.arange(half, dtype=F32) / half)

    def axis_tables(p):
        ang = p.astype(F32)[:, None] * inv[None, :]
        c, s = jnp.cos(ang), jnp.sin(ang)
        return jnp.concatenate([c, c], axis=-1), jnp.concatenate([-s, s], axis=-1)

    cr, sr = axis_tables(pos // GRID_W)
    cc, sc = axis_tables(pos % GRID_W)
    return jnp.concatenate([cr, cc], axis=-1), jnp.concatenate([sr, sc], axis=-1)


def _mlstm_direction(d, qk, v, gcol, grow, cn_s, m_s):
    L = ML_CHUNK
    row = lax.broadcasted_iota(jnp.int32, (L, L), 0)
    col = lax.broadcasted_iota(jnp.int32, (L, L), 1)
    fwd = d == 0
    keep = (row >= col) if fwd else (row <= col)
    ones = jnp.ones((L, ML_HEAD_DIM), BF16)
    end = L - 1 if fwd else 0
    outs = []
    for h in range(ML_HEADS):
        ci = GATE_RAW + (2 * d) * ML_HEADS + h
        cf = (GATE_PREFIX if fwd else GATE_SUFFIX) + (2 * d + 1) * ML_HEADS + h
        bc = jnp.broadcast_to(gcol[:, cf:cf + 1], (L, L))
        ic = jnp.broadcast_to(gcol[:, ci:ci + 1], (L, L))
        br = grow[cf:cf + 1, :]
        ir = grow[ci:ci + 1, :]
        m_prev = m_s[d, h][0:1, :]
        dlog = jnp.where(keep, bc - br + ir, NEG)
        m_t = jnp.maximum(bc + m_prev, jnp.max(dlog, axis=1, keepdims=True))
        dw = jnp.exp(dlog - m_t)
        inter = jnp.exp(bc + m_prev - m_t)
        qh = qk[:, h * LANES:(h + 1) * LANES]
        kh = qk[:, ML_WIDTH + h * LANES:ML_WIDTH + (h + 1) * LANES]
        vp = jnp.concatenate([v[:, h * LANES:(h + 1) * LANES], ones], axis=1)
        s = lax.dot_general(qh, kh, (((1,), (1,)), ((), ())), preferred_element_type=F32) * dw
        cn = cn_s[d, h]
        a1 = jnp.dot(s.astype(BF16), vp, preferred_element_type=F32)
        a2 = jnp.dot(qh, cn.astype(BF16), preferred_element_type=F32)
        num = a1[:, :ML_HEAD_DIM] + inter * a2[:, :ML_HEAD_DIM]
        den = a1[:, ML_HEAD_DIM:] + inter * a2[:, ML_HEAD_DIM:]
        outs.append(num / jnp.maximum(jnp.abs(den), jnp.exp(-m_t)))
        b_end = jnp.broadcast_to(br[:, end:end + 1], (1, L))
        g_row = b_end - br + ir
        m_new = jnp.maximum(b_end + m_prev, jnp.max(g_row, axis=1, keepdims=True))
        decay = jnp.exp(b_end + m_prev - m_new)
        wgt = jnp.exp(b_end - bc + ic - m_new)
        kw = (kh.astype(F32) * wgt).astype(BF16)
        upd = lax.dot_general(kw, vp, (((0,), (0,)), ((), ())), preferred_element_type=F32)
        cn_s[d, h] = jnp.concatenate([decay, decay], axis=1) * cn + upd
        m_s[d, h] = jnp.broadcast_to(m_new, m_s.shape[2:])
    return jnp.concatenate(outs, axis=1)


def _mlstm_kernel(qkf_ref, vf_ref, gcf_ref, grf_ref, qkb_ref, vb_ref, gcb_ref, grb_ref, c0_ref, m0_ref, *rest, emit_h):
    if emit_h:
        hf_ref, hb_ref, ct_ref, mt_ref, cn_s, m_s = rest
    else:
        ct_ref, mt_ref, cn_s, m_s = rest
        hf_ref = hb_ref = None
    c = pl.program_id(1)
    L = ML_CHUNK

    @pl.when(c == 0)
    def _():
        cn_s[...] = c0_ref[0]
        m_s[...] = m0_ref[0]

    for d, refs, h_ref in ((0, (qkf_ref, vf_ref, gcf_ref, grf_ref), hf_ref), (1, (qkb_ref, vb_ref, gcb_ref, grb_ref), hb_ref)):
        qk_ref, v_ref, gc_ref, gr_ref = refs
        n_sub = qk_ref.shape[1] // L
        for s in (range(n_sub) if d == 0 else reversed(range(n_sub))):
            rows = slice(s * L, (s + 1) * L)
            h = _mlstm_direction(d, qk_ref[0, rows, :], v_ref[0, rows, :], gc_ref[0, rows, :], gr_ref[0, s], cn_s, m_s)
            if h_ref is not None:
                h_ref[0, rows, :] = h.astype(h_ref.dtype)

    @pl.when(c == pl.num_programs(1) - 1)
    def _():
        ct_ref[0] = cn_s[...]
        mt_ref[0] = m_s[...]


def _mlstm(qk, zml, gcol, grow, c0, m0, emit_h):
    b, t, _ = qk.shape
    n_sub = min(MLSTM_CHUNKS_PER_STEP, t // ML_CHUNK)
    R = n_sub * ML_CHUNK
    assert t % R == 0
    nc = t // R
    vcol = 2
    f_idx = lambda i, c: (i, c, 0)
    b_idx = lambda i, c: (i, nc - 1 - c, 0)
    st_c = pl.BlockSpec((1, 2, ML_HEADS, ML_HEAD_DIM, 2 * ML_HEAD_DIM), lambda i, c: (i, 0, 0, 0, 0))
    st_m = pl.BlockSpec((1, 2, ML_HEADS, 8, LANES), lambda i, c: (i, 0, 0, 0, 0))
    out_specs = [st_c, st_m]
    out_shape = [jax.ShapeDtypeStruct(c0.shape, F32), jax.ShapeDtypeStruct(m0.shape, F32)]
    if emit_h:
        out_specs = [pl.BlockSpec((1, R, ML_WIDTH), f_idx), pl.BlockSpec((1, R, ML_WIDTH), b_idx)] + out_specs
        out_shape = [jax.ShapeDtypeStruct((b, t, ML_WIDTH), BF16)] * 2 + out_shape
    return pl.pallas_call(
        functools.partial(_mlstm_kernel, emit_h=emit_h),
        grid=(b, nc),
        in_specs=[pl.BlockSpec((1, R, 2 * ML_WIDTH), f_idx),
                  pl.BlockSpec((1, R, ML_WIDTH), lambda i, c: (i, c, vcol)),
                  pl.BlockSpec((1, R, LANES), f_idx),
                  pl.BlockSpec((1, n_sub, LANES, ML_CHUNK), lambda i, c: (i, c, 0, 0)),
                  pl.BlockSpec((1, R, 2 * ML_WIDTH), b_idx),
                  pl.BlockSpec((1, R, ML_WIDTH), lambda i, c: (i, nc - 1 - c, vcol)),
                  pl.BlockSpec((1, R, LANES), b_idx),
                  pl.BlockSpec((1, n_sub, LANES, ML_CHUNK), lambda i, c: (i, nc - 1 - c, 0, 0)),
                  st_c, st_m],
        out_specs=out_specs,
        out_shape=out_shape,
        scratch_shapes=[pltpu.VMEM((2, ML_HEADS, ML_HEAD_DIM, 2 * ML_HEAD_DIM), F32),
                        pltpu.VMEM((2, ML_HEADS, 8, LANES), F32)],
        compiler_params=_cparams("parallel", "arbitrary"),
        name="mlstm" if emit_h else "mlstm_ctx",
    )(qk, zml, gcol, grow, qk, zml, gcol, grow, c0, m0)


def _lane_in(shape, start, width):
    lane = lax.broadcasted_iota(jnp.int32, shape, 1)
    return (lane >= start) & (lane < start + width)


def _na_kernel(q_ref, k0_ref, k1_ref, k2_ref, v0_ref, v1_ref, v2_ref, kc_ref, vc_ref, bias_ref, o_ref):
    nq = q_ref.shape[1]
    lane = lax.broadcasted_iota(jnp.int32, (nq, LANES), 1)
    nt = (((1,), (1,)), ((), ()))
    for p in range(NA_HEADS // 2):
        sl = slice(p * LANES, (p + 1) * LANES)
        q2 = q_ref[0, :, sl] * (NA_HEAD_DIM ** -0.5)
        kwin = jnp.concatenate([k0_ref[0, :, sl], k1_ref[0, :, sl], k2_ref[0, :, sl]], axis=0)
        vwin = jnp.concatenate([v0_ref[0, :, sl], v1_ref[0, :, sl], v2_ref[0, :, sl]], axis=0)
        kc = kc_ref[0, :, sl]
        vc = vc_ref[0, :, sl]
        halves = []
        for a in range(2):
            in_head = (lane >= a * NA_HEAD_DIM) & (lane < (a + 1) * NA_HEAD_DIM)
            qm = jnp.where(in_head, q2, jnp.zeros_like(q2))
            s_win = lax.dot_general(qm, kwin, nt, preferred_element_type=F32) + bias_ref[0, 2 * p + a]
            s_ctx = lax.dot_general(qm, kc, nt, preferred_element_type=F32)
            m = jnp.maximum(jnp.max(s_win, axis=1, keepdims=True), jnp.max(s_ctx, axis=1, keepdims=True))
            p_win = jnp.exp(s_win - m).astype(BF16)
            p_ctx = jnp.exp(s_ctx - m).astype(BF16)
            vw = jnp.where(_lane_in(vwin.shape, a * NA_HEAD_DIM, NA_HEAD_DIM), vwin, jnp.ones_like(vwin))
            vx = jnp.where(_lane_in(vc.shape, a * NA_HEAD_DIM, NA_HEAD_DIM), vc, jnp.ones_like(vc))
            o = jnp.dot(p_win, vw, preferred_element_type=F32) + jnp.dot(p_ctx, vx, preferred_element_type=F32)
            halves.append(o / pltpu.roll(o, NA_HEAD_DIM, 1))
        o_ref[0, :, sl] = jnp.where(lane < NA_HEAD_DIM, halves[0], halves[1]).astype(o_ref.dtype)


def _na_bias_tables(rpb, rows):
    R = NA_ROWS_PER_STEP
    nblk = rows // R
    kr = NA_WIN_ROWS
    cq = np.arange(GRID_W)
    cstart = np.clip(cq - NA_WIN_COLS // 2, 0, GRID_W - NA_WIN_COLS)
    ck = np.arange(GRID_W)
    col_ok = (ck[None, :] >= cstart[:, None]) & (ck[None, :] < cstart[:, None] + NA_WIN_COLS)
    col_off = np.where(col_ok, ck[None, :] - cq[:, None] + NA_WIN_COLS - 1, 0)
    row_ok = np.zeros((3, R, 3 * R), bool)
    row_off = np.zeros((3, R, 3 * R), np.int64)
    for vi, j in enumerate((0, 1, nblk - 1)):
        for i in range(R):
            r = j * R + i
            r0 = min(max(r - kr // 2, 0), rows - kr)
            for t in range(3):
                jb = j - 1 + t
                if jb < 0 or jb >= nblk:
                    continue
                for rr in range(R):
                    krow = jb * R + rr
                    if r0 <= krow < r0 + kr:
                        row_ok[vi, i, t * R + rr] = True
                        row_off[vi, i, t * R + rr] = krow - r + NA_WIN_ROWS - 1
    col_sel = (np.arange(rpb.shape[2])[None, None, :] == col_off[:, :, None]) & col_ok[:, :, None]
    row_sel = (np.arange(rpb.shape[1])[None, None, None, :] == row_off[..., None]) & row_ok[..., None]
    by_col = jnp.einsum('hrc,qkc->hrqk', rpb, jnp.asarray(col_sel, F32), precision=HIGHEST)
    vals = jnp.einsum('vixr,hrqk->vhiqxk', jnp.asarray(row_sel, F32), by_col, precision=HIGHEST)
    ok = row_ok[:, None, :, None, :, None] & col_ok[None, None, None, :, None, :]
    vals = jnp.where(jnp.asarray(ok), vals, NEG)
    return vals.reshape(3, NA_HEADS, R * GRID_W, 3 * R * GRID_W)


def _na(zna, zcna, bias, rows):
    b, n, _ = zna.shape
    ctx = zcna.shape[1]
    R = NA_ROWS_PER_STEP
    nq = R * GRID_W
    nblk = rows // R
    kb = lambda col, off: pl.BlockSpec(
        (1, nq, NA_WIDTH), lambda i, j: (i, jnp.clip(j + off, 0, nblk - 1), col))
    variant = lambda i, j: (jnp.where(j == 0, 0, jnp.where(j == nblk - 1, 2, 1)), 0, 0, 0)
    return pl.pallas_call(
        _na_kernel,
        grid=(b, nblk),
        in_specs=[pl.BlockSpec((1, nq, NA_WIDTH), lambda i, j: (i, j, 0)),
                  kb(1, -1), kb(1, 0), kb(1, 1), kb(2, -1), kb(2, 0), kb(2, 1),
                  pl.BlockSpec((1, ctx, NA_WIDTH), lambda i, j: (i, 0, 1)),
                  pl.BlockSpec((1, ctx, NA_WIDTH), lambda i, j: (i, 0, 2)),
                  pl.BlockSpec((1, NA_HEADS, nq, 3 * nq), variant)],
        out_specs=pl.BlockSpec((1, nq, NA_WIDTH), lambda i, j: (i, j, 0)),
        out_shape=jax.ShapeDtypeStruct((b, n, NA_WIDTH), BF16),
        compiler_params=_cparams("parallel", "parallel"),
        name="na",
    )(zna, zna, zna, zna, zna, zna, zna, zcna, zcna, bias)


def _mix_kernel(na_ref, hf_ref, hb_ref, o_ref, x_ref, g1_ref, sh2_ref, sc2_ref, wo_ref, mg_ref, l1g_ref, l1b_ref,
                wr_ref, br_ref, xmid_ref, rt_ref, rw_ref, cnt_ref, carry_s, *, alpha):
    first = (pl.program_id(0) == 0) & (pl.program_id(1) == 0)

    @pl.when(first)
    def _():
        carry_s[...] = jnp.zeros_like(carry_s)

    tm = x_ref.shape[1]
    h = hf_ref[0].astype(F32) + hb_ref[0].astype(F32)
    parts = []
    for hd in range(ML_HEADS):
        hh = h[:, hd * LANES:(hd + 1) * LANES]
        parts.append(hh * lax.rsqrt(jnp.mean(hh * hh, axis=-1, keepdims=True) + LN_EPS))
    hn = jnp.concatenate(parts, axis=1)
    ml = (hn * mg_ref[...] * jax.nn.sigmoid(o_ref[0].astype(F32))).astype(BF16)
    mix = (jnp.dot(na_ref[0], wo_ref[0:NA_WIDTH, :], preferred_element_type=F32)
           + jnp.dot(ml, wo_ref[NA_WIDTH:, :], preferred_element_type=F32))
    xmid = _ln_rows(alpha * x_ref[0] + g1_ref[0] * mix) * l1g_ref[...] + l1b_ref[...]
    xmid_ref[0] = xmid

    xt = (_ln_rows(xmid) * (1.0 + sc2_ref[0]) + sh2_ref[0]).astype(BF16)
    logits = jnp.dot(xt, wr_ref[...], preferred_element_type=F32) + br_ref[...]
    lane = lax.broadcasted_iota(jnp.int32, (tm, LANES), 1)
    is_g = lane < N_GROUPS
    gl = jnp.where(is_g, logits, NEG)
    gmax = jnp.max(gl, axis=1, keepdims=True)
    grp = jnp.min(jnp.where(gl == gmax, lane, LANES), axis=1, keepdims=True)
    gsum = jnp.sum(jnp.where(is_g, jnp.exp(gl - gmax), 0.0), axis=1, keepdims=True)
    grp_w = 1.0 / gsum
    lo = N_GROUPS + EXPERTS_PER_GROUP * grp
    el = jnp.where((lane >= lo) & (lane < lo + EXPERTS_PER_GROUP), logits, NEG)
    t1 = jnp.max(el, axis=1, keepdims=True)
    i1 = jnp.min(jnp.where(el == t1, lane, LANES), axis=1, keepdims=True)
    el2 = jnp.where(lane == i1, NEG, el)
    t2 = jnp.max(el2, axis=1, keepdims=True)
    i2 = jnp.min(jnp.where(el2 == t2, lane, LANES), axis=1, keepdims=True)
    e21 = jnp.exp(t2 - t1)
    w0 = grp_w / (1.0 + e21)
    w1 = grp_w * e21 / (1.0 + e21)

    hit1 = lane == i1
    hit2 = lane == i2
    onehot = (hit1 | hit2).astype(BF16)
    r_i = lax.broadcasted_iota(jnp.int32, (tm, tm), 0)
    c_i = lax.broadcasted_iota(jnp.int32, (tm, tm), 1)
    before = (r_i > c_i).astype(BF16)
    prefix = jnp.dot(before, onehot, preferred_element_type=F32) + carry_s[0:1, :]
    rank0 = jnp.sum(jnp.where(hit1, prefix, 0.0), axis=1, keepdims=True)
    rank1 = jnp.sum(jnp.where(hit2, prefix, 0.0), axis=1, keepdims=True)
    total = carry_s[0:1, :] + jnp.sum(onehot.astype(F32), axis=0, keepdims=True)
    carry_s[...] = jnp.broadcast_to(total, carry_s.shape)
    cnt_ref[...] = jnp.broadcast_to(total, cnt_ref.shape)

    rf = jnp.where(lane == 0, (i1 - N_GROUPS).astype(F32),
                   jnp.where(lane == 1, (i2 - N_GROUPS).astype(F32),
                             jnp.where(lane == 2, rank0, jnp.where(lane == 3, rank1, 0.0))))
    rt_ref[0] = rf.T[0:SUBLANES, :].astype(jnp.int32)
    rw_ref[0] = jnp.where(lane == 0, w0, jnp.where(lane == 1, w1, 0.0))


def _mix(na, hf, hb, zml, x, g1, sh2, sc2, wo, mg, l1g, l1b, wr, br, alpha, tm):
    b, n, d = x.shape
    row = lambda w: pl.BlockSpec((1, tm, w), lambda i, j: (i, j, 0))
    vec = pl.BlockSpec((1, 1, d), lambda i, j: (i, 0, 0))
    full = lambda a: pl.BlockSpec(a.shape, lambda i, j: (0,) * a.ndim)
    ocol = 3
    return pl.pallas_call(
        functools.partial(_mix_kernel, alpha=alpha),
        grid=(b, n // tm),
        in_specs=[row(NA_WIDTH), row(ML_WIDTH), row(ML_WIDTH),
                  pl.BlockSpec((1, tm, ML_WIDTH), lambda i, j: (i, j, ocol)),
                  row(d), vec, vec, vec, full(wo), full(mg), full(l1g), full(l1b), full(wr), full(br)],
        out_specs=[row(d), pl.BlockSpec((1, SUBLANES, tm), lambda i, j: (i * (n // tm) + j, 0, 0)), row(LANES),
                   pl.BlockSpec((SUBLANES, LANES), lambda i, j: (0, 0))],
        out_shape=[jax.ShapeDtypeStruct((b, n, d), F32),
                   jax.ShapeDtypeStruct((b * (n // tm), SUBLANES, tm), jnp.int32),
                   jax.ShapeDtypeStruct((b, n, LANES), F32),
                   jax.ShapeDtypeStruct((SUBLANES, LANES), F32)],
        scratch_shapes=[pltpu.VMEM((8, LANES), F32)],
        compiler_params=_cparams("arbitrary", "arbitrary"),
        name="mix",
    )(na, hf, hb, zml, x, g1, sh2, sc2, wo, mg, l1g, l1b, wr, br)


def _zero_fill_padding(pad_base_ref, pad_len_ref, nused_ref, xs_ref, zero_s, sem, wait):
    zero_t, xs_t = _token_view(zero_s), _token_view(xs_ref)
    tb = zero_t.shape[0]

    def run(copy):
        copy.wait() if wait else copy.start()

    def fill(off, nrows):
        run(pltpu.make_async_copy(zero_t.at[pl.ds(0, nrows)], xs_t.at[pl.ds(off, nrows)], sem))

    def per_expert(e, _):
        plen = pad_len_ref[e]
        base = pad_base_ref[e]
        bit = tb // 2
        while bit >= 1:
            off = base + (plen & ~(2 * bit - 1))

            @pl.when((plen & bit) != 0)
            def _(bit=bit, off=off):
                fill(off, bit)

            bit //= 2
        return 0

    lax.fori_loop(0, N_EXPERTS, per_expert, 0)

    def per_block(i, _):
        fill(i * tb, tb)
        return 0

    lax.fori_loop(nused_ref[0], xs_t.shape[0] // tb, per_block, 0)


ROW_WORDS = 4


def _token_view(ref):
    return ref.reshape(ref.shape[0] // ROW_WORDS, ROW_WORDS, LANES)


def _bf16_bits(v):
    return lax.bitcast_convert_type(v.astype(BF16).astype(F32), jnp.int32)


def _store_token_rows(dst_ref, base, val):
    tm, d = val.shape
    assert d == 2 * ROW_WORDS * LANES
    for s in range(ROW_WORDS):
        lo = _bf16_bits(val[:, s * LANES:(s + 1) * LANES])
        hi = _bf16_bits(val[:, (s + ROW_WORDS) * LANES:(s + ROW_WORDS + 1) * LANES])
        dst_ref[pl.ds(base * ROW_WORDS + s, tm, stride=ROW_WORDS), :] = hi | lax.shift_right_logical(lo, 16)


def _load_token_rows(src_ref, base, tm, dtype):
    words = [src_ref[pl.ds(base * ROW_WORDS + s, tm, stride=ROW_WORDS), :] for s in range(ROW_WORDS)]
    lo = [lax.bitcast_convert_type(w << 16, F32).astype(dtype) for w in words]
    hi = [lax.bitcast_convert_type(w & -65536, F32).astype(dtype) for w in words]
    return jnp.concatenate(lo + hi, axis=1)


def _dispatch_kernel(pad_base_ref, pad_len_ref, nused_ref, dest_ref, xmid_ref, sh2_ref, sc2_ref, xs_ref,
                     xt_s, zero_s, sem, zsem):
    tm = xmid_ref.shape[1]
    step = pl.program_id(0) * pl.num_programs(1) + pl.program_id(1)
    nsteps = pl.num_programs(0) * pl.num_programs(1)

    def wait_step_copies():
        for _ in range(2):
            pltpu.make_async_copy(_token_view(xt_s), _token_view(xs_ref).at[pl.ds(0, tm)], sem).wait()

    @pl.when(step == 0)
    def _():
        zero_s[...] = jnp.zeros_like(zero_s)
        _zero_fill_padding(pad_base_ref, pad_len_ref, nused_ref, xs_ref, zero_s, zsem, False)

    xt = _ln_rows(xmid_ref[0]) * (1.0 + sc2_ref[0]) + sh2_ref[0]

    @pl.when(step > 0)
    def _():
        wait_step_copies()

    _store_token_rows(xt_s, 0, xt)

    def start(r, _):
        for k in range(2):
            pltpu.make_async_copy(_token_view(xt_s).at[r], _token_view(xs_ref).at[dest_ref[0, k, r]], sem).start()
        return 0

    lax.fori_loop(0, tm, start, 0, unroll=8)

    @pl.when(step == 0)
    def _():
        _zero_fill_padding(pad_base_ref, pad_len_ref, nused_ref, xs_ref, zero_s, zsem, True)

    @pl.when(step == nsteps - 1)
    def _():
        wait_step_copies()


def _dispatch(xmid, sh2, sc2, dest, pad_base, pad_len, nused, cap, tm):
    b, n, d = xmid.shape
    assert d == 2 * ROW_WORDS * LANES
    nt = n // tm
    vec = pl.BlockSpec((1, 1, d), lambda i, j, *_: (i, 0, 0))
    return pl.pallas_call(
        _dispatch_kernel,
        grid_spec=pltpu.PrefetchScalarGridSpec(
            num_scalar_prefetch=3,
            grid=(b, nt),
            in_specs=[pl.BlockSpec((1, 2, tm), lambda i, j, *_: (i * nt + j, 0, 0), memory_space=pltpu.SMEM),
                      pl.BlockSpec((1, tm, d), lambda i, j, *_: (i, j, 0)), vec, vec],
            out_specs=pl.BlockSpec(memory_space=pl.ANY),
            scratch_shapes=[pltpu.VMEM((tm * ROW_WORDS, LANES), ROW_DTYPE),
                            pltpu.VMEM((EXPERT_ROWS * ROW_WORDS, LANES), ROW_DTYPE),
                            pltpu.SemaphoreType.DMA, pltpu.SemaphoreType.DMA]),
        out_shape=jax.ShapeDtypeStruct((cap * ROW_WORDS, LANES), ROW_DTYPE),
        compiler_params=_cparams("arbitrary", "arbitrary"),
        name="dispatch",
    )(pad_base, pad_len, nused, dest, xmid, sh2, sc2)


def _expert_kernel(be_ref, nv_ref, xs_ref, w1_ref, w3_ref, w2_ref, ys_ref, w1b, w3b, w2b):
    tb = xs_ref.shape[0] // ROW_WORDS
    i = pl.program_id(0)
    e = be_ref[i]
    changed = (i == 0) | (be_ref[jnp.maximum(i - 1, 0)] != e)

    @pl.when(changed)
    def _():
        w1b[...] = w1_ref[0].astype(BF16)
        w3b[...] = w3_ref[0].astype(BF16)
        w2b[...] = w2_ref[0].astype(BF16)

    nv = nv_ref[i]

    @pl.when(nv > 0)
    def _():
        xb = _load_token_rows(xs_ref, 0, tb, BF16)
        h1 = jnp.dot(xb, w1b[...], preferred_element_type=F32)
        h3 = jnp.dot(xb, w3b[...], preferred_element_type=F32)
        a = (_silu(h1) * h3).astype(BF16)
        _store_token_rows(ys_ref, 0, jnp.dot(a, w2b[...], preferred_element_type=F32))

    @pl.when(nv == 0)
    def _():
        ys_ref[...] = jnp.zeros_like(ys_ref)


def _experts(xs, block_e, block_nv, w1, w3, w2):
    d, hid = w1.shape[1], w1.shape[2]
    tb = EXPERT_ROWS
    rows = pl.BlockSpec((tb * ROW_WORDS, LANES), lambda i, be, nv: (i, 0))
    return pl.pallas_call(
        _expert_kernel,
        grid_spec=pltpu.PrefetchScalarGridSpec(
            num_scalar_prefetch=2,
            grid=(xs.shape[0] // (tb * ROW_WORDS),),
            in_specs=[rows,
                      pl.BlockSpec((1, d, hid), lambda i, be, nv: (be[i], 0, 0)),
                      pl.BlockSpec((1, d, hid), lambda i, be, nv: (be[i], 0, 0)),
                      pl.BlockSpec((1, hid, d), lambda i, be, nv: (be[i], 0, 0))],
            out_specs=rows,
            scratch_shapes=[pltpu.VMEM((d, hid), BF16), pltpu.VMEM((d, hid), BF16), pltpu.VMEM((hid, d), BF16)]),
        out_shape=jax.ShapeDtypeStruct(xs.shape, xs.dtype),
        compiler_params=_cparams("arbitrary"),
        name="experts",
    )(block_e, block_nv, xs, w1, w3, w2)


def _combine_kernel(dcur_ref, dnext_ref, xmid_ref, rw_ref, g2_ref, l2g_ref, l2b_ref, ys_ref, o_ref,
                    y0_s, y1_s, sem, *, alpha):
    tm = xmid_ref.shape[1]
    step = pl.program_id(0) * pl.num_programs(1) + pl.program_id(1)
    nsteps = pl.num_programs(0) * pl.num_programs(1)
    slot = step % 2

    def gather(dest_ref, into):
        def start(r, _):
            for k, buf in ((0, y0_s), (1, y1_s)):
                pltpu.make_async_copy(_token_view(ys_ref).at[dest_ref[0, k, r]],
                                      _token_view(buf).at[into * tm + r], sem.at[into]).start()
            return 0

        lax.fori_loop(0, tm, start, 0, unroll=8)

    @pl.when(step == 0)
    def _():
        gather(dcur_ref, 0)

    @pl.when(step + 1 < nsteps)
    def _():
        gather(dnext_ref, 1 - slot)

    for buf in (y0_s, y1_s):
        pltpu.make_async_copy(_token_view(ys_ref).at[pl.ds(0, tm)], _token_view(buf).at[pl.ds(slot * tm, tm)],
                              sem.at[slot]).wait()
    base = slot * tm
    rw = rw_ref[0]
    moe = (rw[:, 0:1] * _load_token_rows(y0_s, base, tm, F32) + rw[:, 1:2] * _load_token_rows(y1_s, base, tm, F32))
    o_ref[0] = _ln_rows(alpha * xmid_ref[0] + g2_ref[0] * moe) * l2g_ref[...] + l2b_ref[...]


def _combine(xmid, rw, g2, l2g, l2b, ys, dest, alpha, tm):
    b, n, d = xmid.shape
    nt = n // tm
    full = lambda a: pl.BlockSpec(a.shape, lambda i, j: (0,) * a.ndim)
    return pl.pallas_call(
        functools.partial(_combine_kernel, alpha=alpha),
        grid=(b, nt),
        in_specs=[pl.BlockSpec((1, 2, tm), lambda i, j: (i * nt + j, 0, 0), memory_space=pltpu.SMEM),
                  pl.BlockSpec((1, 2, tm), lambda i, j: (jnp.minimum(i * nt + j + 1, b * nt - 1), 0, 0),
                               memory_space=pltpu.SMEM),
                  pl.BlockSpec((1, tm, d), lambda i, j: (i, j, 0)),
                  pl.BlockSpec((1, tm, LANES), lambda i, j: (i, j, 0)),
                  pl.BlockSpec((1, 1, d), lambda i, j: (i, 0, 0)),
                  full(l2g), full(l2b),
                  pl.BlockSpec(memory_space=pl.ANY)],
        out_specs=pl.BlockSpec((1, tm, d), lambda i, j: (i, j, 0)),
        out_shape=jax.ShapeDtypeStruct((b, n, d), F32),
        scratch_shapes=[pltpu.VMEM((2 * tm * ROW_WORDS, LANES), ys.dtype),
                        pltpu.VMEM((2 * tm * ROW_WORDS, LANES), ys.dtype), pltpu.SemaphoreType.DMA((2,))],
        compiler_params=_cparams("arbitrary", "arbitrary"),
        name="combine",
    )(dest, dest, xmid, rw, g2, l2g, l2b, ys)


def _tile(n, want):
    t = min(n, want)
    assert n % t == 0, (n, t)
    return t


def kernel(x, c, ctx, c_ctx, w_ada, b_ada, w_in, conv_w, conv_b, gate_b, rpb, ml_norm_g, w_out, ln1_g, ln1_b,
           w_router_g, b_router_g, w_router_e, b_router_e, w1, w3, w2, ln2_g, ln2_b):
    B, N, D = x.shape
    T_CTX = ctx.shape[1]
    depth = w_ada.shape[0]
    rows = N // GRID_W
    assert depth == 1 and N % GRID_W == 0 and rows % NA_ROWS_PER_STEP == 0 and rows >= 3 * NA_ROWS_PER_STEP
    assert N % ML_CHUNK == 0 and T_CTX % ML_CHUNK == 0
    alpha = (2.0 * depth) ** 0.25
    l = 0

    pad_rows = -(B + 1) % 8
    cvec = jnp.concatenate([c, c_ctx[None], jnp.zeros((pad_rows, D), F32)], axis=0)
    ada = _ada(cvec, w_ada[l], b_ada[l])
    sh1, sc1, g1, sh2, sc2, g2 = [a[:, None, :] for a in jnp.split(ada[:B], 6, axis=-1)]
    csh1, csc1 = [jnp.broadcast_to(a[None], (B, 1, D)) for a in jnp.split(ada[B:B + 1], 6, axis=-1)[:2]]

    col_ml = 3 * NA_WIDTH
    col_g = col_ml + 4 * ML_WIDTH
    wb = w_in[l].astype(BF16)
    wna, wml = wb[:, :col_ml], wb[:, col_ml:col_g]
    n_gate = 4 * ML_HEADS
    wg = jnp.pad(wb[:, col_g:], ((0, 0), (0, LANES - n_gate)))
    gb = jnp.pad(gate_b[l], (0, LANES - n_gate)).reshape(1, LANES)
    zna, zml, gates = _inproj(x, sh1, sc1, wna, wml, wg, gb, _tile(N, 512))
    zcna, zcml, gates_c = _inproj(ctx, csh1, csc1, wna, wml, wg, gb, _tile(T_CTX, 256))

    cos_t, sin_t = _rope_tables(N)
    qk_c, gcol_c, grow_c = _qk_streams(zcml, gates_c, conv_w[l], conv_b[l], cos_t[:T_CTX], sin_t[:T_CTX], False,
                                       _tile(T_CTX, 512))
    qk_l, gcol_l, grow_l = _qk_streams(zml, gates, conv_w[l], conv_b[l], cos_t, sin_t, True, _tile(N, 512))
    c0 = jnp.zeros((B, 2, ML_HEADS, ML_HEAD_DIM, 2 * ML_HEAD_DIM), F32)
    m0 = jnp.zeros((B, 2, ML_HEADS, 8, LANES), F32)
    c_ctx_end, m_ctx_end = _mlstm(qk_c, zcml, gcol_c, grow_c, c0, m0, False)
    hf, hb, _, _ = _mlstm(qk_l, zml, gcol_l, grow_l, c_ctx_end, m_ctx_end, True)

    na = _na(zna, zcna, _na_bias_tables(rpb[l], rows), rows)

    wr = jnp.pad(jnp.concatenate([w_router_g[l], w_router_e[l]], axis=1),
                 ((0, 0), (0, LANES - N_GROUPS - N_EXPERTS))).astype(BF16)
    br = jnp.pad(jnp.concatenate([b_router_g[l], b_router_e[l]]), (0, LANES - N_GROUPS - N_EXPERTS)).reshape(1, LANES)
    tm = _tile(N, 512)
    xmid, rt, rw, counts = _mix(na, hf, hb, zml, x, g1, sh2, sc2, w_out[l].astype(BF16),
                                ml_norm_g[l].reshape(1, ML_WIDTH), ln1_g[l].reshape(1, D), ln1_b[l].reshape(1, D),
                                wr, br, alpha, tm)

    tb = EXPERT_ROWS
    n_assign = 2 * B * N
    cap = -(-n_assign // tb) * tb + N_EXPERTS * tb
    sizes = counts[0, N_GROUPS:N_GROUPS + N_EXPERTS].astype(jnp.int32)
    padded = (sizes + tb - 1) // tb * tb
    pend = jnp.cumsum(padded)
    pstart = pend - padded
    experts = jnp.arange(N_EXPERTS, dtype=jnp.int32)
    first_row = jnp.sum(jnp.where(rt[:, 0:2, :, None] == experts, pstart, 0), axis=-1)
    dest = first_row + rt[:, 2:4, :]
    blk0 = jnp.arange(cap // tb, dtype=jnp.int32) * tb
    block_e = jnp.minimum(jnp.sum(pend[None, :] <= blk0[:, None], axis=1), N_EXPERTS - 1).astype(jnp.int32)
    is_e = block_e[:, None] == experts
    block_nv = jnp.clip(jnp.sum(jnp.where(is_e, pstart + sizes, 0), axis=1) - blk0, 0, tb).astype(jnp.int32)

    nused = (pend[-1:] // tb).astype(jnp.int32)
    xs = _dispatch(xmid, sh2, sc2, dest, pstart + sizes, padded - sizes, nused, cap, tm)
    ys = _experts(xs, block_e, block_nv, w1[l], w3[l], w2[l])
    return _combine(xmid, rw, g2, ln2_g[l].reshape(1, D), ln2_b[l].reshape(1, D), ys, dest, alpha, tm)
```

```python
import functools

import numpy as np
import jax
import jax.numpy as jnp
from jax import lax
from jax.experimental import pallas as pl
from jax.experimental.pallas import tpu as pltpu

F32 = jnp.float32
BF16 = jnp.bfloat16
ROW_DTYPE = jnp.int32
HIGHEST = lax.Precision.HIGHEST

GRID_W = 64
NA_HEADS = 8
NA_HEAD_DIM = 64
NA_WIDTH = NA_HEADS * NA_HEAD_DIM
NA_WIN_ROWS = 8
NA_WIN_COLS = 16
ML_HEADS = 4
ML_HEAD_DIM = 128
ML_WIDTH = ML_HEADS * ML_HEAD_DIM
ML_CHUNK = 128
CONV_K = 5
N_GROUPS = 8
EXPERTS_PER_GROUP = 8
N_EXPERTS = N_GROUPS * EXPERTS_PER_GROUP
ROPE_BASE = 10000.0
LN_EPS = 1e-5

LANES = 128
SUBLANES = 8
VMEM_LIMIT = 56 * 1024 * 1024

NA_ROWS_PER_STEP = 4
EXPERT_ROWS = 512
MLSTM_CHUNKS_PER_STEP = 4
NEG = -1e30


def _cparams(*sem):
    return pltpu.CompilerParams(dimension_semantics=sem, vmem_limit_bytes=VMEM_LIMIT)


def _silu(v):
    return v * jax.nn.sigmoid(v)


def _ln_rows(v):
    mu = jnp.mean(v, axis=-1, keepdims=True)
    vc = v - mu
    var = jnp.mean(vc * vc, axis=-1, keepdims=True)
    return vc * lax.rsqrt(var + LN_EPS)


def _ada_kernel(c_ref, w_ref, b_ref, o_ref):
    o_ref[...] = jnp.dot(_silu(c_ref[...]), w_ref[...], preferred_element_type=F32,
                         precision=HIGHEST) + b_ref[...]


def _ada(cvec, w, b):
    rows, d = cvec.shape
    cols = w.shape[1]
    tn = 1024
    return pl.pallas_call(
        _ada_kernel,
        grid=(cols // tn,),
        in_specs=[pl.BlockSpec((rows, d), lambda j: (0, 0)),
                  pl.BlockSpec((d, tn), lambda j: (0, j)),
                  pl.BlockSpec((1, tn), lambda j: (0, j))],
        out_specs=pl.BlockSpec((rows, tn), lambda j: (0, j)),
        out_shape=jax.ShapeDtypeStruct((rows, cols), F32),
        compiler_params=_cparams("arbitrary"),
        name="ada",
    )(cvec, w, b.reshape(1, cols))


def _inproj_kernel(x_ref, sh_ref, sc_ref, wna_ref, wml_ref, wg_ref, gb_ref, zna_ref, zml_ref, g_ref):
    y = _ln_rows(x_ref[0]) * (1.0 + sc_ref[0]) + sh_ref[0]
    yb = y.astype(BF16)
    zna_ref[0] = jnp.dot(yb, wna_ref[...], preferred_element_type=F32).astype(BF16)
    zml_ref[0] = jnp.dot(yb, wml_ref[...], preferred_element_type=F32).astype(BF16)
    g_ref[0] = jnp.dot(yb, wg_ref[...], preferred_element_type=F32) + gb_ref[...]


def _inproj(x, shift, scale, wna, wml, wg, gb, tm):
    b, t, d = x.shape
    full = lambda a: pl.BlockSpec(a.shape, lambda i, j: (0,) * a.ndim)
    row = lambda w: pl.BlockSpec((1, tm, w), lambda i, j: (i, j, 0))
    vec = pl.BlockSpec((1, 1, d), lambda i, j: (i, 0, 0))
    return pl.pallas_call(
        _inproj_kernel,
        grid=(b, t // tm),
        in_specs=[row(d), vec, vec, full(wna), full(wml), full(wg), full(gb)],
        out_specs=[row(wna.shape[1]), row(wml.shape[1]), row(LANES)],
        out_shape=[jax.ShapeDtypeStruct((b, t, wna.shape[1]), BF16),
                   jax.ShapeDtypeStruct((b, t, wml.shape[1]), BF16),
                   jax.ShapeDtypeStruct((b, t, LANES), F32)],
        compiler_params=_cparams("parallel", "parallel"),
        name="inproj",
    )(x, shift, scale, wna, wml, wg, gb)


HALO = 16


def _log_sigmoid(v):
    return jnp.minimum(v, 0.0) - jnp.log1p(jnp.exp(-jnp.abs(v)))


GATE_RAW, GATE_PREFIX, GATE_SUFFIX = 0, 16, 32


def _pack_gates(g):
    L = g.shape[0]
    row = lax.broadcasted_iota(jnp.int32, (L, L), 0)
    col = lax.broadcasted_iota(jnp.int32, (L, L), 1)
    lane = lax.broadcasted_iota(jnp.int32, g.shape, 1)
    lf = _log_sigmoid(g)
    prefix = jnp.dot((row >= col).astype(F32), lf, preferred_element_type=F32, precision=HIGHEST)
    suffix = jnp.dot((row <= col).astype(F32), lf, preferred_element_type=F32, precision=HIGHEST)
    return jnp.where(lane < GATE_PREFIX, g,
                     jnp.where(lane < GATE_SUFFIX, pltpu.roll(prefix, GATE_PREFIX, 1),
                               jnp.where(lane < GATE_SUFFIX + GATE_PREFIX, pltpu.roll(suffix, GATE_SUFFIX, 1), 0.0)))


def _qk_kernel(cur_ref, prev_ref, next_ref, cw_ref, cb_ref, cos_ref, sin_ref, g_ref, o_ref, gcol_ref, grow_ref,
               pad_ref, *, rope, tr):
    for c in range(tr // ML_CHUNK):
        packed = _pack_gates(g_ref[0, c * ML_CHUNK:(c + 1) * ML_CHUNK, :])
        gcol_ref[0, c * ML_CHUNK:(c + 1) * ML_CHUNK, :] = packed
        grow_ref[0, c] = packed.T
    i = pl.program_id(1)
    last = pl.num_programs(1) - 1
    pad = CONV_K // 2
    pad_ref[HALO:HALO + tr, :] = cur_ref[0].astype(F32)
    pad_ref[0:HALO, :] = jnp.where(i > 0, prev_ref[0].astype(F32), 0.0)
    pad_ref[HALO + tr:2 * HALO + tr, :] = jnp.where(i < last, next_ref[0].astype(F32), 0.0)
    acc = cw_ref[0:1, :] * pad_ref[HALO - pad:HALO - pad + tr, :] + cb_ref[...]
    for j in range(1, CONV_K):
        acc = acc + cw_ref[j:j + 1, :] * pad_ref[HALO - pad + j:HALO - pad + j + tr, :]
    u = _silu(acc)
    width = u.shape[1]
    lane = lax.broadcasted_iota(jnp.int32, (tr, LANES), 1)
    first_half = (lane % (ML_HEAD_DIM // 2)) < (ML_HEAD_DIM // 4)
    kscale = ML_HEAD_DIM ** -0.5
    for g in range(width // LANES):
        ug = u[:, g * LANES:(g + 1) * LANES]
        if rope:
            partner = jnp.where(first_half, pltpu.roll(ug, LANES - ML_HEAD_DIM // 4, 1),
                                pltpu.roll(ug, ML_HEAD_DIM // 4, 1))
            ug = ug * cos_ref[...] + partner * sin_ref[...]
        if g >= ML_HEADS:
            ug = ug * kscale
        o_ref[0, :, g * LANES:(g + 1) * LANES] = ug.astype(BF16)


def _qk_streams(zml, gates, conv_w, conv_b, cos_t, sin_t, rope, tr):
    b, t, _ = zml.shape
    width = 2 * ML_WIDTH
    nh = tr // HALO
    nblk = t // HALO
    L = ML_CHUNK
    return pl.pallas_call(
        functools.partial(_qk_kernel, rope=rope, tr=tr),
        grid=(b, t // tr),
        in_specs=[pl.BlockSpec((1, tr, width), lambda i, j: (i, j, 0)),
                  pl.BlockSpec((1, HALO, width), lambda i, j: (i, jnp.maximum(j * nh - 1, 0), 0)),
                  pl.BlockSpec((1, HALO, width), lambda i, j: (i, jnp.minimum((j + 1) * nh, nblk - 1), 0)),
                  pl.BlockSpec((CONV_K, width), lambda i, j: (0, 0)),
                  pl.BlockSpec((1, width), lambda i, j: (0, 0)),
                  pl.BlockSpec((tr, LANES), lambda i, j: (j, 0)),
                  pl.BlockSpec((tr, LANES), lambda i, j: (j, 0)),
                  pl.BlockSpec((1, tr, LANES), lambda i, j: (i, j, 0))],
        out_specs=[pl.BlockSpec((1, tr, width), lambda i, j: (i, j, 0)),
                   pl.BlockSpec((1, tr, LANES), lambda i, j: (i, j, 0)),
                   pl.BlockSpec((1, tr // L, LANES, L), lambda i, j: (i, j, 0, 0))],
        out_shape=[jax.ShapeDtypeStruct((b, t, width), BF16),
                   jax.ShapeDtypeStruct((b, t, LANES), F32),
                   jax.ShapeDtypeStruct((b, t // L, LANES, L), F32)],
        scratch_shapes=[pltpu.VMEM((tr + 2 * HALO, width), F32)],
        compiler_params=_cparams("parallel", "parallel"),
        name="qk_rope" if rope else "qk_ctx",
    )(zml, zml, zml, conv_w, conv_b.reshape(1, width), cos_t, sin_t, gates)


def _rope_tables(n):
    pos = jnp.arange(n)
    half = ML_HEAD_DIM // 4
    inv = ROPE_BASE ** (-jnp.arange(half, dtype=F32) / half)

    def axis_tables(p):
        ang = p.astype(F32)[:, None] * inv[None, :]
        c, s = jnp.cos(ang), jnp.sin(ang)
        return jnp.concatenate([c, c], axis=-1), jnp.concatenate([-s, s], axis=-1)

    cr, sr = axis_tables(pos // GRID_W)
    cc, sc = axis_tables(pos % GRID_W)
    return jnp.concatenate([cr, cc], axis=-1), jnp.concatenate([sr, sc], axis=-1)


def _mlstm_direction(d, qk, v, gcol, grow, cn_s, m_s):
    L = ML_CHUNK
    row = lax.broadcasted_iota(jnp.int32, (L, L), 0)
    col = lax.broadcasted_iota(jnp.int32, (L, L), 1)
    fwd = d == 0
    keep = (row >= col) if fwd else (row <= col)
    ones = jnp.ones((L, ML_HEAD_DIM), BF16)
    end = L - 1 if fwd else 0
    outs = []
    for h in range(ML_HEADS):
        ci = GATE_RAW + (2 * d) * ML_HEADS + h
        cf = (GATE_PREFIX if fwd else GATE_SUFFIX) + (2 * d + 1) * ML_HEADS + h
        bc = jnp.broadcast_to(gcol[:, cf:cf + 1], (L, L))
        ic = jnp.broadcast_to(gcol[:, ci:ci + 1], (L, L))
        br = grow[cf:cf + 1, :]
        ir = grow[ci:ci + 1, :]
        m_prev = m_s[d, h][0:1, :]
        dlog = jnp.where(keep, bc - br + ir, NEG)
        m_t = jnp.maximum(bc + m_prev, jnp.max(dlog, axis=1, keepdims=True))
        dw = jnp.exp(dlog - m_t)
        inter = jnp.exp(bc + m_prev - m_t)
        qh = qk[:, h * LANES:(h + 1) * LANES]
        kh = qk[:, ML_WIDTH + h * LANES:ML_WIDTH + (h + 1) * LANES]
        vp = jnp.concatenate([v[:, h * LANES:(h + 1) * LANES], ones], axis=1)
        s = lax.dot_general(qh, kh, (((1,), (1,)), ((), ())), preferred_element_type=F32) * dw
        cn = cn_s[d, h]
        a1 = jnp.dot(s.astype(BF16), vp, preferred_element_type=F32)
        a2 = jnp.dot(qh, cn.astype(BF16), preferred_element_type=F32)
        num = a1[:, :ML_HEAD_DIM] + inter * a2[:, :ML_HEAD_DIM]
        den = a1[:, ML_HEAD_DIM:] + inter * a2[:, ML_HEAD_DIM:]
        outs.append(num / jnp.maximum(jnp.abs(den), jnp.exp(-m_t)))
        b_end = jnp.broadcast_to(br[:, end:end + 1], (1, L))
        g_row = b_end - br + ir
        m_new = jnp.maximum(b_end + m_prev, jnp.max(g_row, axis=1, keepdims=True))
        decay = jnp.exp(b_end + m_prev - m_new)
        wgt = jnp.exp(b_end - bc + ic - m_new)
        kw = (kh.astype(F32) * wgt).astype(BF16)
        upd = lax.dot_general(kw, vp, (((0,), (0,)), ((), ())), preferred_element_type=F32)
        cn_s[d, h] = jnp.concatenate([decay, decay], axis=1) * cn + upd
        m_s[d, h] = jnp.broadcast_to(m_new, m_s.shape[2:])
    return jnp.concatenate(outs, axis=1)


def _mlstm_kernel(qkf_ref, vf_ref, gcf_ref, grf_ref, qkb_ref, vb_ref, gcb_ref, grb_ref, c0_ref, m0_ref, *rest, emit_h):
    if emit_h:
        hf_ref, hb_ref, ct_ref, mt_ref, cn_s, m_s = rest
    else:
        ct_ref, mt_ref, cn_s, m_s = rest
        hf_ref = hb_ref = None
    c = pl.program_id(1)
    L = ML_CHUNK

    @pl.when(c == 0)
    def _():
        cn_s[...] = c0_ref[0]
        m_s[...] = m0_ref[0]

    for d, refs, h_ref in ((0, (qkf_ref, vf_ref, gcf_ref, grf_ref), hf_ref), (1, (qkb_ref, vb_ref, gcb_ref, grb_ref), hb_ref)):
        qk_ref, v_ref, gc_ref, gr_ref = refs
        n_sub = qk_ref.shape[1] // L
        for s in (range(n_sub) if d == 0 else reversed(range(n_sub))):
            rows = slice(s * L, (s + 1) * L)
            h = _mlstm_direction(d, qk_ref[0, rows, :], v_ref[0, rows, :], gc_ref[0, rows, :], gr_ref[0, s], cn_s, m_s)
            if h_ref is not None:
                h_ref[0, rows, :] = h.astype(h_ref.dtype)

    @pl.when(c == pl.num_programs(1) - 1)
    def _():
        ct_ref[0] = cn_s[...]
        mt_ref[0] = m_s[...]


def _mlstm(qk, zml, gcol, grow, c0, m0, emit_h):
    b, t, _ = qk.shape
    n_sub = min(MLSTM_CHUNKS_PER_STEP, t // ML_CHUNK)
    R = n_sub * ML_CHUNK
    assert t % R == 0
    nc = t // R
    vcol = 2
    f_idx = lambda i, c: (i, c, 0)
    b_idx = lambda i, c: (i, nc - 1 - c, 0)
    st_c = pl.BlockSpec((1, 2, ML_HEADS, ML_HEAD_DIM, 2 * ML_HEAD_DIM), lambda i, c: (i, 0, 0, 0, 0))
    st_m = pl.BlockSpec((1, 2, ML_HEADS, 8, LANES), lambda i, c: (i, 0, 0, 0, 0))
    out_specs = [st_c, st_m]
    out_shape = [jax.ShapeDtypeStruct(c0.shape, F32), jax.ShapeDtypeStruct(m0.shape, F32)]
    if emit_h:
        out_specs = [pl.BlockSpec((1, R, ML_WIDTH), f_idx), pl.BlockSpec((1, R, ML_WIDTH), b_idx)] + out_specs
        out_shape = [jax.ShapeDtypeStruct((b, t, ML_WIDTH), BF16)] * 2 + out_shape
    return pl.pallas_call(
        functools.partial(_mlstm_kernel, emit_h=emit_h),
        grid=(b, nc),
        in_specs=[pl.BlockSpec((1, R, 2 * ML_WIDTH), f_idx),
                  pl.BlockSpec((1, R, ML_WIDTH), lambda i, c: (i, c, vcol)),
                  pl.BlockSpec((1, R, LANES), f_idx),
                  pl.BlockSpec((1, n_sub, LANES, ML_CHUNK), lambda i, c: (i, c, 0, 0)),
                  pl.BlockSpec((1, R, 2 * ML_WIDTH), b_idx),
                  pl.BlockSpec((1, R, ML_WIDTH), lambda i, c: (i, nc - 1 - c, vcol)),
                  pl.BlockSpec((1, R, LANES), b_idx),
                  pl.BlockSpec((1, n_sub, LANES, ML_CHUNK), lambda i, c: (i, nc - 1 - c, 0, 0)),
                  st_c, st_m],
        out_specs=out_specs,
        out_shape=out_shape,
        scratch_shapes=[pltpu.VMEM((2, ML_HEADS, ML_HEAD_DIM, 2 * ML_HEAD_DIM), F32),
                        pltpu.VMEM((2, ML_HEADS, 8, LANES), F32)],
        compiler_params=_cparams("parallel", "arbitrary"),
        name="mlstm" if emit_h else "mlstm_ctx",
    )(qk, zml, gcol, grow, qk, zml, gcol, grow, c0, m0)


def _lane_in(shape, start, width):
    lane = lax.broadcasted_iota(jnp.int32, shape, 1)
    return (lane >= start) & (lane < start + width)


def _na_kernel(q_ref, k0_ref, k1_ref, k2_ref, v0_ref, v1_ref, v2_ref, kc_ref, vc_ref, bias_ref, o_ref):
    nq = q_ref.shape[1]
    lane = lax.broadcasted_iota(jnp.int32, (nq, LANES), 1)
    nt = (((1,), (1,)), ((), ()))
    for p in range(NA_HEADS // 2):
        sl = slice(p * LANES, (p + 1) * LANES)
        q2 = q_ref[0, :, sl] * (NA_HEAD_DIM ** -0.5)
        kwin = jnp.concatenate([k0_ref[0, :, sl], k1_ref[0, :, sl], k2_ref[0, :, sl]], axis=0)
        vwin = jnp.concatenate([v0_ref[0, :, sl], v1_ref[0, :, sl], v2_ref[0, :, sl]], axis=0)
        kc = kc_ref[0, :, sl]
        vwin = jnp.concatenate([vwin, jnp.ones_like(vwin)], axis=1)
        vc = jnp.concatenate([vc_ref[0, :, sl], jnp.ones_like(kc)], axis=1)
        halves = []
        for a in range(2):
            in_head = (lane >= a * NA_HEAD_DIM) & (lane < (a + 1) * NA_HEAD_DIM)
            qm = jnp.where(in_head, q2, jnp.zeros_like(q2))
            s_win = lax.dot_general(qm, kwin, nt, preferred_element_type=F32) + bias_ref[0, 2 * p + a]
            s_ctx = lax.dot_general(qm, kc, nt, preferred_element_type=F32)
            m = jnp.maximum(jnp.max(s_win, axis=1, keepdims=True), jnp.max(s_ctx, axis=1, keepdims=True))
            p_win = jnp.exp((s_win - m).astype(BF16))
            p_ctx = jnp.exp((s_ctx - m).astype(BF16))
            o = jnp.dot(p_win, vwin, preferred_element_type=F32) + jnp.dot(p_ctx, vc, preferred_element_type=F32)
            halves.append(o[:, :LANES] / o[:, LANES:])
        o_ref[0, :, sl] = jnp.where(lane < NA_HEAD_DIM, halves[0], halves[1]).astype(o_ref.dtype)


def _na_bias_tables(rpb, rows):
    R = NA_ROWS_PER_STEP
    nblk = rows // R
    kr = NA_WIN_ROWS
    cq = np.arange(GRID_W)
    cstart = np.clip(cq - NA_WIN_COLS // 2, 0, GRID_W - NA_WIN_COLS)
    ck = np.arange(GRID_W)
    col_ok = (ck[None, :] >= cstart[:, None]) & (ck[None, :] < cstart[:, None] + NA_WIN_COLS)
    col_off = np.where(col_ok, ck[None, :] - cq[:, None] + NA_WIN_COLS - 1, 0)
    row_ok = np.zeros((3, R, 3 * R), bool)
    row_off = np.zeros((3, R, 3 * R), np.int64)
    for vi, j in enumerate((0, 1, nblk - 1)):
        for i in range(R):
            r = j * R + i
            r0 = min(max(r - kr // 2, 0), rows - kr)
            for t in range(3):
                jb = j - 1 + t
                if jb < 0 or jb >= nblk:
                    continue
                for rr in range(R):
                    krow = jb * R + rr
                    if r0 <= krow < r0 + kr:
                        row_ok[vi, i, t * R + rr] = True
                        row_off[vi, i, t * R + rr] = krow - r + NA_WIN_ROWS - 1
    col_sel = (np.arange(rpb.shape[2])[None, None, :] == col_off[:, :, None]) & col_ok[:, :, None]
    row_sel = (np.arange(rpb.shape[1])[None, None, None, :] == row_off[..., None]) & row_ok[..., None]
    by_col = jnp.einsum('hrc,qkc->hrqk', rpb, jnp.asarray(col_sel, F32), precision=HIGHEST)
    vals = jnp.einsum('vixr,hrqk->vhiqxk', jnp.asarray(row_sel, F32), by_col, precision=HIGHEST)
    ok = row_ok[:, None, :, None, :, None] & col_ok[None, None, None, :, None, :]
    vals = jnp.where(jnp.asarray(ok), vals, NEG)
    return vals.reshape(3, NA_HEADS, R * GRID_W, 3 * R * GRID_W)


def _na(zna, zcna, bias, rows):
    b, n, _ = zna.shape
    ctx = zcna.shape[1]
    R = NA_ROWS_PER_STEP
    nq = R * GRID_W
    nblk = rows // R
    kb = lambda col, off: pl.BlockSpec(
        (1, nq, NA_WIDTH), lambda i, j: (i, jnp.clip(j + off, 0, nblk - 1), col))
    variant = lambda i, j: (jnp.where(j == 0, 0, jnp.where(j == nblk - 1, 2, 1)), 0, 0, 0)
    return pl.pallas_call(
        _na_kernel,
        grid=(b, nblk),
        in_specs=[pl.BlockSpec((1, nq, NA_WIDTH), lambda i, j: (i, j, 0)),
                  kb(1, -1), kb(1, 0), kb(1, 1), kb(2, -1), kb(2, 0), kb(2, 1),
                  pl.BlockSpec((1, ctx, NA_WIDTH), lambda i, j: (i, 0, 1)),
                  pl.BlockSpec((1, ctx, NA_WIDTH), lambda i, j: (i, 0, 2)),
                  pl.BlockSpec((1, NA_HEADS, nq, 3 * nq), variant)],
        out_specs=pl.BlockSpec((1, nq, NA_WIDTH), lambda i, j: (i, j, 0)),
        out_shape=jax.ShapeDtypeStruct((b, n, NA_WIDTH), BF16),
        compiler_params=_cparams("parallel", "parallel"),
        name="na",
    )(zna, zna, zna, zna, zna, zna, zna, zcna, zcna, bias)


def _mix_kernel(na_ref, hf_ref, hb_ref, o_ref, x_ref, g1_ref, sh2_ref, sc2_ref, wo_ref, mg_ref, l1g_ref, l1b_ref,
                wr_ref, br_ref, xmid_ref, rt_ref, rw_ref, cnt_ref, carry_s, *, alpha):
    first = (pl.program_id(0) == 0) & (pl.program_id(1) == 0)

    @pl.when(first)
    def _():
        carry_s[...] = jnp.zeros_like(carry_s)

    tm = x_ref.shape[1]
    h = hf_ref[0].astype(F32) + hb_ref[0].astype(F32)
    parts = []
    for hd in range(ML_HEADS):
        hh = h[:, hd * LANES:(hd + 1) * LANES]
        parts.append(hh * lax.rsqrt(jnp.mean(hh * hh, axis=-1, keepdims=True) + LN_EPS))
    hn = jnp.concatenate(parts, axis=1)
    ml = (hn * mg_ref[...] * jax.nn.sigmoid(o_ref[0].astype(F32))).astype(BF16)
    mix = (jnp.dot(na_ref[0], wo_ref[0:NA_WIDTH, :], preferred_element_type=F32)
           + jnp.dot(ml, wo_ref[NA_WIDTH:, :], preferred_element_type=F32))
    xmid = _ln_rows(alpha * x_ref[0] + g1_ref[0] * mix) * l1g_ref[...] + l1b_ref[...]
    xmid_ref[0] = xmid

    xt = (_ln_rows(xmid) * (1.0 + sc2_ref[0]) + sh2_ref[0]).astype(BF16)
    logits = jnp.dot(xt, wr_ref[...], preferred_element_type=F32) + br_ref[...]
    lane = lax.broadcasted_iota(jnp.int32, (tm, LANES), 1)
    is_g = lane < N_GROUPS
    gl = jnp.where(is_g, logits, NEG)
    gmax = jnp.max(gl, axis=1, keepdims=True)
    grp = jnp.min(jnp.where(gl == gmax, lane, LANES), axis=1, keepdims=True)
    gsum = jnp.sum(jnp.where(is_g, jnp.exp(gl - gmax), 0.0), axis=1, keepdims=True)
    grp_w = 1.0 / gsum
    lo = N_GROUPS + EXPERTS_PER_GROUP * grp
    el = jnp.where((lane >= lo) & (lane < lo + EXPERTS_PER_GROUP), logits, NEG)
    t1 = jnp.max(el, axis=1, keepdims=True)
    i1 = jnp.min(jnp.where(el == t1, lane, LANES), axis=1, keepdims=True)
    el2 = jnp.where(lane == i1, NEG, el)
    t2 = jnp.max(el2, axis=1, keepdims=True)
    i2 = jnp.min(jnp.where(el2 == t2, lane, LANES), axis=1, keepdims=True)
    e21 = jnp.exp(t2 - t1)
    w0 = grp_w / (1.0 + e21)
    w1 = grp_w * e21 / (1.0 + e21)

    hit1 = lane == i1
    hit2 = lane == i2
    onehot = (hit1 | hit2).astype(BF16)
    r_i = lax.broadcasted_iota(jnp.int32, (tm, tm), 0)
    c_i = lax.broadcasted_iota(jnp.int32, (tm, tm), 1)
    before = (r_i > c_i).astype(BF16)
    prefix = jnp.dot(before, onehot, preferred_element_type=F32) + carry_s[0:1, :]
    rank0 = jnp.sum(jnp.where(hit1, prefix, 0.0), axis=1, keepdims=True)
    rank1 = jnp.sum(jnp.where(hit2, prefix, 0.0), axis=1, keepdims=True)
    total = carry_s[0:1, :] + jnp.sum(onehot.astype(F32), axis=0, keepdims=True)
    carry_s[...] = jnp.broadcast_to(total, carry_s.shape)
    cnt_ref[...] = jnp.broadcast_to(total, cnt_ref.shape)

    rf = jnp.where(lane == 0, (i1 - N_GROUPS).astype(F32),
                   jnp.where(lane == 1, (i2 - N_GROUPS).astype(F32),
                             jnp.where(lane == 2, rank0, jnp.where(lane == 3, rank1, 0.0))))
    rt_ref[0] = rf.T[0:SUBLANES, :].astype(jnp.int32)
    rw_ref[0] = jnp.where(lane == 0, w0, jnp.where(lane == 1, w1, 0.0))


def _mix(na, hf, hb, zml, x, g1, sh2, sc2, wo, mg, l1g, l1b, wr, br, alpha, tm):
    b, n, d = x.shape
    row = lambda w: pl.BlockSpec((1, tm, w), lambda i, j: (i, j, 0))
    vec = pl.BlockSpec((1, 1, d), lambda i, j: (i, 0, 0))
    full = lambda a: pl.BlockSpec(a.shape, lambda i, j: (0,) * a.ndim)
    ocol = 3
    return pl.pallas_call(
        functools.partial(_mix_kernel, alpha=alpha),
        grid=(b, n // tm),
        in_specs=[row(NA_WIDTH), row(ML_WIDTH), row(ML_WIDTH),
                  pl.BlockSpec((1, tm, ML_WIDTH), lambda i, j: (i, j, ocol)),
                  row(d), vec, vec, vec, full(wo), full(mg), full(l1g), full(l1b), full(wr), full(br)],
        out_specs=[row(d), pl.BlockSpec((1, SUBLANES, tm), lambda i, j: (i * (n // tm) + j, 0, 0)), row(LANES),
                   pl.BlockSpec((SUBLANES, LANES), lambda i, j: (0, 0))],
        out_shape=[jax.ShapeDtypeStruct((b, n, d), F32),
                   jax.ShapeDtypeStruct((b * (n // tm), SUBLANES, tm), jnp.int32),
                   jax.ShapeDtypeStruct((b, n, LANES), F32),
                   jax.ShapeDtypeStruct((SUBLANES, LANES), F32)],
        scratch_shapes=[pltpu.VMEM((8, LANES), F32)],
        compiler_params=_cparams("arbitrary", "arbitrary"),
        name="mix",
    )(na, hf, hb, zml, x, g1, sh2, sc2, wo, mg, l1g, l1b, wr, br)


def _zero_fill_padding(pad_base_ref, pad_len_ref, nused_ref, xs_ref, zero_s, sem, wait):
    zero_t, xs_t = _token_view(zero_s), _token_view(xs_ref)
    tb = zero_t.shape[0]

    def run(copy):
        copy.wait() if wait else copy.start()

    def fill(off, nrows):
        run(pltpu.make_async_copy(zero_t.at[pl.ds(0, nrows)], xs_t.at[pl.ds(off, nrows)], sem))

    def per_expert(e, _):
        plen = pad_len_ref[e]
        base = pad_base_ref[e]
        bit = tb // 2
        while bit >= 1:
            off = base + (plen & ~(2 * bit - 1))

            @pl.when((plen & bit) != 0)
            def _(bit=bit, off=off):
                fill(off, bit)

            bit //= 2
        return 0

    lax.fori_loop(0, N_EXPERTS, per_expert, 0)

    def per_block(i, _):
        fill(i * tb, tb)
        return 0

    lax.fori_loop(nused_ref[0], xs_t.shape[0] // tb, per_block, 0)


ROW_WORDS = 4


def _token_view(ref):
    return ref.reshape(ref.shape[0] // ROW_WORDS, ROW_WORDS, LANES)


def _bf16_bits(v):
    return lax.bitcast_convert_type(v.astype(BF16).astype(F32), jnp.int32)


def _store_token_rows(dst_ref, base, val):
    tm, d = val.shape
    assert d == 2 * ROW_WORDS * LANES
    for s in range(ROW_WORDS):
        lo = _bf16_bits(val[:, s * LANES:(s + 1) * LANES])
        hi = _bf16_bits(val[:, (s + ROW_WORDS) * LANES:(s + ROW_WORDS + 1) * LANES])
        dst_ref[pl.ds(base * ROW_WORDS + s, tm, stride=ROW_WORDS), :] = hi | lax.shift_right_logical(lo, 16)


def _load_token_rows(src_ref, base, tm, dtype):
    words = [src_ref[pl.ds(base * ROW_WORDS + s, tm, stride=ROW_WORDS), :] for s in range(ROW_WORDS)]
    lo = [lax.bitcast_convert_type(w << 16, F32).astype(dtype) for w in words]
    hi = [lax.bitcast_convert_type(w & -65536, F32).astype(dtype) for w in words]
    return jnp.concatenate(lo + hi, axis=1)


def _dispatch_kernel(pad_base_ref, pad_len_ref, nused_ref, dest_ref, xmid_ref, sh2_ref, sc2_ref, xs_ref,
                     xt_s, zero_s, sem, zsem):
    tm = xmid_ref.shape[1]
    step = pl.program_id(0) * pl.num_programs(1) + pl.program_id(1)
    nsteps = pl.num_programs(0) * pl.num_programs(1)

    def wait_step_copies():
        for _ in range(2):
            pltpu.make_async_copy(_token_view(xt_s), _token_view(xs_ref).at[pl.ds(0, tm)], sem).wait()

    @pl.when(step == 0)
    def _():
        zero_s[...] = jnp.zeros_like(zero_s)
        _zero_fill_padding(pad_base_ref, pad_len_ref, nused_ref, xs_ref, zero_s, zsem, False)

    xt = _ln_rows(xmid_ref[0]) * (1.0 + sc2_ref[0]) + sh2_ref[0]

    @pl.when(step > 0)
    def _():
        wait_step_copies()

    _store_token_rows(xt_s, 0, xt)

    def start(r, _):
        for k in range(2):
            pltpu.make_async_copy(_token_view(xt_s).at[r], _token_view(xs_ref).at[dest_ref[0, k, r]], sem).start()
        return 0

    lax.fori_loop(0, tm, start, 0, unroll=8)

    @pl.when(step == 0)
    def _():
        _zero_fill_padding(pad_base_ref, pad_len_ref, nused_ref, xs_ref, zero_s, zsem, True)

    @pl.when(step == nsteps - 1)
    def _():
        wait_step_copies()


def _dispatch(xmid, sh2, sc2, dest, pad_base, pad_len, nused, cap, tm):
    b, n, d = xmid.shape
    assert d == 2 * ROW_WORDS * LANES
    nt = n // tm
    vec = pl.BlockSpec((1, 1, d), lambda i, j, *_: (i, 0, 0))
    return pl.pallas_call(
        _dispatch_kernel,
        grid_spec=pltpu.PrefetchScalarGridSpec(
            num_scalar_prefetch=3,
            grid=(b, nt),
            in_specs=[pl.BlockSpec((1, 2, tm), lambda i, j, *_: (i * nt + j, 0, 0), memory_space=pltpu.SMEM),
                      pl.BlockSpec((1, tm, d), lambda i, j, *_: (i, j, 0)), vec, vec],
            out_specs=pl.BlockSpec(memory_space=pl.ANY),
            scratch_shapes=[pltpu.VMEM((tm * ROW_WORDS, LANES), ROW_DTYPE),
                            pltpu.VMEM((EXPERT_ROWS * ROW_WORDS, LANES), ROW_DTYPE),
                            pltpu.SemaphoreType.DMA, pltpu.SemaphoreType.DMA]),
        out_shape=jax.ShapeDtypeStruct((cap * ROW_WORDS, LANES), ROW_DTYPE),
        compiler_params=_cparams("arbitrary", "arbitrary"),
        name="dispatch",
    )(pad_base, pad_len, nused, dest, xmid, sh2, sc2)


def _expert_kernel(be_ref, nv_ref, xs_ref, w1_ref, w3_ref, w2_ref, ys_ref, w1b, w3b, w2b):
    tb = xs_ref.shape[0] // ROW_WORDS
    i = pl.program_id(0)
    e = be_ref[i]
    changed = (i == 0) | (be_ref[jnp.maximum(i - 1, 0)] != e)

    @pl.when(changed)
    def _():
        w1b[...] = w1_ref[0].astype(BF16)
        w3b[...] = w3_ref[0].astype(BF16)
        w2b[...] = w2_ref[0].astype(BF16)

    nv = nv_ref[i]

    @pl.when(nv > 0)
    def _():
        xb = _load_token_rows(xs_ref, 0, tb, BF16)
        h1 = jnp.dot(xb, w1b[...], preferred_element_type=F32)
        h3 = jnp.dot(xb, w3b[...], preferred_element_type=F32)
        a = (_silu(h1) * h3).astype(BF16)
        _store_token_rows(ys_ref, 0, jnp.dot(a, w2b[...], preferred_element_type=F32))

    @pl.when(nv == 0)
    def _():
        ys_ref[...] = jnp.zeros_like(ys_ref)


def _experts(xs, block_e, block_nv, w1, w3, w2):
    d, hid = w1.shape[1], w1.shape[2]
    tb = EXPERT_ROWS
    rows = pl.BlockSpec((tb * ROW_WORDS, LANES), lambda i, be, nv: (i, 0))
    return pl.pallas_call(
        _expert_kernel,
        grid_spec=pltpu.PrefetchScalarGridSpec(
            num_scalar_prefetch=2,
            grid=(xs.shape[0] // (tb * ROW_WORDS),),
            in_specs=[rows,
                      pl.BlockSpec((1, d, hid), lambda i, be, nv: (be[i], 0, 0)),
                      pl.BlockSpec((1, d, hid), lambda i, be, nv: (be[i], 0, 0)),
                      pl.BlockSpec((1, hid, d), lambda i, be, nv: (be[i], 0, 0))],
            out_specs=rows,
            scratch_shapes=[pltpu.VMEM((d, hid), BF16), pltpu.VMEM((d, hid), BF16), pltpu.VMEM((hid, d), BF16)]),
        out_shape=jax.ShapeDtypeStruct(xs.shape, xs.dtype),
        compiler_params=_cparams("arbitrary"),
        name="experts",
    )(block_e, block_nv, xs, w1, w3, w2)


def _combine_kernel(dcur_ref, dnext_ref, xmid_ref, rw_ref, g2_ref, l2g_ref, l2b_ref, ys_ref, o_ref,
                    y0_s, y1_s, sem, *, alpha):
    tm = xmid_ref.shape[1]
    step = pl.program_id(0) * pl.num_programs(1) + pl.program_id(1)
    nsteps = pl.num_programs(0) * pl.num_programs(1)
    slot = step % 2

    def gather(dest_ref, into):
        def start(r, _):
            for k, buf in ((0, y0_s), (1, y1_s)):
                pltpu.make_async_copy(_token_view(ys_ref).at[dest_ref[0, k, r]],
                                      _token_view(buf).at[into * tm + r], sem.at[into]).start()
            return 0

        lax.fori_loop(0, tm, start, 0, unroll=8)

    @pl.when(step == 0)
    def _():
        gather(dcur_ref, 0)

    @pl.when(step + 1 < nsteps)
    def _():
        gather(dnext_ref, 1 - slot)

    for buf in (y0_s, y1_s):
        pltpu.make_async_copy(_token_view(ys_ref).at[pl.ds(0, tm)], _token_view(buf).at[pl.ds(slot * tm, tm)],
                              sem.at[slot]).wait()
    base = slot * tm
    rw = rw_ref[0]
    moe = (rw[:, 0:1] * _load_token_rows(y0_s, base, tm, F32) + rw[:, 1:2] * _load_token_rows(y1_s, base, tm, F32))
    o_ref[0] = _ln_rows(alpha * xmid_ref[0] + g2_ref[0] * moe) * l2g_ref[...] + l2b_ref[...]


def _combine(xmid, rw, g2, l2g, l2b, ys, dest, alpha, tm):
    b, n, d = xmid.shape
    nt = n // tm
    full = lambda a: pl.BlockSpec(a.shape, lambda i, j: (0,) * a.ndim)
    return pl.pallas_call(
        functools.partial(_combine_kernel, alpha=alpha),
        grid=(b, nt),
        in_specs=[pl.BlockSpec((1, 2, tm), lambda i, j: (i * nt + j, 0, 0), memory_space=pltpu.SMEM),
                  pl.BlockSpec((1, 2, tm), lambda i, j: (jnp.minimum(i * nt + j + 1, b * nt - 1), 0, 0),
                               memory_space=pltpu.SMEM),
                  pl.BlockSpec((1, tm, d), lambda i, j: (i, j, 0)),
                  pl.BlockSpec((1, tm, LANES), lambda i, j: (i, j, 0)),
                  pl.BlockSpec((1, 1, d), lambda i, j: (i, 0, 0)),
                  full(l2g), full(l2b),
                  pl.BlockSpec(memory_space=pl.ANY)],
        out_specs=pl.BlockSpec((1, tm, d), lambda i, j: (i, j, 0)),
        out_shape=jax.ShapeDtypeStruct((b, n, d), F32),
        scratch_shapes=[pltpu.VMEM((2 * tm * ROW_WORDS, LANES), ys.dtype),
                        pltpu.VMEM((2 * tm * ROW_WORDS, LANES), ys.dtype), pltpu.SemaphoreType.DMA((2,))],
        compiler_params=_cparams("arbitrary", "arbitrary"),
        name="combine",
    )(dest, dest, xmid, rw, g2, l2g, l2b, ys)


def _tile(n, want):
    t = min(n, want)
    assert n % t == 0, (n, t)
    return t


def kernel(x, c, ctx, c_ctx, w_ada, b_ada, w_in, conv_w, conv_b, gate_b, rpb, ml_norm_g, w_out, ln1_g, ln1_b,
           w_router_g, b_router_g, w_router_e, b_router_e, w1, w3, w2, ln2_g, ln2_b):
    B, N, D = x.shape
    T_CTX = ctx.shape[1]
    depth = w_ada.shape[0]
    rows = N // GRID_W
    assert depth == 1 and N % GRID_W == 0 and rows % NA_ROWS_PER_STEP == 0 and rows >= 3 * NA_ROWS_PER_STEP
    assert N % ML_CHUNK == 0 and T_CTX % ML_CHUNK == 0
    alpha = (2.0 * depth) ** 0.25
    l = 0

    pad_rows = -(B + 1) % 8
    cvec = jnp.concatenate([c, c_ctx[None], jnp.zeros((pad_rows, D), F32)], axis=0)
    ada = _ada(cvec, w_ada[l], b_ada[l])
    sh1, sc1, g1, sh2, sc2, g2 = [a[:, None, :] for a in jnp.split(ada[:B], 6, axis=-1)]
    csh1, csc1 = [jnp.broadcast_to(a[None], (B, 1, D)) for a in jnp.split(ada[B:B + 1], 6, axis=-1)[:2]]

    col_ml = 3 * NA_WIDTH
    col_g = col_ml + 4 * ML_WIDTH
    wb = w_in[l].astype(BF16)
    wna, wml = wb[:, :col_ml], wb[:, col_ml:col_g]
    n_gate = 4 * ML_HEADS
    wg = jnp.pad(wb[:, col_g:], ((0, 0), (0, LANES - n_gate)))
    gb = jnp.pad(gate_b[l], (0, LANES - n_gate)).reshape(1, LANES)
    zna, zml, gates = _inproj(x, sh1, sc1, wna, wml, wg, gb, _tile(N, 512))
    zcna, zcml, gates_c = _inproj(ctx, csh1, csc1, wna, wml, wg, gb, _tile(T_CTX, 256))

    cos_t, sin_t = _rope_tables(N)
    qk_c, gcol_c, grow_c = _qk_streams(zcml, gates_c, conv_w[l], conv_b[l], cos_t[:T_CTX], sin_t[:T_CTX], False,
                                       _tile(T_CTX, 512))
    qk_l, gcol_l, grow_l = _qk_streams(zml, gates, conv_w[l], conv_b[l], cos_t, sin_t, True, _tile(N, 512))
    c0 = jnp.zeros((B, 2, ML_HEADS, ML_HEAD_DIM, 2 * ML_HEAD_DIM), F32)
    m0 = jnp.zeros((B, 2, ML_HEADS, 8, LANES), F32)
    c_ctx_end, m_ctx_end = _mlstm(qk_c, zcml, gcol_c, grow_c, c0, m0, False)
    hf, hb, _, _ = _mlstm(qk_l, zml, gcol_l, grow_l, c_ctx_end, m_ctx_end, True)

    na = _na(zna, zcna, _na_bias_tables(rpb[l], rows), rows)

    wr = jnp.pad(jnp.concatenate([w_router_g[l], w_router_e[l]], axis=1),
                 ((0, 0), (0, LANES - N_GROUPS - N_EXPERTS))).astype(BF16)
    br = jnp.pad(jnp.concatenate([b_router_g[l], b_router_e[l]]), (0, LANES - N_GROUPS - N_EXPERTS)).reshape(1, LANES)
    tm = _tile(N, 512)
    xmid, rt, rw, counts = _mix(na, hf, hb, zml, x, g1, sh2, sc2, w_out[l].astype(BF16),
                                ml_norm_g[l].reshape(1, ML_WIDTH), ln1_g[l].reshape(1, D), ln1_b[l].reshape(1, D),
                                wr, br, alpha, tm)

    tb = EXPERT_ROWS
    n_assign = 2 * B * N
    cap = -(-n_assign // tb) * tb + N_EXPERTS * tb
    sizes = counts[0, N_GROUPS:N_GROUPS + N_EXPERTS].astype(jnp.int32)
    padded = (sizes + tb - 1) // tb * tb
    pend = jnp.cumsum(padded)
    pstart = pend - padded
    experts = jnp.arange(N_EXPERTS, dtype=jnp.int32)
    first_row = jnp.sum(jnp.where(rt[:, 0:2, :, None] == experts, pstart, 0), axis=-1)
    dest = first_row + rt[:, 2:4, :]
    blk0 = jnp.arange(cap // tb, dtype=jnp.int32) * tb
    block_e = jnp.minimum(jnp.sum(pend[None, :] <= blk0[:, None], axis=1), N_EXPERTS - 1).astype(jnp.int32)
    is_e = block_e[:, None] == experts
    block_nv = jnp.clip(jnp.sum(jnp.where(is_e, pstart + sizes, 0), axis=1) - blk0, 0, tb).astype(jnp.int32)

    nused = (pend[-1:] // tb).astype(jnp.int32)
    xs = _dispatch(xmid, sh2, sc2, dest, pstart + sizes, padded - sizes, nused, cap, tm)
    ys = _experts(xs, block_e, block_nv, w1[l], w3[l], w2[l])
    return _combine(xmid, rw, g2, ln2_g[l].reshape(1, D), ln2_b[l].reshape(1, D), ys, dest, alpha, tm)
```

```python
import functools

import numpy as np
import jax
import jax.numpy as jnp
from jax import lax
from jax.experimental import pallas as pl
from jax.experimental.pallas import tpu as pltpu

F32 = jnp.float32
BF16 = jnp.bfloat16
ROW_DTYPE = jnp.int32
HIGHEST = lax.Precision.HIGHEST

GRID_W = 64
NA_HEADS = 8
NA_HEAD_DIM = 64
NA_WIDTH = NA_HEADS * NA_HEAD_DIM
NA_WIN_ROWS = 8
NA_WIN_COLS = 16
ML_HEADS = 4
ML_HEAD_DIM = 128
ML_WIDTH = ML_HEADS * ML_HEAD_DIM
ML_CHUNK = 128
CONV_K = 5
N_GROUPS = 8
EXPERTS_PER_GROUP = 8
N_EXPERTS = N_GROUPS * EXPERTS_PER_GROUP
ROPE_BASE = 10000.0
LN_EPS = 1e-5

LANES = 128
SUBLANES = 8
VMEM_LIMIT = 56 * 1024 * 1024

NA_ROWS_PER_STEP = 4
EXPERT_ROWS = 512
MLSTM_CHUNKS_PER_STEP = 4
NEG = -1e30


def _cparams(*sem):
    return pltpu.CompilerParams(dimension_semantics=sem, vmem_limit_bytes=VMEM_LIMIT)


def _silu(v):
    return v * jax.nn.sigmoid(v)


def _ln_rows(v):
    mu = jnp.mean(v, axis=-1, keepdims=True)
    vc = v - mu
    var = jnp.mean(vc * vc, axis=-1, keepdims=True)
    return vc * lax.rsqrt(var + LN_EPS)


def _ada_kernel(c_ref, w_ref, b_ref, o_ref):
    o_ref[...] = jnp.dot(_silu(c_ref[...]), w_ref[...], preferred_element_type=F32,
                         precision=HIGHEST) + b_ref[...]


def _ada(cvec, w, b):
    rows, d = cvec.shape
    cols = w.shape[1]
    tn = 1024
    return pl.pallas_call(
        _ada_kernel,
        grid=(cols // tn,),
        in_specs=[pl.BlockSpec((rows, d), lambda j: (0, 0)),
                  pl.BlockSpec((d, tn), lambda j: (0, j)),
                  pl.BlockSpec((1, tn), lambda j: (0, j))],
        out_specs=pl.BlockSpec((rows, tn), lambda j: (0, j)),
        out_shape=jax.ShapeDtypeStruct((rows, cols), F32),
        compiler_params=_cparams("arbitrary"),
        name="ada",
    )(cvec, w, b.reshape(1, cols))


def _inproj_kernel(x_ref, sh_ref, sc_ref, wna_ref, wml_ref, wg_ref, gb_ref, zna_ref, zml_ref, g_ref):
    y = _ln_rows(x_ref[0]) * (1.0 + sc_ref[0]) + sh_ref[0]
    yb = y.astype(BF16)
    zna_ref[0] = jnp.dot(yb, wna_ref[...], preferred_element_type=F32).astype(BF16)
    zml_ref[0] = jnp.dot(yb, wml_ref[...], preferred_element_type=F32).astype(BF16)
    g_ref[0] = jnp.dot(yb, wg_ref[...], preferred_element_type=F32) + gb_ref[...]


def _inproj(x, shift, scale, wna, wml, wg, gb, tm):
    b, t, d = x.shape
    full = lambda a: pl.BlockSpec(a.shape, lambda i, j: (0,) * a.ndim)
    row = lambda w: pl.BlockSpec((1, tm, w), lambda i, j: (i, j, 0))
    vec = pl.BlockSpec((1, 1, d), lambda i, j: (i, 0, 0))
    return pl.pallas_call(
        _inproj_kernel,
        grid=(b, t // tm),
        in_specs=[row(d), vec, vec, full(wna), full(wml), full(wg), full(gb)],
        out_specs=[row(wna.shape[1]), row(wml.shape[1]), row(LANES)],
        out_shape=[jax.ShapeDtypeStruct((b, t, wna.shape[1]), BF16),
                   jax.ShapeDtypeStruct((b, t, wml.shape[1]), BF16),
                   jax.ShapeDtypeStruct((b, t, LANES), F32)],
        compiler_params=_cparams("parallel", "parallel"),
        name="inproj",
    )(x, shift, scale, wna, wml, wg, gb)


HALO = 16


def _log_sigmoid(v):
    return jnp.minimum(v, 0.0) - jnp.log1p(jnp.exp(-jnp.abs(v)))


GATE_RAW, GATE_PREFIX, GATE_SUFFIX = 0, 16, 32


def _pack_gates(g):
    L = g.shape[0]
    row = lax.broadcasted_iota(jnp.int32, (L, L), 0)
    col = lax.broadcasted_iota(jnp.int32, (L, L), 1)
    lane = lax.broadcasted_iota(jnp.int32, g.shape, 1)
    lf = _log_sigmoid(g)
    prefix = jnp.dot((row >= col).astype(F32), lf, preferred_element_type=F32, precision=HIGHEST)
    suffix = jnp.dot((row <= col).astype(F32), lf, preferred_element_type=F32, precision=HIGHEST)
    return jnp.where(lane < GATE_PREFIX, g,
                     jnp.where(lane < GATE_SUFFIX, pltpu.roll(prefix, GATE_PREFIX, 1),
                               jnp.where(lane < GATE_SUFFIX + GATE_PREFIX, pltpu.roll(suffix, GATE_SUFFIX, 1), 0.0)))


def _qk_kernel(cur_ref, prev_ref, next_ref, cw_ref, cb_ref, cos_ref, sin_ref, g_ref, o_ref, gcol_ref, grow_ref,
               pad_ref, *, rope, tr):
    for c in range(tr // ML_CHUNK):
        packed = _pack_gates(g_ref[0, c * ML_CHUNK:(c + 1) * ML_CHUNK, :])
        gcol_ref[0, c * ML_CHUNK:(c + 1) * ML_CHUNK, :] = packed
        grow_ref[0, c] = packed.T
    i = pl.program_id(1)
    last = pl.num_programs(1) - 1
    pad = CONV_K // 2
    pad_ref[HALO:HALO + tr, :] = cur_ref[0].astype(F32)
    pad_ref[0:HALO, :] = jnp.where(i > 0, prev_ref[0].astype(F32), 0.0)
    pad_ref[HALO + tr:2 * HALO + tr, :] = jnp.where(i < last, next_ref[0].astype(F32), 0.0)
    acc = cw_ref[0:1, :] * pad_ref[HALO - pad:HALO - pad + tr, :] + cb_ref[...]
    for j in range(1, CONV_K):
        acc = acc + cw_ref[j:j + 1, :] * pad_ref[HALO - pad + j:HALO - pad + j + tr, :]
    u = _silu(acc)
    width = u.shape[1]
    lane = lax.broadcasted_iota(jnp.int32, (tr, LANES), 1)
    first_half = (lane % (ML_HEAD_DIM // 2)) < (ML_HEAD_DIM // 4)
    kscale = ML_HEAD_DIM ** -0.5
    for g in range(width // LANES):
        ug = u[:, g * LANES:(g + 1) * LANES]
        if rope:
            partner = jnp.where(first_half, pltpu.roll(ug, LANES - ML_HEAD_DIM // 4, 1),
                                pltpu.roll(ug, ML_HEAD_DIM // 4, 1))
            ug = ug * cos_ref[...] + partner * sin_ref[...]
        if g >= ML_HEADS:
            ug = ug * kscale
        o_ref[0, :, g * LANES:(g + 1) * LANES] = ug.astype(BF16)


def _qk_streams(zml, gates, conv_w, conv_b, cos_t, sin_t, rope, tr):
    b, t, _ = zml.shape
    width = 2 * ML_WIDTH
    nh = tr // HALO
    nblk = t // HALO
    L = ML_CHUNK
    return pl.pallas_call(
        functools.partial(_qk_kernel, rope=rope, tr=tr),
        grid=(b, t // tr),
        in_specs=[pl.BlockSpec((1, tr, width), lambda i, j: (i, j, 0)),
                  pl.BlockSpec((1, HALO, width), lambda i, j: (i, jnp.maximum(j * nh - 1, 0), 0)),
                  pl.BlockSpec((1, HALO, width), lambda i, j: (i, jnp.minimum((j + 1) * nh, nblk - 1), 0)),
                  pl.BlockSpec((CONV_K, width), lambda i, j: (0, 0)),
                  pl.BlockSpec((1, width), lambda i, j: (0, 0)),
                  pl.BlockSpec((tr, LANES), lambda i, j: (j, 0)),
                  pl.BlockSpec((tr, LANES), lambda i, j: (j, 0)),
                  pl.BlockSpec((1, tr, LANES), lambda i, j: (i, j, 0))],
        out_specs=[pl.BlockSpec((1, tr, width), lambda i, j: (i, j, 0)),
                   pl.BlockSpec((1, tr, LANES), lambda i, j: (i, j, 0)),
                   pl.BlockSpec((1, tr // L, LANES, L), lambda i, j: (i, j, 0, 0))],
        out_shape=[jax.ShapeDtypeStruct((b, t, width), BF16),
                   jax.ShapeDtypeStruct((b, t, LANES), F32),
                   jax.ShapeDtypeStruct((b, t // L, LANES, L), F32)],
        scratch_shapes=[pltpu.VMEM((tr + 2 * HALO, width), F32)],
        compiler_params=_cparams("parallel", "parallel"),
        name="qk_rope" if rope else "qk_ctx",
    )(zml, zml, zml, conv_w, conv_b.reshape(1, width), cos_t, sin_t, gates)


def _rope_tables(n):
    pos = jnp.arange(n)
    half = ML_HEAD_DIM // 4
    inv = ROPE_BASE ** (-jnp.arange(half, dtype=F32) / half)

    def axis_tables(p):
        ang = p.astype(F32)[:, None] * inv[None, :]
        c, s = jnp.cos(ang), jnp.sin(ang)
        return jnp.concatenate([c, c], axis=-1), jnp.concatenate([-s, s], axis=-1)

    cr, sr = axis_tables(pos // GRID_W)
    cc, sc = axis_tables(pos % GRID_W)
    return jnp.concatenate([cr, cc], axis=-1), jnp.concatenate([sr, sc], axis=-1)


def _mlstm_direction(d, qk, v, gcol, grow, cn_s, m_s):
    L = ML_CHUNK
    row = lax.broadcasted_iota(jnp.int32, (L, L), 0)
    col = lax.broadcasted_iota(jnp.int32, (L, L), 1)
    fwd = d == 0
    keep = (row >= col) if fwd else (row <= col)
    ones = jnp.ones((L, ML_HEAD_DIM), BF16)
    end = L - 1 if fwd else 0
    outs = []
    for h in range(ML_HEADS):
        ci = GATE_RAW + (2 * d) * ML_HEADS + h
        cf = (GATE_PREFIX if fwd else GATE_SUFFIX) + (2 * d + 1) * ML_HEADS + h
        bc = jnp.broadcast_to(gcol[:, cf:cf + 1], (L, L))
        ic = jnp.broadcast_to(gcol[:, ci:ci + 1], (L, L))
        br = grow[cf:cf + 1, :]
        ir = grow[ci:ci + 1, :]
        m_prev = m_s[d, h][0:1, :]
        dlog = jnp.where(keep, bc - br + ir, NEG)
        m_t = jnp.maximum(bc + m_prev, jnp.max(dlog, axis=1, keepdims=True))
        dw = jnp.exp((dlog - m_t).astype(BF16))
        inter = jnp.exp(bc + m_prev - m_t)
        qh = qk[:, h * LANES:(h + 1) * LANES]
        kh = qk[:, ML_WIDTH + h * LANES:ML_WIDTH + (h + 1) * LANES]
        vp = jnp.concatenate([v[:, h * LANES:(h + 1) * LANES], ones], axis=1)
        s = lax.dot_general(qh, kh, (((1,), (1,)), ((), ())), preferred_element_type=F32).astype(BF16) * dw
        cn = cn_s[d, h]
        a1 = jnp.dot(s, vp, preferred_element_type=F32)
        a2 = jnp.dot(qh, cn.astype(BF16), preferred_element_type=F32)
        num = a1[:, :ML_HEAD_DIM] + inter * a2[:, :ML_HEAD_DIM]
        den = a1[:, ML_HEAD_DIM:] + inter * a2[:, ML_HEAD_DIM:]
        outs.append(num / jnp.maximum(jnp.abs(den), jnp.exp(-m_t)))
        b_end = jnp.broadcast_to(br[:, end:end + 1], (1, L))
        g_row = b_end - br + ir
        m_new = jnp.maximum(b_end + m_prev, jnp.max(g_row, axis=1, keepdims=True))
        decay = jnp.exp(b_end + m_prev - m_new)
        wgt = jnp.exp((b_end - bc + ic - m_new).astype(BF16))
        kw = kh * wgt
        upd = lax.dot_general(kw, vp, (((0,), (0,)), ((), ())), preferred_element_type=F32)
        cn_s[d, h] = jnp.concatenate([decay, decay], axis=1) * cn + upd
        m_s[d, h] = jnp.broadcast_to(m_new, m_s.shape[2:])
    return jnp.concatenate(outs, axis=1)


def _mlstm_kernel(qkf_ref, vf_ref, gcf_ref, grf_ref, qkb_ref, vb_ref, gcb_ref, grb_ref, c0_ref, m0_ref, *rest, emit_h):
    if emit_h:
        hf_ref, hb_ref, ct_ref, mt_ref, cn_s, m_s = rest
    else:
        ct_ref, mt_ref, cn_s, m_s = rest
        hf_ref = hb_ref = None
    c = pl.program_id(1)
    L = ML_CHUNK

    @pl.when(c == 0)
    def _():
        cn_s[...] = c0_ref[0]
        m_s[...] = m0_ref[0]

    for d, refs, h_ref in ((0, (qkf_ref, vf_ref, gcf_ref, grf_ref), hf_ref), (1, (qkb_ref, vb_ref, gcb_ref, grb_ref), hb_ref)):
        qk_ref, v_ref, gc_ref, gr_ref = refs
        n_sub = qk_ref.shape[1] // L
        for s in (range(n_sub) if d == 0 else reversed(range(n_sub))):
            rows = slice(s * L, (s + 1) * L)
            h = _mlstm_direction(d, qk_ref[0, rows, :], v_ref[0, rows, :], gc_ref[0, rows, :], gr_ref[0, s], cn_s, m_s)
            if h_ref is not None:
                h_ref[0, rows, :] = h.astype(h_ref.dtype)

    @pl.when(c == pl.num_programs(1) - 1)
    def _():
        ct_ref[0] = cn_s[...]
        mt_ref[0] = m_s[...]


def _mlstm(qk, zml, gcol, grow, c0, m0, emit_h):
    b, t, _ = qk.shape
    n_sub = min(MLSTM_CHUNKS_PER_STEP, t // ML_CHUNK)
    R = n_sub * ML_CHUNK
    assert t % R == 0
    nc = t // R
    vcol = 2
    f_idx = lambda i, c: (i, c, 0)
    b_idx = lambda i, c: (i, nc - 1 - c, 0)
    st_c = pl.BlockSpec((1, 2, ML_HEADS, ML_HEAD_DIM, 2 * ML_HEAD_DIM), lambda i, c: (i, 0, 0, 0, 0))
    st_m = pl.BlockSpec((1, 2, ML_HEADS, 8, LANES), lambda i, c: (i, 0, 0, 0, 0))
    out_specs = [st_c, st_m]
    out_shape = [jax.ShapeDtypeStruct(c0.shape, F32), jax.ShapeDtypeStruct(m0.shape, F32)]
    if emit_h:
        out_specs = [pl.BlockSpec((1, R, ML_WIDTH), f_idx), pl.BlockSpec((1, R, ML_WIDTH), b_idx)] + out_specs
        out_shape = [jax.ShapeDtypeStruct((b, t, ML_WIDTH), BF16)] * 2 + out_shape
    return pl.pallas_call(
        functools.partial(_mlstm_kernel, emit_h=emit_h),
        grid=(b, nc),
        in_specs=[pl.BlockSpec((1, R, 2 * ML_WIDTH), f_idx),
                  pl.BlockSpec((1, R, ML_WIDTH), lambda i, c: (i, c, vcol)),
                  pl.BlockSpec((1, R, LANES), f_idx),
                  pl.BlockSpec((1, n_sub, LANES, ML_CHUNK), lambda i, c: (i, c, 0, 0)),
                  pl.BlockSpec((1, R, 2 * ML_WIDTH), b_idx),
                  pl.BlockSpec((1, R, ML_WIDTH), lambda i, c: (i, nc - 1 - c, vcol)),
                  pl.BlockSpec((1, R, LANES), b_idx),
                  pl.BlockSpec((1, n_sub, LANES, ML_CHUNK), lambda i, c: (i, nc - 1 - c, 0, 0)),
                  st_c, st_m],
        out_specs=out_specs,
        out_shape=out_shape,
        scratch_shapes=[pltpu.VMEM((2, ML_HEADS, ML_HEAD_DIM, 2 * ML_HEAD_DIM), F32),
                        pltpu.VMEM((2, ML_HEADS, 8, LANES), F32)],
        compiler_params=_cparams("parallel", "arbitrary"),
        name="mlstm" if emit_h else "mlstm_ctx",
    )(qk, zml, gcol, grow, qk, zml, gcol, grow, c0, m0)


def _lane_in(shape, start, width):
    lane = lax.broadcasted_iota(jnp.int32, shape, 1)
    return (lane >= start) & (lane < start + width)


def _na_kernel(q_ref, k0_ref, k1_ref, k2_ref, v0_ref, v1_ref, v2_ref, kc_ref, vc_ref, bias_ref, o_ref):
    nq = q_ref.shape[1]
    lane = lax.broadcasted_iota(jnp.int32, (nq, LANES), 1)
    nt = (((1,), (1,)), ((), ()))
    for p in range(NA_HEADS // 2):
        sl = slice(p * LANES, (p + 1) * LANES)
        q2 = q_ref[0, :, sl] * (NA_HEAD_DIM ** -0.5)
        kwin = jnp.concatenate([k0_ref[0, :, sl], k1_ref[0, :, sl], k2_ref[0, :, sl]], axis=0)
        vwin = jnp.concatenate([v0_ref[0, :, sl], v1_ref[0, :, sl], v2_ref[0, :, sl]], axis=0)
        kc = kc_ref[0, :, sl]
        vwin = jnp.concatenate([vwin, jnp.ones_like(vwin)], axis=1)
        vc = jnp.concatenate([vc_ref[0, :, sl], jnp.ones_like(kc)], axis=1)
        halves = []
        for a in range(2):
            in_head = (lane >= a * NA_HEAD_DIM) & (lane < (a + 1) * NA_HEAD_DIM)
            qm = jnp.where(in_head, q2, jnp.zeros_like(q2))
            s_win = lax.dot_general(qm, kwin, nt, preferred_element_type=F32) + bias_ref[0, 2 * p + a]
            s_ctx = lax.dot_general(qm, kc, nt, preferred_element_type=F32)
            m = jnp.maximum(jnp.max(s_win, axis=1, keepdims=True), jnp.max(s_ctx, axis=1, keepdims=True))
            p_win = jnp.exp((s_win - m).astype(BF16))
            p_ctx = jnp.exp((s_ctx - m).astype(BF16))
            o = jnp.dot(p_win, vwin, preferred_element_type=F32) + jnp.dot(p_ctx, vc, preferred_element_type=F32)
            halves.append(o[:, :LANES] / o[:, LANES:])
        o_ref[0, :, sl] = jnp.where(lane < NA_HEAD_DIM, halves[0], halves[1]).astype(o_ref.dtype)


def _na_bias_tables(rpb, rows):
    R = NA_ROWS_PER_STEP
    nblk = rows // R
    kr = NA_WIN_ROWS
    cq = np.arange(GRID_W)
    cstart = np.clip(cq - NA_WIN_COLS // 2, 0, GRID_W - NA_WIN_COLS)
    ck = np.arange(GRID_W)
    col_ok = (ck[None, :] >= cstart[:, None]) & (ck[None, :] < cstart[:, None] + NA_WIN_COLS)
    col_off = np.where(col_ok, ck[None, :] - cq[:, None] + NA_WIN_COLS - 1, 0)
    row_ok = np.zeros((3, R, 3 * R), bool)
    row_off = np.zeros((3, R, 3 * R), np.int64)
    for vi, j in enumerate((0, 1, nblk - 1)):
        for i in range(R):
            r = j * R + i
            r0 = min(max(r - kr // 2, 0), rows - kr)
            for t in range(3):
                jb = j - 1 + t
                if jb < 0 or jb >= nblk:
                    continue
                for rr in range(R):
                    krow = jb * R + rr
                    if r0 <= krow < r0 + kr:
                        row_ok[vi, i, t * R + rr] = True
                        row_off[vi, i, t * R + rr] = krow - r + NA_WIN_ROWS - 1
    col_sel = (np.arange(rpb.shape[2])[None, None, :] == col_off[:, :, None]) & col_ok[:, :, None]
    row_sel = (np.arange(rpb.shape[1])[None, None, None, :] == row_off[..., None]) & row_ok[..., None]
    by_col = jnp.einsum('hrc,qkc->hrqk', rpb, jnp.asarray(col_sel, F32), precision=HIGHEST)
    vals = jnp.einsum('vixr,hrqk->vhiqxk', jnp.asarray(row_sel, F32), by_col, precision=HIGHEST)
    ok = row_ok[:, None, :, None, :, None] & col_ok[None, None, None, :, None, :]
    vals = jnp.where(jnp.asarray(ok), vals, NEG)
    return vals.reshape(3, NA_HEADS, R * GRID_W, 3 * R * GRID_W)


def _na(zna, zcna, bias, rows):
    b, n, _ = zna.shape
    ctx = zcna.shape[1]
    R = NA_ROWS_PER_STEP
    nq = R * GRID_W
    nblk = rows // R
    kb = lambda col, off: pl.BlockSpec(
        (1, nq, NA_WIDTH), lambda i, j: (i, jnp.clip(j + off, 0, nblk - 1), col))
    variant = lambda i, j: (jnp.where(j == 0, 0, jnp.where(j == nblk - 1, 2, 1)), 0, 0, 0)
    return pl.pallas_call(
        _na_kernel,
        grid=(b, nblk),
        in_specs=[pl.BlockSpec((1, nq, NA_WIDTH), lambda i, j: (i, j, 0)),
                  kb(1, -1), kb(1, 0), kb(1, 1), kb(2, -1), kb(2, 0), kb(2, 1),
                  pl.BlockSpec((1, ctx, NA_WIDTH), lambda i, j: (i, 0, 1)),
                  pl.BlockSpec((1, ctx, NA_WIDTH), lambda i, j: (i, 0, 2)),
                  pl.BlockSpec((1, NA_HEADS, nq, 3 * nq), variant)],
        out_specs=pl.BlockSpec((1, nq, NA_WIDTH), lambda i, j: (i, j, 0)),
        out_shape=jax.ShapeDtypeStruct((b, n, NA_WIDTH), BF16),
        compiler_params=_cparams("parallel", "parallel"),
        name="na",
    )(zna, zna, zna, zna, zna, zna, zna, zcna, zcna, bias)


def _mix_kernel(na_ref, hf_ref, hb_ref, o_ref, x_ref, g1_ref, sh2_ref, sc2_ref, wo_ref, mg_ref, l1g_ref, l1b_ref,
                wr_ref, br_ref, xmid_ref, rt_ref, rw_ref, cnt_ref, carry_s, *, alpha):
    first = (pl.program_id(0) == 0) & (pl.program_id(1) == 0)

    @pl.when(first)
    def _():
        carry_s[...] = jnp.zeros_like(carry_s)

    tm = x_ref.shape[1]
    h = hf_ref[0].astype(F32) + hb_ref[0].astype(F32)
    parts = []
    for hd in range(ML_HEADS):
        hh = h[:, hd * LANES:(hd + 1) * LANES]
        parts.append(hh * lax.rsqrt(jnp.mean(hh * hh, axis=-1, keepdims=True) + LN_EPS))
    hn = jnp.concatenate(parts, axis=1)
    ml = (hn * mg_ref[...] * jax.nn.sigmoid(o_ref[0].astype(F32))).astype(BF16)
    mix = (jnp.dot(na_ref[0], wo_ref[0:NA_WIDTH, :], preferred_element_type=F32)
           + jnp.dot(ml, wo_ref[NA_WIDTH:, :], preferred_element_type=F32))
    xmid = _ln_rows(alpha * x_ref[0] + g1_ref[0] * mix) * l1g_ref[...] + l1b_ref[...]
    xmid_ref[0] = xmid

    xt = (_ln_rows(xmid) * (1.0 + sc2_ref[0]) + sh2_ref[0]).astype(BF16)
    logits = jnp.dot(xt, wr_ref[...], preferred_element_type=F32) + br_ref[...]
    lane = lax.broadcasted_iota(jnp.int32, (tm, LANES), 1)
    is_g = lane < N_GROUPS
    gl = jnp.where(is_g, logits, NEG)
    gmax = jnp.max(gl, axis=1, keepdims=True)
    grp = jnp.min(jnp.where(gl == gmax, lane, LANES), axis=1, keepdims=True)
    gsum = jnp.sum(jnp.where(is_g, jnp.exp(gl - gmax), 0.0), axis=1, keepdims=True)
    grp_w = 1.0 / gsum
    lo = N_GROUPS + EXPERTS_PER_GROUP * grp
    el = jnp.where((lane >= lo) & (lane < lo + EXPERTS_PER_GROUP), logits, NEG)
    t1 = jnp.max(el, axis=1, keepdims=True)
    i1 = jnp.min(jnp.where(el == t1, lane, LANES), axis=1, keepdims=True)
    el2 = jnp.where(lane == i1, NEG, el)
    t2 = jnp.max(el2, axis=1, keepdims=True)
    i2 = jnp.min(jnp.where(el2 == t2, lane, LANES), axis=1, keepdims=True)
    e21 = jnp.exp(t2 - t1)
    w0 = grp_w / (1.0 + e21)
    w1 = grp_w * e21 / (1.0 + e21)

    hit1 = lane == i1
    hit2 = lane == i2
    onehot = (hit1 | hit2).astype(BF16)
    r_i = lax.broadcasted_iota(jnp.int32, (tm, tm), 0)
    c_i = lax.broadcasted_iota(jnp.int32, (tm, tm), 1)
    before = (r_i > c_i).astype(BF16)
    prefix = jnp.dot(before, onehot, preferred_element_type=F32) + carry_s[0:1, :]
    rank0 = jnp.sum(jnp.where(hit1, prefix, 0.0), axis=1, keepdims=True)
    rank1 = jnp.sum(jnp.where(hit2, prefix, 0.0), axis=1, keepdims=True)
    total = carry_s[0:1, :] + jnp.sum(onehot.astype(F32), axis=0, keepdims=True)
    carry_s[...] = jnp.broadcast_to(total, carry_s.shape)
    cnt_ref[...] = jnp.broadcast_to(total, cnt_ref.shape)

    rf = jnp.where(lane == 0, (i1 - N_GROUPS).astype(F32),
                   jnp.where(lane == 1, (i2 - N_GROUPS).astype(F32),
                             jnp.where(lane == 2, rank0, jnp.where(lane == 3, rank1, 0.0))))
    rt_ref[0] = rf.T[0:SUBLANES, :].astype(jnp.int32)
    rw_ref[0] = jnp.where(lane == 0, w0, jnp.where(lane == 1, w1, 0.0))


def _mix(na, hf, hb, zml, x, g1, sh2, sc2, wo, mg, l1g, l1b, wr, br, alpha, tm):
    b, n, d = x.shape
    row = lambda w: pl.BlockSpec((1, tm, w), lambda i, j: (i, j, 0))
    vec = pl.BlockSpec((1, 1, d), lambda i, j: (i, 0, 0))
    full = lambda a: pl.BlockSpec(a.shape, lambda i, j: (0,) * a.ndim)
    ocol = 3
    return pl.pallas_call(
        functools.partial(_mix_kernel, alpha=alpha),
        grid=(b, n // tm),
        in_specs=[row(NA_WIDTH), row(ML_WIDTH), row(ML_WIDTH),
                  pl.BlockSpec((1, tm, ML_WIDTH), lambda i, j: (i, j, ocol)),
                  row(d), vec, vec, vec, full(wo), full(mg), full(l1g), full(l1b), full(wr), full(br)],
        out_specs=[row(d), pl.BlockSpec((1, SUBLANES, tm), lambda i, j: (i * (n // tm) + j, 0, 0)), row(LANES),
                   pl.BlockSpec((SUBLANES, LANES), lambda i, j: (0, 0))],
        out_shape=[jax.ShapeDtypeStruct((b, n, d), F32),
                   jax.ShapeDtypeStruct((b * (n // tm), SUBLANES, tm), jnp.int32),
                   jax.ShapeDtypeStruct((b, n, LANES), F32),
                   jax.ShapeDtypeStruct((SUBLANES, LANES), F32)],
        scratch_shapes=[pltpu.VMEM((8, LANES), F32)],
        compiler_params=_cparams("arbitrary", "arbitrary"),
        name="mix",
    )(na, hf, hb, zml, x, g1, sh2, sc2, wo, mg, l1g, l1b, wr, br)


def _zero_fill_padding(pad_base_ref, pad_len_ref, nused_ref, xs_ref, zero_s, sem, wait):
    zero_t, xs_t = _token_view(zero_s), _token_view(xs_ref)
    tb = zero_t.shape[0]

    def run(copy):
        copy.wait() if wait else copy.start()

    def fill(off, nrows):
        run(pltpu.make_async_copy(zero_t.at[pl.ds(0, nrows)], xs_t.at[pl.ds(off, nrows)], sem))

    def per_expert(e, _):
        plen = pad_len_ref[e]
        base = pad_base_ref[e]
        bit = tb // 2
        while bit >= 1:
            off = base + (plen & ~(2 * bit - 1))

            @pl.when((plen & bit) != 0)
            def _(bit=bit, off=off):
                fill(off, bit)

            bit //= 2
        return 0

    lax.fori_loop(0, N_EXPERTS, per_expert, 0)

    def per_block(i, _):
        fill(i * tb, tb)
        return 0

    lax.fori_loop(nused_ref[0], xs_t.shape[0] // tb, per_block, 0)


ROW_WORDS = 4


def _token_view(ref):
    return ref.reshape(ref.shape[0] // ROW_WORDS, ROW_WORDS, LANES)


def _bf16_bits(v):
    return lax.bitcast_convert_type(v.astype(BF16).astype(F32), jnp.int32)


def _store_token_rows(dst_ref, base, val):
    tm, d = val.shape
    assert d == 2 * ROW_WORDS * LANES
    for s in range(ROW_WORDS):
        lo = _bf16_bits(val[:, s * LANES:(s + 1) * LANES])
        hi = _bf16_bits(val[:, (s + ROW_WORDS) * LANES:(s + ROW_WORDS + 1) * LANES])
        dst_ref[pl.ds(base * ROW_WORDS + s, tm, stride=ROW_WORDS), :] = hi | lax.shift_right_logical(lo, 16)


def _load_token_rows(src_ref, base, tm, dtype):
    words = [src_ref[pl.ds(base * ROW_WORDS + s, tm, stride=ROW_WORDS), :] for s in range(ROW_WORDS)]
    lo = [lax.bitcast_convert_type(w << 16, F32).astype(dtype) for w in words]
    hi = [lax.bitcast_convert_type(w & -65536, F32).astype(dtype) for w in words]
    return jnp.concatenate(lo + hi, axis=1)


def _dispatch_kernel(pad_base_ref, pad_len_ref, nused_ref, dest_ref, xmid_ref, sh2_ref, sc2_ref, xs_ref,
                     xt_s, zero_s, sem, zsem):
    tm = xmid_ref.shape[1]
    step = pl.program_id(0) * pl.num_programs(1) + pl.program_id(1)
    nsteps = pl.num_programs(0) * pl.num_programs(1)

    def wait_step_copies():
        for _ in range(2):
            pltpu.make_async_copy(_token_view(xt_s), _token_view(xs_ref).at[pl.ds(0, tm)], sem).wait()

    @pl.when(step == 0)
    def _():
        zero_s[...] = jnp.zeros_like(zero_s)
        _zero_fill_padding(pad_base_ref, pad_len_ref, nused_ref, xs_ref, zero_s, zsem, False)

    xt = _ln_rows(xmid_ref[0]) * (1.0 + sc2_ref[0]) + sh2_ref[0]

    @pl.when(step > 0)
    def _():
        wait_step_copies()

    _store_token_rows(xt_s, 0, xt)

    def start(r, _):
        for k in range(2):
            pltpu.make_async_copy(_token_view(xt_s).at[r], _token_view(xs_ref).at[dest_ref[0, k, r]],
                                  sem).start(priority=k)
        return 0

    lax.fori_loop(0, tm, start, 0, unroll=8)

    @pl.when(step == 0)
    def _():
        _zero_fill_padding(pad_base_ref, pad_len_ref, nused_ref, xs_ref, zero_s, zsem, True)

    @pl.when(step == nsteps - 1)
    def _():
        wait_step_copies()


def _dispatch(xmid, sh2, sc2, dest, pad_base, pad_len, nused, cap, tm):
    b, n, d = xmid.shape
    assert d == 2 * ROW_WORDS * LANES
    nt = n // tm
    vec = pl.BlockSpec((1, 1, d), lambda i, j, *_: (i, 0, 0))
    return pl.pallas_call(
        _dispatch_kernel,
        grid_spec=pltpu.PrefetchScalarGridSpec(
            num_scalar_prefetch=3,
            grid=(b, nt),
            in_specs=[pl.BlockSpec((1, 2, tm), lambda i, j, *_: (i * nt + j, 0, 0), memory_space=pltpu.SMEM),
                      pl.BlockSpec((1, tm, d), lambda i, j, *_: (i, j, 0)), vec, vec],
            out_specs=pl.BlockSpec(memory_space=pl.ANY),
            scratch_shapes=[pltpu.VMEM((tm * ROW_WORDS, LANES), ROW_DTYPE),
                            pltpu.VMEM((EXPERT_ROWS * ROW_WORDS, LANES), ROW_DTYPE),
                            pltpu.SemaphoreType.DMA, pltpu.SemaphoreType.DMA]),
        out_shape=jax.ShapeDtypeStruct((cap * ROW_WORDS, LANES), ROW_DTYPE),
        compiler_params=_cparams("arbitrary", "arbitrary"),
        name="dispatch",
    )(pad_base, pad_len, nused, dest, xmid, sh2, sc2)


def _expert_kernel(be_ref, nv_ref, xs_ref, w1_ref, w3_ref, w2_ref, ys_ref, w1b, w3b, w2b):
    tb = xs_ref.shape[0] // ROW_WORDS
    i = pl.program_id(0)
    e = be_ref[i]
    changed = (i == 0) | (be_ref[jnp.maximum(i - 1, 0)] != e)

    @pl.when(changed)
    def _():
        w1b[...] = w1_ref[0].astype(BF16)
        w3b[...] = w3_ref[0].astype(BF16)
        w2b[...] = w2_ref[0].astype(BF16)

    nv = nv_ref[i]

    @pl.when(nv > 0)
    def _():
        xb = _load_token_rows(xs_ref, 0, tb, BF16)
        h1 = jnp.dot(xb, w1b[...], preferred_element_type=F32)
        h3 = jnp.dot(xb, w3b[...], preferred_element_type=F32)
        a = (_silu(h1) * h3).astype(BF16)
        _store_token_rows(ys_ref, 0, jnp.dot(a, w2b[...], preferred_element_type=F32))

    @pl.when(nv == 0)
    def _():
        ys_ref[...] = jnp.zeros_like(ys_ref)


def _experts(xs, block_e, block_nv, w1, w3, w2):
    d, hid = w1.shape[1], w1.shape[2]
    tb = EXPERT_ROWS
    rows = pl.BlockSpec((tb * ROW_WORDS, LANES), lambda i, be, nv: (i, 0))
    return pl.pallas_call(
        _expert_kernel,
        grid_spec=pltpu.PrefetchScalarGridSpec(
            num_scalar_prefetch=2,
            grid=(xs.shape[0] // (tb * ROW_WORDS),),
            in_specs=[rows,
                      pl.BlockSpec((1, d, hid), lambda i, be, nv: (be[i], 0, 0)),
                      pl.BlockSpec((1, d, hid), lambda i, be, nv: (be[i], 0, 0)),
                      pl.BlockSpec((1, hid, d), lambda i, be, nv: (be[i], 0, 0))],
            out_specs=rows,
            scratch_shapes=[pltpu.VMEM((d, hid), BF16), pltpu.VMEM((d, hid), BF16), pltpu.VMEM((hid, d), BF16)]),
        out_shape=jax.ShapeDtypeStruct(xs.shape, xs.dtype),
        compiler_params=_cparams("arbitrary"),
        name="experts",
    )(block_e, block_nv, xs, w1, w3, w2)


def _combine_kernel(dcur_ref, dnext_ref, xmid_ref, rw_ref, g2_ref, l2g_ref, l2b_ref, ys_ref, o_ref,
                    y0_s, y1_s, sem, *, alpha):
    tm = xmid_ref.shape[1]
    step = pl.program_id(0) * pl.num_programs(1) + pl.program_id(1)
    nsteps = pl.num_programs(0) * pl.num_programs(1)
    slot = step % 2

    def gather(dest_ref, into):
        def start(r, _):
            for k, buf in ((0, y0_s), (1, y1_s)):
                pltpu.make_async_copy(_token_view(ys_ref).at[dest_ref[0, k, r]],
                                      _token_view(buf).at[into * tm + r], sem.at[into]).start(priority=k)
            return 0

        lax.fori_loop(0, tm, start, 0, unroll=8)

    @pl.when(step == 0)
    def _():
        gather(dcur_ref, 0)

    @pl.when(step + 1 < nsteps)
    def _():
        gather(dnext_ref, 1 - slot)

    for buf in (y0_s, y1_s):
        pltpu.make_async_copy(_token_view(ys_ref).at[pl.ds(0, tm)], _token_view(buf).at[pl.ds(slot * tm, tm)],
                              sem.at[slot]).wait()
    base = slot * tm
    rw = rw_ref[0]
    moe = (rw[:, 0:1] * _load_token_rows(y0_s, base, tm, F32) + rw[:, 1:2] * _load_token_rows(y1_s, base, tm, F32))
    o_ref[0] = _ln_rows(alpha * xmid_ref[0] + g2_ref[0] * moe) * l2g_ref[...] + l2b_ref[...]


def _combine(xmid, rw, g2, l2g, l2b, ys, dest, alpha, tm):
    b, n, d = xmid.shape
    nt = n // tm
    full = lambda a: pl.BlockSpec(a.shape, lambda i, j: (0,) * a.ndim)
    return pl.pallas_call(
        functools.partial(_combine_kernel, alpha=alpha),
        grid=(b, nt),
        in_specs=[pl.BlockSpec((1, 2, tm), lambda i, j: (i * nt + j, 0, 0), memory_space=pltpu.SMEM),
                  pl.BlockSpec((1, 2, tm), lambda i, j: (jnp.minimum(i * nt + j + 1, b * nt - 1), 0, 0),
                               memory_space=pltpu.SMEM),
                  pl.BlockSpec((1, tm, d), lambda i, j: (i, j, 0)),
                  pl.BlockSpec((1, tm, LANES), lambda i, j: (i, j, 0)),
                  pl.BlockSpec((1, 1, d), lambda i, j: (i, 0, 0)),
                  full(l2g), full(l2b),
                  pl.BlockSpec(memory_space=pl.ANY)],
        out_specs=pl.BlockSpec((1, tm, d), lambda i, j: (i, j, 0)),
        out_shape=jax.ShapeDtypeStruct((b, n, d), F32),
        scratch_shapes=[pltpu.VMEM((2 * tm * ROW_WORDS, LANES), ys.dtype),
                        pltpu.VMEM((2 * tm * ROW_WORDS, LANES), ys.dtype), pltpu.SemaphoreType.DMA((2,))],
        compiler_params=_cparams("arbitrary", "arbitrary"),
        name="combine",
    )(dest, dest, xmid, rw, g2, l2g, l2b, ys)


def _tile(n, want):
    t = min(n, want)
    assert n % t == 0, (n, t)
    return t


def kernel(x, c, ctx, c_ctx, w_ada, b_ada, w_in, conv_w, conv_b, gate_b, rpb, ml_norm_g, w_out, ln1_g, ln1_b,
           w_router_g, b_router_g, w_router_e, b_router_e, w1, w3, w2, ln2_g, ln2_b):
    B, N, D = x.shape
    T_CTX = ctx.shape[1]
    depth = w_ada.shape[0]
    rows = N // GRID_W
    assert depth == 1 and N % GRID_W == 0 and rows % NA_ROWS_PER_STEP == 0 and rows >= 3 * NA_ROWS_PER_STEP
    assert N % ML_CHUNK == 0 and T_CTX % ML_CHUNK == 0
    alpha = (2.0 * depth) ** 0.25
    l = 0

    pad_rows = -(B + 1) % 8
    cvec = jnp.concatenate([c, c_ctx[None], jnp.zeros((pad_rows, D), F32)], axis=0)
    ada = _ada(cvec, w_ada[l], b_ada[l])
    sh1, sc1, g1, sh2, sc2, g2 = [a[:, None, :] for a in jnp.split(ada[:B], 6, axis=-1)]
    csh1, csc1 = [jnp.broadcast_to(a[None], (B, 1, D)) for a in jnp.split(ada[B:B + 1], 6, axis=-1)[:2]]

    col_ml = 3 * NA_WIDTH
    col_g = col_ml + 4 * ML_WIDTH
    wb = w_in[l].astype(BF16)
    wna, wml = wb[:, :col_ml], wb[:, col_ml:col_g]
    n_gate = 4 * ML_HEADS
    wg = jnp.pad(wb[:, col_g:], ((0, 0), (0, LANES - n_gate)))
    gb = jnp.pad(gate_b[l], (0, LANES - n_gate)).reshape(1, LANES)
    zna, zml, gates = _inproj(x, sh1, sc1, wna, wml, wg, gb, _tile(N, 1024))
    zcna, zcml, gates_c = _inproj(ctx, csh1, csc1, wna, wml, wg, gb, _tile(T_CTX, 256))

    cos_t, sin_t = _rope_tables(N)
    qk_c, gcol_c, grow_c = _qk_streams(zcml, gates_c, conv_w[l], conv_b[l], cos_t[:T_CTX], sin_t[:T_CTX], False,
                                       _tile(T_CTX, 512))
    qk_l, gcol_l, grow_l = _qk_streams(zml, gates, conv_w[l], conv_b[l], cos_t, sin_t, True, _tile(N, 512))
    c0 = jnp.zeros((B, 2, ML_HEADS, ML_HEAD_DIM, 2 * ML_HEAD_DIM), F32)
    m0 = jnp.zeros((B, 2, ML_HEADS, 8, LANES), F32)
    c_ctx_end, m_ctx_end = _mlstm(qk_c, zcml, gcol_c, grow_c, c0, m0, False)
    hf, hb, _, _ = _mlstm(qk_l, zml, gcol_l, grow_l, c_ctx_end, m_ctx_end, True)

    na = _na(zna, zcna, _na_bias_tables(rpb[l], rows), rows)

    wr = jnp.pad(jnp.concatenate([w_router_g[l], w_router_e[l]], axis=1),
                 ((0, 0), (0, LANES - N_GROUPS - N_EXPERTS))).astype(BF16)
    br = jnp.pad(jnp.concatenate([b_router_g[l], b_router_e[l]]), (0, LANES - N_GROUPS - N_EXPERTS)).reshape(1, LANES)
    tm = _tile(N, 512)
    xmid, rt, rw, counts = _mix(na, hf, hb, zml, x, g1, sh2, sc2, w_out[l].astype(BF16),
                                ml_norm_g[l].reshape(1, ML_WIDTH), ln1_g[l].reshape(1, D), ln1_b[l].reshape(1, D),
                                wr, br, alpha, tm)

    tb = EXPERT_ROWS
    n_assign = 2 * B * N
    cap = -(-n_assign // tb) * tb + N_EXPERTS * tb
    sizes = counts[0, N_GROUPS:N_GROUPS + N_EXPERTS].astype(jnp.int32)
    padded = (sizes + tb - 1) // tb * tb
    pend = jnp.cumsum(padded)
    pstart = pend - padded
    experts = jnp.arange(N_EXPERTS, dtype=jnp.int32)
    first_row = jnp.sum(jnp.where(rt[:, 0:2, :, None] == experts, pstart, 0), axis=-1)
    dest = first_row + rt[:, 2:4, :]
    blk0 = jnp.arange(cap // tb, dtype=jnp.int32) * tb
    block_e = jnp.minimum(jnp.sum(pend[None, :] <= blk0[:, None], axis=1), N_EXPERTS - 1).astype(jnp.int32)
    is_e = block_e[:, None] == experts
    block_nv = jnp.clip(jnp.sum(jnp.where(is_e, pstart + sizes, 0), axis=1) - blk0, 0, tb).astype(jnp.int32)

    nused = (pend[-1:] // tb).astype(jnp.int32)
    xs = _dispatch(xmid, sh2, sc2, dest, pstart + sizes, padded - sizes, nused, cap, tm)
    ys = _experts(xs, block_e, block_nv, w1[l], w3[l], w2[l])
    return _combine(xmid, rw, g2, ln2_g[l].reshape(1, D), ln2_b[l].reshape(1, D), ys, dest, alpha, tm)
```

```python
import functools

import numpy as np
import jax
import jax.numpy as jnp
from jax import lax
from jax.experimental import pallas as pl
from jax.experimental.pallas import tpu as pltpu

F32 = jnp.float32
BF16 = jnp.bfloat16
ROW_DTYPE = jnp.int32
HIGHEST = lax.Precision.HIGHEST

GRID_W = 64
NA_HEADS = 8
NA_HEAD_DIM = 64
NA_WIDTH = NA_HEADS * NA_HEAD_DIM
NA_WIN_ROWS = 8
NA_WIN_COLS = 16
ML_HEADS = 4
ML_HEAD_DIM = 128
ML_WIDTH = ML_HEADS * ML_HEAD_DIM
ML_CHUNK = 128
CONV_K = 5
N_GROUPS = 8
EXPERTS_PER_GROUP = 8
N_EXPERTS = N_GROUPS * EXPERTS_PER_GROUP
ROPE_BASE = 10000.0
LN_EPS = 1e-5

LANES = 128
SUBLANES = 8
VMEM_LIMIT = 56 * 1024 * 1024

NA_ROWS_PER_STEP = 4
EXPERT_ROWS = 512
MLSTM_CHUNKS_PER_STEP = 4
NEG = -1e30


def _cparams(*sem):
    return pltpu.CompilerParams(dimension_semantics=sem, vmem_limit_bytes=VMEM_LIMIT)


def _silu(v):
    return v * jax.nn.sigmoid(v)


def _ln_rows(v):
    mu = jnp.mean(v, axis=-1, keepdims=True)
    vc = v - mu
    var = jnp.mean(vc * vc, axis=-1, keepdims=True)
    return vc * lax.rsqrt(var + LN_EPS)


def _ada_kernel(c_ref, w_ref, b_ref, o_ref):
    o_ref[...] = jnp.dot(_silu(c_ref[...]), w_ref[...], preferred_element_type=F32,
                         precision=HIGHEST) + b_ref[...]


def _ada(cvec, w, b):
    rows, d = cvec.shape
    cols = w.shape[1]
    tn = 1024
    return pl.pallas_call(
        _ada_kernel,
        grid=(cols // tn,),
        in_specs=[pl.BlockSpec((rows, d), lambda j: (0, 0)),
                  pl.BlockSpec((d, tn), lambda j: (0, j)),
                  pl.BlockSpec((1, tn), lambda j: (0, j))],
        out_specs=pl.BlockSpec((rows, tn), lambda j: (0, j)),
        out_shape=jax.ShapeDtypeStruct((rows, cols), F32),
        compiler_params=_cparams("arbitrary"),
        name="ada",
    )(cvec, w, b.reshape(1, cols))


HALO = 16


def _log_sigmoid(v):
    return jnp.minimum(v, 0.0) - jnp.log1p(jnp.exp(-jnp.abs(v)))


GATE_RAW, GATE_PREFIX, GATE_SUFFIX = 0, 16, 32


def _pack_gates(g):
    L = g.shape[0]
    row = lax.broadcasted_iota(jnp.int32, (L, L), 0)
    col = lax.broadcasted_iota(jnp.int32, (L, L), 1)
    lane = lax.broadcasted_iota(jnp.int32, g.shape, 1)
    lf = _log_sigmoid(g)
    prefix = jnp.dot((row >= col).astype(F32), lf, preferred_element_type=F32, precision=HIGHEST)
    suffix = jnp.dot((row <= col).astype(F32), lf, preferred_element_type=F32, precision=HIGHEST)
    return jnp.where(lane < GATE_PREFIX, g,
                     jnp.where(lane < GATE_SUFFIX, pltpu.roll(prefix, GATE_PREFIX, 1),
                               jnp.where(lane < GATE_SUFFIX + GATE_PREFIX, pltpu.roll(suffix, GATE_SUFFIX, 1), 0.0)))


def _inproj_kernel(x_ref, xp_ref, xn_ref, sh_ref, sc_ref, wna_ref, wqk_ref, wvo_ref, wg_ref, gb_ref, cw_ref, cb_ref,
                   cos_ref, sin_ref, zna_ref, zvo_ref, o_ref, gcol_ref, grow_ref, *pad_refs, rope, tr):
    i = pl.program_id(1)
    last = pl.num_programs(1) - 1
    xa = jnp.concatenate([xp_ref[0], x_ref[0], xn_ref[0]], axis=0)
    ya = (_ln_rows(xa) * (1.0 + sc_ref[0]) + sh_ref[0]).astype(BF16)
    yb = ya[HALO:HALO + tr]
    pad = CONV_K // 2
    width = wqk_ref.shape[1]
    lane = lax.broadcasted_iota(jnp.int32, (tr, LANES), 1)
    first_half = (lane % (ML_HEAD_DIM // 2)) < (ML_HEAD_DIM // 4)
    kscale = ML_HEAD_DIM ** -0.5
    cb = 2 * LANES

    def project_qk(c):
        z = jnp.dot(ya, wqk_ref[:, c * cb:(c + 1) * cb], preferred_element_type=F32)
        pad_refs[c][HALO:HALO + tr, :] = z[HALO:HALO + tr]
        pad_refs[c][0:HALO, :] = jnp.where(i > 0, z[0:HALO], 0.0)
        pad_refs[c][HALO + tr:2 * HALO + tr, :] = jnp.where(i < last, z[HALO + tr:], 0.0)

    def conv_group(g):
        cols = slice(g * LANES, (g + 1) * LANES)
        pad_ref = pad_refs[g * LANES // cb]
        pc = slice(g * LANES % cb, g * LANES % cb + LANES)
        acc = cw_ref[0:1, cols] * pad_ref[HALO - pad:HALO - pad + tr, pc] + cb_ref[:, cols]
        for j in range(1, CONV_K):
            acc = acc + cw_ref[j:j + 1, cols] * pad_ref[HALO - pad + j:HALO - pad + j + tr, pc]
        ug = _silu(acc)
        if rope:
            partner = jnp.where(first_half, pltpu.roll(ug, LANES - ML_HEAD_DIM // 4, 1),
                                pltpu.roll(ug, ML_HEAD_DIM // 4, 1))
            ug = ug * cos_ref[...] + partner * sin_ref[...]
        if g >= ML_HEADS:
            ug = ug * kscale
        o_ref[0, :, cols] = ug.astype(BF16)

    other = ([(zna_ref, wna_ref, c) for c in range(wna_ref.shape[1] // cb)]
             + [(zvo_ref, wvo_ref, c) for c in range(wvo_ref.shape[1] // cb)])

    def project_other(n):
        for _ in range(n):
            if other:
                dst, w, c = other.pop(0)
                cols = slice(c * cb, (c + 1) * cb)
                dst[0, :, cols] = jnp.dot(yb, w[:, cols], preferred_element_type=F32).astype(BF16)

    n_qk = width // cb
    per_round = -(-len(other) // n_qk)
    project_qk(0)
    for c in range(n_qk):
        if c + 1 < n_qk:
            project_qk(c + 1)
        project_other(per_round)
        for g in range(c * cb // LANES, (c + 1) * cb // LANES):
            conv_group(g)
    project_other(len(other))
    gates = jnp.dot(yb, wg_ref[...], preferred_element_type=F32) + gb_ref[...]
    for c in range(tr // ML_CHUNK):
        packed = _pack_gates(gates[c * ML_CHUNK:(c + 1) * ML_CHUNK, :])
        gcol_ref[0, c * ML_CHUNK:(c + 1) * ML_CHUNK, :] = packed
        grow_ref[0, c] = packed.T


def _inproj(x, shift, scale, wna, wqk, wvo, wg, gb, conv_w, conv_b, cos_t, sin_t, rope, tr):
    b, t, d = x.shape
    width = 2 * ML_WIDTH
    nh = tr // HALO
    nblk = t // HALO
    L = ML_CHUNK
    full = lambda a: pl.BlockSpec(a.shape, lambda i, j: (0,) * a.ndim)
    row = lambda w: pl.BlockSpec((1, tr, w), lambda i, j: (i, j, 0))
    vec = pl.BlockSpec((1, 1, d), lambda i, j: (i, 0, 0))
    return pl.pallas_call(
        functools.partial(_inproj_kernel, rope=rope, tr=tr),
        grid=(b, t // tr),
        in_specs=[row(d),
                  pl.BlockSpec((1, HALO, d), lambda i, j: (i, jnp.maximum(j * nh - 1, 0), 0)),
                  pl.BlockSpec((1, HALO, d), lambda i, j: (i, jnp.minimum((j + 1) * nh, nblk - 1), 0)),
                  vec, vec, full(wna), full(wqk), full(wvo), full(wg), full(gb),
                  pl.BlockSpec((CONV_K, width), lambda i, j: (0, 0)),
                  pl.BlockSpec((1, width), lambda i, j: (0, 0)),
                  pl.BlockSpec((tr, LANES), lambda i, j: (j, 0)),
                  pl.BlockSpec((tr, LANES), lambda i, j: (j, 0))],
        out_specs=[row(wna.shape[1]), row(wvo.shape[1]), row(width), row(LANES),
                   pl.BlockSpec((1, tr // L, LANES, L), lambda i, j: (i, j, 0, 0))],
        out_shape=[jax.ShapeDtypeStruct((b, t, wna.shape[1]), BF16),
                   jax.ShapeDtypeStruct((b, t, wvo.shape[1]), BF16),
                   jax.ShapeDtypeStruct((b, t, width), BF16),
                   jax.ShapeDtypeStruct((b, t, LANES), F32),
                   jax.ShapeDtypeStruct((b, t // L, LANES, L), F32)],
        scratch_shapes=[pltpu.VMEM((tr + 2 * HALO, 2 * LANES), F32)] * (width // (2 * LANES)),
        compiler_params=_cparams("parallel", "parallel"),
        name="inproj" if rope else "inproj_ctx",
    )(x, x, x, shift, scale, wna, wqk, wvo, wg, gb, conv_w, conv_b.reshape(1, width), cos_t, sin_t)


def _rope_tables(n):
    pos = jnp.arange(n)
    half = ML_HEAD_DIM // 4
    inv = ROPE_BASE ** (-jnp.arange(half, dtype=F32) / half)

    def axis_tables(p):
        ang = p.astype(F32)[:, None] * inv[None, :]
        c, s = jnp.cos(ang), jnp.sin(ang)
        return jnp.concatenate([c, c], axis=-1), jnp.concatenate([-s, s], axis=-1)

    cr, sr = axis_tables(pos // GRID_W)
    cc, sc = axis_tables(pos % GRID_W)
    return jnp.concatenate([cr, cc], axis=-1), jnp.concatenate([sr, sc], axis=-1)


def _mlstm_direction(d, qk, v, gcol, grow, cn_s, m_s):
    L = ML_CHUNK
    row = lax.broadcasted_iota(jnp.int32, (L, L), 0)
    col = lax.broadcasted_iota(jnp.int32, (L, L), 1)
    fwd = d == 0
    keep = (row >= col) if fwd else (row <= col)
    ones = jnp.ones((L, ML_HEAD_DIM), BF16)
    end = L - 1 if fwd else 0
    outs = []
    for h in range(ML_HEADS):
        ci = GATE_RAW + (2 * d) * ML_HEADS + h
        cf = (GATE_PREFIX if fwd else GATE_SUFFIX) + (2 * d + 1) * ML_HEADS + h
        bc = jnp.broadcast_to(gcol[:, cf:cf + 1], (L, L))
        ic = jnp.broadcast_to(gcol[:, ci:ci + 1], (L, L))
        br = grow[cf:cf + 1, :]
        ir = grow[ci:ci + 1, :]
        m_prev = m_s[d, h][0:1, :]
        dlog = jnp.where(keep, bc - br + ir, NEG)
        m_t = jnp.maximum(bc + m_prev, jnp.max(dlog, axis=1, keepdims=True))
        dw = jnp.exp((dlog - m_t).astype(BF16))
        inter = jnp.exp(bc + m_prev - m_t)
        qh = qk[:, h * LANES:(h + 1) * LANES]
        kh = qk[:, ML_WIDTH + h * LANES:ML_WIDTH + (h + 1) * LANES]
        vp = jnp.concatenate([v[:, h * LANES:(h + 1) * LANES], ones], axis=1)
        s = lax.dot_general(qh, kh, (((1,), (1,)), ((), ())), preferred_element_type=F32).astype(BF16) * dw
        cn = cn_s[d, h]
        a1 = jnp.dot(s, vp, preferred_element_type=F32)
        a2 = jnp.dot(qh, cn.astype(BF16), preferred_element_type=F32)
        num = a1[:, :ML_HEAD_DIM] + inter * a2[:, :ML_HEAD_DIM]
        den = a1[:, ML_HEAD_DIM:] + inter * a2[:, ML_HEAD_DIM:]
        outs.append(num / jnp.maximum(jnp.abs(den), jnp.exp(-m_t)))
        b_end = jnp.broadcast_to(br[:, end:end + 1], (1, L))
        g_row = b_end - br + ir
        m_new = jnp.maximum(b_end + m_prev, jnp.max(g_row, axis=1, keepdims=True))
        decay = jnp.exp(b_end + m_prev - m_new)
        wgt = jnp.exp((b_end - bc + ic - m_new).astype(BF16))
        kw = kh * wgt
        upd = lax.dot_general(kw, vp, (((0,), (0,)), ((), ())), preferred_element_type=F32)
        cn_s[d, h] = jnp.concatenate([decay, decay], axis=1) * cn + upd
        m_s[d, h] = jnp.broadcast_to(m_new, m_s.shape[2:])
    return jnp.concatenate(outs, axis=1)


def _mlstm_kernel(qkf_ref, vf_ref, gcf_ref, grf_ref, qkb_ref, vb_ref, gcb_ref, grb_ref, c0_ref, m0_ref, *rest, emit_h):
    if emit_h:
        hf_ref, hb_ref, ct_ref, mt_ref, cn_s, m_s = rest
    else:
        ct_ref, mt_ref, cn_s, m_s = rest
        hf_ref = hb_ref = None
    c = pl.program_id(1)
    L = ML_CHUNK

    @pl.when(c == 0)
    def _():
        cn_s[...] = c0_ref[0]
        m_s[...] = m0_ref[0]

    for d, refs, h_ref in ((0, (qkf_ref, vf_ref, gcf_ref, grf_ref), hf_ref), (1, (qkb_ref, vb_ref, gcb_ref, grb_ref), hb_ref)):
        qk_ref, v_ref, gc_ref, gr_ref = refs
        n_sub = qk_ref.shape[1] // L
        for s in (range(n_sub) if d == 0 else reversed(range(n_sub))):
            rows = slice(s * L, (s + 1) * L)
            h = _mlstm_direction(d, qk_ref[0, rows, :], v_ref[0, rows, :], gc_ref[0, rows, :], gr_ref[0, s], cn_s, m_s)
            if h_ref is not None:
                h_ref[0, rows, :] = h.astype(h_ref.dtype)

    @pl.when(c == pl.num_programs(1) - 1)
    def _():
        ct_ref[0] = cn_s[...]
        mt_ref[0] = m_s[...]


def _mlstm(qk, zvo, gcol, grow, c0, m0, emit_h):
    b, t, _ = qk.shape
    n_sub = min(MLSTM_CHUNKS_PER_STEP, t // ML_CHUNK)
    R = n_sub * ML_CHUNK
    assert t % R == 0
    nc = t // R
    vcol = 0
    f_idx = lambda i, c: (i, c, 0)
    b_idx = lambda i, c: (i, nc - 1 - c, 0)
    st_c = pl.BlockSpec((1, 2, ML_HEADS, ML_HEAD_DIM, 2 * ML_HEAD_DIM), lambda i, c: (i, 0, 0, 0, 0))
    st_m = pl.BlockSpec((1, 2, ML_HEADS, 8, LANES), lambda i, c: (i, 0, 0, 0, 0))
    out_specs = [st_c, st_m]
    out_shape = [jax.ShapeDtypeStruct(c0.shape, F32), jax.ShapeDtypeStruct(m0.shape, F32)]
    if emit_h:
        out_specs = [pl.BlockSpec((1, R, ML_WIDTH), f_idx), pl.BlockSpec((1, R, ML_WIDTH), b_idx)] + out_specs
        out_shape = [jax.ShapeDtypeStruct((b, t, ML_WIDTH), BF16)] * 2 + out_shape
    return pl.pallas_call(
        functools.partial(_mlstm_kernel, emit_h=emit_h),
        grid=(b, nc),
        in_specs=[pl.BlockSpec((1, R, 2 * ML_WIDTH), f_idx),
                  pl.BlockSpec((1, R, ML_WIDTH), lambda i, c: (i, c, vcol)),
                  pl.BlockSpec((1, R, LANES), f_idx),
                  pl.BlockSpec((1, n_sub, LANES, ML_CHUNK), lambda i, c: (i, c, 0, 0)),
                  pl.BlockSpec((1, R, 2 * ML_WIDTH), b_idx),
                  pl.BlockSpec((1, R, ML_WIDTH), lambda i, c: (i, nc - 1 - c, vcol)),
                  pl.BlockSpec((1, R, LANES), b_idx),
                  pl.BlockSpec((1, n_sub, LANES, ML_CHUNK), lambda i, c: (i, nc - 1 - c, 0, 0)),
                  st_c, st_m],
        out_specs=out_specs,
        out_shape=out_shape,
        scratch_shapes=[pltpu.VMEM((2, ML_HEADS, ML_HEAD_DIM, 2 * ML_HEAD_DIM), F32),
                        pltpu.VMEM((2, ML_HEADS, 8, LANES), F32)],
        compiler_params=_cparams("parallel", "arbitrary"),
        name="mlstm" if emit_h else "mlstm_ctx",
    )(qk, zvo, gcol, grow, qk, zvo, gcol, grow, c0, m0)


def _lane_in(shape, start, width):
    lane = lax.broadcasted_iota(jnp.int32, shape, 1)
    return (lane >= start) & (lane < start + width)


def _na_kernel(q_ref, k0_ref, k1_ref, k2_ref, v0_ref, v1_ref, v2_ref, kc_ref, vc_ref, bias_ref, o_ref):
    nq = q_ref.shape[1]
    lane = lax.broadcasted_iota(jnp.int32, (nq, LANES), 1)
    nt = (((1,), (1,)), ((), ()))
    for p in range(NA_HEADS // 2):
        sl = slice(p * LANES, (p + 1) * LANES)
        q2 = q_ref[0, :, sl] * (NA_HEAD_DIM ** -0.5)
        kwin = jnp.concatenate([k0_ref[0, :, sl], k1_ref[0, :, sl], k2_ref[0, :, sl]], axis=0)
        vwin = jnp.concatenate([v0_ref[0, :, sl], v1_ref[0, :, sl], v2_ref[0, :, sl]], axis=0)
        kc = kc_ref[0, :, sl]
        vwin = jnp.concatenate([vwin, jnp.ones_like(vwin)], axis=1)
        vc = jnp.concatenate([vc_ref[0, :, sl], jnp.ones_like(kc)], axis=1)
        halves = []
        for a in range(2):
            in_head = (lane >= a * NA_HEAD_DIM) & (lane < (a + 1) * NA_HEAD_DIM)
            qm = jnp.where(in_head, q2, jnp.zeros_like(q2))
            s_win = lax.dot_general(qm, kwin, nt, preferred_element_type=F32) + bias_ref[0, 2 * p + a]
            s_ctx = lax.dot_general(qm, kc, nt, preferred_element_type=F32)
            m = jnp.maximum(jnp.max(s_win, axis=1, keepdims=True), jnp.max(s_ctx, axis=1, keepdims=True))
            p_win = jnp.exp((s_win - m).astype(BF16))
            p_ctx = jnp.exp((s_ctx - m).astype(BF16))
            o = jnp.dot(p_win, vwin, preferred_element_type=F32) + jnp.dot(p_ctx, vc, preferred_element_type=F32)
            halves.append(o[:, :LANES] / o[:, LANES:])
        o_ref[0, :, sl] = jnp.where(lane < NA_HEAD_DIM, halves[0], halves[1]).astype(o_ref.dtype)


def _na_bias_tables(rpb, rows):
    R = NA_ROWS_PER_STEP
    nblk = rows // R
    kr = NA_WIN_ROWS
    cq = np.arange(GRID_W)
    cstart = np.clip(cq - NA_WIN_COLS // 2, 0, GRID_W - NA_WIN_COLS)
    ck = np.arange(GRID_W)
    col_ok = (ck[None, :] >= cstart[:, None]) & (ck[None, :] < cstart[:, None] + NA_WIN_COLS)
    col_off = np.where(col_ok, ck[None, :] - cq[:, None] + NA_WIN_COLS - 1, 0)
    row_ok = np.zeros((3, R, 3 * R), bool)
    row_off = np.zeros((3, R, 3 * R), np.int64)
    for vi, j in enumerate((0, 1, nblk - 1)):
        for i in range(R):
            r = j * R + i
            r0 = min(max(r - kr // 2, 0), rows - kr)
            for t in range(3):
                jb = j - 1 + t
                if jb < 0 or jb >= nblk:
                    continue
                for rr in range(R):
                    krow = jb * R + rr
                    if r0 <= krow < r0 + kr:
                        row_ok[vi, i, t * R + rr] = True
                        row_off[vi, i, t * R + rr] = krow - r + NA_WIN_ROWS - 1
    col_sel = (np.arange(rpb.shape[2])[None, None, :] == col_off[:, :, None]) & col_ok[:, :, None]
    row_sel = (np.arange(rpb.shape[1])[None, None, None, :] == row_off[..., None]) & row_ok[..., None]
    by_col = jnp.einsum('hrc,qkc->hrqk', rpb, jnp.asarray(col_sel, F32), precision=HIGHEST)
    vals = jnp.einsum('vixr,hrqk->vhiqxk', jnp.asarray(row_sel, F32), by_col, precision=HIGHEST)
    ok = row_ok[:, None, :, None, :, None] & col_ok[None, None, None, :, None, :]
    vals = jnp.where(jnp.asarray(ok), vals, NEG)
    return vals.reshape(3, NA_HEADS, R * GRID_W, 3 * R * GRID_W)


def _na(zna, zcna, bias, rows):
    b, n, _ = zna.shape
    ctx = zcna.shape[1]
    R = NA_ROWS_PER_STEP
    nq = R * GRID_W
    nblk = rows // R
    kb = lambda col, off: pl.BlockSpec(
        (1, nq, NA_WIDTH), lambda i, j: (i, jnp.clip(j + off, 0, nblk - 1), col))
    variant = lambda i, j: (jnp.where(j == 0, 0, jnp.where(j == nblk - 1, 2, 1)), 0, 0, 0)
    return pl.pallas_call(
        _na_kernel,
        grid=(b, nblk),
        in_specs=[pl.BlockSpec((1, nq, NA_WIDTH), lambda i, j: (i, j, 0)),
                  kb(1, -1), kb(1, 0), kb(1, 1), kb(2, -1), kb(2, 0), kb(2, 1),
                  pl.BlockSpec((1, ctx, NA_WIDTH), lambda i, j: (i, 0, 1)),
                  pl.BlockSpec((1, ctx, NA_WIDTH), lambda i, j: (i, 0, 2)),
                  pl.BlockSpec((1, NA_HEADS, nq, 3 * nq), variant)],
        out_specs=pl.BlockSpec((1, nq, NA_WIDTH), lambda i, j: (i, j, 0)),
        out_shape=jax.ShapeDtypeStruct((b, n, NA_WIDTH), BF16),
        compiler_params=_cparams("parallel", "parallel"),
        name="na",
    )(zna, zna, zna, zna, zna, zna, zna, zcna, zcna, bias)


def _mix_kernel(na_ref, hf_ref, hb_ref, o_ref, x_ref, g1_ref, sh2_ref, sc2_ref, wo_ref, mg_ref, l1g_ref, l1b_ref,
                wr_ref, br_ref, xmid_ref, rt_ref, rw_ref, cnt_ref, carry_s, *, alpha):
    first = (pl.program_id(0) == 0) & (pl.program_id(1) == 0)

    @pl.when(first)
    def _():
        carry_s[...] = jnp.zeros_like(carry_s)

    tm = x_ref.shape[1]
    h = hf_ref[0].astype(F32) + hb_ref[0].astype(F32)
    parts = []
    for hd in range(ML_HEADS):
        hh = h[:, hd * LANES:(hd + 1) * LANES]
        parts.append(hh * lax.rsqrt(jnp.mean(hh * hh, axis=-1, keepdims=True) + LN_EPS))
    hn = jnp.concatenate(parts, axis=1)
    ml = (hn * mg_ref[...] * jax.nn.sigmoid(o_ref[0].astype(F32))).astype(BF16)
    mix = (jnp.dot(na_ref[0], wo_ref[0:NA_WIDTH, :], preferred_element_type=F32)
           + jnp.dot(ml, wo_ref[NA_WIDTH:, :], preferred_element_type=F32))
    xmid = _ln_rows(alpha * x_ref[0] + g1_ref[0] * mix) * l1g_ref[...] + l1b_ref[...]
    xmid_ref[0] = xmid

    xt = (_ln_rows(xmid) * (1.0 + sc2_ref[0]) + sh2_ref[0]).astype(BF16)
    logits = jnp.dot(xt, wr_ref[...], preferred_element_type=F32) + br_ref[...]
    lane = lax.broadcasted_iota(jnp.int32, (tm, LANES), 1)
    is_g = lane < N_GROUPS
    gl = jnp.where(is_g, logits, NEG)
    gmax = jnp.max(gl, axis=1, keepdims=True)
    grp = jnp.min(jnp.where(gl == gmax, lane, LANES), axis=1, keepdims=True)
    gsum = jnp.sum(jnp.where(is_g, jnp.exp(gl - gmax), 0.0), axis=1, keepdims=True)
    grp_w = 1.0 / gsum
    lo = N_GROUPS + EXPERTS_PER_GROUP * grp
    el = jnp.where((lane >= lo) & (lane < lo + EXPERTS_PER_GROUP), logits, NEG)
    t1 = jnp.max(el, axis=1, keepdims=True)
    i1 = jnp.min(jnp.where(el == t1, lane, LANES), axis=1, keepdims=True)
    el2 = jnp.where(lane == i1, NEG, el)
    t2 = jnp.max(el2, axis=1, keepdims=True)
    i2 = jnp.min(jnp.where(el2 == t2, lane, LANES), axis=1, keepdims=True)
    e21 = jnp.exp(t2 - t1)
    w0 = grp_w / (1.0 + e21)
    w1 = grp_w * e21 / (1.0 + e21)

    hit1 = lane == i1
    hit2 = lane == i2
    onehot = (hit1 | hit2).astype(BF16)
    r_i = lax.broadcasted_iota(jnp.int32, (tm, tm), 0)
    c_i = lax.broadcasted_iota(jnp.int32, (tm, tm), 1)
    before = (r_i > c_i).astype(BF16)
    prefix = jnp.dot(before, onehot, preferred_element_type=F32) + carry_s[0:1, :]
    rank0 = jnp.sum(jnp.where(hit1, prefix, 0.0), axis=1, keepdims=True)
    rank1 = jnp.sum(jnp.where(hit2, prefix, 0.0), axis=1, keepdims=True)
    total = carry_s[0:1, :] + jnp.sum(onehot.astype(F32), axis=0, keepdims=True)
    carry_s[...] = jnp.broadcast_to(total, carry_s.shape)
    cnt_ref[...] = jnp.broadcast_to(total, cnt_ref.shape)

    rf = jnp.where(lane == 0, (i1 - N_GROUPS).astype(F32),
                   jnp.where(lane == 1, (i2 - N_GROUPS).astype(F32),
                             jnp.where(lane == 2, rank0, jnp.where(lane == 3, rank1, 0.0))))
    rt_ref[0] = rf.T[0:SUBLANES, :].astype(jnp.int32)
    rw_ref[0] = jnp.where(lane == 0, w0, jnp.where(lane == 1, w1, 0.0))


def _mix(na, hf, hb, zvo, x, g1, sh2, sc2, wo, mg, l1g, l1b, wr, br, alpha, tm):
    b, n, d = x.shape
    row = lambda w: pl.BlockSpec((1, tm, w), lambda i, j: (i, j, 0))
    vec = pl.BlockSpec((1, 1, d), lambda i, j: (i, 0, 0))
    full = lambda a: pl.BlockSpec(a.shape, lambda i, j: (0,) * a.ndim)
    ocol = 1
    return pl.pallas_call(
        functools.partial(_mix_kernel, alpha=alpha),
        grid=(b, n // tm),
        in_specs=[row(NA_WIDTH), row(ML_WIDTH), row(ML_WIDTH),
                  pl.BlockSpec((1, tm, ML_WIDTH), lambda i, j: (i, j, ocol)),
                  row(d), vec, vec, vec, full(wo), full(mg), full(l1g), full(l1b), full(wr), full(br)],
        out_specs=[row(d), pl.BlockSpec((1, SUBLANES, tm), lambda i, j: (i * (n // tm) + j, 0, 0)), row(LANES),
                   pl.BlockSpec((SUBLANES, LANES), lambda i, j: (0, 0))],
        out_shape=[jax.ShapeDtypeStruct((b, n, d), F32),
                   jax.ShapeDtypeStruct((b * (n // tm), SUBLANES, tm), jnp.int32),
                   jax.ShapeDtypeStruct((b, n, LANES), F32),
                   jax.ShapeDtypeStruct((SUBLANES, LANES), F32)],
        scratch_shapes=[pltpu.VMEM((8, LANES), F32)],
        compiler_params=_cparams("arbitrary", "arbitrary"),
        name="mix",
    )(na, hf, hb, zvo, x, g1, sh2, sc2, wo, mg, l1g, l1b, wr, br)


def _zero_fill_padding(pad_base_ref, pad_len_ref, nused_ref, xs_ref, zero_s, sem, wait):
    zero_t, xs_t = _token_view(zero_s), _token_view(xs_ref)
    tb = zero_t.shape[0]

    def run(copy):
        copy.wait() if wait else copy.start()

    def fill(off, nrows):
        run(pltpu.make_async_copy(zero_t.at[pl.ds(0, nrows)], xs_t.at[pl.ds(off, nrows)], sem))

    def per_expert(e, _):
        plen = pad_len_ref[e]
        base = pad_base_ref[e]
        bit = tb // 2
        while bit >= 1:
            off = base + (plen & ~(2 * bit - 1))

            @pl.when((plen & bit) != 0)
            def _(bit=bit, off=off):
                fill(off, bit)

            bit //= 2
        return 0

    lax.fori_loop(0, N_EXPERTS, per_expert, 0)

    def per_block(i, _):
        fill(i * tb, tb)
        return 0

    lax.fori_loop(nused_ref[0], xs_t.shape[0] // tb, per_block, 0)


ROW_WORDS = 4


def _token_view(ref):
    return ref.reshape(ref.shape[0] // ROW_WORDS, ROW_WORDS, LANES)


def _bf16_bits(v):
    return lax.bitcast_convert_type(v.astype(BF16).astype(F32), jnp.int32)


def _store_token_rows(dst_ref, base, val):
    tm, d = val.shape
    assert d == 2 * ROW_WORDS * LANES
    for s in range(ROW_WORDS):
        lo = _bf16_bits(val[:, s * LANES:(s + 1) * LANES])
        hi = _bf16_bits(val[:, (s + ROW_WORDS) * LANES:(s + ROW_WORDS + 1) * LANES])
        dst_ref[pl.ds(base * ROW_WORDS + s, tm, stride=ROW_WORDS), :] = hi | lax.shift_right_logical(lo, 16)


def _load_token_rows(src_ref, base, tm, dtype):
    words = [src_ref[pl.ds(base * ROW_WORDS + s, tm, stride=ROW_WORDS), :] for s in range(ROW_WORDS)]
    lo = [lax.bitcast_convert_type(w << 16, F32).astype(dtype) for w in words]
    hi = [lax.bitcast_convert_type(w & -65536, F32).astype(dtype) for w in words]
    return jnp.concatenate(lo + hi, axis=1)


def _dispatch_kernel(pad_base_ref, pad_len_ref, nused_ref, dest_ref, xmid_ref, sh2_ref, sc2_ref, xs_ref,
                     xt_s, zero_s, sem, zsem):
    tm = xmid_ref.shape[1]
    step = pl.program_id(0) * pl.num_programs(1) + pl.program_id(1)
    nsteps = pl.num_programs(0) * pl.num_programs(1)

    def wait_step_copies():
        for _ in range(2):
            pltpu.make_async_copy(_token_view(xt_s), _token_view(xs_ref).at[pl.ds(0, tm)], sem).wait()

    @pl.when(step == 0)
    def _():
        zero_s[...] = jnp.zeros_like(zero_s)
        _zero_fill_padding(pad_base_ref, pad_len_ref, nused_ref, xs_ref, zero_s, zsem, False)

    xt = _ln_rows(xmid_ref[0]) * (1.0 + sc2_ref[0]) + sh2_ref[0]

    @pl.when(step > 0)
    def _():
        wait_step_copies()

    _store_token_rows(xt_s, 0, xt)

    def start(r, _):
        for k in range(2):
            pltpu.make_async_copy(_token_view(xt_s).at[r], _token_view(xs_ref).at[dest_ref[0, k, r]],
                                  sem).start(priority=k)
        return 0

    lax.fori_loop(0, tm, start, 0, unroll=8)

    @pl.when(step == 0)
    def _():
        _zero_fill_padding(pad_base_ref, pad_len_ref, nused_ref, xs_ref, zero_s, zsem, True)

    @pl.when(step == nsteps - 1)
    def _():
        wait_step_copies()


def _dispatch(xmid, sh2, sc2, dest, pad_base, pad_len, nused, cap, tm):
    b, n, d = xmid.shape
    assert d == 2 * ROW_WORDS * LANES
    nt = n // tm
    vec = pl.BlockSpec((1, 1, d), lambda i, j, *_: (i, 0, 0))
    return pl.pallas_call(
        _dispatch_kernel,
        grid_spec=pltpu.PrefetchScalarGridSpec(
            num_scalar_prefetch=3,
            grid=(b, nt),
            in_specs=[pl.BlockSpec((1, 2, tm), lambda i, j, *_: (i * nt + j, 0, 0), memory_space=pltpu.SMEM),
                      pl.BlockSpec((1, tm, d), lambda i, j, *_: (i, j, 0)), vec, vec],
            out_specs=pl.BlockSpec(memory_space=pl.ANY),
            scratch_shapes=[pltpu.VMEM((tm * ROW_WORDS, LANES), ROW_DTYPE),
                            pltpu.VMEM((EXPERT_ROWS * ROW_WORDS, LANES), ROW_DTYPE),
                            pltpu.SemaphoreType.DMA, pltpu.SemaphoreType.DMA]),
        out_shape=jax.ShapeDtypeStruct((cap * ROW_WORDS, LANES), ROW_DTYPE),
        compiler_params=_cparams("arbitrary", "arbitrary"),
        name="dispatch",
    )(pad_base, pad_len, nused, dest, xmid, sh2, sc2)


def _expert_kernel(be_ref, nv_ref, xs_ref, w1_ref, w3_ref, w2_ref, ys_ref, w1b, w3b, w2b):
    tb = xs_ref.shape[0] // ROW_WORDS
    i = pl.program_id(0)
    e = be_ref[i]
    changed = (i == 0) | (be_ref[jnp.maximum(i - 1, 0)] != e)

    @pl.when(changed)
    def _():
        w1b[...] = w1_ref[0].astype(BF16)
        w3b[...] = w3_ref[0].astype(BF16)
        w2b[...] = w2_ref[0].astype(BF16)

    nv = nv_ref[i]

    @pl.when(nv > 0)
    def _():
        xb = _load_token_rows(xs_ref, 0, tb, BF16)
        h1 = jnp.dot(xb, w1b[...], preferred_element_type=F32)
        h3 = jnp.dot(xb, w3b[...], preferred_element_type=F32)
        a = (_silu(h1) * h3).astype(BF16)
        _store_token_rows(ys_ref, 0, jnp.dot(a, w2b[...], preferred_element_type=F32))

    @pl.when(nv == 0)
    def _():
        ys_ref[...] = jnp.zeros_like(ys_ref)


def _experts(xs, block_e, block_nv, w1, w3, w2):
    d, hid = w1.shape[1], w1.shape[2]
    tb = EXPERT_ROWS
    rows = pl.BlockSpec((tb * ROW_WORDS, LANES), lambda i, be, nv: (i, 0))
    return pl.pallas_call(
        _expert_kernel,
        grid_spec=pltpu.PrefetchScalarGridSpec(
            num_scalar_prefetch=2,
            grid=(xs.shape[0] // (tb * ROW_WORDS),),
            in_specs=[rows,
                      pl.BlockSpec((1, d, hid), lambda i, be, nv: (be[i], 0, 0)),
                      pl.BlockSpec((1, d, hid), lambda i, be, nv: (be[i], 0, 0)),
                      pl.BlockSpec((1, hid, d), lambda i, be, nv: (be[i], 0, 0))],
            out_specs=rows,
            scratch_shapes=[pltpu.VMEM((d, hid), BF16), pltpu.VMEM((d, hid), BF16), pltpu.VMEM((hid, d), BF16)]),
        out_shape=jax.ShapeDtypeStruct(xs.shape, xs.dtype),
        compiler_params=_cparams("arbitrary"),
        name="experts",
    )(block_e, block_nv, xs, w1, w3, w2)


def _combine_kernel(dcur_ref, dnext_ref, xmid_ref, rw_ref, g2_ref, l2g_ref, l2b_ref, ys_ref, o_ref,
                    y0_s, y1_s, sem, *, alpha):
    tm = xmid_ref.shape[1]
    step = pl.program_id(0) * pl.num_programs(1) + pl.program_id(1)
    nsteps = pl.num_programs(0) * pl.num_programs(1)
    slot = step % 2

    def gather(dest_ref, into):
        def start(r, _):
            for k, buf in ((0, y0_s), (1, y1_s)):
                pltpu.make_async_copy(_token_view(ys_ref).at[dest_ref[0, k, r]],
                                      _token_view(buf).at[into * tm + r], sem.at[into]).start(priority=k)
            return 0

        lax.fori_loop(0, tm, start, 0, unroll=8)

    @pl.when(step == 0)
    def _():
        gather(dcur_ref, 0)

    @pl.when(step + 1 < nsteps)
    def _():
        gather(dnext_ref, 1 - slot)

    for buf in (y0_s, y1_s):
        pltpu.make_async_copy(_token_view(ys_ref).at[pl.ds(0, tm)], _token_view(buf).at[pl.ds(slot * tm, tm)],
                              sem.at[slot]).wait()
    base = slot * tm
    rw = rw_ref[0]
    moe = (rw[:, 0:1] * _load_token_rows(y0_s, base, tm, F32) + rw[:, 1:2] * _load_token_rows(y1_s, base, tm, F32))
    o_ref[0] = _ln_rows(alpha * xmid_ref[0] + g2_ref[0] * moe) * l2g_ref[...] + l2b_ref[...]


def _combine(xmid, rw, g2, l2g, l2b, ys, dest, alpha, tm):
    b, n, d = xmid.shape
    nt = n // tm
    full = lambda a: pl.BlockSpec(a.shape, lambda i, j: (0,) * a.ndim)
    return pl.pallas_call(
        functools.partial(_combine_kernel, alpha=alpha),
        grid=(b, nt),
        in_specs=[pl.BlockSpec((1, 2, tm), lambda i, j: (i * nt + j, 0, 0), memory_space=pltpu.SMEM),
                  pl.BlockSpec((1, 2, tm), lambda i, j: (jnp.minimum(i * nt + j + 1, b * nt - 1), 0, 0),
                               memory_space=pltpu.SMEM),
                  pl.BlockSpec((1, tm, d), lambda i, j: (i, j, 0)),
                  pl.BlockSpec((1, tm, LANES), lambda i, j: (i, j, 0)),
                  pl.BlockSpec((1, 1, d), lambda i, j: (i, 0, 0)),
                  full(l2g), full(l2b),
                  pl.BlockSpec(memory_space=pl.ANY)],
        out_specs=pl.BlockSpec((1, tm, d), lambda i, j: (i, j, 0)),
        out_shape=jax.ShapeDtypeStruct((b, n, d), F32),
        scratch_shapes=[pltpu.VMEM((2 * tm * ROW_WORDS, LANES), ys.dtype),
                        pltpu.VMEM((2 * tm * ROW_WORDS, LANES), ys.dtype), pltpu.SemaphoreType.DMA((2,))],
        compiler_params=_cparams("arbitrary", "arbitrary"),
        name="combine",
    )(dest, dest, xmid, rw, g2, l2g, l2b, ys)


def _tile(n, want):
    t = min(n, want)
    assert n % t == 0, (n, t)
    return t


def kernel(x, c, ctx, c_ctx, w_ada, b_ada, w_in, conv_w, conv_b, gate_b, rpb, ml_norm_g, w_out, ln1_g, ln1_b,
           w_router_g, b_router_g, w_router_e, b_router_e, w1, w3, w2, ln2_g, ln2_b):
    B, N, D = x.shape
    T_CTX = ctx.shape[1]
    depth = w_ada.shape[0]
    rows = N // GRID_W
    assert depth == 1 and N % GRID_W == 0 and rows % NA_ROWS_PER_STEP == 0 and rows >= 3 * NA_ROWS_PER_STEP
    assert N % ML_CHUNK == 0 and T_CTX % ML_CHUNK == 0
    alpha = (2.0 * depth) ** 0.25
    l = 0

    pad_rows = -(B + 1) % 8
    cvec = jnp.concatenate([c, c_ctx[None], jnp.zeros((pad_rows, D), F32)], axis=0)
    ada = _ada(cvec, w_ada[l], b_ada[l])
    sh1, sc1, g1, sh2, sc2, g2 = [a[:, None, :] for a in jnp.split(ada[:B], 6, axis=-1)]
    csh1, csc1 = [jnp.broadcast_to(a[None], (B, 1, D)) for a in jnp.split(ada[B:B + 1], 6, axis=-1)[:2]]

    col_ml = 3 * NA_WIDTH
    col_g = col_ml + 4 * ML_WIDTH
    col_v = col_ml + 2 * ML_WIDTH
    wb = w_in[l].astype(BF16)
    wna, wqk, wvo = wb[:, :col_ml], wb[:, col_ml:col_v], wb[:, col_v:col_g]
    n_gate = 4 * ML_HEADS
    wg = jnp.pad(wb[:, col_g:], ((0, 0), (0, LANES - n_gate)))
    gb = jnp.pad(gate_b[l], (0, LANES - n_gate)).reshape(1, LANES)
    cos_t, sin_t = _rope_tables(N)
    zna, zvo, qk_l, gcol_l, grow_l = _inproj(x, sh1, sc1, wna, wqk, wvo, wg, gb, conv_w[l], conv_b[l], cos_t, sin_t,
                                             True, _tile(N, 512))
    zcna, zcvo, qk_c, gcol_c, grow_c = _inproj(ctx, csh1, csc1, wna, wqk, wvo, wg, gb, conv_w[l], conv_b[l],
                                               cos_t[:T_CTX], sin_t[:T_CTX], False, _tile(T_CTX, 256))

    c0 = jnp.zeros((B, 2, ML_HEADS, ML_HEAD_DIM, 2 * ML_HEAD_DIM), F32)
    m0 = jnp.zeros((B, 2, ML_HEADS, 8, LANES), F32)
    c_ctx_end, m_ctx_end = _mlstm(qk_c, zcvo, gcol_c, grow_c, c0, m0, False)
    hf, hb, _, _ = _mlstm(qk_l, zvo, gcol_l, grow_l, c_ctx_end, m_ctx_end, True)

    na = _na(zna, zcna, _na_bias_tables(rpb[l], rows), rows)

    wr = jnp.pad(jnp.concatenate([w_router_g[l], w_router_e[l]], axis=1),
                 ((0, 0), (0, LANES - N_GROUPS - N_EXPERTS))).astype(BF16)
    br = jnp.pad(jnp.concatenate([b_router_g[l], b_router_e[l]]), (0, LANES - N_GROUPS - N_EXPERTS)).reshape(1, LANES)
    tm = _tile(N, 512)
    xmid, rt, rw, counts = _mix(na, hf, hb, zvo, x, g1, sh2, sc2, w_out[l].astype(BF16),
                                ml_norm_g[l].reshape(1, ML_WIDTH), ln1_g[l].reshape(1, D), ln1_b[l].reshape(1, D),
                                wr, br, alpha, tm)

    tb = EXPERT_ROWS
    n_assign = 2 * B * N
    cap = -(-n_assign // tb) * tb + N_EXPERTS * tb
    sizes = counts[0, N_GROUPS:N_GROUPS + N_EXPERTS].astype(jnp.int32)
    padded = (sizes + tb - 1) // tb * tb
    pend = jnp.cumsum(padded)
    pstart = pend - padded
    experts = jnp.arange(N_EXPERTS, dtype=jnp.int32)
    first_row = jnp.sum(jnp.where(rt[:, 0:2, :, None] == experts, pstart, 0), axis=-1)
    dest = first_row + rt[:, 2:4, :]
    blk0 = jnp.arange(cap // tb, dtype=jnp.int32) * tb
    block_e = jnp.minimum(jnp.sum(pend[None, :] <= blk0[:, None], axis=1), N_EXPERTS - 1).astype(jnp.int32)
    is_e = block_e[:, None] == experts
    block_nv = jnp.clip(jnp.sum(jnp.where(is_e, pstart + sizes, 0), axis=1) - blk0, 0, tb).astype(jnp.int32)

    nused = (pend[-1:] // tb).astype(jnp.int32)
    xs = _dispatch(xmid, sh2, sc2, dest, pstart + sizes, padded - sizes, nused, cap, tm)
    ys = _experts(xs, block_e, block_nv, w1[l], w3[l], w2[l])
    return _combine(xmid, rw, g2, ln2_g[l].reshape(1, D), ln2_b[l].reshape(1, D), ys, dest, alpha, tm)
```

```python
import functools

import numpy as np
import jax
import jax.numpy as jnp
from jax import lax
from jax.experimental import pallas as pl
from jax.experimental.pallas import tpu as pltpu

F32 = jnp.float32
BF16 = jnp.bfloat16
ROW_DTYPE = jnp.int32
HIGHEST = lax.Precision.HIGHEST

GRID_W = 64
NA_HEADS = 8
NA_HEAD_DIM = 64
NA_WIDTH = NA_HEADS * NA_HEAD_DIM
NA_WIN_ROWS = 8
NA_WIN_COLS = 16
ML_HEADS = 4
ML_HEAD_DIM = 128
ML_WIDTH = ML_HEADS * ML_HEAD_DIM
ML_CHUNK = 128
CONV_K = 5
N_GROUPS = 8
EXPERTS_PER_GROUP = 8
N_EXPERTS = N_GROUPS * EXPERTS_PER_GROUP
ROPE_BASE = 10000.0
LN_EPS = 1e-5

LANES = 128
SUBLANES = 8
VMEM_LIMIT = 56 * 1024 * 1024

NA_ROWS_PER_STEP = 4
EXPERT_ROWS = 512
COMBINE_CHUNK = 64
MLSTM_CHUNKS_PER_STEP = 4
NEG = -1e30


def _cparams(*sem):
    return pltpu.CompilerParams(dimension_semantics=sem, vmem_limit_bytes=VMEM_LIMIT)


def _silu(v):
    return v * jax.nn.sigmoid(v)


def _ln_rows(v):
    mu = jnp.mean(v, axis=-1, keepdims=True)
    vc = v - mu
    var = jnp.mean(vc * vc, axis=-1, keepdims=True)
    return vc * lax.rsqrt(var + LN_EPS)


def _ada_kernel(c_ref, w_ref, b_ref, o_ref):
    o_ref[...] = jnp.dot(_silu(c_ref[...]), w_ref[...], preferred_element_type=F32,
                         precision=HIGHEST) + b_ref[...]


def _ada(cvec, w, b):
    rows, d = cvec.shape
    cols = w.shape[1]
    tn = 1024
    return pl.pallas_call(
        _ada_kernel,
        grid=(cols // tn,),
        in_specs=[pl.BlockSpec((rows, d), lambda j: (0, 0)),
                  pl.BlockSpec((d, tn), lambda j: (0, j)),
                  pl.BlockSpec((1, tn), lambda j: (0, j))],
        out_specs=pl.BlockSpec((rows, tn), lambda j: (0, j)),
        out_shape=jax.ShapeDtypeStruct((rows, cols), F32),
        compiler_params=_cparams("arbitrary"),
        name="ada",
    )(cvec, w, b.reshape(1, cols))


HALO = 16


def _log_sigmoid(v):
    return jnp.minimum(v, 0.0) - jnp.log1p(jnp.exp(-jnp.abs(v)))


GATE_RAW, GATE_PREFIX, GATE_SUFFIX = 0, 16, 32


def _pack_gates(g):
    L = g.shape[0]
    row = lax.broadcasted_iota(jnp.int32, (L, L), 0)
    col = lax.broadcasted_iota(jnp.int32, (L, L), 1)
    lane = lax.broadcasted_iota(jnp.int32, g.shape, 1)
    lf = _log_sigmoid(g)
    prefix = jnp.dot((row >= col).astype(F32), lf, preferred_element_type=F32, precision=HIGHEST)
    suffix = jnp.dot((row <= col).astype(F32), lf, preferred_element_type=F32, precision=HIGHEST)
    return jnp.where(lane < GATE_PREFIX, g,
                     jnp.where(lane < GATE_SUFFIX, pltpu.roll(prefix, GATE_PREFIX, 1),
                               jnp.where(lane < GATE_SUFFIX + GATE_PREFIX, pltpu.roll(suffix, GATE_SUFFIX, 1), 0.0)))


def _inproj_kernel(x_ref, xp_ref, xn_ref, sh_ref, sc_ref, wna_ref, wqk_ref, wvo_ref, wg_ref, gb_ref, cw_ref, cb_ref,
                   cos_ref, sin_ref, zna_ref, zvo_ref, o_ref, gcol_ref, grow_ref, *pad_refs, rope, tr):
    i = pl.program_id(1)
    last = pl.num_programs(1) - 1
    xa = jnp.concatenate([xp_ref[0], x_ref[0], xn_ref[0]], axis=0)
    ya = (_ln_rows(xa) * (1.0 + sc_ref[0]) + sh_ref[0]).astype(BF16)
    yb = ya[HALO:HALO + tr]
    pad = CONV_K // 2
    width = wqk_ref.shape[1]
    lane = lax.broadcasted_iota(jnp.int32, (tr, LANES), 1)
    first_half = (lane % (ML_HEAD_DIM // 2)) < (ML_HEAD_DIM // 4)
    kscale = ML_HEAD_DIM ** -0.5
    cb = 2 * LANES

    def project_qk(c):
        z = jnp.dot(ya, wqk_ref[:, c * cb:(c + 1) * cb], preferred_element_type=F32)
        pad_refs[c][HALO:HALO + tr, :] = z[HALO:HALO + tr]
        pad_refs[c][0:HALO, :] = jnp.where(i > 0, z[0:HALO], 0.0)
        pad_refs[c][HALO + tr:2 * HALO + tr, :] = jnp.where(i < last, z[HALO + tr:], 0.0)

    def conv_group(g):
        cols = slice(g * LANES, (g + 1) * LANES)
        pad_ref = pad_refs[g * LANES // cb]
        pc = slice(g * LANES % cb, g * LANES % cb + LANES)
        acc = cw_ref[0:1, cols] * pad_ref[HALO - pad:HALO - pad + tr, pc] + cb_ref[:, cols]
        for j in range(1, CONV_K):
            acc = acc + cw_ref[j:j + 1, cols] * pad_ref[HALO - pad + j:HALO - pad + j + tr, pc]
        ug = _silu(acc)
        if rope:
            partner = jnp.where(first_half, pltpu.roll(ug, LANES - ML_HEAD_DIM // 4, 1),
                                pltpu.roll(ug, ML_HEAD_DIM // 4, 1))
            ug = ug * cos_ref[...] + partner * sin_ref[...]
        if g >= ML_HEADS:
            ug = ug * kscale
        o_ref[0, :, cols] = ug.astype(BF16)

    other = ([(zna_ref, wna_ref, c) for c in range(wna_ref.shape[1] // cb)]
             + [(zvo_ref, wvo_ref, c) for c in range(wvo_ref.shape[1] // cb)])

    def project_other(n):
        for _ in range(n):
            if other:
                dst, w, c = other.pop(0)
                cols = slice(c * cb, (c + 1) * cb)
                dst[0, :, cols] = jnp.dot(yb, w[:, cols], preferred_element_type=F32).astype(BF16)

    n_qk = width // cb
    per_round = -(-len(other) // n_qk)
    project_qk(0)
    for c in range(n_qk):
        if c + 1 < n_qk:
            project_qk(c + 1)
        project_other(per_round)
        for g in range(c * cb // LANES, (c + 1) * cb // LANES):
            conv_group(g)
    project_other(len(other))
    gates = jnp.dot(yb, wg_ref[...], preferred_element_type=F32) + gb_ref[...]
    for c in range(tr // ML_CHUNK):
        packed = _pack_gates(gates[c * ML_CHUNK:(c + 1) * ML_CHUNK, :])
        gcol_ref[0, c * ML_CHUNK:(c + 1) * ML_CHUNK, :] = packed
        grow_ref[0, c] = packed.T


def _inproj(x, shift, scale, wna, wqk, wvo, wg, gb, conv_w, conv_b, cos_t, sin_t, rope, tr):
    b, t, d = x.shape
    width = 2 * ML_WIDTH
    nh = tr // HALO
    nblk = t // HALO
    L = ML_CHUNK
    full = lambda a: pl.BlockSpec(a.shape, lambda i, j: (0,) * a.ndim)
    row = lambda w: pl.BlockSpec((1, tr, w), lambda i, j: (i, j, 0))
    vec = pl.BlockSpec((1, 1, d), lambda i, j: (i, 0, 0))
    return pl.pallas_call(
        functools.partial(_inproj_kernel, rope=rope, tr=tr),
        grid=(b, t // tr),
        in_specs=[row(d),
                  pl.BlockSpec((1, HALO, d), lambda i, j: (i, jnp.maximum(j * nh - 1, 0), 0)),
                  pl.BlockSpec((1, HALO, d), lambda i, j: (i, jnp.minimum((j + 1) * nh, nblk - 1), 0)),
                  vec, vec, full(wna), full(wqk), full(wvo), full(wg), full(gb),
                  pl.BlockSpec((CONV_K, width), lambda i, j: (0, 0)),
                  pl.BlockSpec((1, width), lambda i, j: (0, 0)),
                  pl.BlockSpec((tr, LANES), lambda i, j: (j, 0)),
                  pl.BlockSpec((tr, LANES), lambda i, j: (j, 0))],
        out_specs=[row(wna.shape[1]), row(wvo.shape[1]), row(width), row(LANES),
                   pl.BlockSpec((1, tr // L, LANES, L), lambda i, j: (i, j, 0, 0))],
        out_shape=[jax.ShapeDtypeStruct((b, t, wna.shape[1]), BF16),
                   jax.ShapeDtypeStruct((b, t, wvo.shape[1]), BF16),
                   jax.ShapeDtypeStruct((b, t, width), BF16),
                   jax.ShapeDtypeStruct((b, t, LANES), F32),
                   jax.ShapeDtypeStruct((b, t // L, LANES, L), F32)],
        scratch_shapes=[pltpu.VMEM((tr + 2 * HALO, 2 * LANES), F32)] * (width // (2 * LANES)),
        compiler_params=_cparams("parallel", "parallel"),
        name="inproj" if rope else "inproj_ctx",
    )(x, x, x, shift, scale, wna, wqk, wvo, wg, gb, conv_w, conv_b.reshape(1, width), cos_t, sin_t)


def _rope_tables(n):
    pos = jnp.arange(n)
    half = ML_HEAD_DIM // 4
    inv = ROPE_BASE ** (-jnp.arange(half, dtype=F32) / half)

    def axis_tables(p):
        ang = p.astype(F32)[:, None] * inv[None, :]
        c, s = jnp.cos(ang), jnp.sin(ang)
        return jnp.concatenate([c, c], axis=-1), jnp.concatenate([-s, s], axis=-1)

    cr, sr = axis_tables(pos // GRID_W)
    cc, sc = axis_tables(pos % GRID_W)
    return jnp.concatenate([cr, cc], axis=-1), jnp.concatenate([sr, sc], axis=-1)


def _mlstm_direction(d, qk, v, gcol, grow, cn_s, m_s):
    L = ML_CHUNK
    row = lax.broadcasted_iota(jnp.int32, (L, L), 0)
    col = lax.broadcasted_iota(jnp.int32, (L, L), 1)
    fwd = d == 0
    keep = (row >= col) if fwd else (row <= col)
    ones = jnp.ones((L, ML_HEAD_DIM), BF16)
    end = L - 1 if fwd else 0
    outs = []
    for h in range(ML_HEADS):
        ci = GATE_RAW + (2 * d) * ML_HEADS + h
        cf = (GATE_PREFIX if fwd else GATE_SUFFIX) + (2 * d + 1) * ML_HEADS + h
        bc = jnp.broadcast_to(gcol[:, cf:cf + 1], (L, L))
        ic = jnp.broadcast_to(gcol[:, ci:ci + 1], (L, L))
        br = grow[cf:cf + 1, :]
        ir = grow[ci:ci + 1, :]
        m_prev = m_s[d, h][0:1, :]
        dlog = jnp.where(keep, bc - br + ir, NEG)
        m_t = jnp.maximum(bc + m_prev, jnp.max(dlog, axis=1, keepdims=True))
        dw = jnp.exp((dlog - m_t).astype(BF16))
        inter = jnp.exp(bc + m_prev - m_t)
        qh = qk[:, h * LANES:(h + 1) * LANES]
        kh = qk[:, ML_WIDTH + h * LANES:ML_WIDTH + (h + 1) * LANES]
        vp = jnp.concatenate([v[:, h * LANES:(h + 1) * LANES], ones], axis=1)
        s = lax.dot_general(qh, kh, (((1,), (1,)), ((), ())), preferred_element_type=F32).astype(BF16) * dw
        cn = cn_s[d, h]
        a1 = jnp.dot(s, vp, preferred_element_type=F32)
        a2 = jnp.dot(qh, cn.astype(BF16), preferred_element_type=F32)
        num = a1[:, :ML_HEAD_DIM] + inter * a2[:, :ML_HEAD_DIM]
        den = a1[:, ML_HEAD_DIM:] + inter * a2[:, ML_HEAD_DIM:]
        outs.append(num / jnp.maximum(jnp.abs(den), jnp.exp(-m_t)))
        b_end = jnp.broadcast_to(br[:, end:end + 1], (1, L))
        g_row = b_end - br + ir
        m_new = jnp.maximum(b_end + m_prev, jnp.max(g_row, axis=1, keepdims=True))
        decay = jnp.exp(b_end + m_prev - m_new)
        wgt = jnp.exp((b_end - bc + ic - m_new).astype(BF16))
        kw = kh * wgt
        upd = lax.dot_general(kw, vp, (((0,), (0,)), ((), ())), preferred_element_type=F32)
        cn_s[d, h] = jnp.concatenate([decay, decay], axis=1) * cn + upd
        m_s[d, h] = jnp.broadcast_to(m_new, m_s.shape[2:])
    return jnp.concatenate(outs, axis=1)


def _mlstm_kernel(qkf_ref, vf_ref, gcf_ref, grf_ref, qkb_ref, vb_ref, gcb_ref, grb_ref, c0_ref, m0_ref, *rest, emit_h):
    if emit_h:
        hf_ref, hb_ref, ct_ref, mt_ref, cn_s, m_s = rest
    else:
        ct_ref, mt_ref, cn_s, m_s = rest
        hf_ref = hb_ref = None
    c = pl.program_id(1)
    L = ML_CHUNK

    @pl.when(c == 0)
    def _():
        cn_s[...] = c0_ref[0]
        m_s[...] = m0_ref[0]

    for d, refs, h_ref in ((0, (qkf_ref, vf_ref, gcf_ref, grf_ref), hf_ref), (1, (qkb_ref, vb_ref, gcb_ref, grb_ref), hb_ref)):
        qk_ref, v_ref, gc_ref, gr_ref = refs
        n_sub = qk_ref.shape[1] // L
        for s in (range(n_sub) if d == 0 else reversed(range(n_sub))):
            rows = slice(s * L, (s + 1) * L)
            h = _mlstm_direction(d, qk_ref[0, rows, :], v_ref[0, rows, :], gc_ref[0, rows, :], gr_ref[0, s], cn_s, m_s)
            if h_ref is not None:
                h_ref[0, rows, :] = h.astype(h_ref.dtype)

    @pl.when(c == pl.num_programs(1) - 1)
    def _():
        ct_ref[0] = cn_s[...]
        mt_ref[0] = m_s[...]


def _mlstm(qk, zvo, gcol, grow, c0, m0, emit_h):
    b, t, _ = qk.shape
    n_sub = min(MLSTM_CHUNKS_PER_STEP, t // ML_CHUNK)
    R = n_sub * ML_CHUNK
    assert t % R == 0
    nc = t // R
    vcol = 0
    f_idx = lambda i, c: (i, c, 0)
    b_idx = lambda i, c: (i, nc - 1 - c, 0)
    st_c = pl.BlockSpec((1, 2, ML_HEADS, ML_HEAD_DIM, 2 * ML_HEAD_DIM), lambda i, c: (i, 0, 0, 0, 0))
    st_m = pl.BlockSpec((1, 2, ML_HEADS, 8, LANES), lambda i, c: (i, 0, 0, 0, 0))
    out_specs = [st_c, st_m]
    out_shape = [jax.ShapeDtypeStruct(c0.shape, F32), jax.ShapeDtypeStruct(m0.shape, F32)]
    if emit_h:
        out_specs = [pl.BlockSpec((1, R, ML_WIDTH), f_idx), pl.BlockSpec((1, R, ML_WIDTH), b_idx)] + out_specs
        out_shape = [jax.ShapeDtypeStruct((b, t, ML_WIDTH), BF16)] * 2 + out_shape
    return pl.pallas_call(
        functools.partial(_mlstm_kernel, emit_h=emit_h),
        grid=(b, nc),
        in_specs=[pl.BlockSpec((1, R, 2 * ML_WIDTH), f_idx),
                  pl.BlockSpec((1, R, ML_WIDTH), lambda i, c: (i, c, vcol)),
                  pl.BlockSpec((1, R, LANES), f_idx),
                  pl.BlockSpec((1, n_sub, LANES, ML_CHUNK), lambda i, c: (i, c, 0, 0)),
                  pl.BlockSpec((1, R, 2 * ML_WIDTH), b_idx),
                  pl.BlockSpec((1, R, ML_WIDTH), lambda i, c: (i, nc - 1 - c, vcol)),
                  pl.BlockSpec((1, R, LANES), b_idx),
                  pl.BlockSpec((1, n_sub, LANES, ML_CHUNK), lambda i, c: (i, nc - 1 - c, 0, 0)),
                  st_c, st_m],
        out_specs=out_specs,
        out_shape=out_shape,
        scratch_shapes=[pltpu.VMEM((2, ML_HEADS, ML_HEAD_DIM, 2 * ML_HEAD_DIM), F32),
                        pltpu.VMEM((2, ML_HEADS, 8, LANES), F32)],
        compiler_params=_cparams("parallel", "arbitrary"),
        name="mlstm" if emit_h else "mlstm_ctx",
    )(qk, zvo, gcol, grow, qk, zvo, gcol, grow, c0, m0)


def _lane_in(shape, start, width):
    lane = lax.broadcasted_iota(jnp.int32, shape, 1)
    return (lane >= start) & (lane < start + width)


def _na_kernel(q_ref, k0_ref, k1_ref, k2_ref, v0_ref, v1_ref, v2_ref, kc_ref, vc_ref, bias_ref, o_ref):
    nq = q_ref.shape[1]
    lane = lax.broadcasted_iota(jnp.int32, (nq, LANES), 1)
    nt = (((1,), (1,)), ((), ()))
    for p in range(NA_HEADS // 2):
        sl = slice(p * LANES, (p + 1) * LANES)
        q2 = q_ref[0, :, sl] * (NA_HEAD_DIM ** -0.5)
        kwin = jnp.concatenate([k0_ref[0, :, sl], k1_ref[0, :, sl], k2_ref[0, :, sl]], axis=0)
        vwin = jnp.concatenate([v0_ref[0, :, sl], v1_ref[0, :, sl], v2_ref[0, :, sl]], axis=0)
        kc = kc_ref[0, :, sl]
        vwin = jnp.concatenate([vwin, jnp.ones_like(vwin)], axis=1)
        vc = jnp.concatenate([vc_ref[0, :, sl], jnp.ones_like(kc)], axis=1)
        halves = []
        for a in range(2):
            in_head = (lane >= a * NA_HEAD_DIM) & (lane < (a + 1) * NA_HEAD_DIM)
            qm = jnp.where(in_head, q2, jnp.zeros_like(q2))
            s_win = lax.dot_general(qm, kwin, nt, preferred_element_type=F32) + bias_ref[0, 2 * p + a]
            s_ctx = lax.dot_general(qm, kc, nt, preferred_element_type=F32)
            m = jnp.maximum(jnp.max(s_win, axis=1, keepdims=True), jnp.max(s_ctx, axis=1, keepdims=True))
            p_win = jnp.exp((s_win - m).astype(BF16))
            p_ctx = jnp.exp((s_ctx - m).astype(BF16))
            o = jnp.dot(p_win, vwin, preferred_element_type=F32) + jnp.dot(p_ctx, vc, preferred_element_type=F32)
            halves.append(o[:, :LANES] / o[:, LANES:])
        o_ref[0, :, sl] = jnp.where(lane < NA_HEAD_DIM, halves[0], halves[1]).astype(o_ref.dtype)


def _na_bias_tables(rpb, rows):
    R = NA_ROWS_PER_STEP
    nblk = rows // R
    kr = NA_WIN_ROWS
    cq = np.arange(GRID_W)
    cstart = np.clip(cq - NA_WIN_COLS // 2, 0, GRID_W - NA_WIN_COLS)
    ck = np.arange(GRID_W)
    col_ok = (ck[None, :] >= cstart[:, None]) & (ck[None, :] < cstart[:, None] + NA_WIN_COLS)
    col_off = np.where(col_ok, ck[None, :] - cq[:, None] + NA_WIN_COLS - 1, 0)
    row_ok = np.zeros((3, R, 3 * R), bool)
    row_off = np.zeros((3, R, 3 * R), np.int64)
    for vi, j in enumerate((0, 1, nblk - 1)):
        for i in range(R):
            r = j * R + i
            r0 = min(max(r - kr // 2, 0), rows - kr)
            for t in range(3):
                jb = j - 1 + t
                if jb < 0 or jb >= nblk:
                    continue
                for rr in range(R):
                    krow = jb * R + rr
                    if r0 <= krow < r0 + kr:
                        row_ok[vi, i, t * R + rr] = True
                        row_off[vi, i, t * R + rr] = krow - r + NA_WIN_ROWS - 1
    col_sel = (np.arange(rpb.shape[2])[None, None, :] == col_off[:, :, None]) & col_ok[:, :, None]
    row_sel = (np.arange(rpb.shape[1])[None, None, None, :] == row_off[..., None]) & row_ok[..., None]
    by_col = jnp.einsum('hrc,qkc->hrqk', rpb, jnp.asarray(col_sel, F32), precision=HIGHEST)
    vals = jnp.einsum('vixr,hrqk->vhiqxk', jnp.asarray(row_sel, F32), by_col, precision=HIGHEST)
    ok = row_ok[:, None, :, None, :, None] & col_ok[None, None, None, :, None, :]
    vals = jnp.where(jnp.asarray(ok), vals, NEG)
    return vals.reshape(3, NA_HEADS, R * GRID_W, 3 * R * GRID_W)


def _na(zna, zcna, bias, rows):
    b, n, _ = zna.shape
    ctx = zcna.shape[1]
    R = NA_ROWS_PER_STEP
    nq = R * GRID_W
    nblk = rows // R
    kb = lambda col, off: pl.BlockSpec(
        (1, nq, NA_WIDTH), lambda i, j: (i, jnp.clip(j + off, 0, nblk - 1), col))
    variant = lambda i, j: (jnp.where(j == 0, 0, jnp.where(j == nblk - 1, 2, 1)), 0, 0, 0)
    return pl.pallas_call(
        _na_kernel,
        grid=(b, nblk),
        in_specs=[pl.BlockSpec((1, nq, NA_WIDTH), lambda i, j: (i, j, 0)),
                  kb(1, -1), kb(1, 0), kb(1, 1), kb(2, -1), kb(2, 0), kb(2, 1),
                  pl.BlockSpec((1, ctx, NA_WIDTH), lambda i, j: (i, 0, 1)),
                  pl.BlockSpec((1, ctx, NA_WIDTH), lambda i, j: (i, 0, 2)),
                  pl.BlockSpec((1, NA_HEADS, nq, 3 * nq), variant)],
        out_specs=pl.BlockSpec((1, nq, NA_WIDTH), lambda i, j: (i, j, 0)),
        out_shape=jax.ShapeDtypeStruct((b, n, NA_WIDTH), BF16),
        compiler_params=_cparams("parallel", "parallel"),
        name="na",
    )(zna, zna, zna, zna, zna, zna, zna, zcna, zcna, bias)


def _mix_kernel(na_ref, hf_ref, hb_ref, o_ref, x_ref, g1_ref, sh2_ref, sc2_ref, wo_ref, mg_ref, l1g_ref, l1b_ref,
                wr_ref, br_ref, xmid_ref, rt_ref, rw_ref, cnt_ref, carry_s, *, alpha):
    first = (pl.program_id(0) == 0) & (pl.program_id(1) == 0)

    @pl.when(first)
    def _():
        carry_s[...] = jnp.zeros_like(carry_s)

    tm = x_ref.shape[1]
    h = hf_ref[0].astype(F32) + hb_ref[0].astype(F32)
    parts = []
    for hd in range(ML_HEADS):
        hh = h[:, hd * LANES:(hd + 1) * LANES]
        parts.append(hh * lax.rsqrt(jnp.mean(hh * hh, axis=-1, keepdims=True) + LN_EPS))
    hn = jnp.concatenate(parts, axis=1)
    ml = (hn * mg_ref[...] * jax.nn.sigmoid(o_ref[0].astype(F32))).astype(BF16)
    mix = (jnp.dot(na_ref[0], wo_ref[0:NA_WIDTH, :], preferred_element_type=F32)
           + jnp.dot(ml, wo_ref[NA_WIDTH:, :], preferred_element_type=F32))
    xmid = _ln_rows(alpha * x_ref[0] + g1_ref[0] * mix) * l1g_ref[...] + l1b_ref[...]
    xmid_ref[0] = xmid

    xt = (_ln_rows(xmid) * (1.0 + sc2_ref[0]) + sh2_ref[0]).astype(BF16)
    logits = jnp.dot(xt, wr_ref[...], preferred_element_type=F32) + br_ref[...]
    lane = lax.broadcasted_iota(jnp.int32, (tm, LANES), 1)
    is_g = lane < N_GROUPS
    gl = jnp.where(is_g, logits, NEG)
    gmax = jnp.max(gl, axis=1, keepdims=True)
    grp = jnp.min(jnp.where(gl == gmax, lane, LANES), axis=1, keepdims=True)
    gsum = jnp.sum(jnp.where(is_g, jnp.exp(gl - gmax), 0.0), axis=1, keepdims=True)
    grp_w = 1.0 / gsum
    lo = N_GROUPS + EXPERTS_PER_GROUP * grp
    el = jnp.where((lane >= lo) & (lane < lo + EXPERTS_PER_GROUP), logits, NEG)
    t1 = jnp.max(el, axis=1, keepdims=True)
    i1 = jnp.min(jnp.where(el == t1, lane, LANES), axis=1, keepdims=True)
    el2 = jnp.where(lane == i1, NEG, el)
    t2 = jnp.max(el2, axis=1, keepdims=True)
    i2 = jnp.min(jnp.where(el2 == t2, lane, LANES), axis=1, keepdims=True)
    e21 = jnp.exp(t2 - t1)
    w0 = grp_w / (1.0 + e21)
    w1 = grp_w * e21 / (1.0 + e21)

    hit1 = lane == i1
    hit2 = lane == i2
    onehot = (hit1 | hit2).astype(BF16)
    r_i = lax.broadcasted_iota(jnp.int32, (tm, tm), 0)
    c_i = lax.broadcasted_iota(jnp.int32, (tm, tm), 1)
    before = (r_i > c_i).astype(BF16)
    prefix = jnp.dot(before, onehot, preferred_element_type=F32) + carry_s[0:1, :]
    rank0 = jnp.sum(jnp.where(hit1, prefix, 0.0), axis=1, keepdims=True)
    rank1 = jnp.sum(jnp.where(hit2, prefix, 0.0), axis=1, keepdims=True)
    total = carry_s[0:1, :] + jnp.sum(onehot.astype(F32), axis=0, keepdims=True)
    carry_s[...] = jnp.broadcast_to(total, carry_s.shape)
    cnt_ref[...] = jnp.broadcast_to(total, cnt_ref.shape)

    rf = jnp.where(lane == 0, (i1 - N_GROUPS).astype(F32),
                   jnp.where(lane == 1, (i2 - N_GROUPS).astype(F32),
                             jnp.where(lane == 2, rank0, jnp.where(lane == 3, rank1, 0.0))))
    rt_ref[0] = rf.T[0:SUBLANES, :].astype(jnp.int32)
    rw_ref[0] = jnp.where(lane == 0, w0, jnp.where(lane == 1, w1, 0.0))


def _mix(na, hf, hb, zvo, x, g1, sh2, sc2, wo, mg, l1g, l1b, wr, br, alpha, tm):
    b, n, d = x.shape
    row = lambda w: pl.BlockSpec((1, tm, w), lambda i, j: (i, j, 0))
    vec = pl.BlockSpec((1, 1, d), lambda i, j: (i, 0, 0))
    full = lambda a: pl.BlockSpec(a.shape, lambda i, j: (0,) * a.ndim)
    ocol = 1
    return pl.pallas_call(
        functools.partial(_mix_kernel, alpha=alpha),
        grid=(b, n // tm),
        in_specs=[row(NA_WIDTH), row(ML_WIDTH), row(ML_WIDTH),
                  pl.BlockSpec((1, tm, ML_WIDTH), lambda i, j: (i, j, ocol)),
                  row(d), vec, vec, vec, full(wo), full(mg), full(l1g), full(l1b), full(wr), full(br)],
        out_specs=[row(d), pl.BlockSpec((1, SUBLANES, tm), lambda i, j: (i * (n // tm) + j, 0, 0)), row(LANES),
                   pl.BlockSpec((SUBLANES, LANES), lambda i, j: (0, 0))],
        out_shape=[jax.ShapeDtypeStruct((b, n, d), F32),
                   jax.ShapeDtypeStruct((b * (n // tm), SUBLANES, tm), jnp.int32),
                   jax.ShapeDtypeStruct((b, n, LANES), F32),
                   jax.ShapeDtypeStruct((SUBLANES, LANES), F32)],
        scratch_shapes=[pltpu.VMEM((8, LANES), F32)],
        compiler_params=_cparams("arbitrary", "arbitrary"),
        name="mix",
    )(na, hf, hb, zvo, x, g1, sh2, sc2, wo, mg, l1g, l1b, wr, br)


def _zero_fill_padding(pad_base_ref, pad_len_ref, nused_ref, xs_ref, zero_s, sem, wait):
    zero_t, xs_t = _token_view(zero_s), _token_view(xs_ref)
    tb = zero_t.shape[0]

    def run(copy):
        copy.wait() if wait else copy.start()

    def fill(off, nrows):
        run(pltpu.make_async_copy(zero_t.at[pl.ds(0, nrows)], xs_t.at[pl.ds(off, nrows)], sem))

    def per_expert(e, _):
        plen = pad_len_ref[e]
        base = pad_base_ref[e]
        bit = tb // 2
        while bit >= 1:
            off = base + (plen & ~(2 * bit - 1))

            @pl.when((plen & bit) != 0)
            def _(bit=bit, off=off):
                fill(off, bit)

            bit //= 2
        return 0

    lax.fori_loop(0, N_EXPERTS, per_expert, 0)

    def per_block(i, _):
        fill(i * tb, tb)
        return 0

    lax.fori_loop(nused_ref[0], xs_t.shape[0] // tb, per_block, 0)


ROW_WORDS = 4


def _token_view(ref):
    return ref.reshape(ref.shape[0] // ROW_WORDS, ROW_WORDS, LANES)


def _bf16_bits(v):
    return lax.bitcast_convert_type(v.astype(BF16).astype(F32), jnp.int32)


def _store_token_rows(dst_ref, base, val):
    tm, d = val.shape
    assert d == 2 * ROW_WORDS * LANES
    for s in range(ROW_WORDS):
        lo = _bf16_bits(val[:, s * LANES:(s + 1) * LANES])
        hi = _bf16_bits(val[:, (s + ROW_WORDS) * LANES:(s + ROW_WORDS + 1) * LANES])
        dst_ref[pl.ds(base * ROW_WORDS + s, tm, stride=ROW_WORDS), :] = hi | lax.shift_right_logical(lo, 16)


def _load_token_rows(src_ref, base, tm, dtype):
    words = [src_ref[pl.ds(base * ROW_WORDS + s, tm, stride=ROW_WORDS), :] for s in range(ROW_WORDS)]
    lo = [lax.bitcast_convert_type(w << 16, F32).astype(dtype) for w in words]
    hi = [lax.bitcast_convert_type(w & -65536, F32).astype(dtype) for w in words]
    return jnp.concatenate(lo + hi, axis=1)


def _dispatch_kernel(pad_base_ref, pad_len_ref, nused_ref, dest_ref, xmid_ref, sh2_ref, sc2_ref, xs_ref,
                     xt_s, zero_s, sem, zsem):
    tm = xmid_ref.shape[1]
    step = pl.program_id(0) * pl.num_programs(1) + pl.program_id(1)
    nsteps = pl.num_programs(0) * pl.num_programs(1)

    slot = step % 2

    def wait_slot_copies(s):
        for _ in range(2):
            pltpu.make_async_copy(_token_view(xt_s).at[pl.ds(s * tm, tm)], _token_view(xs_ref).at[pl.ds(0, tm)],
                                  sem.at[s]).wait()

    @pl.when(step == 0)
    def _():
        zero_s[...] = jnp.zeros_like(zero_s)
        _zero_fill_padding(pad_base_ref, pad_len_ref, nused_ref, xs_ref, zero_s, zsem, False)

    for c in range(tm // COMBINE_CHUNK):
        rows = slice(c * COMBINE_CHUNK, (c + 1) * COMBINE_CHUNK)
        xt = _ln_rows(xmid_ref[0, rows, :]) * (1.0 + sc2_ref[0]) + sh2_ref[0]
        _store_token_rows(xt_s, slot * tm + c * COMBINE_CHUNK, xt)
        for r in range(c * COMBINE_CHUNK, (c + 1) * COMBINE_CHUNK):
            for k in range(2):
                pltpu.make_async_copy(_token_view(xt_s).at[slot * tm + r], _token_view(xs_ref).at[dest_ref[0, k, r]],
                                      sem.at[slot]).start(priority=k)

    @pl.when(step == 0)
    def _():
        _zero_fill_padding(pad_base_ref, pad_len_ref, nused_ref, xs_ref, zero_s, zsem, True)

    @pl.when(step > 0)
    def _():
        wait_slot_copies(1 - slot)

    @pl.when(step == nsteps - 1)
    def _():
        wait_slot_copies(slot)


def _dispatch(xmid, sh2, sc2, dest, pad_base, pad_len, nused, cap, tm):
    b, n, d = xmid.shape
    assert d == 2 * ROW_WORDS * LANES
    nt = n // tm
    vec = pl.BlockSpec((1, 1, d), lambda i, j, *_: (i, 0, 0))
    return pl.pallas_call(
        _dispatch_kernel,
        grid_spec=pltpu.PrefetchScalarGridSpec(
            num_scalar_prefetch=3,
            grid=(b, nt),
            in_specs=[pl.BlockSpec((1, 2, tm), lambda i, j, *_: (i * nt + j, 0, 0), memory_space=pltpu.SMEM),
                      pl.BlockSpec((1, tm, d), lambda i, j, *_: (i, j, 0)), vec, vec],
            out_specs=pl.BlockSpec(memory_space=pl.ANY),
            scratch_shapes=[pltpu.VMEM((2 * tm * ROW_WORDS, LANES), ROW_DTYPE),
                            pltpu.VMEM((EXPERT_ROWS * ROW_WORDS, LANES), ROW_DTYPE),
                            pltpu.SemaphoreType.DMA((2,)), pltpu.SemaphoreType.DMA]),
        out_shape=jax.ShapeDtypeStruct((cap * ROW_WORDS, LANES), ROW_DTYPE),
        compiler_params=_cparams("arbitrary", "arbitrary"),
        name="dispatch",
    )(pad_base, pad_len, nused, dest, xmid, sh2, sc2)


def _expert_kernel(be_ref, nv_ref, xs_ref, w1_ref, w3_ref, w2_ref, ys_ref, w1b, w3b, w2b):
    tb = xs_ref.shape[0] // ROW_WORDS
    i = pl.program_id(0)
    e = be_ref[i]
    changed = (i == 0) | (be_ref[jnp.maximum(i - 1, 0)] != e)

    @pl.when(changed)
    def _():
        w1b[...] = w1_ref[0].astype(BF16)
        w3b[...] = w3_ref[0].astype(BF16)
        w2b[...] = w2_ref[0].astype(BF16)

    nv = nv_ref[i]

    @pl.when(nv > 0)
    def _():
        xb = _load_token_rows(xs_ref, 0, tb, BF16)
        h1 = jnp.dot(xb, w1b[...], preferred_element_type=F32)
        h3 = jnp.dot(xb, w3b[...], preferred_element_type=F32)
        a = (_silu(h1) * h3).astype(BF16)
        _store_token_rows(ys_ref, 0, jnp.dot(a, w2b[...], preferred_element_type=F32))

    @pl.when(nv == 0)
    def _():
        ys_ref[...] = jnp.zeros_like(ys_ref)


def _experts(xs, block_e, block_nv, w1, w3, w2):
    d, hid = w1.shape[1], w1.shape[2]
    tb = EXPERT_ROWS
    rows = pl.BlockSpec((tb * ROW_WORDS, LANES), lambda i, be, nv: (i, 0))
    return pl.pallas_call(
        _expert_kernel,
        grid_spec=pltpu.PrefetchScalarGridSpec(
            num_scalar_prefetch=2,
            grid=(xs.shape[0] // (tb * ROW_WORDS),),
            in_specs=[rows,
                      pl.BlockSpec((1, d, hid), lambda i, be, nv: (be[i], 0, 0)),
                      pl.BlockSpec((1, d, hid), lambda i, be, nv: (be[i], 0, 0)),
                      pl.BlockSpec((1, hid, d), lambda i, be, nv: (be[i], 0, 0))],
            out_specs=rows,
            scratch_shapes=[pltpu.VMEM((d, hid), BF16), pltpu.VMEM((d, hid), BF16), pltpu.VMEM((hid, d), BF16)]),
        out_shape=jax.ShapeDtypeStruct(xs.shape, xs.dtype),
        compiler_params=_cparams("arbitrary"),
        name="experts",
    )(block_e, block_nv, xs, w1, w3, w2)


def _combine_kernel(dcur_ref, dnext_ref, xmid_ref, rw_ref, g2_ref, l2g_ref, l2b_ref, ys_ref, o_ref,
                    y0_s, y1_s, sem, *, alpha):
    tm = xmid_ref.shape[1]
    step = pl.program_id(0) * pl.num_programs(1) + pl.program_id(1)
    nsteps = pl.num_programs(0) * pl.num_programs(1)
    slot = step % 2
    other = 1 - slot

    def start_row(dest_ref, into, r):
        for k, buf in ((0, y0_s), (1, y1_s)):
            pltpu.make_async_copy(_token_view(ys_ref).at[dest_ref[0, k, r]],
                                  _token_view(buf).at[into * tm + r], sem.at[into]).start(priority=k)

    def wait_slot(into):
        for buf in (y0_s, y1_s):
            pltpu.make_async_copy(_token_view(ys_ref).at[pl.ds(0, tm)], _token_view(buf).at[pl.ds(into * tm, tm)],
                                  sem.at[into]).wait()

    @pl.when(step == 0)
    def _():
        lax.fori_loop(0, tm, lambda r, c: (start_row(dcur_ref, 0, r), c)[1], 0, unroll=8)

    wait_slot(slot)
    for c in range(tm // COMBINE_CHUNK):
        for r in range(c * COMBINE_CHUNK, (c + 1) * COMBINE_CHUNK):
            start_row(dnext_ref, other, r)
        rows = slice(c * COMBINE_CHUNK, (c + 1) * COMBINE_CHUNK)
        base = slot * tm + c * COMBINE_CHUNK
        rw = rw_ref[0, rows, :]
        moe = (rw[:, 0:1] * _load_token_rows(y0_s, base, COMBINE_CHUNK, F32)
               + rw[:, 1:2] * _load_token_rows(y1_s, base, COMBINE_CHUNK, F32))
        o_ref[0, rows, :] = (_ln_rows(alpha * xmid_ref[0, rows, :] + g2_ref[0] * moe) * l2g_ref[...] + l2b_ref[...])

    @pl.when(step == nsteps - 1)
    def _():
        wait_slot(other)


def _combine(xmid, rw, g2, l2g, l2b, ys, dest, alpha, tm):
    b, n, d = xmid.shape
    nt = n // tm
    full = lambda a: pl.BlockSpec(a.shape, lambda i, j: (0,) * a.ndim)
    return pl.pallas_call(
        functools.partial(_combine_kernel, alpha=alpha),
        grid=(b, nt),
        in_specs=[pl.BlockSpec((1, 2, tm), lambda i, j: (i * nt + j, 0, 0), memory_space=pltpu.SMEM),
                  pl.BlockSpec((1, 2, tm), lambda i, j: (jnp.minimum(i * nt + j + 1, b * nt - 1), 0, 0),
                               memory_space=pltpu.SMEM),
                  pl.BlockSpec((1, tm, d), lambda i, j: (i, j, 0)),
                  pl.BlockSpec((1, tm, LANES), lambda i, j: (i, j, 0)),
                  pl.BlockSpec((1, 1, d), lambda i, j: (i, 0, 0)),
                  full(l2g), full(l2b),
                  pl.BlockSpec(memory_space=pl.ANY)],
        out_specs=pl.BlockSpec((1, tm, d), lambda i, j: (i, j, 0)),
        out_shape=jax.ShapeDtypeStruct((b, n, d), F32),
        scratch_shapes=[pltpu.VMEM((2 * tm * ROW_WORDS, LANES), ys.dtype),
                        pltpu.VMEM((2 * tm * ROW_WORDS, LANES), ys.dtype), pltpu.SemaphoreType.DMA((2,))],
        compiler_params=_cparams("arbitrary", "arbitrary"),
        name="combine",
    )(dest, dest, xmid, rw, g2, l2g, l2b, ys)


def _tile(n, want):
    t = min(n, want)
    assert n % t == 0, (n, t)
    return t


def kernel(x, c, ctx, c_ctx, w_ada, b_ada, w_in, conv_w, conv_b, gate_b, rpb, ml_norm_g, w_out, ln1_g, ln1_b,
           w_router_g, b_router_g, w_router_e, b_router_e, w1, w3, w2, ln2_g, ln2_b):
    B, N, D = x.shape
    T_CTX = ctx.shape[1]
    depth = w_ada.shape[0]
    rows = N // GRID_W
    assert depth == 1 and N % GRID_W == 0 and rows % NA_ROWS_PER_STEP == 0 and rows >= 3 * NA_ROWS_PER_STEP
    assert N % ML_CHUNK == 0 and T_CTX % ML_CHUNK == 0
    alpha = (2.0 * depth) ** 0.25
    l = 0

    pad_rows = -(B + 1) % 8
    cvec = jnp.concatenate([c, c_ctx[None], jnp.zeros((pad_rows, D), F32)], axis=0)
    ada = _ada(cvec, w_ada[l], b_ada[l])
    sh1, sc1, g1, sh2, sc2, g2 = [a[:, None, :] for a in jnp.split(ada[:B], 6, axis=-1)]
    csh1, csc1 = [jnp.broadcast_to(a[None], (B, 1, D)) for a in jnp.split(ada[B:B + 1], 6, axis=-1)[:2]]

    col_ml = 3 * NA_WIDTH
    col_g = col_ml + 4 * ML_WIDTH
    col_v = col_ml + 2 * ML_WIDTH
    wb = w_in[l].astype(BF16)
    wna, wqk, wvo = wb[:, :col_ml], wb[:, col_ml:col_v], wb[:, col_v:col_g]
    n_gate = 4 * ML_HEADS
    wg = jnp.pad(wb[:, col_g:], ((0, 0), (0, LANES - n_gate)))
    gb = jnp.pad(gate_b[l], (0, LANES - n_gate)).reshape(1, LANES)
    cos_t, sin_t = _rope_tables(N)
    zna, zvo, qk_l, gcol_l, grow_l = _inproj(x, sh1, sc1, wna, wqk, wvo, wg, gb, conv_w[l], conv_b[l], cos_t, sin_t,
                                             True, _tile(N, 512))
    zcna, zcvo, qk_c, gcol_c, grow_c = _inproj(ctx, csh1, csc1, wna, wqk, wvo, wg, gb, conv_w[l], conv_b[l],
                                               cos_t[:T_CTX], sin_t[:T_CTX], False, _tile(T_CTX, 256))

    c0 = jnp.zeros((B, 2, ML_HEADS, ML_HEAD_DIM, 2 * ML_HEAD_DIM), F32)
    m0 = jnp.zeros((B, 2, ML_HEADS, 8, LANES), F32)
    c_ctx_end, m_ctx_end = _mlstm(qk_c, zcvo, gcol_c, grow_c, c0, m0, False)
    hf, hb, _, _ = _mlstm(qk_l, zvo, gcol_l, grow_l, c_ctx_end, m_ctx_end, True)

    na = _na(zna, zcna, _na_bias_tables(rpb[l], rows), rows)

    wr = jnp.pad(jnp.concatenate([w_router_g[l], w_router_e[l]], axis=1),
                 ((0, 0), (0, LANES - N_GROUPS - N_EXPERTS))).astype(BF16)
    br = jnp.pad(jnp.concatenate([b_router_g[l], b_router_e[l]]), (0, LANES - N_GROUPS - N_EXPERTS)).reshape(1, LANES)
    tm = _tile(N, 512)
    xmid, rt, rw, counts = _mix(na, hf, hb, zvo, x, g1, sh2, sc2, w_out[l].astype(BF16),
                                ml_norm_g[l].reshape(1, ML_WIDTH), ln1_g[l].reshape(1, D), ln1_b[l].reshape(1, D),
                                wr, br, alpha, tm)

    tb = EXPERT_ROWS
    n_assign = 2 * B * N
    cap = -(-n_assign // tb) * tb + N_EXPERTS * tb
    sizes = counts[0, N_GROUPS:N_GROUPS + N_EXPERTS].astype(jnp.int32)
    padded = (sizes + tb - 1) // tb * tb
    pend = jnp.cumsum(padded)
    pstart = pend - padded
    experts = jnp.arange(N_EXPERTS, dtype=jnp.int32)
    first_row = jnp.sum(jnp.where(rt[:, 0:2, :, None] == experts, pstart, 0), axis=-1)
    dest = first_row + rt[:, 2:4, :]
    blk0 = jnp.arange(cap // tb, dtype=jnp.int32) * tb
    block_e = jnp.minimum(jnp.sum(pend[None, :] <= blk0[:, None], axis=1), N_EXPERTS - 1).astype(jnp.int32)
    is_e = block_e[:, None] == experts
    block_nv = jnp.clip(jnp.sum(jnp.where(is_e, pstart + sizes, 0), axis=1) - blk0, 0, tb).astype(jnp.int32)

    nused = (pend[-1:] // tb).astype(jnp.int32)
    xs = _dispatch(xmid, sh2, sc2, dest, pstart + sizes, padded - sizes, nused, cap, tm)
    ys = _experts(xs, block_e, block_nv, w1[l], w3[l], w2[l])
    return _combine(xmid, rw, g2, ln2_g[l].reshape(1, D), ln2_b[l].reshape(1, D), ys, dest, alpha, tm)
```

```python
import functools

import numpy as np
import jax
import jax.numpy as jnp
from jax import lax
from jax.experimental import pallas as pl
from jax.experimental.pallas import tpu as pltpu

F32 = jnp.float32
BF16 = jnp.bfloat16
ROW_DTYPE = jnp.int32
HIGHEST = lax.Precision.HIGHEST

GRID_W = 64
NA_HEADS = 8
NA_HEAD_DIM = 64
NA_WIDTH = NA_HEADS * NA_HEAD_DIM
NA_WIN_ROWS = 8
NA_WIN_COLS = 16
ML_HEADS = 4
ML_HEAD_DIM = 128
ML_WIDTH = ML_HEADS * ML_HEAD_DIM
ML_CHUNK = 128
CONV_K = 5
N_GROUPS = 8
EXPERTS_PER_GROUP = 8
N_EXPERTS = N_GROUPS * EXPERTS_PER_GROUP
ROPE_BASE = 10000.0
LN_EPS = 1e-5

LANES = 128
SUBLANES = 8
VMEM_LIMIT = 56 * 1024 * 1024

NA_ROWS_PER_STEP = 4
EXPERT_ROWS = 512
COMBINE_CHUNK = 64
MLSTM_CHUNKS_PER_STEP = 4
NEG = -1e30


def _cparams(*sem):
    return pltpu.CompilerParams(dimension_semantics=sem, vmem_limit_bytes=VMEM_LIMIT)


def _silu(v):
    return v * jax.nn.sigmoid(v)


def _ln_rows(v):
    mu = jnp.mean(v, axis=-1, keepdims=True)
    vc = v - mu
    var = jnp.mean(vc * vc, axis=-1, keepdims=True)
    return vc * lax.rsqrt(var + LN_EPS)


def _ada_kernel(c_ref, w_ref, b_ref, o_ref):
    o_ref[...] = jnp.dot(_silu(c_ref[...]), w_ref[...], preferred_element_type=F32,
                         precision=HIGHEST) + b_ref[...]


def _ada(cvec, w, b):
    rows, d = cvec.shape
    cols = w.shape[1]
    tn = 1024
    return pl.pallas_call(
        _ada_kernel,
        grid=(cols // tn,),
        in_specs=[pl.BlockSpec((rows, d), lambda j: (0, 0)),
                  pl.BlockSpec((d, tn), lambda j: (0, j)),
                  pl.BlockSpec((1, tn), lambda j: (0, j))],
        out_specs=pl.BlockSpec((rows, tn), lambda j: (0, j)),
        out_shape=jax.ShapeDtypeStruct((rows, cols), F32),
        compiler_params=_cparams("arbitrary"),
        name="ada",
    )(cvec, w, b.reshape(1, cols))


HALO = 16


def _log_sigmoid(v):
    return jnp.minimum(v, 0.0) - jnp.log1p(jnp.exp(-jnp.abs(v)))


GATE_RAW, GATE_PREFIX, GATE_SUFFIX = 0, 16, 32


def _pack_gates(g):
    L = g.shape[0]
    row = lax.broadcasted_iota(jnp.int32, (L, L), 0)
    col = lax.broadcasted_iota(jnp.int32, (L, L), 1)
    lane = lax.broadcasted_iota(jnp.int32, g.shape, 1)
    lf = _log_sigmoid(g)
    prefix = jnp.dot((row >= col).astype(F32), lf, preferred_element_type=F32, precision=HIGHEST)
    suffix = jnp.dot((row <= col).astype(F32), lf, preferred_element_type=F32, precision=HIGHEST)
    return jnp.where(lane < GATE_PREFIX, g,
                     jnp.where(lane < GATE_SUFFIX, pltpu.roll(prefix, GATE_PREFIX, 1),
                               jnp.where(lane < GATE_SUFFIX + GATE_PREFIX, pltpu.roll(suffix, GATE_SUFFIX, 1), 0.0)))


def _inproj_kernel(x_ref, xp_ref, xn_ref, sh_ref, sc_ref, wna_ref, wqk_ref, wvo_ref, wg_ref, gb_ref, cw_ref, cb_ref,
                   cos_ref, sin_ref, zna_ref, zvo_ref, o_ref, gcol_ref, grow_ref, *pad_refs, rope, tr):
    i = pl.program_id(1)
    last = pl.num_programs(1) - 1
    xa = jnp.concatenate([xp_ref[0], x_ref[0], xn_ref[0]], axis=0)
    ya = (_ln_rows(xa) * (1.0 + sc_ref[0]) + sh_ref[0]).astype(BF16)
    yb = ya[HALO:HALO + tr]
    pad = CONV_K // 2
    width = wqk_ref.shape[1]
    lane = lax.broadcasted_iota(jnp.int32, (tr, LANES), 1)
    first_half = (lane % (ML_HEAD_DIM // 2)) < (ML_HEAD_DIM // 4)
    kscale = ML_HEAD_DIM ** -0.5
    cb = 2 * LANES

    def project_qk(c):
        z = jnp.dot(ya, wqk_ref[:, c * cb:(c + 1) * cb], preferred_element_type=F32)
        pad_refs[c][HALO:HALO + tr, :] = z[HALO:HALO + tr]
        pad_refs[c][0:HALO, :] = jnp.where(i > 0, z[0:HALO], 0.0)
        pad_refs[c][HALO + tr:2 * HALO + tr, :] = jnp.where(i < last, z[HALO + tr:], 0.0)

    def conv_group(g):
        cols = slice(g * LANES, (g + 1) * LANES)
        pad_ref = pad_refs[g * LANES // cb]
        pc = slice(g * LANES % cb, g * LANES % cb + LANES)
        acc = cw_ref[0:1, cols] * pad_ref[HALO - pad:HALO - pad + tr, pc] + cb_ref[:, cols]
        for j in range(1, CONV_K):
            acc = acc + cw_ref[j:j + 1, cols] * pad_ref[HALO - pad + j:HALO - pad + j + tr, pc]
        ug = _silu(acc)
        if rope:
            partner = jnp.where(first_half, pltpu.roll(ug, LANES - ML_HEAD_DIM // 4, 1),
                                pltpu.roll(ug, ML_HEAD_DIM // 4, 1))
            ug = ug * cos_ref[...] + partner * sin_ref[...]
        if g >= ML_HEADS:
            ug = ug * kscale
        o_ref[0, :, cols] = ug.astype(BF16)

    other = ([(zna_ref, wna_ref, c) for c in range(wna_ref.shape[1] // cb)]
             + [(zvo_ref, wvo_ref, c) for c in range(wvo_ref.shape[1] // cb)])

    def project_other(n):
        for _ in range(n):
            if other:
                dst, w, c = other.pop(0)
                cols = slice(c * cb, (c + 1) * cb)
                dst[0, :, cols] = jnp.dot(yb, w[:, cols], preferred_element_type=F32).astype(BF16)

    n_qk = width // cb
    per_round = -(-len(other) // n_qk)
    project_qk(0)
    for c in range(n_qk):
        if c + 1 < n_qk:
            project_qk(c + 1)
        project_other(per_round)
        for g in range(c * cb // LANES, (c + 1) * cb // LANES):
            conv_group(g)
    project_other(len(other))
    gates = jnp.dot(yb, wg_ref[...], preferred_element_type=F32) + gb_ref[...]
    for c in range(tr // ML_CHUNK):
        packed = _pack_gates(gates[c * ML_CHUNK:(c + 1) * ML_CHUNK, :])
        gcol_ref[0, c * ML_CHUNK:(c + 1) * ML_CHUNK, :] = packed
        grow_ref[0, c] = packed.T


def _inproj(x, shift, scale, wna, wqk, wvo, wg, gb, conv_w, conv_b, cos_t, sin_t, rope, tr):
    b, t, d = x.shape
    width = 2 * ML_WIDTH
    nh = tr // HALO
    nblk = t // HALO
    L = ML_CHUNK
    full = lambda a: pl.BlockSpec(a.shape, lambda i, j: (0,) * a.ndim)
    row = lambda w: pl.BlockSpec((1, tr, w), lambda i, j: (i, j, 0))
    vec = pl.BlockSpec((1, 1, d), lambda i, j: (i, 0, 0))
    return pl.pallas_call(
        functools.partial(_inproj_kernel, rope=rope, tr=tr),
        grid=(b, t // tr),
        in_specs=[row(d),
                  pl.BlockSpec((1, HALO, d), lambda i, j: (i, jnp.maximum(j * nh - 1, 0), 0)),
                  pl.BlockSpec((1, HALO, d), lambda i, j: (i, jnp.minimum((j + 1) * nh, nblk - 1), 0)),
                  vec, vec, full(wna), full(wqk), full(wvo), full(wg), full(gb),
                  pl.BlockSpec((CONV_K, width), lambda i, j: (0, 0)),
                  pl.BlockSpec((1, width), lambda i, j: (0, 0)),
                  pl.BlockSpec((tr, LANES), lambda i, j: (j, 0)),
                  pl.BlockSpec((tr, LANES), lambda i, j: (j, 0))],
        out_specs=[row(wna.shape[1]), row(wvo.shape[1]), row(width), row(LANES),
                   pl.BlockSpec((1, tr // L, LANES, L), lambda i, j: (i, j, 0, 0))],
        out_shape=[jax.ShapeDtypeStruct((b, t, wna.shape[1]), BF16),
                   jax.ShapeDtypeStruct((b, t, wvo.shape[1]), BF16),
                   jax.ShapeDtypeStruct((b, t, width), BF16),
                   jax.ShapeDtypeStruct((b, t, LANES), F32),
                   jax.ShapeDtypeStruct((b, t // L, LANES, L), F32)],
        scratch_shapes=[pltpu.VMEM((tr + 2 * HALO, 2 * LANES), F32)] * (width // (2 * LANES)),
        compiler_params=_cparams("parallel", "parallel"),
        name="inproj" if rope else "inproj_ctx",
    )(x, x, x, shift, scale, wna, wqk, wvo, wg, gb, conv_w, conv_b.reshape(1, width), cos_t, sin_t)


def _rope_tables(n):
    pos = np.arange(n)
    half = ML_HEAD_DIM // 4
    inv = np.float32(ROPE_BASE) ** (-np.arange(half, dtype=np.float32) / np.float32(half))

    def axis_tables(p):
        ang = p.astype(np.float32)[:, None] * inv[None, :]
        c, s = np.cos(ang), np.sin(ang)
        return np.concatenate([c, c], axis=-1), np.concatenate([-s, s], axis=-1)

    cr, sr = axis_tables(pos // GRID_W)
    cc, sc = axis_tables(pos % GRID_W)
    return (jnp.asarray(np.concatenate([cr, cc], axis=-1), F32), jnp.asarray(np.concatenate([sr, sc], axis=-1), F32))


def _mlstm_direction(d, qk, v, gcol, grow, cn_s, m_s):
    L = ML_CHUNK
    row = lax.broadcasted_iota(jnp.int32, (L, L), 0)
    col = lax.broadcasted_iota(jnp.int32, (L, L), 1)
    fwd = d == 0
    keep = (row >= col) if fwd else (row <= col)
    ones = jnp.ones((L, ML_HEAD_DIM), BF16)
    end = L - 1 if fwd else 0
    outs = []
    for h in range(ML_HEADS):
        ci = GATE_RAW + (2 * d) * ML_HEADS + h
        cf = (GATE_PREFIX if fwd else GATE_SUFFIX) + (2 * d + 1) * ML_HEADS + h
        bc = jnp.broadcast_to(gcol[:, cf:cf + 1], (L, L))
        ic = jnp.broadcast_to(gcol[:, ci:ci + 1], (L, L))
        br = grow[cf:cf + 1, :]
        ir = grow[ci:ci + 1, :]
        m_prev = m_s[d, h][0:1, :]
        dlog = jnp.where(keep, bc - br + ir, NEG)
        m_t = jnp.maximum(bc + m_prev, jnp.max(dlog, axis=1, keepdims=True))
        dw = jnp.exp((dlog - m_t).astype(BF16))
        inter = jnp.exp(bc + m_prev - m_t)
        qh = qk[:, h * LANES:(h + 1) * LANES]
        kh = qk[:, ML_WIDTH + h * LANES:ML_WIDTH + (h + 1) * LANES]
        vp = jnp.concatenate([v[:, h * LANES:(h + 1) * LANES], ones], axis=1)
        s = lax.dot_general(qh, kh, (((1,), (1,)), ((), ())), preferred_element_type=F32).astype(BF16) * dw
        cn = cn_s[d, h]
        a1 = jnp.dot(s, vp, preferred_element_type=F32)
        a2 = jnp.dot(qh, cn.astype(BF16), preferred_element_type=F32)
        num = a1[:, :ML_HEAD_DIM] + inter * a2[:, :ML_HEAD_DIM]
        den = a1[:, ML_HEAD_DIM:] + inter * a2[:, ML_HEAD_DIM:]
        outs.append(num / jnp.maximum(jnp.abs(den), jnp.exp(-m_t)))
        b_end = jnp.broadcast_to(br[:, end:end + 1], (1, L))
        g_row = b_end - br + ir
        m_new = jnp.maximum(b_end + m_prev, jnp.max(g_row, axis=1, keepdims=True))
        decay = jnp.exp(b_end + m_prev - m_new)
        wgt = jnp.exp((b_end - bc + ic - m_new).astype(BF16))
        kw = kh * wgt
        upd = lax.dot_general(kw, vp, (((0,), (0,)), ((), ())), preferred_element_type=F32)
        cn_s[d, h] = jnp.concatenate([decay, decay], axis=1) * cn + upd
        m_s[d, h] = jnp.broadcast_to(m_new, m_s.shape[2:])
    return jnp.concatenate(outs, axis=1)


def _mlstm_kernel(qkf_ref, vf_ref, gcf_ref, grf_ref, qkb_ref, vb_ref, gcb_ref, grb_ref, c0_ref, m0_ref, *rest, emit_h):
    if emit_h:
        hf_ref, hb_ref, ct_ref, mt_ref, cn_s, m_s = rest
    else:
        ct_ref, mt_ref, cn_s, m_s = rest
        hf_ref = hb_ref = None
    c = pl.program_id(1)
    L = ML_CHUNK

    @pl.when(c == 0)
    def _():
        cn_s[...] = c0_ref[0]
        m_s[...] = m0_ref[0]

    for d, refs, h_ref in ((0, (qkf_ref, vf_ref, gcf_ref, grf_ref), hf_ref), (1, (qkb_ref, vb_ref, gcb_ref, grb_ref), hb_ref)):
        qk_ref, v_ref, gc_ref, gr_ref = refs
        n_sub = qk_ref.shape[1] // L
        for s in (range(n_sub) if d == 0 else reversed(range(n_sub))):
            rows = slice(s * L, (s + 1) * L)
            h = _mlstm_direction(d, qk_ref[0, rows, :], v_ref[0, rows, :], gc_ref[0, rows, :], gr_ref[0, s], cn_s, m_s)
            if h_ref is not None:
                h_ref[0, rows, :] = h.astype(h_ref.dtype)

    @pl.when(c == pl.num_programs(1) - 1)
    def _():
        ct_ref[0] = cn_s[...]
        mt_ref[0] = m_s[...]


def _mlstm(qk, zvo, gcol, grow, c0, m0, emit_h):
    b, t, _ = qk.shape
    n_sub = min(MLSTM_CHUNKS_PER_STEP, t // ML_CHUNK)
    R = n_sub * ML_CHUNK
    assert t % R == 0
    nc = t // R
    vcol = 0
    f_idx = lambda i, c: (i, c, 0)
    b_idx = lambda i, c: (i, nc - 1 - c, 0)
    st_c = pl.BlockSpec((1, 2, ML_HEADS, ML_HEAD_DIM, 2 * ML_HEAD_DIM), lambda i, c: (i, 0, 0, 0, 0))
    st_m = pl.BlockSpec((1, 2, ML_HEADS, 8, LANES), lambda i, c: (i, 0, 0, 0, 0))
    out_specs = [st_c, st_m]
    out_shape = [jax.ShapeDtypeStruct(c0.shape, F32), jax.ShapeDtypeStruct(m0.shape, F32)]
    if emit_h:
        out_specs = [pl.BlockSpec((1, R, ML_WIDTH), f_idx), pl.BlockSpec((1, R, ML_WIDTH), b_idx)] + out_specs
        out_shape = [jax.ShapeDtypeStruct((b, t, ML_WIDTH), BF16)] * 2 + out_shape
    return pl.pallas_call(
        functools.partial(_mlstm_kernel, emit_h=emit_h),
        grid=(b, nc),
        in_specs=[pl.BlockSpec((1, R, 2 * ML_WIDTH), f_idx),
                  pl.BlockSpec((1, R, ML_WIDTH), lambda i, c: (i, c, vcol)),
                  pl.BlockSpec((1, R, LANES), f_idx),
                  pl.BlockSpec((1, n_sub, LANES, ML_CHUNK), lambda i, c: (i, c, 0, 0)),
                  pl.BlockSpec((1, R, 2 * ML_WIDTH), b_idx),
                  pl.BlockSpec((1, R, ML_WIDTH), lambda i, c: (i, nc - 1 - c, vcol)),
                  pl.BlockSpec((1, R, LANES), b_idx),
                  pl.BlockSpec((1, n_sub, LANES, ML_CHUNK), lambda i, c: (i, nc - 1 - c, 0, 0)),
                  st_c, st_m],
        out_specs=out_specs,
        out_shape=out_shape,
        scratch_shapes=[pltpu.VMEM((2, ML_HEADS, ML_HEAD_DIM, 2 * ML_HEAD_DIM), F32),
                        pltpu.VMEM((2, ML_HEADS, 8, LANES), F32)],
        compiler_params=_cparams("parallel", "arbitrary"),
        name="mlstm" if emit_h else "mlstm_ctx",
    )(qk, zvo, gcol, grow, qk, zvo, gcol, grow, c0, m0)


def _lane_in(shape, start, width):
    lane = lax.broadcasted_iota(jnp.int32, shape, 1)
    return (lane >= start) & (lane < start + width)


def _na_kernel(q_ref, k0_ref, k1_ref, k2_ref, v0_ref, v1_ref, v2_ref, kc_ref, vc_ref, bias_ref, o_ref):
    nq = q_ref.shape[1]
    lane = lax.broadcasted_iota(jnp.int32, (nq, LANES), 1)
    nt = (((1,), (1,)), ((), ()))
    for p in range(NA_HEADS // 2):
        sl = slice(p * LANES, (p + 1) * LANES)
        q2 = q_ref[0, :, sl] * (NA_HEAD_DIM ** -0.5)
        kwin = jnp.concatenate([k0_ref[0, :, sl], k1_ref[0, :, sl], k2_ref[0, :, sl]], axis=0)
        vwin = jnp.concatenate([v0_ref[0, :, sl], v1_ref[0, :, sl], v2_ref[0, :, sl]], axis=0)
        kc = kc_ref[0, :, sl]
        vwin = jnp.concatenate([vwin, jnp.ones_like(vwin)], axis=1)
        vc = jnp.concatenate([vc_ref[0, :, sl], jnp.ones_like(kc)], axis=1)
        halves = []
        for a in range(2):
            in_head = (lane >= a * NA_HEAD_DIM) & (lane < (a + 1) * NA_HEAD_DIM)
            qm = jnp.where(in_head, q2, jnp.zeros_like(q2))
            s_win = lax.dot_general(qm, kwin, nt, preferred_element_type=F32) + bias_ref[0, 2 * p + a]
            s_ctx = lax.dot_general(qm, kc, nt, preferred_element_type=F32)
            m = jnp.maximum(jnp.max(s_win, axis=1, keepdims=True), jnp.max(s_ctx, axis=1, keepdims=True))
            p_win = jnp.exp((s_win - m).astype(BF16))
            p_ctx = jnp.exp((s_ctx - m).astype(BF16))
            o = jnp.dot(p_win, vwin, preferred_element_type=F32) + jnp.dot(p_ctx, vc, preferred_element_type=F32)
            halves.append(o[:, :LANES] / o[:, LANES:])
        o_ref[0, :, sl] = jnp.where(lane < NA_HEAD_DIM, halves[0], halves[1]).astype(o_ref.dtype)


def _na_bias_tables(rpb, rows):
    R = NA_ROWS_PER_STEP
    nblk = rows // R
    kr = NA_WIN_ROWS
    cq = np.arange(GRID_W)
    cstart = np.clip(cq - NA_WIN_COLS // 2, 0, GRID_W - NA_WIN_COLS)
    ck = np.arange(GRID_W)
    col_ok = (ck[None, :] >= cstart[:, None]) & (ck[None, :] < cstart[:, None] + NA_WIN_COLS)
    col_off = np.where(col_ok, ck[None, :] - cq[:, None] + NA_WIN_COLS - 1, 0)
    row_ok = np.zeros((3, R, 3 * R), bool)
    row_off = np.zeros((3, R, 3 * R), np.int64)
    for vi, j in enumerate((0, 1, nblk - 1)):
        for i in range(R):
            r = j * R + i
            r0 = min(max(r - kr // 2, 0), rows - kr)
            for t in range(3):
                jb = j - 1 + t
                if jb < 0 or jb >= nblk:
                    continue
                for rr in range(R):
                    krow = jb * R + rr
                    if r0 <= krow < r0 + kr:
                        row_ok[vi, i, t * R + rr] = True
                        row_off[vi, i, t * R + rr] = krow - r + NA_WIN_ROWS - 1
    col_sel = (np.arange(rpb.shape[2])[None, None, :] == col_off[:, :, None]) & col_ok[:, :, None]
    row_sel = (np.arange(rpb.shape[1])[None, None, None, :] == row_off[..., None]) & row_ok[..., None]
    by_col = jnp.einsum('hrc,qkc->hrqk', rpb, jnp.asarray(col_sel, F32), precision=HIGHEST)
    vals = jnp.einsum('vixr,hrqk->vhiqxk', jnp.asarray(row_sel, F32), by_col, precision=HIGHEST)
    ok = row_ok[:, None, :, None, :, None] & col_ok[None, None, None, :, None, :]
    vals = jnp.where(jnp.asarray(ok), vals, NEG)
    return vals.reshape(3, NA_HEADS, R * GRID_W, 3 * R * GRID_W)


def _na(zna, zcna, bias, rows):
    b, n, _ = zna.shape
    ctx = zcna.shape[1]
    R = NA_ROWS_PER_STEP
    nq = R * GRID_W
    nblk = rows // R
    kb = lambda col, off: pl.BlockSpec(
        (1, nq, NA_WIDTH), lambda i, j: (i, jnp.clip(j + off, 0, nblk - 1), col))
    variant = lambda i, j: (jnp.where(j == 0, 0, jnp.where(j == nblk - 1, 2, 1)), 0, 0, 0)
    return pl.pallas_call(
        _na_kernel,
        grid=(b, nblk),
        in_specs=[pl.BlockSpec((1, nq, NA_WIDTH), lambda i, j: (i, j, 0)),
                  kb(1, -1), kb(1, 0), kb(1, 1), kb(2, -1), kb(2, 0), kb(2, 1),
                  pl.BlockSpec((1, ctx, NA_WIDTH), lambda i, j: (i, 0, 1)),
                  pl.BlockSpec((1, ctx, NA_WIDTH), lambda i, j: (i, 0, 2)),
                  pl.BlockSpec((1, NA_HEADS, nq, 3 * nq), variant)],
        out_specs=pl.BlockSpec((1, nq, NA_WIDTH), lambda i, j: (i, j, 0)),
        out_shape=jax.ShapeDtypeStruct((b, n, NA_WIDTH), BF16),
        compiler_params=_cparams("parallel", "parallel"),
        name="na",
    )(zna, zna, zna, zna, zna, zna, zna, zcna, zcna, bias)


def _mix_kernel(na_ref, hf_ref, hb_ref, o_ref, x_ref, g1_ref, sh2_ref, sc2_ref, wo_ref, mg_ref, l1g_ref, l1b_ref,
                wr_ref, br_ref, xmid_ref, rt_ref, rw_ref, cnt_ref, carry_s, *, alpha):
    first = (pl.program_id(0) == 0) & (pl.program_id(1) == 0)

    @pl.when(first)
    def _():
        carry_s[...] = jnp.zeros_like(carry_s)

    tm = x_ref.shape[1]
    h = hf_ref[0].astype(F32) + hb_ref[0].astype(F32)
    parts = []
    for hd in range(ML_HEADS):
        hh = h[:, hd * LANES:(hd + 1) * LANES]
        parts.append(hh * lax.rsqrt(jnp.mean(hh * hh, axis=-1, keepdims=True) + LN_EPS))
    hn = jnp.concatenate(parts, axis=1)
    ml = (hn * mg_ref[...] * jax.nn.sigmoid(o_ref[0].astype(F32))).astype(BF16)
    mix = (jnp.dot(na_ref[0], wo_ref[0:NA_WIDTH, :], preferred_element_type=F32)
           + jnp.dot(ml, wo_ref[NA_WIDTH:, :], preferred_element_type=F32))
    xmid = _ln_rows(alpha * x_ref[0] + g1_ref[0] * mix) * l1g_ref[...] + l1b_ref[...]
    xmid_ref[0] = xmid

    xt = (_ln_rows(xmid) * (1.0 + sc2_ref[0]) + sh2_ref[0]).astype(BF16)
    logits = jnp.dot(xt, wr_ref[...], preferred_element_type=F32) + br_ref[...]
    lane = lax.broadcasted_iota(jnp.int32, (tm, LANES), 1)
    is_g = lane < N_GROUPS
    gl = jnp.where(is_g, logits, NEG)
    gmax = jnp.max(gl, axis=1, keepdims=True)
    grp = jnp.min(jnp.where(gl == gmax, lane, LANES), axis=1, keepdims=True)
    gsum = jnp.sum(jnp.where(is_g, jnp.exp(gl - gmax), 0.0), axis=1, keepdims=True)
    grp_w = 1.0 / gsum
    lo = N_GROUPS + EXPERTS_PER_GROUP * grp
    el = jnp.where((lane >= lo) & (lane < lo + EXPERTS_PER_GROUP), logits, NEG)
    t1 = jnp.max(el, axis=1, keepdims=True)
    i1 = jnp.min(jnp.where(el == t1, lane, LANES), axis=1, keepdims=True)
    el2 = jnp.where(lane == i1, NEG, el)
    t2 = jnp.max(el2, axis=1, keepdims=True)
    i2 = jnp.min(jnp.where(el2 == t2, lane, LANES), axis=1, keepdims=True)
    e21 = jnp.exp(t2 - t1)
    w0 = grp_w / (1.0 + e21)
    w1 = grp_w * e21 / (1.0 + e21)

    hit1 = lane == i1
    hit2 = lane == i2
    onehot = (hit1 | hit2).astype(BF16)
    r_i = lax.broadcasted_iota(jnp.int32, (tm, tm), 0)
    c_i = lax.broadcasted_iota(jnp.int32, (tm, tm), 1)
    before = (r_i > c_i).astype(BF16)
    prefix = jnp.dot(before, onehot, preferred_element_type=F32) + carry_s[0:1, :]
    rank0 = jnp.sum(jnp.where(hit1, prefix, 0.0), axis=1, keepdims=True)
    rank1 = jnp.sum(jnp.where(hit2, prefix, 0.0), axis=1, keepdims=True)
    total = carry_s[0:1, :] + jnp.sum(onehot.astype(F32), axis=0, keepdims=True)
    carry_s[...] = jnp.broadcast_to(total, carry_s.shape)
    cnt_ref[...] = jnp.broadcast_to(total, cnt_ref.shape)

    rf = jnp.where(lane == 0, (i1 - N_GROUPS).astype(F32),
                   jnp.where(lane == 1, (i2 - N_GROUPS).astype(F32),
                             jnp.where(lane == 2, rank0, jnp.where(lane == 3, rank1, 0.0))))
    rt_ref[0] = rf.T[0:SUBLANES, :].astype(jnp.int32)
    rw_ref[0] = jnp.where(lane == 0, w0, jnp.where(lane == 1, w1, 0.0))


def _mix(na, hf, hb, zvo, x, g1, sh2, sc2, wo, mg, l1g, l1b, wr, br, alpha, tm):
    b, n, d = x.shape
    row = lambda w: pl.BlockSpec((1, tm, w), lambda i, j: (i, j, 0))
    vec = pl.BlockSpec((1, 1, d), lambda i, j: (i, 0, 0))
    full = lambda a: pl.BlockSpec(a.shape, lambda i, j: (0,) * a.ndim)
    ocol = 1
    return pl.pallas_call(
        functools.partial(_mix_kernel, alpha=alpha),
        grid=(b, n // tm),
        in_specs=[row(NA_WIDTH), row(ML_WIDTH), row(ML_WIDTH),
                  pl.BlockSpec((1, tm, ML_WIDTH), lambda i, j: (i, j, ocol)),
                  row(d), vec, vec, vec, full(wo), full(mg), full(l1g), full(l1b), full(wr), full(br)],
        out_specs=[row(d), pl.BlockSpec((1, SUBLANES, tm), lambda i, j: (i * (n // tm) + j, 0, 0)), row(LANES),
                   pl.BlockSpec((SUBLANES, LANES), lambda i, j: (0, 0))],
        out_shape=[jax.ShapeDtypeStruct((b, n, d), F32),
                   jax.ShapeDtypeStruct((b * (n // tm), SUBLANES, tm), jnp.int32),
                   jax.ShapeDtypeStruct((b, n, LANES), F32),
                   jax.ShapeDtypeStruct((SUBLANES, LANES), F32)],
        scratch_shapes=[pltpu.VMEM((8, LANES), F32)],
        compiler_params=_cparams("arbitrary", "arbitrary"),
        name="mix",
    )(na, hf, hb, zvo, x, g1, sh2, sc2, wo, mg, l1g, l1b, wr, br)


def _zero_fill_padding(pad_base_ref, pad_len_ref, nused_ref, xs_ref, zero_s, sem, wait):
    zero_t, xs_t = _token_view(zero_s), _token_view(xs_ref)
    tb = zero_t.shape[0]

    def run(copy):
        copy.wait() if wait else copy.start()

    def fill(off, nrows):
        run(pltpu.make_async_copy(zero_t.at[pl.ds(0, nrows)], xs_t.at[pl.ds(off, nrows)], sem))

    def per_expert(e, _):
        plen = pad_len_ref[e]
        base = pad_base_ref[e]
        bit = tb // 2
        while bit >= 1:
            off = base + (plen & ~(2 * bit - 1))

            @pl.when((plen & bit) != 0)
            def _(bit=bit, off=off):
                fill(off, bit)

            bit //= 2
        return 0

    lax.fori_loop(0, N_EXPERTS, per_expert, 0)

    def per_block(i, _):
        fill(i * tb, tb)
        return 0

    lax.fori_loop(nused_ref[0], xs_t.shape[0] // tb, per_block, 0)


ROW_WORDS = 4


def _token_view(ref):
    return ref.reshape(ref.shape[0] // ROW_WORDS, ROW_WORDS, LANES)


def _bf16_bits(v):
    return lax.bitcast_convert_type(v.astype(BF16).astype(F32), jnp.int32)


def _store_token_rows(dst_ref, base, val):
    tm, d = val.shape
    assert d == 2 * ROW_WORDS * LANES
    for s in range(ROW_WORDS):
        lo = _bf16_bits(val[:, s * LANES:(s + 1) * LANES])
        hi = _bf16_bits(val[:, (s + ROW_WORDS) * LANES:(s + ROW_WORDS + 1) * LANES])
        dst_ref[pl.ds(base * ROW_WORDS + s, tm, stride=ROW_WORDS), :] = hi | lax.shift_right_logical(lo, 16)


def _load_token_rows(src_ref, base, tm, dtype):
    words = [src_ref[pl.ds(base * ROW_WORDS + s, tm, stride=ROW_WORDS), :] for s in range(ROW_WORDS)]
    lo = [lax.bitcast_convert_type(w << 16, F32).astype(dtype) for w in words]
    hi = [lax.bitcast_convert_type(w & -65536, F32).astype(dtype) for w in words]
    return jnp.concatenate(lo + hi, axis=1)


def _dispatch_kernel(pad_base_ref, pad_len_ref, nused_ref, dest_ref, xmid_ref, sh2_ref, sc2_ref, xs_ref,
                     xt_s, zero_s, sem, zsem):
    tm = xmid_ref.shape[1]
    step = pl.program_id(0) * pl.num_programs(1) + pl.program_id(1)
    nsteps = pl.num_programs(0) * pl.num_programs(1)

    slot = step % 2

    def wait_slot_copies(s):
        for _ in range(2):
            pltpu.make_async_copy(_token_view(xt_s).at[pl.ds(s * tm, tm)], _token_view(xs_ref).at[pl.ds(0, tm)],
                                  sem.at[s]).wait()

    @pl.when(step == 0)
    def _():
        zero_s[...] = jnp.zeros_like(zero_s)
        _zero_fill_padding(pad_base_ref, pad_len_ref, nused_ref, xs_ref, zero_s, zsem, False)

    for c in range(tm // COMBINE_CHUNK):
        rows = slice(c * COMBINE_CHUNK, (c + 1) * COMBINE_CHUNK)
        xt = _ln_rows(xmid_ref[0, rows, :]) * (1.0 + sc2_ref[0]) + sh2_ref[0]
        _store_token_rows(xt_s, slot * tm + c * COMBINE_CHUNK, xt)
        for r in range(c * COMBINE_CHUNK, (c + 1) * COMBINE_CHUNK):
            for k in range(2):
                pltpu.make_async_copy(_token_view(xt_s).at[slot * tm + r], _token_view(xs_ref).at[dest_ref[0, k, r]],
                                      sem.at[slot]).start(priority=k)

    @pl.when(step == 0)
    def _():
        _zero_fill_padding(pad_base_ref, pad_len_ref, nused_ref, xs_ref, zero_s, zsem, True)

    @pl.when(step > 0)
    def _():
        wait_slot_copies(1 - slot)

    @pl.when(step == nsteps - 1)
    def _():
        wait_slot_copies(slot)


def _dispatch(xmid, sh2, sc2, dest, pad_base, pad_len, nused, cap, tm):
    b, n, d = xmid.shape
    assert d == 2 * ROW_WORDS * LANES
    nt = n // tm
    vec = pl.BlockSpec((1, 1, d), lambda i, j, *_: (i, 0, 0))
    return pl.pallas_call(
        _dispatch_kernel,
        grid_spec=pltpu.PrefetchScalarGridSpec(
            num_scalar_prefetch=3,
            grid=(b, nt),
            in_specs=[pl.BlockSpec((1, 2, tm), lambda i, j, *_: (i * nt + j, 0, 0), memory_space=pltpu.SMEM),
                      pl.BlockSpec((1, tm, d), lambda i, j, *_: (i, j, 0)), vec, vec],
            out_specs=pl.BlockSpec(memory_space=pl.ANY),
            scratch_shapes=[pltpu.VMEM((2 * tm * ROW_WORDS, LANES), ROW_DTYPE),
                            pltpu.VMEM((EXPERT_ROWS * ROW_WORDS, LANES), ROW_DTYPE),
                            pltpu.SemaphoreType.DMA((2,)), pltpu.SemaphoreType.DMA]),
        out_shape=jax.ShapeDtypeStruct((cap * ROW_WORDS, LANES), ROW_DTYPE),
        compiler_params=_cparams("arbitrary", "arbitrary"),
        name="dispatch",
    )(pad_base, pad_len, nused, dest, xmid, sh2, sc2)


def _expert_kernel(be_ref, nv_ref, xs_ref, w1_ref, w3_ref, w2_ref, ys_ref, w1b, w3b, w2b):
    tb = xs_ref.shape[0] // ROW_WORDS
    i = pl.program_id(0)
    e = be_ref[i]
    changed = (i == 0) | (be_ref[jnp.maximum(i - 1, 0)] != e)

    @pl.when(changed)
    def _():
        w1b[...] = w1_ref[0].astype(BF16)
        w3b[...] = w3_ref[0].astype(BF16)
        w2b[...] = w2_ref[0].astype(BF16)

    nv = nv_ref[i]

    @pl.when(nv > 0)
    def _():
        xb = _load_token_rows(xs_ref, 0, tb, BF16)
        h1 = jnp.dot(xb, w1b[...], preferred_element_type=F32)
        h3 = jnp.dot(xb, w3b[...], preferred_element_type=F32)
        a = (_silu(h1) * h3).astype(BF16)
        _store_token_rows(ys_ref, 0, jnp.dot(a, w2b[...], preferred_element_type=F32))

    @pl.when(nv == 0)
    def _():
        ys_ref[...] = jnp.zeros_like(ys_ref)


def _experts(xs, block_e, block_nv, w1, w3, w2):
    d, hid = w1.shape[1], w1.shape[2]
    tb = EXPERT_ROWS
    rows = pl.BlockSpec((tb * ROW_WORDS, LANES), lambda i, be, nv: (i, 0))
    return pl.pallas_call(
        _expert_kernel,
        grid_spec=pltpu.PrefetchScalarGridSpec(
            num_scalar_prefetch=2,
            grid=(xs.shape[0] // (tb * ROW_WORDS),),
            in_specs=[rows,
                      pl.BlockSpec((1, d, hid), lambda i, be, nv: (be[i], 0, 0)),
                      pl.BlockSpec((1, d, hid), lambda i, be, nv: (be[i], 0, 0)),
                      pl.BlockSpec((1, hid, d), lambda i, be, nv: (be[i], 0, 0))],
            out_specs=rows,
            scratch_shapes=[pltpu.VMEM((d, hid), BF16), pltpu.VMEM((d, hid), BF16), pltpu.VMEM((hid, d), BF16)]),
        out_shape=jax.ShapeDtypeStruct(xs.shape, xs.dtype),
        compiler_params=_cparams("arbitrary"),
        name="experts",
    )(block_e, block_nv, xs, w1, w3, w2)


def _combine_kernel(dcur_ref, dnext_ref, xmid_ref, rw_ref, g2_ref, l2g_ref, l2b_ref, ys_ref, o_ref,
                    y0_s, y1_s, sem, *, alpha):
    tm = xmid_ref.shape[1]
    step = pl.program_id(0) * pl.num_programs(1) + pl.program_id(1)
    nsteps = pl.num_programs(0) * pl.num_programs(1)
    slot = step % 2
    other = 1 - slot

    def start_row(dest_ref, into, r):
        for k, buf in ((0, y0_s), (1, y1_s)):
            pltpu.make_async_copy(_token_view(ys_ref).at[dest_ref[0, k, r]],
                                  _token_view(buf).at[into * tm + r], sem.at[into]).start(priority=k)

    def wait_slot(into):
        for buf in (y0_s, y1_s):
            pltpu.make_async_copy(_token_view(ys_ref).at[pl.ds(0, tm)], _token_view(buf).at[pl.ds(into * tm, tm)],
                                  sem.at[into]).wait()

    @pl.when(step == 0)
    def _():
        lax.fori_loop(0, tm, lambda r, c: (start_row(dcur_ref, 0, r), c)[1], 0, unroll=8)

    wait_slot(slot)
    for c in range(tm // COMBINE_CHUNK):
        for r in range(c * COMBINE_CHUNK, (c + 1) * COMBINE_CHUNK):
            start_row(dnext_ref, other, r)
        rows = slice(c * COMBINE_CHUNK, (c + 1) * COMBINE_CHUNK)
        base = slot * tm + c * COMBINE_CHUNK
        rw = rw_ref[0, rows, :]
        moe = (rw[:, 0:1] * _load_token_rows(y0_s, base, COMBINE_CHUNK, F32)
               + rw[:, 1:2] * _load_token_rows(y1_s, base, COMBINE_CHUNK, F32))
        o_ref[0, rows, :] = (_ln_rows(alpha * xmid_ref[0, rows, :] + g2_ref[0] * moe) * l2g_ref[...] + l2b_ref[...])

    @pl.when(step == nsteps - 1)
    def _():
        wait_slot(other)


def _combine(xmid, rw, g2, l2g, l2b, ys, dest, alpha, tm):
    b, n, d = xmid.shape
    nt = n // tm
    full = lambda a: pl.BlockSpec(a.shape, lambda i, j: (0,) * a.ndim)
    return pl.pallas_call(
        functools.partial(_combine_kernel, alpha=alpha),
        grid=(b, nt),
        in_specs=[pl.BlockSpec((1, 2, tm), lambda i, j: (i * nt + j, 0, 0), memory_space=pltpu.SMEM),
                  pl.BlockSpec((1, 2, tm), lambda i, j: (jnp.minimum(i * nt + j + 1, b * nt - 1), 0, 0),
                               memory_space=pltpu.SMEM),
                  pl.BlockSpec((1, tm, d), lambda i, j: (i, j, 0)),
                  pl.BlockSpec((1, tm, LANES), lambda i, j: (i, j, 0)),
                  pl.BlockSpec((1, 1, d), lambda i, j: (i, 0, 0)),
                  full(l2g), full(l2b),
                  pl.BlockSpec(memory_space=pl.ANY)],
        out_specs=pl.BlockSpec((1, tm, d), lambda i, j: (i, j, 0)),
        out_shape=jax.ShapeDtypeStruct((b, n, d), F32),
        scratch_shapes=[pltpu.VMEM((2 * tm * ROW_WORDS, LANES), ys.dtype),
                        pltpu.VMEM((2 * tm * ROW_WORDS, LANES), ys.dtype), pltpu.SemaphoreType.DMA((2,))],
        compiler_params=_cparams("arbitrary", "arbitrary"),
        name="combine",
    )(dest, dest, xmid, rw, g2, l2g, l2b, ys)


def _tile(n, want):
    t = min(n, want)
    assert n % t == 0, (n, t)
    return t


def kernel(x, c, ctx, c_ctx, w_ada, b_ada, w_in, conv_w, conv_b, gate_b, rpb, ml_norm_g, w_out, ln1_g, ln1_b,
           w_router_g, b_router_g, w_router_e, b_router_e, w1, w3, w2, ln2_g, ln2_b):
    B, N, D = x.shape
    T_CTX = ctx.shape[1]
    depth = w_ada.shape[0]
    rows = N // GRID_W
    assert depth == 1 and N % GRID_W == 0 and rows % NA_ROWS_PER_STEP == 0 and rows >= 3 * NA_ROWS_PER_STEP
    assert N % ML_CHUNK == 0 and T_CTX % ML_CHUNK == 0
    alpha = (2.0 * depth) ** 0.25
    l = 0

    pad_rows = -(B + 1) % 8
    cvec = jnp.concatenate([c, c_ctx[None], jnp.zeros((pad_rows, D), F32)], axis=0)
    ada = _ada(cvec, w_ada[l], b_ada[l])
    sh1, sc1, g1, sh2, sc2, g2 = [a[:, None, :] for a in jnp.split(ada[:B], 6, axis=-1)]
    csh1, csc1 = [jnp.broadcast_to(a[None], (B, 1, D)) for a in jnp.split(ada[B:B + 1], 6, axis=-1)[:2]]

    col_ml = 3 * NA_WIDTH
    col_g = col_ml + 4 * ML_WIDTH
    col_v = col_ml + 2 * ML_WIDTH
    wb = w_in[l].astype(BF16)
    wna, wqk, wvo = wb[:, :col_ml], wb[:, col_ml:col_v], wb[:, col_v:col_g]
    n_gate = 4 * ML_HEADS
    wg = jnp.pad(wb[:, col_g:], ((0, 0), (0, LANES - n_gate)))
    gb = jnp.pad(gate_b[l], (0, LANES - n_gate)).reshape(1, LANES)
    cos_t, sin_t = _rope_tables(N)
    zna, zvo, qk_l, gcol_l, grow_l = _inproj(x, sh1, sc1, wna, wqk, wvo, wg, gb, conv_w[l], conv_b[l], cos_t, sin_t,
                                             True, _tile(N, 512))
    zcna, zcvo, qk_c, gcol_c, grow_c = _inproj(ctx, csh1, csc1, wna, wqk, wvo, wg, gb, conv_w[l], conv_b[l],
                                               cos_t[:T_CTX], sin_t[:T_CTX], False, _tile(T_CTX, 256))

    c0 = jnp.zeros((B, 2, ML_HEADS, ML_HEAD_DIM, 2 * ML_HEAD_DIM), F32)
    m0 = jnp.zeros((B, 2, ML_HEADS, 8, LANES), F32)
    c_ctx_end, m_ctx_end = _mlstm(qk_c, zcvo, gcol_c, grow_c, c0, m0, False)
    hf, hb, _, _ = _mlstm(qk_l, zvo, gcol_l, grow_l, c_ctx_end, m_ctx_end, True)

    na = _na(zna, zcna, _na_bias_tables(rpb[l], rows), rows)

    wr = jnp.pad(jnp.concatenate([w_router_g[l], w_router_e[l]], axis=1),
                 ((0, 0), (0, LANES - N_GROUPS - N_EXPERTS))).astype(BF16)
    br = jnp.pad(jnp.concatenate([b_router_g[l], b_router_e[l]]), (0, LANES - N_GROUPS - N_EXPERTS)).reshape(1, LANES)
    tm = _tile(N, 512)
    xmid, rt, rw, counts = _mix(na, hf, hb, zvo, x, g1, sh2, sc2, w_out[l].astype(BF16),
                                ml_norm_g[l].reshape(1, ML_WIDTH), ln1_g[l].reshape(1, D), ln1_b[l].reshape(1, D),
                                wr, br, alpha, tm)

    tb = EXPERT_ROWS
    n_assign = 2 * B * N
    cap = -(-n_assign // tb) * tb + N_EXPERTS * tb
    sizes = counts[0, N_GROUPS:N_GROUPS + N_EXPERTS].astype(jnp.int32)
    padded = (sizes + tb - 1) // tb * tb
    pend = jnp.cumsum(padded)
    pstart = pend - padded
    experts = jnp.arange(N_EXPERTS, dtype=jnp.int32)
    first_row = jnp.sum(jnp.where(rt[:, 0:2, :, None] == experts, pstart, 0), axis=-1)
    dest = first_row + rt[:, 2:4, :]
    blk0 = jnp.arange(cap // tb, dtype=jnp.int32) * tb
    block_e = jnp.minimum(jnp.sum(pend[None, :] <= blk0[:, None], axis=1), N_EXPERTS - 1).astype(jnp.int32)
    is_e = block_e[:, None] == experts
    block_nv = jnp.clip(jnp.sum(jnp.where(is_e, pstart + sizes, 0), axis=1) - blk0, 0, tb).astype(jnp.int32)

    nused = (pend[-1:] // tb).astype(jnp.int32)
    xs = _dispatch(xmid, sh2, sc2, dest, pstart + sizes, padded - sizes, nused, cap, tm)
    ys = _experts(xs, block_e, block_nv, w1[l], w3[l], w2[l])
    return _combine(xmid, rw, g2, ln2_g[l].reshape(1, D), ln2_b[l].reshape(1, D), ys, dest, alpha, tm)
```

```python
import functools

import numpy as np
import jax
import jax.numpy as jnp
from jax import lax
from jax.experimental import pallas as pl
from jax.experimental.pallas import tpu as pltpu

F32 = jnp.float32
BF16 = jnp.bfloat16
ROW_DTYPE = jnp.int32
HIGHEST = lax.Precision.HIGHEST

GRID_W = 64
NA_HEADS = 8
NA_HEAD_DIM = 64
NA_WIDTH = NA_HEADS * NA_HEAD_DIM
NA_WIN_ROWS = 8
NA_WIN_COLS = 16
ML_HEADS = 4
ML_HEAD_DIM = 128
ML_WIDTH = ML_HEADS * ML_HEAD_DIM
ML_CHUNK = 128
CONV_K = 5
N_GROUPS = 8
EXPERTS_PER_GROUP = 8
N_EXPERTS = N_GROUPS * EXPERTS_PER_GROUP
ROPE_BASE = 10000.0
LN_EPS = 1e-5

LANES = 128
SUBLANES = 8
VMEM_LIMIT = 56 * 1024 * 1024

NA_ROWS_PER_STEP = 4
EXPERT_ROWS = 512
COMBINE_CHUNK = 64
MLSTM_CHUNKS_PER_STEP = 4
NEG = -1e30


def _cparams(*sem):
    return pltpu.CompilerParams(dimension_semantics=sem, vmem_limit_bytes=VMEM_LIMIT)


def _silu(v):
    return v * jax.nn.sigmoid(v)


def _ln_rows(v):
    mu = jnp.mean(v, axis=-1, keepdims=True)
    vc = v - mu
    var = jnp.mean(vc * vc, axis=-1, keepdims=True)
    return vc * lax.rsqrt(var + LN_EPS)


def _ada_kernel(c_ref, w_ref, b_ref, o_ref):
    o_ref[...] = jnp.dot(_silu(c_ref[...]), w_ref[...], preferred_element_type=F32,
                         precision=HIGHEST) + b_ref[...]


def _ada(cvec, w, b):
    rows, d = cvec.shape
    cols = w.shape[1]
    tn = 1024
    return pl.pallas_call(
        _ada_kernel,
        grid=(cols // tn,),
        in_specs=[pl.BlockSpec((rows, d), lambda j: (0, 0)),
                  pl.BlockSpec((d, tn), lambda j: (0, j)),
                  pl.BlockSpec((1, tn), lambda j: (0, j))],
        out_specs=pl.BlockSpec((rows, tn), lambda j: (0, j)),
        out_shape=jax.ShapeDtypeStruct((rows, cols), F32),
        compiler_params=_cparams("arbitrary"),
        name="ada",
    )(cvec, w, b.reshape(1, cols))


HALO = 16


def _log_sigmoid(v):
    return jnp.minimum(v, 0.0) - jnp.log1p(jnp.exp(-jnp.abs(v)))


GATE_RAW, GATE_PREFIX, GATE_SUFFIX = 0, 16, 32


def _pack_gates(g):
    L = g.shape[0]
    row = lax.broadcasted_iota(jnp.int32, (L, L), 0)
    col = lax.broadcasted_iota(jnp.int32, (L, L), 1)
    lane = lax.broadcasted_iota(jnp.int32, g.shape, 1)
    lf = _log_sigmoid(g)
    prefix = jnp.dot((row >= col).astype(F32), lf, preferred_element_type=F32, precision=HIGHEST)
    suffix = jnp.dot((row <= col).astype(F32), lf, preferred_element_type=F32, precision=HIGHEST)
    return jnp.where(lane < GATE_PREFIX, g,
                     jnp.where(lane < GATE_SUFFIX, pltpu.roll(prefix, GATE_PREFIX, 1),
                               jnp.where(lane < GATE_SUFFIX + GATE_PREFIX, pltpu.roll(suffix, GATE_SUFFIX, 1), 0.0)))


def _inproj_kernel(x_ref, xp_ref, xn_ref, sh_ref, sc_ref, wna_ref, wqk_ref, wvo_ref, wg_ref, gb_ref, cw_ref, cb_ref,
                   cos_ref, sin_ref, zna_ref, zvo_ref, o_ref, gcol_ref, grow_ref, *pad_refs, rope, tr):
    i = pl.program_id(1)
    last = pl.num_programs(1) - 1
    xa = jnp.concatenate([xp_ref[0], x_ref[0], xn_ref[0]], axis=0)
    ya = (_ln_rows(xa) * (1.0 + sc_ref[0]) + sh_ref[0]).astype(BF16)
    yb = ya[HALO:HALO + tr]
    pad = CONV_K // 2
    width = wqk_ref.shape[1]
    lane = lax.broadcasted_iota(jnp.int32, (tr, LANES), 1)
    first_half = (lane % (ML_HEAD_DIM // 2)) < (ML_HEAD_DIM // 4)
    kscale = ML_HEAD_DIM ** -0.5
    cb = 2 * LANES

    def project_qk(c):
        z = jnp.dot(ya, wqk_ref[:, c * cb:(c + 1) * cb], preferred_element_type=F32)
        pad_refs[c][HALO:HALO + tr, :] = z[HALO:HALO + tr]
        pad_refs[c][0:HALO, :] = jnp.where(i > 0, z[0:HALO], 0.0)
        pad_refs[c][HALO + tr:2 * HALO + tr, :] = jnp.where(i < last, z[HALO + tr:], 0.0)

    def conv_group(g):
        cols = slice(g * LANES, (g + 1) * LANES)
        pad_ref = pad_refs[g * LANES // cb]
        pc = slice(g * LANES % cb, g * LANES % cb + LANES)
        acc = cw_ref[0:1, cols] * pad_ref[HALO - pad:HALO - pad + tr, pc] + cb_ref[:, cols]
        for j in range(1, CONV_K):
            acc = acc + cw_ref[j:j + 1, cols] * pad_ref[HALO - pad + j:HALO - pad + j + tr, pc]
        ug = _silu(acc)
        if rope:
            partner = jnp.where(first_half, pltpu.roll(ug, LANES - ML_HEAD_DIM // 4, 1),
                                pltpu.roll(ug, ML_HEAD_DIM // 4, 1))
            ug = ug * cos_ref[...] + partner * sin_ref[...]
        if g >= ML_HEADS:
            ug = ug * kscale
        o_ref[0, :, cols] = ug.astype(BF16)

    other = ([(zna_ref, wna_ref, c) for c in range(wna_ref.shape[1] // cb)]
             + [(zvo_ref, wvo_ref, c) for c in range(wvo_ref.shape[1] // cb)])

    def project_other(n):
        for _ in range(n):
            if other:
                dst, w, c = other.pop(0)
                cols = slice(c * cb, (c + 1) * cb)
                dst[0, :, cols] = jnp.dot(yb, w[:, cols], preferred_element_type=F32).astype(BF16)

    n_qk = width // cb
    per_round = -(-len(other) // n_qk)
    project_qk(0)
    for c in range(n_qk):
        if c + 1 < n_qk:
            project_qk(c + 1)
        project_other(per_round)
        for g in range(c * cb // LANES, (c + 1) * cb // LANES):
            conv_group(g)
    project_other(len(other))
    gates = jnp.dot(yb, wg_ref[...], preferred_element_type=F32) + gb_ref[...]
    for c in range(tr // ML_CHUNK):
        packed = _pack_gates(gates[c * ML_CHUNK:(c + 1) * ML_CHUNK, :])
        gcol_ref[0, c * ML_CHUNK:(c + 1) * ML_CHUNK, :] = packed
        grow_ref[0, c] = packed.T


def _inproj(x, shift, scale, wna, wqk, wvo, wg, gb, conv_w, conv_b, cos_t, sin_t, rope, tr):
    b, t, d = x.shape
    width = 2 * ML_WIDTH
    nh = tr // HALO
    nblk = t // HALO
    L = ML_CHUNK
    full = lambda a: pl.BlockSpec(a.shape, lambda i, j: (0,) * a.ndim)
    row = lambda w: pl.BlockSpec((1, tr, w), lambda i, j: (i, j, 0))
    vec = pl.BlockSpec((1, 1, d), lambda i, j: (i, 0, 0))
    return pl.pallas_call(
        functools.partial(_inproj_kernel, rope=rope, tr=tr),
        grid=(b, t // tr),
        in_specs=[row(d),
                  pl.BlockSpec((1, HALO, d), lambda i, j: (i, jnp.maximum(j * nh - 1, 0), 0)),
                  pl.BlockSpec((1, HALO, d), lambda i, j: (i, jnp.minimum((j + 1) * nh, nblk - 1), 0)),
                  vec, vec, full(wna), full(wqk), full(wvo), full(wg), full(gb),
                  pl.BlockSpec((CONV_K, width), lambda i, j: (0, 0)),
                  pl.BlockSpec((1, width), lambda i, j: (0, 0)),
                  pl.BlockSpec((tr, LANES), lambda i, j: (j, 0)),
                  pl.BlockSpec((tr, LANES), lambda i, j: (j, 0))],
        out_specs=[row(wna.shape[1]), row(wvo.shape[1]), row(width), row(LANES),
                   pl.BlockSpec((1, tr // L, LANES, L), lambda i, j: (i, j, 0, 0))],
        out_shape=[jax.ShapeDtypeStruct((b, t, wna.shape[1]), BF16),
                   jax.ShapeDtypeStruct((b, t, wvo.shape[1]), BF16),
                   jax.ShapeDtypeStruct((b, t, width), BF16),
                   jax.ShapeDtypeStruct((b, t, LANES), F32),
                   jax.ShapeDtypeStruct((b, t // L, LANES, L), F32)],
        scratch_shapes=[pltpu.VMEM((tr + 2 * HALO, 2 * LANES), F32)] * (width // (2 * LANES)),
        compiler_params=_cparams("parallel", "parallel"),
        name="inproj" if rope else "inproj_ctx",
    )(x, x, x, shift, scale, wna, wqk, wvo, wg, gb, conv_w, conv_b.reshape(1, width), cos_t, sin_t)


def _rope_tables(n):
    pos = np.arange(n)
    half = ML_HEAD_DIM // 4
    inv = np.float32(ROPE_BASE) ** (-np.arange(half, dtype=np.float32) / np.float32(half))

    def axis_tables(p):
        ang = p.astype(np.float32)[:, None] * inv[None, :]
        c, s = np.cos(ang), np.sin(ang)
        return np.concatenate([c, c], axis=-1), np.concatenate([-s, s], axis=-1)

    cr, sr = axis_tables(pos // GRID_W)
    cc, sc = axis_tables(pos % GRID_W)
    return (jnp.asarray(np.concatenate([cr, cc], axis=-1), F32), jnp.asarray(np.concatenate([sr, sc], axis=-1), F32))


def _mlstm_direction(d, qk, v, gcol, grow, cn_s, m_s):
    L = ML_CHUNK
    row = lax.broadcasted_iota(jnp.int32, (L, L), 0)
    col = lax.broadcasted_iota(jnp.int32, (L, L), 1)
    fwd = d == 0
    keep = (row >= col) if fwd else (row <= col)
    ones = jnp.ones((L, ML_HEAD_DIM), BF16)
    end = L - 1 if fwd else 0
    outs = []
    for h in range(ML_HEADS):
        ci = GATE_RAW + (2 * d) * ML_HEADS + h
        cf = (GATE_PREFIX if fwd else GATE_SUFFIX) + (2 * d + 1) * ML_HEADS + h
        bc = jnp.broadcast_to(gcol[:, cf:cf + 1], (L, L))
        ic = jnp.broadcast_to(gcol[:, ci:ci + 1], (L, L))
        br = grow[cf:cf + 1, :]
        ir = grow[ci:ci + 1, :]
        m_prev = m_s[d, h][0:1, :]
        dlog = jnp.where(keep, bc - br + ir, NEG)
        m_t = jnp.maximum(bc + m_prev, jnp.max(dlog, axis=1, keepdims=True))
        dw = jnp.exp((dlog - m_t).astype(BF16))
        inter = jnp.exp(bc + m_prev - m_t)
        qh = qk[:, h * LANES:(h + 1) * LANES]
        kh = qk[:, ML_WIDTH + h * LANES:ML_WIDTH + (h + 1) * LANES]
        vp = jnp.concatenate([v[:, h * LANES:(h + 1) * LANES], ones], axis=1)
        s = lax.dot_general(qh, kh, (((1,), (1,)), ((), ())), preferred_element_type=F32).astype(BF16) * dw
        cn = cn_s[d, h]
        a1 = jnp.dot(s, vp, preferred_element_type=F32)
        a2 = jnp.dot(qh, cn.astype(BF16), preferred_element_type=F32)
        num = a1[:, :ML_HEAD_DIM] + inter * a2[:, :ML_HEAD_DIM]
        den = a1[:, ML_HEAD_DIM:] + inter * a2[:, ML_HEAD_DIM:]
        outs.append(num / jnp.maximum(jnp.abs(den), jnp.exp(-m_t)))
        b_end = jnp.broadcast_to(br[:, end:end + 1], (1, L))
        g_row = b_end - br + ir
        m_new = jnp.maximum(b_end + m_prev, jnp.max(g_row, axis=1, keepdims=True))
        decay = jnp.exp(b_end + m_prev - m_new)
        wgt = jnp.exp((b_end - bc + ic - m_new).astype(BF16))
        kw = kh * wgt
        upd = lax.dot_general(kw, vp, (((0,), (0,)), ((), ())), preferred_element_type=F32)
        cn_s[d, h] = jnp.concatenate([decay, decay], axis=1) * cn + upd
        m_s[d, h] = jnp.broadcast_to(m_new, m_s.shape[2:])
    return jnp.concatenate(outs, axis=1)


def _mlstm_kernel(qkf_ref, vf_ref, gcf_ref, grf_ref, qkb_ref, vb_ref, gcb_ref, grb_ref, c0_ref, m0_ref, *rest, emit_h):
    if emit_h:
        hf_ref, hb_ref, ct_ref, mt_ref, cn_s, m_s = rest
    else:
        ct_ref, mt_ref, cn_s, m_s = rest
        hf_ref = hb_ref = None
    c = pl.program_id(1)
    L = ML_CHUNK

    @pl.when(c == 0)
    def _():
        cn_s[...] = c0_ref[0]
        m_s[...] = m0_ref[0]

    for d, refs, h_ref in ((0, (qkf_ref, vf_ref, gcf_ref, grf_ref), hf_ref), (1, (qkb_ref, vb_ref, gcb_ref, grb_ref), hb_ref)):
        qk_ref, v_ref, gc_ref, gr_ref = refs
        n_sub = qk_ref.shape[1] // L
        for s in (range(n_sub) if d == 0 else reversed(range(n_sub))):
            rows = slice(s * L, (s + 1) * L)
            h = _mlstm_direction(d, qk_ref[0, rows, :], v_ref[0, rows, :], gc_ref[0, rows, :], gr_ref[0, s], cn_s, m_s)
            if h_ref is not None:
                h_ref[0, rows, :] = h.astype(h_ref.dtype)

    @pl.when(c == pl.num_programs(1) - 1)
    def _():
        ct_ref[0] = cn_s[...]
        mt_ref[0] = m_s[...]


def _mlstm(qk, zvo, gcol, grow, c0, m0, emit_h):
    b, t, _ = qk.shape
    n_sub = min(MLSTM_CHUNKS_PER_STEP, t // ML_CHUNK)
    R = n_sub * ML_CHUNK
    assert t % R == 0
    nc = t // R
    vcol = 0
    f_idx = lambda i, c: (i, c, 0)
    b_idx = lambda i, c: (i, nc - 1 - c, 0)
    st_c = pl.BlockSpec((1, 2, ML_HEADS, ML_HEAD_DIM, 2 * ML_HEAD_DIM), lambda i, c: (i, 0, 0, 0, 0))
    st_m = pl.BlockSpec((1, 2, ML_HEADS, 8, LANES), lambda i, c: (i, 0, 0, 0, 0))
    out_specs = [st_c, st_m]
    out_shape = [jax.ShapeDtypeStruct(c0.shape, F32), jax.ShapeDtypeStruct(m0.shape, F32)]
    if emit_h:
        out_specs = [pl.BlockSpec((1, R, ML_WIDTH), f_idx), pl.BlockSpec((1, R, ML_WIDTH), b_idx)] + out_specs
        out_shape = [jax.ShapeDtypeStruct((b, t, ML_WIDTH), BF16)] * 2 + out_shape
    return pl.pallas_call(
        functools.partial(_mlstm_kernel, emit_h=emit_h),
        grid=(b, nc),
        in_specs=[pl.BlockSpec((1, R, 2 * ML_WIDTH), f_idx),
                  pl.BlockSpec((1, R, ML_WIDTH), lambda i, c: (i, c, vcol)),
                  pl.BlockSpec((1, R, LANES), f_idx),
                  pl.BlockSpec((1, n_sub, LANES, ML_CHUNK), lambda i, c: (i, c, 0, 0)),
                  pl.BlockSpec((1, R, 2 * ML_WIDTH), b_idx),
                  pl.BlockSpec((1, R, ML_WIDTH), lambda i, c: (i, nc - 1 - c, vcol)),
                  pl.BlockSpec((1, R, LANES), b_idx),
                  pl.BlockSpec((1, n_sub, LANES, ML_CHUNK), lambda i, c: (i, nc - 1 - c, 0, 0)),
                  st_c, st_m],
        out_specs=out_specs,
        out_shape=out_shape,
        scratch_shapes=[pltpu.VMEM((2, ML_HEADS, ML_HEAD_DIM, 2 * ML_HEAD_DIM), F32),
                        pltpu.VMEM((2, ML_HEADS, 8, LANES), F32)],
        compiler_params=_cparams("parallel", "arbitrary"),
        name="mlstm" if emit_h else "mlstm_ctx",
    )(qk, zvo, gcol, grow, qk, zvo, gcol, grow, c0, m0)


def _lane_in(shape, start, width):
    lane = lax.broadcasted_iota(jnp.int32, shape, 1)
    return (lane >= start) & (lane < start + width)


def _na_kernel(q_ref, k0_ref, k1_ref, k2_ref, v0_ref, v1_ref, v2_ref, kc_ref, vc_ref, bias_ref, o_ref):
    nq = q_ref.shape[1]
    lane = lax.broadcasted_iota(jnp.int32, (nq, LANES), 1)
    nt = (((1,), (1,)), ((), ()))
    for p in range(NA_HEADS // 2):
        sl = slice(p * LANES, (p + 1) * LANES)
        q2 = q_ref[0, :, sl] * (NA_HEAD_DIM ** -0.5)
        kwin = jnp.concatenate([k0_ref[0, :, sl], k1_ref[0, :, sl], k2_ref[0, :, sl]], axis=0)
        vwin = jnp.concatenate([v0_ref[0, :, sl], v1_ref[0, :, sl], v2_ref[0, :, sl]], axis=0)
        kc = kc_ref[0, :, sl]
        vwin = jnp.concatenate([vwin, jnp.ones_like(vwin)], axis=1)
        vc = jnp.concatenate([vc_ref[0, :, sl], jnp.ones_like(kc)], axis=1)
        halves = []
        for a in range(2):
            in_head = (lane >= a * NA_HEAD_DIM) & (lane < (a + 1) * NA_HEAD_DIM)
            qm = jnp.where(in_head, q2, jnp.zeros_like(q2))
            s_win = lax.dot_general(qm, kwin, nt, preferred_element_type=F32) + bias_ref[0, 2 * p + a]
            s_ctx = lax.dot_general(qm, kc, nt, preferred_element_type=F32)
            m = jnp.maximum(jnp.max(s_win, axis=1, keepdims=True), jnp.max(s_ctx, axis=1, keepdims=True))
            p_win = jnp.exp((s_win - m).astype(BF16))
            p_ctx = jnp.exp((s_ctx - m).astype(BF16))
            o = jnp.dot(p_win, vwin, preferred_element_type=F32) + jnp.dot(p_ctx, vc, preferred_element_type=F32)
            halves.append(o[:, :LANES] / o[:, LANES:])
        o_ref[0, :, sl] = jnp.where(lane < NA_HEAD_DIM, halves[0], halves[1]).astype(o_ref.dtype)


def _na_bias_tables(rpb, rows):
    R = NA_ROWS_PER_STEP
    nblk = rows // R
    kr = NA_WIN_ROWS
    cq = np.arange(GRID_W)
    cstart = np.clip(cq - NA_WIN_COLS // 2, 0, GRID_W - NA_WIN_COLS)
    ck = np.arange(GRID_W)
    col_ok = (ck[None, :] >= cstart[:, None]) & (ck[None, :] < cstart[:, None] + NA_WIN_COLS)
    col_off = np.where(col_ok, ck[None, :] - cq[:, None] + NA_WIN_COLS - 1, 0)
    row_ok = np.zeros((3, R, 3 * R), bool)
    row_off = np.zeros((3, R, 3 * R), np.int64)
    for vi, j in enumerate((0, 1, nblk - 1)):
        for i in range(R):
            r = j * R + i
            r0 = min(max(r - kr // 2, 0), rows - kr)
            for t in range(3):
                jb = j - 1 + t
                if jb < 0 or jb >= nblk:
                    continue
                for rr in range(R):
                    krow = jb * R + rr
                    if r0 <= krow < r0 + kr:
                        row_ok[vi, i, t * R + rr] = True
                        row_off[vi, i, t * R + rr] = krow - r + NA_WIN_ROWS - 1
    n_ro = rpb.shape[1]
    col_sel = (np.arange(rpb.shape[2])[None, None, :] == col_off[:, :, None]) & col_ok[:, :, None]
    planes = jnp.einsum('hrc,qkc->hrqk', rpb, jnp.asarray(col_sel, F32), precision=HIGHEST)
    planes = jnp.where(jnp.asarray(col_ok), planes, NEG)
    planes = jnp.concatenate([planes, jnp.full((NA_HEADS, 1, GRID_W, GRID_W), NEG, F32)], axis=1)
    plane_of = np.where(row_ok, row_off, n_ro).astype(np.int32).reshape(-1)

    def assemble(sel_ref, planes_ref, o_ref):
        v = pl.program_id(0)
        for i in range(R):
            for x in range(3 * R):
                o_ref[0, 0, i * GRID_W:(i + 1) * GRID_W, x * GRID_W:(x + 1) * GRID_W] = (
                    planes_ref[0, sel_ref[(v * R + i) * 3 * R + x]])

    return pl.pallas_call(
        assemble,
        grid_spec=pltpu.PrefetchScalarGridSpec(
            num_scalar_prefetch=1,
            grid=(3, NA_HEADS),
            in_specs=[pl.BlockSpec((1, n_ro + 1, GRID_W, GRID_W), lambda v, h, sel: (h, 0, 0, 0))],
            out_specs=pl.BlockSpec((1, 1, R * GRID_W, 3 * R * GRID_W), lambda v, h, sel: (v, h, 0, 0))),
        out_shape=jax.ShapeDtypeStruct((3, NA_HEADS, R * GRID_W, 3 * R * GRID_W), F32),
        compiler_params=_cparams("arbitrary", "arbitrary"),
        name="na_bias",
    )(jnp.asarray(plane_of), planes)


def _na(zna, zcna, bias, rows):
    b, n, _ = zna.shape
    ctx = zcna.shape[1]
    R = NA_ROWS_PER_STEP
    nq = R * GRID_W
    nblk = rows // R
    kb = lambda col, off: pl.BlockSpec(
        (1, nq, NA_WIDTH), lambda i, j: (i, jnp.clip(j + off, 0, nblk - 1), col))
    variant = lambda i, j: (jnp.where(j == 0, 0, jnp.where(j == nblk - 1, 2, 1)), 0, 0, 0)
    return pl.pallas_call(
        _na_kernel,
        grid=(b, nblk),
        in_specs=[pl.BlockSpec((1, nq, NA_WIDTH), lambda i, j: (i, j, 0)),
                  kb(1, -1), kb(1, 0), kb(1, 1), kb(2, -1), kb(2, 0), kb(2, 1),
                  pl.BlockSpec((1, ctx, NA_WIDTH), lambda i, j: (i, 0, 1)),
                  pl.BlockSpec((1, ctx, NA_WIDTH), lambda i, j: (i, 0, 2)),
                  pl.BlockSpec((1, NA_HEADS, nq, 3 * nq), variant)],
        out_specs=pl.BlockSpec((1, nq, NA_WIDTH), lambda i, j: (i, j, 0)),
        out_shape=jax.ShapeDtypeStruct((b, n, NA_WIDTH), BF16),
        compiler_params=_cparams("parallel", "parallel"),
        name="na",
    )(zna, zna, zna, zna, zna, zna, zna, zcna, zcna, bias)


def _mix_kernel(na_ref, hf_ref, hb_ref, o_ref, x_ref, g1_ref, sh2_ref, sc2_ref, wo_ref, mg_ref, l1g_ref, l1b_ref,
                wr_ref, br_ref, xmid_ref, rt_ref, rw_ref, cnt_ref, carry_s, *, alpha):
    first = (pl.program_id(0) == 0) & (pl.program_id(1) == 0)

    @pl.when(first)
    def _():
        carry_s[...] = jnp.zeros_like(carry_s)

    tm = x_ref.shape[1]
    h = hf_ref[0].astype(F32) + hb_ref[0].astype(F32)
    parts = []
    for hd in range(ML_HEADS):
        hh = h[:, hd * LANES:(hd + 1) * LANES]
        parts.append(hh * lax.rsqrt(jnp.mean(hh * hh, axis=-1, keepdims=True) + LN_EPS))
    hn = jnp.concatenate(parts, axis=1)
    ml = (hn * mg_ref[...] * jax.nn.sigmoid(o_ref[0].astype(F32))).astype(BF16)
    mix = (jnp.dot(na_ref[0], wo_ref[0:NA_WIDTH, :], preferred_element_type=F32)
           + jnp.dot(ml, wo_ref[NA_WIDTH:, :], preferred_element_type=F32))
    xmid = _ln_rows(alpha * x_ref[0] + g1_ref[0] * mix) * l1g_ref[...] + l1b_ref[...]
    xmid_ref[0] = xmid

    xt = (_ln_rows(xmid) * (1.0 + sc2_ref[0]) + sh2_ref[0]).astype(BF16)
    logits = jnp.dot(xt, wr_ref[...], preferred_element_type=F32) + br_ref[...]
    lane = lax.broadcasted_iota(jnp.int32, (tm, LANES), 1)
    is_g = lane < N_GROUPS
    gl = jnp.where(is_g, logits, NEG)
    gmax = jnp.max(gl, axis=1, keepdims=True)
    grp = jnp.min(jnp.where(gl == gmax, lane, LANES), axis=1, keepdims=True)
    gsum = jnp.sum(jnp.where(is_g, jnp.exp(gl - gmax), 0.0), axis=1, keepdims=True)
    grp_w = 1.0 / gsum
    lo = N_GROUPS + EXPERTS_PER_GROUP * grp
    el = jnp.where((lane >= lo) & (lane < lo + EXPERTS_PER_GROUP), logits, NEG)
    t1 = jnp.max(el, axis=1, keepdims=True)
    i1 = jnp.min(jnp.where(el == t1, lane, LANES), axis=1, keepdims=True)
    el2 = jnp.where(lane == i1, NEG, el)
    t2 = jnp.max(el2, axis=1, keepdims=True)
    i2 = jnp.min(jnp.where(el2 == t2, lane, LANES), axis=1, keepdims=True)
    e21 = jnp.exp(t2 - t1)
    w0 = grp_w / (1.0 + e21)
    w1 = grp_w * e21 / (1.0 + e21)

    hit1 = lane == i1
    hit2 = lane == i2
    onehot = (hit1 | hit2).astype(BF16)
    r_i = lax.broadcasted_iota(jnp.int32, (tm, tm), 0)
    c_i = lax.broadcasted_iota(jnp.int32, (tm, tm), 1)
    before = (r_i > c_i).astype(BF16)
    prefix = jnp.dot(before, onehot, preferred_element_type=F32) + carry_s[0:1, :]
    rank0 = jnp.sum(jnp.where(hit1, prefix, 0.0), axis=1, keepdims=True)
    rank1 = jnp.sum(jnp.where(hit2, prefix, 0.0), axis=1, keepdims=True)
    total = carry_s[0:1, :] + jnp.sum(onehot.astype(F32), axis=0, keepdims=True)
    carry_s[...] = jnp.broadcast_to(total, carry_s.shape)
    cnt_ref[...] = jnp.broadcast_to(total, cnt_ref.shape)

    rf = jnp.where(lane == 0, (i1 - N_GROUPS).astype(F32),
                   jnp.where(lane == 1, (i2 - N_GROUPS).astype(F32),
                             jnp.where(lane == 2, rank0, jnp.where(lane == 3, rank1, 0.0))))
    rt_ref[0] = rf.T[0:SUBLANES, :].astype(jnp.int32)
    rw_ref[0] = jnp.where(lane == 0, w0, jnp.where(lane == 1, w1, 0.0))


def _mix(na, hf, hb, zvo, x, g1, sh2, sc2, wo, mg, l1g, l1b, wr, br, alpha, tm):
    b, n, d = x.shape
    row = lambda w: pl.BlockSpec((1, tm, w), lambda i, j: (i, j, 0))
    vec = pl.BlockSpec((1, 1, d), lambda i, j: (i, 0, 0))
    full = lambda a: pl.BlockSpec(a.shape, lambda i, j: (0,) * a.ndim)
    ocol = 1
    return pl.pallas_call(
        functools.partial(_mix_kernel, alpha=alpha),
        grid=(b, n // tm),
        in_specs=[row(NA_WIDTH), row(ML_WIDTH), row(ML_WIDTH),
                  pl.BlockSpec((1, tm, ML_WIDTH), lambda i, j: (i, j, ocol)),
                  row(d), vec, vec, vec, full(wo), full(mg), full(l1g), full(l1b), full(wr), full(br)],
        out_specs=[row(d), pl.BlockSpec((1, SUBLANES, tm), lambda i, j: (i * (n // tm) + j, 0, 0)), row(LANES),
                   pl.BlockSpec((SUBLANES, LANES), lambda i, j: (0, 0))],
        out_shape=[jax.ShapeDtypeStruct((b, n, d), F32),
                   jax.ShapeDtypeStruct((b * (n // tm), SUBLANES, tm), jnp.int32),
                   jax.ShapeDtypeStruct((b, n, LANES), F32),
                   jax.ShapeDtypeStruct((SUBLANES, LANES), F32)],
        scratch_shapes=[pltpu.VMEM((8, LANES), F32)],
        compiler_params=_cparams("arbitrary", "arbitrary"),
        name="mix",
    )(na, hf, hb, zvo, x, g1, sh2, sc2, wo, mg, l1g, l1b, wr, br)


def _zero_fill_padding(pad_base_ref, pad_len_ref, nused_ref, xs_ref, zero_s, sem, wait):
    zero_t, xs_t = _token_view(zero_s), _token_view(xs_ref)
    tb = zero_t.shape[0]

    def run(copy):
        copy.wait() if wait else copy.start()

    def fill(off, nrows):
        run(pltpu.make_async_copy(zero_t.at[pl.ds(0, nrows)], xs_t.at[pl.ds(off, nrows)], sem))

    def per_expert(e, _):
        plen = pad_len_ref[e]
        base = pad_base_ref[e]
        bit = tb // 2
        while bit >= 1:
            off = base + (plen & ~(2 * bit - 1))

            @pl.when((plen & bit) != 0)
            def _(bit=bit, off=off):
                fill(off, bit)

            bit //= 2
        return 0

    lax.fori_loop(0, N_EXPERTS, per_expert, 0)

    def per_block(i, _):
        fill(i * tb, tb)
        return 0

    lax.fori_loop(nused_ref[0], xs_t.shape[0] // tb, per_block, 0)


ROW_WORDS = 4


def _token_view(ref):
    return ref.reshape(ref.shape[0] // ROW_WORDS, ROW_WORDS, LANES)


def _bf16_bits(v):
    return lax.bitcast_convert_type(v.astype(BF16).astype(F32), jnp.int32)


def _store_token_rows(dst_ref, base, val):
    tm, d = val.shape
    assert d == 2 * ROW_WORDS * LANES
    for s in range(ROW_WORDS):
        lo = _bf16_bits(val[:, s * LANES:(s + 1) * LANES])
        hi = _bf16_bits(val[:, (s + ROW_WORDS) * LANES:(s + ROW_WORDS + 1) * LANES])
        dst_ref[pl.ds(base * ROW_WORDS + s, tm, stride=ROW_WORDS), :] = hi | lax.shift_right_logical(lo, 16)


def _load_token_rows(src_ref, base, tm, dtype):
    words = [src_ref[pl.ds(base * ROW_WORDS + s, tm, stride=ROW_WORDS), :] for s in range(ROW_WORDS)]
    lo = [lax.bitcast_convert_type(w << 16, F32).astype(dtype) for w in words]
    hi = [lax.bitcast_convert_type(w & -65536, F32).astype(dtype) for w in words]
    return jnp.concatenate(lo + hi, axis=1)


def _dispatch_kernel(pad_base_ref, pad_len_ref, nused_ref, dest_ref, xmid_ref, sh2_ref, sc2_ref, xs_ref,
                     xt_s, zero_s, sem, zsem):
    tm = xmid_ref.shape[1]
    step = pl.program_id(0) * pl.num_programs(1) + pl.program_id(1)
    nsteps = pl.num_programs(0) * pl.num_programs(1)

    slot = step % 2

    def wait_slot_copies(s):
        for _ in range(2):
            pltpu.make_async_copy(_token_view(xt_s).at[pl.ds(s * tm, tm)], _token_view(xs_ref).at[pl.ds(0, tm)],
                                  sem.at[s]).wait()

    @pl.when(step == 0)
    def _():
        zero_s[...] = jnp.zeros_like(zero_s)
        _zero_fill_padding(pad_base_ref, pad_len_ref, nused_ref, xs_ref, zero_s, zsem, False)

    for c in range(tm // COMBINE_CHUNK):
        rows = slice(c * COMBINE_CHUNK, (c + 1) * COMBINE_CHUNK)
        xt = _ln_rows(xmid_ref[0, rows, :]) * (1.0 + sc2_ref[0]) + sh2_ref[0]
        _store_token_rows(xt_s, slot * tm + c * COMBINE_CHUNK, xt)
        for r in range(c * COMBINE_CHUNK, (c + 1) * COMBINE_CHUNK):
            for k in range(2):
                pltpu.make_async_copy(_token_view(xt_s).at[slot * tm + r], _token_view(xs_ref).at[dest_ref[0, k, r]],
                                      sem.at[slot]).start(priority=k)

    @pl.when(step == 0)
    def _():
        _zero_fill_padding(pad_base_ref, pad_len_ref, nused_ref, xs_ref, zero_s, zsem, True)

    @pl.when(step > 0)
    def _():
        wait_slot_copies(1 - slot)

    @pl.when(step == nsteps - 1)
    def _():
        wait_slot_copies(slot)


def _dispatch(xmid, sh2, sc2, dest, pad_base, pad_len, nused, cap, tm):
    b, n, d = xmid.shape
    assert d == 2 * ROW_WORDS * LANES
    nt = n // tm
    vec = pl.BlockSpec((1, 1, d), lambda i, j, *_: (i, 0, 0))
    return pl.pallas_call(
        _dispatch_kernel,
        grid_spec=pltpu.PrefetchScalarGridSpec(
            num_scalar_prefetch=3,
            grid=(b, nt),
            in_specs=[pl.BlockSpec((1, 2, tm), lambda i, j, *_: (i * nt + j, 0, 0), memory_space=pltpu.SMEM),
                      pl.BlockSpec((1, tm, d), lambda i, j, *_: (i, j, 0)), vec, vec],
            out_specs=pl.BlockSpec(memory_space=pl.ANY),
            scratch_shapes=[pltpu.VMEM((2 * tm * ROW_WORDS, LANES), ROW_DTYPE),
                            pltpu.VMEM((EXPERT_ROWS * ROW_WORDS, LANES), ROW_DTYPE),
                            pltpu.SemaphoreType.DMA((2,)), pltpu.SemaphoreType.DMA]),
        out_shape=jax.ShapeDtypeStruct((cap * ROW_WORDS, LANES), ROW_DTYPE),
        compiler_params=_cparams("arbitrary", "arbitrary"),
        name="dispatch",
    )(pad_base, pad_len, nused, dest, xmid, sh2, sc2)


def _expert_kernel(be_ref, nv_ref, xs_ref, w1_ref, w3_ref, w2_ref, ys_ref, w1b, w3b, w2b):
    tb = xs_ref.shape[0] // ROW_WORDS
    i = pl.program_id(0)
    e = be_ref[i]
    changed = (i == 0) | (be_ref[jnp.maximum(i - 1, 0)] != e)

    @pl.when(changed)
    def _():
        w1b[...] = w1_ref[0].astype(BF16)
        w3b[...] = w3_ref[0].astype(BF16)
        w2b[...] = w2_ref[0].astype(BF16)

    nv = nv_ref[i]

    @pl.when(nv > 0)
    def _():
        xb = _load_token_rows(xs_ref, 0, tb, BF16)
        h1 = jnp.dot(xb, w1b[...], preferred_element_type=F32)
        h3 = jnp.dot(xb, w3b[...], preferred_element_type=F32)
        a = (_silu(h1) * h3).astype(BF16)
        _store_token_rows(ys_ref, 0, jnp.dot(a, w2b[...], preferred_element_type=F32))

    @pl.when(nv == 0)
    def _():
        ys_ref[...] = jnp.zeros_like(ys_ref)


def _experts(xs, block_e, block_nv, w1, w3, w2):
    d, hid = w1.shape[1], w1.shape[2]
    tb = EXPERT_ROWS
    rows = pl.BlockSpec((tb * ROW_WORDS, LANES), lambda i, be, nv: (i, 0))
    return pl.pallas_call(
        _expert_kernel,
        grid_spec=pltpu.PrefetchScalarGridSpec(
            num_scalar_prefetch=2,
            grid=(xs.shape[0] // (tb * ROW_WORDS),),
            in_specs=[rows,
                      pl.BlockSpec((1, d, hid), lambda i, be, nv: (be[i], 0, 0)),
                      pl.BlockSpec((1, d, hid), lambda i, be, nv: (be[i], 0, 0)),
                      pl.BlockSpec((1, hid, d), lambda i, be, nv: (be[i], 0, 0))],
            out_specs=rows,
            scratch_shapes=[pltpu.VMEM((d, hid), BF16), pltpu.VMEM((d, hid), BF16), pltpu.VMEM((hid, d), BF16)]),
        out_shape=jax.ShapeDtypeStruct(xs.shape, xs.dtype),
        compiler_params=_cparams("arbitrary"),
        name="experts",
    )(block_e, block_nv, xs, w1, w3, w2)


def _combine_kernel(dcur_ref, dnext_ref, xmid_ref, rw_ref, g2_ref, l2g_ref, l2b_ref, ys_ref, o_ref,
                    y0_s, y1_s, sem, *, alpha):
    tm = xmid_ref.shape[1]
    step = pl.program_id(0) * pl.num_programs(1) + pl.program_id(1)
    nsteps = pl.num_programs(0) * pl.num_programs(1)
    slot = step % 2
    other = 1 - slot

    def start_row(dest_ref, into, r):
        for k, buf in ((0, y0_s), (1, y1_s)):
            pltpu.make_async_copy(_token_view(ys_ref).at[dest_ref[0, k, r]],
                                  _token_view(buf).at[into * tm + r], sem.at[into]).start(priority=k)

    def wait_slot(into):
        for buf in (y0_s, y1_s):
            pltpu.make_async_copy(_token_view(ys_ref).at[pl.ds(0, tm)], _token_view(buf).at[pl.ds(into * tm, tm)],
                                  sem.at[into]).wait()

    @pl.when(step == 0)
    def _():
        lax.fori_loop(0, tm, lambda r, c: (start_row(dcur_ref, 0, r), c)[1], 0, unroll=8)

    wait_slot(slot)
    for c in range(tm // COMBINE_CHUNK):
        for r in range(c * COMBINE_CHUNK, (c + 1) * COMBINE_CHUNK):
            start_row(dnext_ref, other, r)
        rows = slice(c * COMBINE_CHUNK, (c + 1) * COMBINE_CHUNK)
        base = slot * tm + c * COMBINE_CHUNK
        rw = rw_ref[0, rows, :]
        moe = (rw[:, 0:1] * _load_token_rows(y0_s, base, COMBINE_CHUNK, F32)
               + rw[:, 1:2] * _load_token_rows(y1_s, base, COMBINE_CHUNK, F32))
        o_ref[0, rows, :] = (_ln_rows(alpha * xmid_ref[0, rows, :] + g2_ref[0] * moe) * l2g_ref[...] + l2b_ref[...])

    @pl.when(step == nsteps - 1)
    def _():
        wait_slot(other)


def _combine(xmid, rw, g2, l2g, l2b, ys, dest, alpha, tm):
    b, n, d = xmid.shape
    nt = n // tm
    full = lambda a: pl.BlockSpec(a.shape, lambda i, j: (0,) * a.ndim)
    return pl.pallas_call(
        functools.partial(_combine_kernel, alpha=alpha),
        grid=(b, nt),
        in_specs=[pl.BlockSpec((1, 2, tm), lambda i, j: (i * nt + j, 0, 0), memory_space=pltpu.SMEM),
                  pl.BlockSpec((1, 2, tm), lambda i, j: (jnp.minimum(i * nt + j + 1, b * nt - 1), 0, 0),
                               memory_space=pltpu.SMEM),
                  pl.BlockSpec((1, tm, d), lambda i, j: (i, j, 0)),
                  pl.BlockSpec((1, tm, LANES), lambda i, j: (i, j, 0)),
                  pl.BlockSpec((1, 1, d), lambda i, j: (i, 0, 0)),
                  full(l2g), full(l2b),
                  pl.BlockSpec(memory_space=pl.ANY)],
        out_specs=pl.BlockSpec((1, tm, d), lambda i, j: (i, j, 0)),
        out_shape=jax.ShapeDtypeStruct((b, n, d), F32),
        scratch_shapes=[pltpu.VMEM((2 * tm * ROW_WORDS, LANES), ys.dtype),
                        pltpu.VMEM((2 * tm * ROW_WORDS, LANES), ys.dtype), pltpu.SemaphoreType.DMA((2,))],
        compiler_params=_cparams("arbitrary", "arbitrary"),
        name="combine",
    )(dest, dest, xmid, rw, g2, l2g, l2b, ys)


def _tile(n, want):
    t = min(n, want)
    assert n % t == 0, (n, t)
    return t


def kernel(x, c, ctx, c_ctx, w_ada, b_ada, w_in, conv_w, conv_b, gate_b, rpb, ml_norm_g, w_out, ln1_g, ln1_b,
           w_router_g, b_router_g, w_router_e, b_router_e, w1, w3, w2, ln2_g, ln2_b):
    B, N, D = x.shape
    T_CTX = ctx.shape[1]
    depth = w_ada.shape[0]
    rows = N // GRID_W
    assert depth == 1 and N % GRID_W == 0 and rows % NA_ROWS_PER_STEP == 0 and rows >= 3 * NA_ROWS_PER_STEP
    assert N % ML_CHUNK == 0 and T_CTX % ML_CHUNK == 0
    alpha = (2.0 * depth) ** 0.25
    l = 0

    pad_rows = -(B + 1) % 8
    cvec = jnp.concatenate([c, c_ctx[None], jnp.zeros((pad_rows, D), F32)], axis=0)
    ada = _ada(cvec, w_ada[l], b_ada[l])
    sh1, sc1, g1, sh2, sc2, g2 = [a[:, None, :] for a in jnp.split(ada[:B], 6, axis=-1)]
    csh1, csc1 = [jnp.broadcast_to(a[None], (B, 1, D)) for a in jnp.split(ada[B:B + 1], 6, axis=-1)[:2]]

    col_ml = 3 * NA_WIDTH
    col_g = col_ml + 4 * ML_WIDTH
    col_v = col_ml + 2 * ML_WIDTH
    wb = w_in[l].astype(BF16)
    wna, wqk, wvo = wb[:, :col_ml], wb[:, col_ml:col_v], wb[:, col_v:col_g]
    n_gate = 4 * ML_HEADS
    wg = jnp.pad(wb[:, col_g:], ((0, 0), (0, LANES - n_gate)))
    gb = jnp.pad(gate_b[l], (0, LANES - n_gate)).reshape(1, LANES)
    cos_t, sin_t = _rope_tables(N)
    zna, zvo, qk_l, gcol_l, grow_l = _inproj(x, sh1, sc1, wna, wqk, wvo, wg, gb, conv_w[l], conv_b[l], cos_t, sin_t,
                                             True, _tile(N, 512))
    zcna, zcvo, qk_c, gcol_c, grow_c = _inproj(ctx, csh1, csc1, wna, wqk, wvo, wg, gb, conv_w[l], conv_b[l],
                                               cos_t[:T_CTX], sin_t[:T_CTX], False, _tile(T_CTX, 256))

    c0 = jnp.zeros((B, 2, ML_HEADS, ML_HEAD_DIM, 2 * ML_HEAD_DIM), F32)
    m0 = jnp.zeros((B, 2, ML_HEADS, 8, LANES), F32)
    c_ctx_end, m_ctx_end = _mlstm(qk_c, zcvo, gcol_c, grow_c, c0, m0, False)
    hf, hb, _, _ = _mlstm(qk_l, zvo, gcol_l, grow_l, c_ctx_end, m_ctx_end, True)

    na = _na(zna, zcna, _na_bias_tables(rpb[l], rows), rows)

    wr = jnp.pad(jnp.concatenate([w_router_g[l], w_router_e[l]], axis=1),
                 ((0, 0), (0, LANES - N_GROUPS - N_EXPERTS))).astype(BF16)
    br = jnp.pad(jnp.concatenate([b_router_g[l], b_router_e[l]]), (0, LANES - N_GROUPS - N_EXPERTS)).reshape(1, LANES)
    tm = _tile(N, 512)
    xmid, rt, rw, counts = _mix(na, hf, hb, zvo, x, g1, sh2, sc2, w_out[l].astype(BF16),
                                ml_norm_g[l].reshape(1, ML_WIDTH), ln1_g[l].reshape(1, D), ln1_b[l].reshape(1, D),
                                wr, br, alpha, tm)

    tb = EXPERT_ROWS
    n_assign = 2 * B * N
    cap = -(-n_assign // tb) * tb + N_EXPERTS * tb
    sizes = counts[0, N_GROUPS:N_GROUPS + N_EXPERTS].astype(jnp.int32)
    padded = (sizes + tb - 1) // tb * tb
    pend = jnp.cumsum(padded)
    pstart = pend - padded
    experts = jnp.arange(N_EXPERTS, dtype=jnp.int32)
    first_row = jnp.sum(jnp.where(rt[:, 0:2, :, None] == experts, pstart, 0), axis=-1)
    dest = first_row + rt[:, 2:4, :]
    blk0 = jnp.arange(cap // tb, dtype=jnp.int32) * tb
    block_e = jnp.minimum(jnp.sum(pend[None, :] <= blk0[:, None], axis=1), N_EXPERTS - 1).astype(jnp.int32)
    is_e = block_e[:, None] == experts
    block_nv = jnp.clip(jnp.sum(jnp.where(is_e, pstart + sizes, 0), axis=1) - blk0, 0, tb).astype(jnp.int32)

    nused = (pend[-1:] // tb).astype(jnp.int32)
    xs = _dispatch(xmid, sh2, sc2, dest, pstart + sizes, padded - sizes, nused, cap, tm)
    ys = _experts(xs, block_e, block_nv, w1[l], w3[l], w2[l])
    return _combine(xmid, rw, g2, ln2_g[l].reshape(1, D), ln2_b[l].reshape(1, D), ys, dest, alpha, tm)
```

```python
import functools

import numpy as np
import jax
import jax.numpy as jnp
from jax import lax
from jax.experimental import pallas as pl
from jax.experimental.pallas import tpu as pltpu

F32 = jnp.float32
BF16 = jnp.bfloat16
ROW_DTYPE = jnp.int32
HIGHEST = lax.Precision.HIGHEST

GRID_W = 64
NA_HEADS = 8
NA_HEAD_DIM = 64
NA_WIDTH = NA_HEADS * NA_HEAD_DIM
NA_WIN_ROWS = 8
NA_WIN_COLS = 16
ML_HEADS = 4
ML_HEAD_DIM = 128
ML_WIDTH = ML_HEADS * ML_HEAD_DIM
ML_CHUNK = 128
CONV_K = 5
N_GROUPS = 8
EXPERTS_PER_GROUP = 8
N_EXPERTS = N_GROUPS * EXPERTS_PER_GROUP
ROPE_BASE = 10000.0
LN_EPS = 1e-5

LANES = 128
SUBLANES = 8
VMEM_LIMIT = 56 * 1024 * 1024

NA_ROWS_PER_STEP = 4
EXPERT_ROWS = 512
EXPERT_ROW_GROUP = 128
COMBINE_CHUNK = 64
MLSTM_CHUNKS_PER_STEP = 4
NEG = -1e30


def _cparams(*sem):
    return pltpu.CompilerParams(dimension_semantics=sem, vmem_limit_bytes=VMEM_LIMIT)


def _silu(v):
    return v * jax.nn.sigmoid(v)


def _ln_rows(v):
    mu = jnp.mean(v, axis=-1, keepdims=True)
    vc = v - mu
    var = jnp.mean(vc * vc, axis=-1, keepdims=True)
    return vc * lax.rsqrt(var + LN_EPS)


def _ada_kernel(c_ref, w_ref, b_ref, o_ref):
    o_ref[...] = jnp.dot(_silu(c_ref[...]), w_ref[...], preferred_element_type=F32,
                         precision=HIGHEST) + b_ref[...]


def _ada(cvec, w, b):
    rows, d = cvec.shape
    cols = w.shape[1]
    tn = d
    assert cols % tn == 0
    return pl.pallas_call(
        _ada_kernel,
        grid=(cols // tn,),
        in_specs=[pl.BlockSpec((rows, d), lambda j: (0, 0)),
                  pl.BlockSpec((d, tn), lambda j: (0, j)),
                  pl.BlockSpec((1, tn), lambda j: (0, j))],
        out_specs=pl.BlockSpec((rows, tn), lambda j: (0, j)),
        out_shape=jax.ShapeDtypeStruct((rows, cols), F32),
        compiler_params=_cparams("arbitrary"),
        name="ada",
    )(cvec, w, b.reshape(1, cols))


HALO = 16


def _log_sigmoid(v):
    return jnp.minimum(v, 0.0) - jnp.log1p(jnp.exp(-jnp.abs(v)))


GATE_RAW, GATE_PREFIX, GATE_SUFFIX = 0, 16, 32


def _pack_gates(g):
    L = g.shape[0]
    row = lax.broadcasted_iota(jnp.int32, (L, L), 0)
    col = lax.broadcasted_iota(jnp.int32, (L, L), 1)
    lane = lax.broadcasted_iota(jnp.int32, g.shape, 1)
    lf = _log_sigmoid(g)
    prefix = jnp.dot((row >= col).astype(F32), lf, preferred_element_type=F32, precision=HIGHEST)
    suffix = jnp.dot((row <= col).astype(F32), lf, preferred_element_type=F32, precision=HIGHEST)
    return jnp.where(lane < GATE_PREFIX, g,
                     jnp.where(lane < GATE_SUFFIX, pltpu.roll(prefix, GATE_PREFIX, 1),
                               jnp.where(lane < GATE_SUFFIX + GATE_PREFIX, pltpu.roll(suffix, GATE_SUFFIX, 1), 0.0)))


def _inproj_kernel(x_ref, xp_ref, xn_ref, sh_ref, sc_ref, wna_ref, wqk_ref, wvo_ref, wg_ref, gb_ref, cw_ref, cb_ref,
                   cos_ref, sin_ref, zna_ref, zvo_ref, o_ref, gcol_ref, grow_ref, *pad_refs, rope, tr):
    i = pl.program_id(1)
    last = pl.num_programs(1) - 1
    xa = jnp.concatenate([xp_ref[0], x_ref[0], xn_ref[0]], axis=0)
    ya = (_ln_rows(xa) * (1.0 + sc_ref[0]) + sh_ref[0]).astype(BF16)
    yb = ya[HALO:HALO + tr]
    pad = CONV_K // 2
    width = wqk_ref.shape[1]
    lane = lax.broadcasted_iota(jnp.int32, (tr, LANES), 1)
    first_half = (lane % (ML_HEAD_DIM // 2)) < (ML_HEAD_DIM // 4)
    kscale = ML_HEAD_DIM ** -0.5
    cb = 2 * LANES

    def project_qk(c):
        z = jnp.dot(ya, wqk_ref[:, c * cb:(c + 1) * cb], preferred_element_type=F32)
        pad_refs[c][HALO:HALO + tr, :] = z[HALO:HALO + tr]
        pad_refs[c][0:HALO, :] = jnp.where(i > 0, z[0:HALO], 0.0)
        pad_refs[c][HALO + tr:2 * HALO + tr, :] = jnp.where(i < last, z[HALO + tr:], 0.0)

    def conv_group(g):
        cols = slice(g * LANES, (g + 1) * LANES)
        pad_ref = pad_refs[g * LANES // cb]
        pc = slice(g * LANES % cb, g * LANES % cb + LANES)
        acc = cw_ref[0:1, cols] * pad_ref[HALO - pad:HALO - pad + tr, pc] + cb_ref[:, cols]
        for j in range(1, CONV_K):
            acc = acc + cw_ref[j:j + 1, cols] * pad_ref[HALO - pad + j:HALO - pad + j + tr, pc]
        ug = _silu(acc)
        if rope:
            partner = jnp.where(first_half, pltpu.roll(ug, LANES - ML_HEAD_DIM // 4, 1),
                                pltpu.roll(ug, ML_HEAD_DIM // 4, 1))
            ug = ug * cos_ref[...] + partner * sin_ref[...]
        if g >= ML_HEADS:
            ug = ug * kscale
        o_ref[0, :, cols] = ug.astype(BF16)

    other = ([(zna_ref, wna_ref, c) for c in range(wna_ref.shape[1] // cb)]
             + [(zvo_ref, wvo_ref, c) for c in range(wvo_ref.shape[1] // cb)])

    def project_other(n):
        for _ in range(n):
            if other:
                dst, w, c = other.pop(0)
                cols = slice(c * cb, (c + 1) * cb)
                dst[0, :, cols] = jnp.dot(yb, w[:, cols], preferred_element_type=F32).astype(BF16)

    n_qk = width // cb
    per_round = -(-len(other) // n_qk)
    project_qk(0)
    for c in range(n_qk):
        if c + 1 < n_qk:
            project_qk(c + 1)
        project_other(per_round)
        for g in range(c * cb // LANES, (c + 1) * cb // LANES):
            conv_group(g)
    project_other(len(other))
    gates = jnp.dot(yb, wg_ref[...], preferred_element_type=F32) + gb_ref[...]
    for c in range(tr // ML_CHUNK):
        packed = _pack_gates(gates[c * ML_CHUNK:(c + 1) * ML_CHUNK, :])
        gcol_ref[0, c * ML_CHUNK:(c + 1) * ML_CHUNK, :] = packed
        grow_ref[0, c] = packed.T


def _inproj(x, shift, scale, wna, wqk, wvo, wg, gb, conv_w, conv_b, cos_t, sin_t, rope, tr):
    b, t, d = x.shape
    width = 2 * ML_WIDTH
    nh = tr // HALO
    nblk = t // HALO
    L = ML_CHUNK
    full = lambda a: pl.BlockSpec(a.shape, lambda i, j: (0,) * a.ndim)
    row = lambda w: pl.BlockSpec((1, tr, w), lambda i, j: (i, j, 0))
    vec = pl.BlockSpec((1, 1, d), lambda i, j: (i, 0, 0))
    return pl.pallas_call(
        functools.partial(_inproj_kernel, rope=rope, tr=tr),
        grid=(b, t // tr),
        in_specs=[row(d),
                  pl.BlockSpec((1, HALO, d), lambda i, j: (i, jnp.maximum(j * nh - 1, 0), 0)),
                  pl.BlockSpec((1, HALO, d), lambda i, j: (i, jnp.minimum((j + 1) * nh, nblk - 1), 0)),
                  vec, vec, full(wna), full(wqk), full(wvo), full(wg), full(gb),
                  pl.BlockSpec((CONV_K, width), lambda i, j: (0, 0)),
                  pl.BlockSpec((1, width), lambda i, j: (0, 0)),
                  pl.BlockSpec((tr, LANES), lambda i, j: (j, 0)),
                  pl.BlockSpec((tr, LANES), lambda i, j: (j, 0))],
        out_specs=[row(wna.shape[1]), row(wvo.shape[1]), row(width), row(LANES),
                   pl.BlockSpec((1, tr // L, LANES, L), lambda i, j: (i, j, 0, 0))],
        out_shape=[jax.ShapeDtypeStruct((b, t, wna.shape[1]), BF16),
                   jax.ShapeDtypeStruct((b, t, wvo.shape[1]), BF16),
                   jax.ShapeDtypeStruct((b, t, width), BF16),
                   jax.ShapeDtypeStruct((b, t, LANES), F32),
                   jax.ShapeDtypeStruct((b, t // L, LANES, L), F32)],
        scratch_shapes=[pltpu.VMEM((tr + 2 * HALO, 2 * LANES), F32)] * (width // (2 * LANES)),
        compiler_params=_cparams("parallel", "parallel"),
        name="inproj" if rope else "inproj_ctx",
    )(x, x, x, shift, scale, wna, wqk, wvo, wg, gb, conv_w, conv_b.reshape(1, width), cos_t, sin_t)


def _rope_tables(n):
    pos = np.arange(n)
    half = ML_HEAD_DIM // 4
    inv = np.float32(ROPE_BASE) ** (-np.arange(half, dtype=np.float32) / np.float32(half))

    def axis_tables(p):
        ang = p.astype(np.float32)[:, None] * inv[None, :]
        c, s = np.cos(ang), np.sin(ang)
        return np.concatenate([c, c], axis=-1), np.concatenate([-s, s], axis=-1)

    cr, sr = axis_tables(pos // GRID_W)
    cc, sc = axis_tables(pos % GRID_W)
    return (jnp.asarray(np.concatenate([cr, cc], axis=-1), F32), jnp.asarray(np.concatenate([sr, sc], axis=-1), F32))


def _mlstm_direction(d, qk, v, gcol, grow, cn_s, m_s):
    L = ML_CHUNK
    row = lax.broadcasted_iota(jnp.int32, (L, L), 0)
    col = lax.broadcasted_iota(jnp.int32, (L, L), 1)
    fwd = d == 0
    keep = (row >= col) if fwd else (row <= col)
    ones = jnp.ones((L, ML_HEAD_DIM), BF16)
    end = L - 1 if fwd else 0
    outs = []
    for h in range(ML_HEADS):
        ci = GATE_RAW + (2 * d) * ML_HEADS + h
        cf = (GATE_PREFIX if fwd else GATE_SUFFIX) + (2 * d + 1) * ML_HEADS + h
        bc = jnp.broadcast_to(gcol[:, cf:cf + 1], (L, L))
        ic = jnp.broadcast_to(gcol[:, ci:ci + 1], (L, L))
        br = grow[cf:cf + 1, :]
        ir = grow[ci:ci + 1, :]
        m_prev = m_s[d, h][0:1, :]
        dlog = jnp.where(keep, bc - br + ir, NEG)
        m_t = jnp.maximum(bc + m_prev, jnp.max(dlog, axis=1, keepdims=True))
        dw = jnp.exp((dlog - m_t).astype(BF16))
        inter = jnp.exp(bc + m_prev - m_t)
        qh = qk[:, h * LANES:(h + 1) * LANES]
        kh = qk[:, ML_WIDTH + h * LANES:ML_WIDTH + (h + 1) * LANES]
        vp = jnp.concatenate([v[:, h * LANES:(h + 1) * LANES], ones], axis=1)
        s = lax.dot_general(qh, kh, (((1,), (1,)), ((), ())), preferred_element_type=F32).astype(BF16) * dw
        cn = cn_s[d, h]
        a1 = jnp.dot(s, vp, preferred_element_type=F32)
        a2 = jnp.dot(qh, cn.astype(BF16), preferred_element_type=F32)
        num = a1[:, :ML_HEAD_DIM] + inter * a2[:, :ML_HEAD_DIM]
        den = a1[:, ML_HEAD_DIM:] + inter * a2[:, ML_HEAD_DIM:]
        outs.append(num / jnp.maximum(jnp.abs(den), jnp.exp(-m_t)))
        b_end = jnp.broadcast_to(br[:, end:end + 1], (1, L))
        g_row = b_end - br + ir
        m_new = jnp.maximum(b_end + m_prev, jnp.max(g_row, axis=1, keepdims=True))
        decay = jnp.exp(b_end + m_prev - m_new)
        wgt = jnp.exp((b_end - bc + ic - m_new).astype(BF16))
        kw = kh * wgt
        upd = lax.dot_general(kw, vp, (((0,), (0,)), ((), ())), preferred_element_type=F32)
        cn_s[d, h] = jnp.concatenate([decay, decay], axis=1) * cn + upd
        m_s[d, h] = jnp.broadcast_to(m_new, m_s.shape[2:])
    return jnp.concatenate(outs, axis=1)


def _mlstm_kernel(qkf_ref, vf_ref, gcf_ref, grf_ref, qkb_ref, vb_ref, gcb_ref, grb_ref, c0_ref, m0_ref, *rest, emit_h):
    if emit_h:
        hf_ref, hb_ref, ct_ref, mt_ref, cn_s, m_s = rest
    else:
        ct_ref, mt_ref, cn_s, m_s = rest
        hf_ref = hb_ref = None
    c = pl.program_id(1)
    L = ML_CHUNK

    @pl.when(c == 0)
    def _():
        cn_s[...] = c0_ref[0]
        m_s[...] = m0_ref[0]

    for d, refs, h_ref in ((0, (qkf_ref, vf_ref, gcf_ref, grf_ref), hf_ref), (1, (qkb_ref, vb_ref, gcb_ref, grb_ref), hb_ref)):
        qk_ref, v_ref, gc_ref, gr_ref = refs
        n_sub = qk_ref.shape[1] // L
        for s in (range(n_sub) if d == 0 else reversed(range(n_sub))):
            rows = slice(s * L, (s + 1) * L)
            h = _mlstm_direction(d, qk_ref[0, rows, :], v_ref[0, rows, :], gc_ref[0, rows, :], gr_ref[0, s], cn_s, m_s)
            if h_ref is not None:
                h_ref[0, rows, :] = h.astype(h_ref.dtype)

    @pl.when(c == pl.num_programs(1) - 1)
    def _():
        ct_ref[0] = cn_s[...]
        mt_ref[0] = m_s[...]


def _mlstm(qk, zvo, gcol, grow, c0, m0, emit_h):
    b, t, _ = qk.shape
    n_sub = min(MLSTM_CHUNKS_PER_STEP, t // ML_CHUNK)
    R = n_sub * ML_CHUNK
    assert t % R == 0
    nc = t // R
    vcol = 0
    f_idx = lambda i, c: (i, c, 0)
    b_idx = lambda i, c: (i, nc - 1 - c, 0)
    st_c = pl.BlockSpec((1, 2, ML_HEADS, ML_HEAD_DIM, 2 * ML_HEAD_DIM), lambda i, c: (i, 0, 0, 0, 0))
    st_m = pl.BlockSpec((1, 2, ML_HEADS, SUBLANES, LANES), lambda i, c: (i, 0, 0, 0, 0))
    out_specs = [st_c, st_m]
    out_shape = [jax.ShapeDtypeStruct(c0.shape, F32), jax.ShapeDtypeStruct(m0.shape, F32)]
    if emit_h:
        out_specs = [pl.BlockSpec((1, R, ML_WIDTH), f_idx), pl.BlockSpec((1, R, ML_WIDTH), b_idx)] + out_specs
        out_shape = [jax.ShapeDtypeStruct((b, t, ML_WIDTH), BF16)] * 2 + out_shape
    return pl.pallas_call(
        functools.partial(_mlstm_kernel, emit_h=emit_h),
        grid=(b, nc),
        in_specs=[pl.BlockSpec((1, R, 2 * ML_WIDTH), f_idx),
                  pl.BlockSpec((1, R, ML_WIDTH), lambda i, c: (i, c, vcol)),
                  pl.BlockSpec((1, R, LANES), f_idx),
                  pl.BlockSpec((1, n_sub, LANES, ML_CHUNK), lambda i, c: (i, c, 0, 0)),
                  pl.BlockSpec((1, R, 2 * ML_WIDTH), b_idx),
                  pl.BlockSpec((1, R, ML_WIDTH), lambda i, c: (i, nc - 1 - c, vcol)),
                  pl.BlockSpec((1, R, LANES), b_idx),
                  pl.BlockSpec((1, n_sub, LANES, ML_CHUNK), lambda i, c: (i, nc - 1 - c, 0, 0)),
                  st_c, st_m],
        out_specs=out_specs,
        out_shape=out_shape,
        scratch_shapes=[pltpu.VMEM((2, ML_HEADS, ML_HEAD_DIM, 2 * ML_HEAD_DIM), F32),
                        pltpu.VMEM((2, ML_HEADS, SUBLANES, LANES), F32)],
        compiler_params=_cparams("parallel", "arbitrary"),
        name="mlstm" if emit_h else "mlstm_ctx",
    )(qk, zvo, gcol, grow, qk, zvo, gcol, grow, c0, m0)


def _lane_in(shape, start, width):
    lane = lax.broadcasted_iota(jnp.int32, shape, 1)
    return (lane >= start) & (lane < start + width)


def _na_kernel(q_ref, k0_ref, k1_ref, k2_ref, v0_ref, v1_ref, v2_ref, kc_ref, vc_ref, bias_ref, o_ref):
    nq = q_ref.shape[1]
    lane = lax.broadcasted_iota(jnp.int32, (nq, LANES), 1)
    nt = (((1,), (1,)), ((), ()))
    for p in range(NA_HEADS // 2):
        sl = slice(p * LANES, (p + 1) * LANES)
        q2 = q_ref[0, :, sl] * (NA_HEAD_DIM ** -0.5)
        kwin = jnp.concatenate([k0_ref[0, :, sl], k1_ref[0, :, sl], k2_ref[0, :, sl]], axis=0)
        vwin = jnp.concatenate([v0_ref[0, :, sl], v1_ref[0, :, sl], v2_ref[0, :, sl]], axis=0)
        kc = kc_ref[0, :, sl]
        vwin = jnp.concatenate([vwin, jnp.ones_like(vwin)], axis=1)
        vc = jnp.concatenate([vc_ref[0, :, sl], jnp.ones_like(kc)], axis=1)
        halves = []
        for a in range(2):
            in_head = (lane >= a * NA_HEAD_DIM) & (lane < (a + 1) * NA_HEAD_DIM)
            qm = jnp.where(in_head, q2, jnp.zeros_like(q2))
            s_win = lax.dot_general(qm, kwin, nt, preferred_element_type=F32) + bias_ref[0, 2 * p + a]
            s_ctx = lax.dot_general(qm, kc, nt, preferred_element_type=F32)
            m = jnp.maximum(jnp.max(s_win, axis=1, keepdims=True), jnp.max(s_ctx, axis=1, keepdims=True))
            p_win = jnp.exp((s_win - m).astype(BF16))
            p_ctx = jnp.exp((s_ctx - m).astype(BF16))
            o = jnp.dot(p_win, vwin, preferred_element_type=F32) + jnp.dot(p_ctx, vc, preferred_element_type=F32)
            halves.append(o[:, :LANES] / o[:, LANES:])
        o_ref[0, :, sl] = jnp.where(lane < NA_HEAD_DIM, halves[0], halves[1]).astype(o_ref.dtype)


def _na_bias_tables(rpb, rows):
    R = NA_ROWS_PER_STEP
    nblk = rows // R
    kr = NA_WIN_ROWS
    cq = np.arange(GRID_W)
    cstart = np.clip(cq - NA_WIN_COLS // 2, 0, GRID_W - NA_WIN_COLS)
    ck = np.arange(GRID_W)
    col_ok = (ck[None, :] >= cstart[:, None]) & (ck[None, :] < cstart[:, None] + NA_WIN_COLS)
    col_off = np.where(col_ok, ck[None, :] - cq[:, None] + NA_WIN_COLS - 1, 0)
    row_ok = np.zeros((3, R, 3 * R), bool)
    row_off = np.zeros((3, R, 3 * R), np.int64)
    for vi, j in enumerate((0, 1, nblk - 1)):
        for i in range(R):
            r = j * R + i
            r0 = min(max(r - kr // 2, 0), rows - kr)
            for t in range(3):
                jb = j - 1 + t
                if jb < 0 or jb >= nblk:
                    continue
                for rr in range(R):
                    krow = jb * R + rr
                    if r0 <= krow < r0 + kr:
                        row_ok[vi, i, t * R + rr] = True
                        row_off[vi, i, t * R + rr] = krow - r + NA_WIN_ROWS - 1
    n_ro = rpb.shape[1]
    col_sel = (np.arange(rpb.shape[2])[None, None, :] == col_off[:, :, None]) & col_ok[:, :, None]
    planes = jnp.einsum('hrc,qkc->hrqk', rpb, jnp.asarray(col_sel, F32), precision=HIGHEST)
    planes = jnp.where(jnp.asarray(col_ok), planes, NEG)
    planes = jnp.concatenate([planes, jnp.full((NA_HEADS, 1, GRID_W, GRID_W), NEG, F32)], axis=1)
    plane_of = np.where(row_ok, row_off, n_ro).astype(np.int32).reshape(-1)

    def assemble(sel_ref, planes_ref, o_ref):
        v = pl.program_id(0)
        for i in range(R):
            for x in range(3 * R):
                o_ref[0, 0, i * GRID_W:(i + 1) * GRID_W, x * GRID_W:(x + 1) * GRID_W] = (
                    planes_ref[0, sel_ref[(v * R + i) * 3 * R + x]])

    return pl.pallas_call(
        assemble,
        grid_spec=pltpu.PrefetchScalarGridSpec(
            num_scalar_prefetch=1,
            grid=(3, NA_HEADS),
            in_specs=[pl.BlockSpec((1, n_ro + 1, GRID_W, GRID_W), lambda v, h, sel: (h, 0, 0, 0))],
            out_specs=pl.BlockSpec((1, 1, R * GRID_W, 3 * R * GRID_W), lambda v, h, sel: (v, h, 0, 0))),
        out_shape=jax.ShapeDtypeStruct((3, NA_HEADS, R * GRID_W, 3 * R * GRID_W), F32),
        compiler_params=_cparams("arbitrary", "arbitrary"),
        name="na_bias",
    )(jnp.asarray(plane_of), planes)


def _na(zna, zcna, bias, rows):
    b, n, _ = zna.shape
    ctx = zcna.shape[1]
    R = NA_ROWS_PER_STEP
    nq = R * GRID_W
    nblk = rows // R
    kb = lambda col, off: pl.BlockSpec(
        (1, nq, NA_WIDTH), lambda i, j: (i, jnp.clip(j + off, 0, nblk - 1), col))
    variant = lambda i, j: (jnp.where(j == 0, 0, jnp.where(j == nblk - 1, 2, 1)), 0, 0, 0)
    return pl.pallas_call(
        _na_kernel,
        grid=(b, nblk),
        in_specs=[pl.BlockSpec((1, nq, NA_WIDTH), lambda i, j: (i, j, 0)),
                  kb(1, -1), kb(1, 0), kb(1, 1), kb(2, -1), kb(2, 0), kb(2, 1),
                  pl.BlockSpec((1, ctx, NA_WIDTH), lambda i, j: (i, 0, 1)),
                  pl.BlockSpec((1, ctx, NA_WIDTH), lambda i, j: (i, 0, 2)),
                  pl.BlockSpec((1, NA_HEADS, nq, 3 * nq), variant)],
        out_specs=pl.BlockSpec((1, nq, NA_WIDTH), lambda i, j: (i, j, 0)),
        out_shape=jax.ShapeDtypeStruct((b, n, NA_WIDTH), BF16),
        compiler_params=_cparams("parallel", "parallel"),
        name="na",
    )(zna, zna, zna, zna, zna, zna, zna, zcna, zcna, bias)


def _mix_kernel(na_ref, hf_ref, hb_ref, o_ref, x_ref, g1_ref, sh2_ref, sc2_ref, wo_ref, mg_ref, l1g_ref, l1b_ref,
                wr_ref, br_ref, xmid_ref, rt_ref, rw_ref, cnt_ref, carry_s, *, alpha):
    first = (pl.program_id(0) == 0) & (pl.program_id(1) == 0)

    @pl.when(first)
    def _():
        carry_s[...] = jnp.zeros_like(carry_s)

    tm = x_ref.shape[1]
    h = hf_ref[0].astype(F32) + hb_ref[0].astype(F32)
    parts = []
    for hd in range(ML_HEADS):
        hh = h[:, hd * LANES:(hd + 1) * LANES]
        parts.append(hh * lax.rsqrt(jnp.mean(hh * hh, axis=-1, keepdims=True) + LN_EPS))
    hn = jnp.concatenate(parts, axis=1)
    ml = (hn * mg_ref[...] * jax.nn.sigmoid(o_ref[0].astype(F32))).astype(BF16)
    mix = (jnp.dot(na_ref[0], wo_ref[0:NA_WIDTH, :], preferred_element_type=F32)
           + jnp.dot(ml, wo_ref[NA_WIDTH:, :], preferred_element_type=F32))
    xmid = _ln_rows(alpha * x_ref[0] + g1_ref[0] * mix) * l1g_ref[...] + l1b_ref[...]
    xmid_ref[0] = xmid

    xt = (_ln_rows(xmid) * (1.0 + sc2_ref[0]) + sh2_ref[0]).astype(BF16)
    logits = jnp.dot(xt, wr_ref[...], preferred_element_type=F32) + br_ref[...]
    lane = lax.broadcasted_iota(jnp.int32, (tm, LANES), 1)
    is_g = lane < N_GROUPS
    gl = jnp.where(is_g, logits, NEG)
    gmax = jnp.max(gl, axis=1, keepdims=True)
    grp = jnp.min(jnp.where(gl == gmax, lane, LANES), axis=1, keepdims=True)
    gsum = jnp.sum(jnp.where(is_g, jnp.exp(gl - gmax), 0.0), axis=1, keepdims=True)
    grp_w = 1.0 / gsum
    lo = N_GROUPS + EXPERTS_PER_GROUP * grp
    el = jnp.where((lane >= lo) & (lane < lo + EXPERTS_PER_GROUP), logits, NEG)
    t1 = jnp.max(el, axis=1, keepdims=True)
    i1 = jnp.min(jnp.where(el == t1, lane, LANES), axis=1, keepdims=True)
    el2 = jnp.where(lane == i1, NEG, el)
    t2 = jnp.max(el2, axis=1, keepdims=True)
    i2 = jnp.min(jnp.where(el2 == t2, lane, LANES), axis=1, keepdims=True)
    e21 = jnp.exp(t2 - t1)
    w0 = grp_w / (1.0 + e21)
    w1 = grp_w * e21 / (1.0 + e21)

    hit1 = lane == i1
    hit2 = lane == i2
    onehot = (hit1 | hit2).astype(BF16)
    r_i = lax.broadcasted_iota(jnp.int32, (tm, tm), 0)
    c_i = lax.broadcasted_iota(jnp.int32, (tm, tm), 1)
    before = (r_i > c_i).astype(BF16)
    prefix = jnp.dot(before, onehot, preferred_element_type=F32) + carry_s[0:1, :]
    rank0 = jnp.sum(jnp.where(hit1, prefix, 0.0), axis=1, keepdims=True)
    rank1 = jnp.sum(jnp.where(hit2, prefix, 0.0), axis=1, keepdims=True)
    total = carry_s[0:1, :] + jnp.sum(onehot.astype(F32), axis=0, keepdims=True)
    carry_s[...] = jnp.broadcast_to(total, carry_s.shape)
    cnt_ref[...] = jnp.broadcast_to(total, cnt_ref.shape)

    rf = jnp.where(lane == 0, (i1 - N_GROUPS).astype(F32),
                   jnp.where(lane == 1, (i2 - N_GROUPS).astype(F32),
                             jnp.where(lane == 2, rank0, jnp.where(lane == 3, rank1, 0.0))))
    rt_ref[0] = rf.T[0:SUBLANES, :].astype(jnp.int32)
    rw_ref[0] = jnp.where(lane == 0, w0, jnp.where(lane == 1, w1, 0.0))


def _mix(na, hf, hb, zvo, x, g1, sh2, sc2, wo, mg, l1g, l1b, wr, br, alpha, tm):
    b, n, d = x.shape
    row = lambda w: pl.BlockSpec((1, tm, w), lambda i, j: (i, j, 0))
    vec = pl.BlockSpec((1, 1, d), lambda i, j: (i, 0, 0))
    full = lambda a: pl.BlockSpec(a.shape, lambda i, j: (0,) * a.ndim)
    ocol = 1
    return pl.pallas_call(
        functools.partial(_mix_kernel, alpha=alpha),
        grid=(b, n // tm),
        in_specs=[row(NA_WIDTH), row(ML_WIDTH), row(ML_WIDTH),
                  pl.BlockSpec((1, tm, ML_WIDTH), lambda i, j: (i, j, ocol)),
                  row(d), vec, vec, vec, full(wo), full(mg), full(l1g), full(l1b), full(wr), full(br)],
        out_specs=[row(d), pl.BlockSpec((1, SUBLANES, tm), lambda i, j: (i * (n // tm) + j, 0, 0)), row(LANES),
                   pl.BlockSpec((SUBLANES, LANES), lambda i, j: (0, 0))],
        out_shape=[jax.ShapeDtypeStruct((b, n, d), F32),
                   jax.ShapeDtypeStruct((b * (n // tm), SUBLANES, tm), jnp.int32),
                   jax.ShapeDtypeStruct((b, n, LANES), F32),
                   jax.ShapeDtypeStruct((SUBLANES, LANES), F32)],
        scratch_shapes=[pltpu.VMEM((SUBLANES, LANES), F32)],
        compiler_params=_cparams("arbitrary", "arbitrary"),
        name="mix",
    )(na, hf, hb, zvo, x, g1, sh2, sc2, wo, mg, l1g, l1b, wr, br)


def _zero_fill_padding(pad_base_ref, pad_len_ref, nused_ref, xs_ref, zero_s, sem, wait):
    zero_t, xs_t = _token_view(zero_s), _token_view(xs_ref)
    tb = zero_t.shape[0]

    def run(copy):
        copy.wait() if wait else copy.start()

    def fill(off, nrows):
        run(pltpu.make_async_copy(zero_t.at[pl.ds(0, nrows)], xs_t.at[pl.ds(off, nrows)], sem))

    def per_expert(e, _):
        plen = pad_len_ref[e]
        base = pad_base_ref[e]
        bit = tb // 2
        while bit >= 1:
            off = base + (plen & ~(2 * bit - 1))

            @pl.when((plen & bit) != 0)
            def _(bit=bit, off=off):
                fill(off, bit)

            bit //= 2
        return 0

    lax.fori_loop(0, N_EXPERTS, per_expert, 0)

    def per_block(i, _):
        fill(i * tb, tb)
        return 0

    lax.fori_loop(nused_ref[0], xs_t.shape[0] // tb, per_block, 0)


ROW_WORDS = 4
BF16_BITS = 16
HIGH_HALF = -(1 << BF16_BITS)


def _token_view(ref):
    return ref.reshape(ref.shape[0] // ROW_WORDS, ROW_WORDS, LANES)


def _bf16_bits(v):
    return lax.bitcast_convert_type(v.astype(BF16).astype(F32), jnp.int32)


def _store_token_rows(dst_ref, base, val):
    tm, d = val.shape
    assert d == 2 * ROW_WORDS * LANES
    for s in range(ROW_WORDS):
        lo = _bf16_bits(val[:, s * LANES:(s + 1) * LANES])
        hi = _bf16_bits(val[:, (s + ROW_WORDS) * LANES:(s + ROW_WORDS + 1) * LANES])
        dst_ref[pl.ds(base * ROW_WORDS + s, tm, stride=ROW_WORDS), :] = hi | lax.shift_right_logical(lo, BF16_BITS)


def _load_token_rows(src_ref, base, tm, dtype):
    words = [src_ref[pl.ds(base * ROW_WORDS + s, tm, stride=ROW_WORDS), :] for s in range(ROW_WORDS)]
    lo = [lax.bitcast_convert_type(w << BF16_BITS, F32).astype(dtype) for w in words]
    hi = [lax.bitcast_convert_type(w & HIGH_HALF, F32).astype(dtype) for w in words]
    return jnp.concatenate(lo + hi, axis=1)


def _dispatch_kernel(pad_base_ref, pad_len_ref, nused_ref, dest_ref, xmid_ref, sh2_ref, sc2_ref, xs_ref,
                     xt_s, zero_s, sem, zsem):
    tm = xmid_ref.shape[1]
    step = pl.program_id(0) * pl.num_programs(1) + pl.program_id(1)
    nsteps = pl.num_programs(0) * pl.num_programs(1)

    slot = step % 2

    def wait_slot_copies(s):
        for _ in range(2):
            pltpu.make_async_copy(_token_view(xt_s).at[pl.ds(s * tm, tm)], _token_view(xs_ref).at[pl.ds(0, tm)],
                                  sem.at[s]).wait()

    @pl.when(step == 0)
    def _():
        zero_s[...] = jnp.zeros_like(zero_s)
        _zero_fill_padding(pad_base_ref, pad_len_ref, nused_ref, xs_ref, zero_s, zsem, False)

    for c in range(tm // COMBINE_CHUNK):
        rows = slice(c * COMBINE_CHUNK, (c + 1) * COMBINE_CHUNK)
        xt = _ln_rows(xmid_ref[0, rows, :]) * (1.0 + sc2_ref[0]) + sh2_ref[0]
        _store_token_rows(xt_s, slot * tm + c * COMBINE_CHUNK, xt)
        for r in range(c * COMBINE_CHUNK, (c + 1) * COMBINE_CHUNK):
            for k in range(2):
                pltpu.make_async_copy(_token_view(xt_s).at[slot * tm + r], _token_view(xs_ref).at[dest_ref[0, k, r]],
                                      sem.at[slot]).start(priority=k)

    @pl.when(step == 0)
    def _():
        _zero_fill_padding(pad_base_ref, pad_len_ref, nused_ref, xs_ref, zero_s, zsem, True)

    @pl.when(step > 0)
    def _():
        wait_slot_copies(1 - slot)

    @pl.when(step == nsteps - 1)
    def _():
        wait_slot_copies(slot)


def _dispatch(xmid, sh2, sc2, dest, pad_base, pad_len, nused, cap, tm):
    b, n, d = xmid.shape
    assert d == 2 * ROW_WORDS * LANES
    nt = n // tm
    vec = pl.BlockSpec((1, 1, d), lambda i, j, *_: (i, 0, 0))
    return pl.pallas_call(
        _dispatch_kernel,
        grid_spec=pltpu.PrefetchScalarGridSpec(
            num_scalar_prefetch=3,
            grid=(b, nt),
            in_specs=[pl.BlockSpec((1, 2, tm), lambda i, j, *_: (i * nt + j, 0, 0), memory_space=pltpu.SMEM),
                      pl.BlockSpec((1, tm, d), lambda i, j, *_: (i, j, 0)), vec, vec],
            out_specs=pl.BlockSpec(memory_space=pl.ANY),
            scratch_shapes=[pltpu.VMEM((2 * tm * ROW_WORDS, LANES), ROW_DTYPE),
                            pltpu.VMEM((EXPERT_ROWS * ROW_WORDS, LANES), ROW_DTYPE),
                            pltpu.SemaphoreType.DMA((2,)), pltpu.SemaphoreType.DMA]),
        out_shape=jax.ShapeDtypeStruct((cap * ROW_WORDS, LANES), ROW_DTYPE),
        compiler_params=_cparams("arbitrary", "arbitrary"),
        name="dispatch",
    )(pad_base, pad_len, nused, dest, xmid, sh2, sc2)


def _expert_kernel(be_ref, nv_ref, xs_ref, w1_ref, w3_ref, w2_ref, ys_ref, w1b, w3b, w2b):
    tb = xs_ref.shape[0] // ROW_WORDS
    i = pl.program_id(0)
    e = be_ref[i]
    changed = (i == 0) | (be_ref[jnp.maximum(i - 1, 0)] != e)

    @pl.when(changed)
    def _():
        w1b[...] = w1_ref[0].astype(BF16)
        w3b[...] = w3_ref[0].astype(BF16)
        w2b[...] = w2_ref[0].astype(BF16)

    groups_used = (nv_ref[i] + EXPERT_ROW_GROUP - 1) // EXPERT_ROW_GROUP
    for g in range(tb // EXPERT_ROW_GROUP + 1):
        @pl.when(groups_used == g)
        def _(m=g * EXPERT_ROW_GROUP):
            if m > 0:
                xb = _load_token_rows(xs_ref, 0, m, BF16)
                h1 = jnp.dot(xb, w1b[...], preferred_element_type=F32)
                h3 = jnp.dot(xb, w3b[...], preferred_element_type=F32)
                a = (_silu(h1) * h3).astype(BF16)
                _store_token_rows(ys_ref, 0, jnp.dot(a, w2b[...], preferred_element_type=F32))
            if m < tb:
                ys_ref[m * ROW_WORDS:, :] = jnp.zeros(((tb - m) * ROW_WORDS, LANES), ys_ref.dtype)


def _experts(xs, block_e, block_nv, w1, w3, w2):
    d, hid = w1.shape[1], w1.shape[2]
    tb = EXPERT_ROWS
    rows = pl.BlockSpec((tb * ROW_WORDS, LANES), lambda i, be, nv: (i, 0))
    return pl.pallas_call(
        _expert_kernel,
        grid_spec=pltpu.PrefetchScalarGridSpec(
            num_scalar_prefetch=2,
            grid=(xs.shape[0] // (tb * ROW_WORDS),),
            in_specs=[rows,
                      pl.BlockSpec((1, d, hid), lambda i, be, nv: (be[i], 0, 0)),
                      pl.BlockSpec((1, d, hid), lambda i, be, nv: (be[i], 0, 0)),
                      pl.BlockSpec((1, hid, d), lambda i, be, nv: (be[i], 0, 0))],
            out_specs=rows,
            scratch_shapes=[pltpu.VMEM((d, hid), BF16), pltpu.VMEM((d, hid), BF16), pltpu.VMEM((hid, d), BF16)]),
        out_shape=jax.ShapeDtypeStruct(xs.shape, xs.dtype),
        compiler_params=_cparams("arbitrary"),
        name="experts",
    )(block_e, block_nv, xs, w1, w3, w2)


def _combine_kernel(dcur_ref, dnext_ref, xmid_ref, rw_ref, g2_ref, l2g_ref, l2b_ref, ys_ref, o_ref,
                    y0_s, y1_s, sem, *, alpha):
    tm = xmid_ref.shape[1]
    step = pl.program_id(0) * pl.num_programs(1) + pl.program_id(1)
    nsteps = pl.num_programs(0) * pl.num_programs(1)
    slot = step % 2
    other = 1 - slot

    def start_row(dest_ref, into, r):
        for k, buf in ((0, y0_s), (1, y1_s)):
            pltpu.make_async_copy(_token_view(ys_ref).at[dest_ref[0, k, r]],
                                  _token_view(buf).at[into * tm + r], sem.at[into]).start(priority=k)

    def wait_slot(into):
        for buf in (y0_s, y1_s):
            pltpu.make_async_copy(_token_view(ys_ref).at[pl.ds(0, tm)], _token_view(buf).at[pl.ds(into * tm, tm)],
                                  sem.at[into]).wait()

    @pl.when(step == 0)
    def _():
        lax.fori_loop(0, tm, lambda r, c: (start_row(dcur_ref, 0, r), c)[1], 0, unroll=8)

    wait_slot(slot)
    for c in range(tm // COMBINE_CHUNK):
        for r in range(c * COMBINE_CHUNK, (c + 1) * COMBINE_CHUNK):
            start_row(dnext_ref, other, r)
        rows = slice(c * COMBINE_CHUNK, (c + 1) * COMBINE_CHUNK)
        base = slot * tm + c * COMBINE_CHUNK
        rw = rw_ref[0, rows, :]
        moe = (rw[:, 0:1] * _load_token_rows(y0_s, base, COMBINE_CHUNK, F32)
               + rw[:, 1:2] * _load_token_rows(y1_s, base, COMBINE_CHUNK, F32))
        o_ref[0, rows, :] = (_ln_rows(alpha * xmid_ref[0, rows, :] + g2_ref[0] * moe) * l2g_ref[...] + l2b_ref[...])

    @pl.when(step == nsteps - 1)
    def _():
        wait_slot(other)


def _combine(xmid, rw, g2, l2g, l2b, ys, dest, alpha, tm):
    b, n, d = xmid.shape
    nt = n // tm
    full = lambda a: pl.BlockSpec(a.shape, lambda i, j: (0,) * a.ndim)
    return pl.pallas_call(
        functools.partial(_combine_kernel, alpha=alpha),
        grid=(b, nt),
        in_specs=[pl.BlockSpec((1, 2, tm), lambda i, j: (i * nt + j, 0, 0), memory_space=pltpu.SMEM),
                  pl.BlockSpec((1, 2, tm), lambda i, j: (jnp.minimum(i * nt + j + 1, b * nt - 1), 0, 0),
                               memory_space=pltpu.SMEM),
                  pl.BlockSpec((1, tm, d), lambda i, j: (i, j, 0)),
                  pl.BlockSpec((1, tm, LANES), lambda i, j: (i, j, 0)),
                  pl.BlockSpec((1, 1, d), lambda i, j: (i, 0, 0)),
                  full(l2g), full(l2b),
                  pl.BlockSpec(memory_space=pl.ANY)],
        out_specs=pl.BlockSpec((1, tm, d), lambda i, j: (i, j, 0)),
        out_shape=jax.ShapeDtypeStruct((b, n, d), F32),
        scratch_shapes=[pltpu.VMEM((2 * tm * ROW_WORDS, LANES), ys.dtype),
                        pltpu.VMEM((2 * tm * ROW_WORDS, LANES), ys.dtype), pltpu.SemaphoreType.DMA((2,))],
        compiler_params=_cparams("arbitrary", "arbitrary"),
        name="combine",
    )(dest, dest, xmid, rw, g2, l2g, l2b, ys)


def _tile(n, want):
    t = min(n, want)
    assert n % t == 0, (n, t)
    return t


def kernel(x, c, ctx, c_ctx, w_ada, b_ada, w_in, conv_w, conv_b, gate_b, rpb, ml_norm_g, w_out, ln1_g, ln1_b,
           w_router_g, b_router_g, w_router_e, b_router_e, w1, w3, w2, ln2_g, ln2_b):
    B, N, D = x.shape
    T_CTX = ctx.shape[1]
    depth = w_ada.shape[0]
    rows = N // GRID_W
    assert depth == 1 and N % GRID_W == 0 and rows % NA_ROWS_PER_STEP == 0 and rows >= 3 * NA_ROWS_PER_STEP
    assert N % ML_CHUNK == 0 and T_CTX % ML_CHUNK == 0
    alpha = (2.0 * depth) ** 0.25
    l = 0

    pad_rows = -(B + 1) % SUBLANES
    cvec = jnp.concatenate([c, c_ctx[None], jnp.zeros((pad_rows, D), F32)], axis=0)
    ada = _ada(cvec, w_ada[l], b_ada[l])
    sh1, sc1, g1, sh2, sc2, g2 = [a[:, None, :] for a in jnp.split(ada[:B], 6, axis=-1)]
    csh1, csc1 = [jnp.broadcast_to(a[None], (B, 1, D)) for a in jnp.split(ada[B:B + 1], 6, axis=-1)[:2]]

    col_ml = 3 * NA_WIDTH
    col_g = col_ml + 4 * ML_WIDTH
    col_v = col_ml + 2 * ML_WIDTH
    wb = w_in[l].astype(BF16)
    wna, wqk, wvo = wb[:, :col_ml], wb[:, col_ml:col_v], wb[:, col_v:col_g]
    n_gate = 4 * ML_HEADS
    wg = jnp.pad(wb[:, col_g:], ((0, 0), (0, LANES - n_gate)))
    gb = jnp.pad(gate_b[l], (0, LANES - n_gate)).reshape(1, LANES)
    cos_t, sin_t = _rope_tables(N)
    zna, zvo, qk_l, gcol_l, grow_l = _inproj(x, sh1, sc1, wna, wqk, wvo, wg, gb, conv_w[l], conv_b[l], cos_t, sin_t,
                                             True, _tile(N, 512))
    zcna, zcvo, qk_c, gcol_c, grow_c = _inproj(ctx, csh1, csc1, wna, wqk, wvo, wg, gb, conv_w[l], conv_b[l],
                                               cos_t[:T_CTX], sin_t[:T_CTX], False, _tile(T_CTX, 256))

    c0 = jnp.zeros((B, 2, ML_HEADS, ML_HEAD_DIM, 2 * ML_HEAD_DIM), F32)
    m0 = jnp.zeros((B, 2, ML_HEADS, SUBLANES, LANES), F32)
    c_ctx_end, m_ctx_end = _mlstm(qk_c, zcvo, gcol_c, grow_c, c0, m0, False)
    hf, hb, _, _ = _mlstm(qk_l, zvo, gcol_l, grow_l, c_ctx_end, m_ctx_end, True)

    na = _na(zna, zcna, _na_bias_tables(rpb[l], rows), rows)

    wr = jnp.pad(jnp.concatenate([w_router_g[l], w_router_e[l]], axis=1),
                 ((0, 0), (0, LANES - N_GROUPS - N_EXPERTS))).astype(BF16)
    br = jnp.pad(jnp.concatenate([b_router_g[l], b_router_e[l]]), (0, LANES - N_GROUPS - N_EXPERTS)).reshape(1, LANES)
    tm = _tile(N, 512)
    xmid, rt, rw, counts = _mix(na, hf, hb, zvo, x, g1, sh2, sc2, w_out[l].astype(BF16),
                                ml_norm_g[l].reshape(1, ML_WIDTH), ln1_g[l].reshape(1, D), ln1_b[l].reshape(1, D),
                                wr, br, alpha, tm)

    tb = EXPERT_ROWS
    n_assign = 2 * B * N
    cap = -(-n_assign // tb) * tb + N_EXPERTS * tb
    sizes = counts[0, N_GROUPS:N_GROUPS + N_EXPERTS].astype(jnp.int32)
    padded = (sizes + tb - 1) // tb * tb
    pend = jnp.cumsum(padded)
    pstart = pend - padded
    experts = jnp.arange(N_EXPERTS, dtype=jnp.int32)
    first_row = jnp.sum(jnp.where(rt[:, 0:2, :, None] == experts, pstart, 0), axis=-1)
    dest = first_row + rt[:, 2:4, :]
    blk0 = jnp.arange(cap // tb, dtype=jnp.int32) * tb
    block_e = jnp.minimum(jnp.sum(pend[None, :] <= blk0[:, None], axis=1), N_EXPERTS - 1).astype(jnp.int32)
    is_e = block_e[:, None] == experts
    block_nv = jnp.clip(jnp.sum(jnp.where(is_e, pstart + sizes, 0), axis=1) - blk0, 0, tb).astype(jnp.int32)

    nused = (pend[-1:] // tb).astype(jnp.int32)
    xs = _dispatch(xmid, sh2, sc2, dest, pstart + sizes, padded - sizes, nused, cap, tm)
    ys = _experts(xs, block_e, block_nv, w1[l], w3[l], w2[l])
    return _combine(xmid, rw, g2, ln2_g[l].reshape(1, D), ln2_b[l].reshape(1, D), ys, dest, alpha, tm)
```

```python
import functools

import numpy as np
import jax
import jax.numpy as jnp
from jax import lax
from jax.experimental import pallas as pl
from jax.experimental.pallas import tpu as pltpu

F32 = jnp.float32
BF16 = jnp.bfloat16
ROW_DTYPE = jnp.int32
HIGHEST = lax.Precision.HIGHEST

GRID_W = 64
NA_HEADS = 8
NA_HEAD_DIM = 64
NA_WIDTH = NA_HEADS * NA_HEAD_DIM
NA_WIN_ROWS = 8
NA_WIN_COLS = 16
ML_HEADS = 4
ML_HEAD_DIM = 128
ML_WIDTH = ML_HEADS * ML_HEAD_DIM
ML_CHUNK = 128
CONV_K = 5
N_GROUPS = 8
EXPERTS_PER_GROUP = 8
N_EXPERTS = N_GROUPS * EXPERTS_PER_GROUP
ROPE_BASE = 10000.0
LN_EPS = 1e-5

LANES = 128
SUBLANES = 8
VMEM_LIMIT = 56 * 1024 * 1024

NA_ROWS_PER_STEP = 4
EXPERT_ROWS = 512
COMBINE_CHUNK = 64
MLSTM_CHUNKS_PER_STEP = 4
NEG = -1e30


def _cparams(*sem):
    return pltpu.CompilerParams(dimension_semantics=sem, vmem_limit_bytes=VMEM_LIMIT)


def _silu(v):
    return v * jax.nn.sigmoid(v)


def _ln_rows(v):
    mu = jnp.mean(v, axis=-1, keepdims=True)
    vc = v - mu
    var = jnp.mean(vc * vc, axis=-1, keepdims=True)
    return vc * lax.rsqrt(var + LN_EPS)


def _ada_kernel(c_ref, w_ref, b_ref, o_ref):
    o_ref[...] = jnp.dot(_silu(c_ref[...]), w_ref[...], preferred_element_type=F32,
                         precision=HIGHEST) + b_ref[...]


def _ada(cvec, w, b):
    rows, d = cvec.shape
    cols = w.shape[1]
    tn = d
    assert cols % tn == 0
    return pl.pallas_call(
        _ada_kernel,
        grid=(cols // tn,),
        in_specs=[pl.BlockSpec((rows, d), lambda j: (0, 0)),
                  pl.BlockSpec((d, tn), lambda j: (0, j)),
                  pl.BlockSpec((1, tn), lambda j: (0, j))],
        out_specs=pl.BlockSpec((rows, tn), lambda j: (0, j)),
        out_shape=jax.ShapeDtypeStruct((rows, cols), F32),
        compiler_params=_cparams("arbitrary"),
        name="ada",
    )(cvec, w, b.reshape(1, cols))


HALO = 16


def _log_sigmoid(v):
    return jnp.minimum(v, 0.0) - jnp.log1p(jnp.exp(-jnp.abs(v)))


GATE_RAW, GATE_PREFIX, GATE_SUFFIX = 0, 16, 32


def _pack_gates(g):
    L = g.shape[0]
    row = lax.broadcasted_iota(jnp.int32, (L, L), 0)
    col = lax.broadcasted_iota(jnp.int32, (L, L), 1)
    lane = lax.broadcasted_iota(jnp.int32, g.shape, 1)
    lf = _log_sigmoid(g)
    prefix = jnp.dot((row >= col).astype(F32), lf, preferred_element_type=F32, precision=HIGHEST)
    suffix = jnp.dot((row <= col).astype(F32), lf, preferred_element_type=F32, precision=HIGHEST)
    return jnp.where(lane < GATE_PREFIX, g,
                     jnp.where(lane < GATE_SUFFIX, pltpu.roll(prefix, GATE_PREFIX, 1),
                               jnp.where(lane < GATE_SUFFIX + GATE_PREFIX, pltpu.roll(suffix, GATE_SUFFIX, 1), 0.0)))


def _inproj_kernel(x_ref, xp_ref, xn_ref, sh_ref, sc_ref, wna_ref, wqk_ref, wvo_ref, wg_ref, gb_ref, cw_ref, cb_ref,
                   cos_ref, sin_ref, zna_ref, zvo_ref, o_ref, gcol_ref, grow_ref, *pad_refs, rope, tr):
    i = pl.program_id(1)
    last = pl.num_programs(1) - 1
    xa = jnp.concatenate([xp_ref[0], x_ref[0], xn_ref[0]], axis=0)
    ya = (_ln_rows(xa) * (1.0 + sc_ref[0]) + sh_ref[0]).astype(BF16)
    yb = ya[HALO:HALO + tr]
    pad = CONV_K // 2
    width = wqk_ref.shape[1]
    lane = lax.broadcasted_iota(jnp.int32, (tr, LANES), 1)
    first_half = (lane % (ML_HEAD_DIM // 2)) < (ML_HEAD_DIM // 4)
    kscale = ML_HEAD_DIM ** -0.5
    cb = 2 * LANES

    def project_qk(c):
        z = jnp.dot(ya, wqk_ref[:, c * cb:(c + 1) * cb], preferred_element_type=F32)
        pad_refs[c][HALO:HALO + tr, :] = z[HALO:HALO + tr]
        pad_refs[c][0:HALO, :] = jnp.where(i > 0, z[0:HALO], 0.0)
        pad_refs[c][HALO + tr:2 * HALO + tr, :] = jnp.where(i < last, z[HALO + tr:], 0.0)

    def conv_group(g):
        cols = slice(g * LANES, (g + 1) * LANES)
        pad_ref = pad_refs[g * LANES // cb]
        pc = slice(g * LANES % cb, g * LANES % cb + LANES)
        acc = cw_ref[0:1, cols] * pad_ref[HALO - pad:HALO - pad + tr, pc] + cb_ref[:, cols]
        for j in range(1, CONV_K):
            acc = acc + cw_ref[j:j + 1, cols] * pad_ref[HALO - pad + j:HALO - pad + j + tr, pc]
        ug = _silu(acc)
        if rope:
            partner = jnp.where(first_half, pltpu.roll(ug, LANES - ML_HEAD_DIM // 4, 1),
                                pltpu.roll(ug, ML_HEAD_DIM // 4, 1))
            ug = ug * cos_ref[...] + partner * sin_ref[...]
        if g >= ML_HEADS:
            ug = ug * kscale
        o_ref[0, :, cols] = ug.astype(BF16)

    other = ([(zna_ref, wna_ref, c) for c in range(wna_ref.shape[1] // cb)]
             + [(zvo_ref, wvo_ref, c) for c in range(wvo_ref.shape[1] // cb)])

    def project_other(n):
        for _ in range(n):
            if other:
                dst, w, c = other.pop(0)
                cols = slice(c * cb, (c + 1) * cb)
                dst[0, :, cols] = jnp.dot(yb, w[:, cols], preferred_element_type=F32).astype(BF16)

    n_qk = width // cb
    per_round = -(-len(other) // n_qk)
    project_qk(0)
    for c in range(n_qk):
        if c + 1 < n_qk:
            project_qk(c + 1)
        project_other(per_round)
        for g in range(c * cb // LANES, (c + 1) * cb // LANES):
            conv_group(g)
    project_other(len(other))
    gates = jnp.dot(yb, wg_ref[...], preferred_element_type=F32) + gb_ref[...]
    for c in range(tr // ML_CHUNK):
        packed = _pack_gates(gates[c * ML_CHUNK:(c + 1) * ML_CHUNK, :])
        gcol_ref[0, c * ML_CHUNK:(c + 1) * ML_CHUNK, :] = packed
        grow_ref[0, c] = packed.T


def _inproj(x, shift, scale, wna, wqk, wvo, wg, gb, conv_w, conv_b, cos_t, sin_t, rope, tr):
    b, t, d = x.shape
    width = 2 * ML_WIDTH
    nh = tr // HALO
    nblk = t // HALO
    L = ML_CHUNK
    full = lambda a: pl.BlockSpec(a.shape, lambda i, j: (0,) * a.ndim)
    row = lambda w: pl.BlockSpec((1, tr, w), lambda i, j: (i, j, 0))
    vec = pl.BlockSpec((1, 1, d), lambda i, j: (i, 0, 0))
    return pl.pallas_call(
        functools.partial(_inproj_kernel, rope=rope, tr=tr),
        grid=(b, t // tr),
        in_specs=[row(d),
                  pl.BlockSpec((1, HALO, d), lambda i, j: (i, jnp.maximum(j * nh - 1, 0), 0)),
                  pl.BlockSpec((1, HALO, d), lambda i, j: (i, jnp.minimum((j + 1) * nh, nblk - 1), 0)),
                  vec, vec, full(wna), full(wqk), full(wvo), full(wg), full(gb),
                  pl.BlockSpec((CONV_K, width), lambda i, j: (0, 0)),
                  pl.BlockSpec((1, width), lambda i, j: (0, 0)),
                  pl.BlockSpec((tr, LANES), lambda i, j: (j, 0)),
                  pl.BlockSpec((tr, LANES), lambda i, j: (j, 0))],
        out_specs=[row(wna.shape[1]), row(wvo.shape[1]), row(width), row(LANES),
                   pl.BlockSpec((1, tr // L, LANES, L), lambda i, j: (i, j, 0, 0))],
        out_shape=[jax.ShapeDtypeStruct((b, t, wna.shape[1]), BF16),
                   jax.ShapeDtypeStruct((b, t, wvo.shape[1]), BF16),
                   jax.ShapeDtypeStruct((b, t, width), BF16),
                   jax.ShapeDtypeStruct((b, t, LANES), F32),
                   jax.ShapeDtypeStruct((b, t // L, LANES, L), F32)],
        scratch_shapes=[pltpu.VMEM((tr + 2 * HALO, 2 * LANES), F32)] * (width // (2 * LANES)),
        compiler_params=_cparams("parallel", "parallel"),
        name="inproj" if rope else "inproj_ctx",
    )(x, x, x, shift, scale, wna, wqk, wvo, wg, gb, conv_w, conv_b.reshape(1, width), cos_t, sin_t)


def _rope_tables(n):
    pos = np.arange(n)
    half = ML_HEAD_DIM // 4
    inv = np.float32(ROPE_BASE) ** (-np.arange(half, dtype=np.float32) / np.float32(half))

    def axis_tables(p):
        ang = p.astype(np.float32)[:, None] * inv[None, :]
        c, s = np.cos(ang), np.sin(ang)
        return np.concatenate([c, c], axis=-1), np.concatenate([-s, s], axis=-1)

    cr, sr = axis_tables(pos // GRID_W)
    cc, sc = axis_tables(pos % GRID_W)
    return (jnp.asarray(np.concatenate([cr, cc], axis=-1), F32), jnp.asarray(np.concatenate([sr, sc], axis=-1), F32))


def _mlstm_direction(d, qk, v, gcol, grow, cn_s, m_s):
    L = ML_CHUNK
    row = lax.broadcasted_iota(jnp.int32, (L, L), 0)
    col = lax.broadcasted_iota(jnp.int32, (L, L), 1)
    fwd = d == 0
    keep = (row >= col) if fwd else (row <= col)
    ones = jnp.ones((L, ML_HEAD_DIM), BF16)
    end = L - 1 if fwd else 0
    outs = []
    for h in range(ML_HEADS):
        ci = GATE_RAW + (2 * d) * ML_HEADS + h
        cf = (GATE_PREFIX if fwd else GATE_SUFFIX) + (2 * d + 1) * ML_HEADS + h
        bc = jnp.broadcast_to(gcol[:, cf:cf + 1], (L, L))
        ic = jnp.broadcast_to(gcol[:, ci:ci + 1], (L, L))
        br = grow[cf:cf + 1, :]
        ir = grow[ci:ci + 1, :]
        m_prev = m_s[d, h][0:1, :]
        dlog = jnp.where(keep, bc - br + ir, NEG)
        m_t = jnp.maximum(bc + m_prev, jnp.max(dlog, axis=1, keepdims=True))
        dw = jnp.exp((dlog - m_t).astype(BF16))
        inter = jnp.exp(bc + m_prev - m_t)
        qh = qk[:, h * LANES:(h + 1) * LANES]
        kh = qk[:, ML_WIDTH + h * LANES:ML_WIDTH + (h + 1) * LANES]
        vp = jnp.concatenate([v[:, h * LANES:(h + 1) * LANES], ones], axis=1)
        s = lax.dot_general(qh, kh, (((1,), (1,)), ((), ())), preferred_element_type=F32).astype(BF16) * dw
        cn = cn_s[d, h]
        a1 = jnp.dot(s, vp, preferred_element_type=F32)
        a2 = jnp.dot(qh, cn.astype(BF16), preferred_element_type=F32)
        num = a1[:, :ML_HEAD_DIM] + inter * a2[:, :ML_HEAD_DIM]
        den = a1[:, ML_HEAD_DIM:] + inter * a2[:, ML_HEAD_DIM:]
        outs.append(num / jnp.maximum(jnp.abs(den), jnp.exp(-m_t)))
        b_end = jnp.broadcast_to(br[:, end:end + 1], (1, L))
        g_row = b_end - br + ir
        m_new = jnp.maximum(b_end + m_prev, jnp.max(g_row, axis=1, keepdims=True))
        decay = jnp.exp(b_end + m_prev - m_new)
        wgt = jnp.exp((b_end - bc + ic - m_new).astype(BF16))
        kw = kh * wgt
        upd = lax.dot_general(kw, vp, (((0,), (0,)), ((), ())), preferred_element_type=F32)
        cn_s[d, h] = jnp.concatenate([decay, decay], axis=1) * cn + upd
        m_s[d, h] = jnp.broadcast_to(m_new, m_s.shape[2:])
    return jnp.concatenate(outs, axis=1)


def _mlstm_kernel(qkf_ref, vf_ref, gcf_ref, grf_ref, qkb_ref, vb_ref, gcb_ref, grb_ref, c0_ref, m0_ref, *rest, emit_h):
    if emit_h:
        hf_ref, hb_ref, ct_ref, mt_ref, cn_s, m_s = rest
    else:
        ct_ref, mt_ref, cn_s, m_s = rest
        hf_ref = hb_ref = None
    c = pl.program_id(1)
    L = ML_CHUNK

    @pl.when(c == 0)
    def _():
        cn_s[...] = c0_ref[0]
        m_s[...] = m0_ref[0]

    for d, refs, h_ref in ((0, (qkf_ref, vf_ref, gcf_ref, grf_ref), hf_ref), (1, (qkb_ref, vb_ref, gcb_ref, grb_ref), hb_ref)):
        qk_ref, v_ref, gc_ref, gr_ref = refs
        n_sub = qk_ref.shape[1] // L
        for s in (range(n_sub) if d == 0 else reversed(range(n_sub))):
            rows = slice(s * L, (s + 1) * L)
            h = _mlstm_direction(d, qk_ref[0, rows, :], v_ref[0, rows, :], gc_ref[0, rows, :], gr_ref[0, s], cn_s, m_s)
            if h_ref is not None:
                h_ref[0, rows, :] = h.astype(h_ref.dtype)

    @pl.when(c == pl.num_programs(1) - 1)
    def _():
        ct_ref[0] = cn_s[...]
        mt_ref[0] = m_s[...]


def _mlstm(qk, zvo, gcol, grow, c0, m0, emit_h):
    b, t, _ = qk.shape
    n_sub = min(MLSTM_CHUNKS_PER_STEP, t // ML_CHUNK)
    R = n_sub * ML_CHUNK
    assert t % R == 0
    nc = t // R
    vcol = 0
    f_idx = lambda i, c: (i, c, 0)
    b_idx = lambda i, c: (i, nc - 1 - c, 0)
    st_c = pl.BlockSpec((1, 2, ML_HEADS, ML_HEAD_DIM, 2 * ML_HEAD_DIM), lambda i, c: (i, 0, 0, 0, 0))
    st_m = pl.BlockSpec((1, 2, ML_HEADS, SUBLANES, LANES), lambda i, c: (i, 0, 0, 0, 0))
    out_specs = [st_c, st_m]
    out_shape = [jax.ShapeDtypeStruct(c0.shape, F32), jax.ShapeDtypeStruct(m0.shape, F32)]
    if emit_h:
        out_specs = [pl.BlockSpec((1, R, ML_WIDTH), f_idx), pl.BlockSpec((1, R, ML_WIDTH), b_idx)] + out_specs
        out_shape = [jax.ShapeDtypeStruct((b, t, ML_WIDTH), BF16)] * 2 + out_shape
    return pl.pallas_call(
        functools.partial(_mlstm_kernel, emit_h=emit_h),
        grid=(b, nc),
        in_specs=[pl.BlockSpec((1, R, 2 * ML_WIDTH), f_idx),
                  pl.BlockSpec((1, R, ML_WIDTH), lambda i, c: (i, c, vcol)),
                  pl.BlockSpec((1, R, LANES), f_idx),
                  pl.BlockSpec((1, n_sub, LANES, ML_CHUNK), lambda i, c: (i, c, 0, 0)),
                  pl.BlockSpec((1, R, 2 * ML_WIDTH), b_idx),
                  pl.BlockSpec((1, R, ML_WIDTH), lambda i, c: (i, nc - 1 - c, vcol)),
                  pl.BlockSpec((1, R, LANES), b_idx),
                  pl.BlockSpec((1, n_sub, LANES, ML_CHUNK), lambda i, c: (i, nc - 1 - c, 0, 0)),
                  st_c, st_m],
        out_specs=out_specs,
        out_shape=out_shape,
        scratch_shapes=[pltpu.VMEM((2, ML_HEADS, ML_HEAD_DIM, 2 * ML_HEAD_DIM), F32),
                        pltpu.VMEM((2, ML_HEADS, SUBLANES, LANES), F32)],
        compiler_params=_cparams("parallel", "arbitrary"),
        name="mlstm" if emit_h else "mlstm_ctx",
    )(qk, zvo, gcol, grow, qk, zvo, gcol, grow, c0, m0)


def _lane_in(shape, start, width):
    lane = lax.broadcasted_iota(jnp.int32, shape, 1)
    return (lane >= start) & (lane < start + width)


def _na_kernel(q_ref, k0_ref, k1_ref, k2_ref, v0_ref, v1_ref, v2_ref, kc_ref, vc_ref, bias_ref, o_ref):
    nq = q_ref.shape[1]
    lane = lax.broadcasted_iota(jnp.int32, (nq, LANES), 1)
    nt = (((1,), (1,)), ((), ()))
    for p in range(NA_HEADS // 2):
        sl = slice(p * LANES, (p + 1) * LANES)
        q2 = q_ref[0, :, sl] * (NA_HEAD_DIM ** -0.5)
        kwin = jnp.concatenate([k0_ref[0, :, sl], k1_ref[0, :, sl], k2_ref[0, :, sl]], axis=0)
        vwin = jnp.concatenate([v0_ref[0, :, sl], v1_ref[0, :, sl], v2_ref[0, :, sl]], axis=0)
        kc = kc_ref[0, :, sl]
        vwin = jnp.concatenate([vwin, jnp.ones_like(vwin)], axis=1)
        vc = jnp.concatenate([vc_ref[0, :, sl], jnp.ones_like(kc)], axis=1)
        halves = []
        for a in range(2):
            in_head = (lane >= a * NA_HEAD_DIM) & (lane < (a + 1) * NA_HEAD_DIM)
            qm = jnp.where(in_head, q2, jnp.zeros_like(q2))
            s_win = lax.dot_general(qm, kwin, nt, preferred_element_type=F32) + bias_ref[0, 2 * p + a]
            s_ctx = lax.dot_general(qm, kc, nt, preferred_element_type=F32)
            m = jnp.maximum(jnp.max(s_win, axis=1, keepdims=True), jnp.max(s_ctx, axis=1, keepdims=True))
            p_win = jnp.exp((s_win - m).astype(BF16))
            p_ctx = jnp.exp((s_ctx - m).astype(BF16))
            o = jnp.dot(p_win, vwin, preferred_element_type=F32) + jnp.dot(p_ctx, vc, preferred_element_type=F32)
            halves.append(o[:, :LANES] / o[:, LANES:])
        o_ref[0, :, sl] = jnp.where(lane < NA_HEAD_DIM, halves[0], halves[1]).astype(o_ref.dtype)


def _na_bias_tables(rpb, rows):
    R = NA_ROWS_PER_STEP
    nblk = rows // R
    kr = NA_WIN_ROWS
    cq = np.arange(GRID_W)
    cstart = np.clip(cq - NA_WIN_COLS // 2, 0, GRID_W - NA_WIN_COLS)
    ck = np.arange(GRID_W)
    col_ok = (ck[None, :] >= cstart[:, None]) & (ck[None, :] < cstart[:, None] + NA_WIN_COLS)
    col_off = np.where(col_ok, ck[None, :] - cq[:, None] + NA_WIN_COLS - 1, 0)
    row_ok = np.zeros((3, R, 3 * R), bool)
    row_off = np.zeros((3, R, 3 * R), np.int64)
    for vi, j in enumerate((0, 1, nblk - 1)):
        for i in range(R):
            r = j * R + i
            r0 = min(max(r - kr // 2, 0), rows - kr)
            for t in range(3):
                jb = j - 1 + t
                if jb < 0 or jb >= nblk:
                    continue
                for rr in range(R):
                    krow = jb * R + rr
                    if r0 <= krow < r0 + kr:
                        row_ok[vi, i, t * R + rr] = True
                        row_off[vi, i, t * R + rr] = krow - r + NA_WIN_ROWS - 1
    n_ro = rpb.shape[1]
    col_sel = (np.arange(rpb.shape[2])[None, None, :] == col_off[:, :, None]) & col_ok[:, :, None]
    planes = jnp.einsum('hrc,qkc->hrqk', rpb, jnp.asarray(col_sel, F32), precision=HIGHEST)
    planes = jnp.where(jnp.asarray(col_ok), planes, NEG)
    planes = jnp.concatenate([planes, jnp.full((NA_HEADS, 1, GRID_W, GRID_W), NEG, F32)], axis=1)
    plane_of = np.where(row_ok, row_off, n_ro).astype(np.int32).reshape(-1)

    def assemble(sel_ref, planes_ref, o_ref):
        v = pl.program_id(0)
        for i in range(R):
            for x in range(3 * R):
                o_ref[0, 0, i * GRID_W:(i + 1) * GRID_W, x * GRID_W:(x + 1) * GRID_W] = (
                    planes_ref[0, sel_ref[(v * R + i) * 3 * R + x]])

    return pl.pallas_call(
        assemble,
        grid_spec=pltpu.PrefetchScalarGridSpec(
            num_scalar_prefetch=1,
            grid=(3, NA_HEADS),
            in_specs=[pl.BlockSpec((1, n_ro + 1, GRID_W, GRID_W), lambda v, h, sel: (h, 0, 0, 0))],
            out_specs=pl.BlockSpec((1, 1, R * GRID_W, 3 * R * GRID_W), lambda v, h, sel: (v, h, 0, 0))),
        out_shape=jax.ShapeDtypeStruct((3, NA_HEADS, R * GRID_W, 3 * R * GRID_W), F32),
        compiler_params=_cparams("arbitrary", "arbitrary"),
        name="na_bias",
    )(jnp.asarray(plane_of), planes)


def _na(zna, zcna, bias, rows):
    b, n, _ = zna.shape
    ctx = zcna.shape[1]
    R = NA_ROWS_PER_STEP
    nq = R * GRID_W
    nblk = rows // R
    kb = lambda col, off: pl.BlockSpec(
        (1, nq, NA_WIDTH), lambda i, j: (i, jnp.clip(j + off, 0, nblk - 1), col))
    variant = lambda i, j: (jnp.where(j == 0, 0, jnp.where(j == nblk - 1, 2, 1)), 0, 0, 0)
    return pl.pallas_call(
        _na_kernel,
        grid=(b, nblk),
        in_specs=[pl.BlockSpec((1, nq, NA_WIDTH), lambda i, j: (i, j, 0)),
                  kb(1, -1), kb(1, 0), kb(1, 1), kb(2, -1), kb(2, 0), kb(2, 1),
                  pl.BlockSpec((1, ctx, NA_WIDTH), lambda i, j: (i, 0, 1)),
                  pl.BlockSpec((1, ctx, NA_WIDTH), lambda i, j: (i, 0, 2)),
                  pl.BlockSpec((1, NA_HEADS, nq, 3 * nq), variant)],
        out_specs=pl.BlockSpec((1, nq, NA_WIDTH), lambda i, j: (i, j, 0)),
        out_shape=jax.ShapeDtypeStruct((b, n, NA_WIDTH), BF16),
        compiler_params=_cparams("parallel", "parallel"),
        name="na",
    )(zna, zna, zna, zna, zna, zna, zna, zcna, zcna, bias)


def _mix_kernel(na_ref, hf_ref, hb_ref, o_ref, x_ref, g1_ref, sh2_ref, sc2_ref, wo_ref, mg_ref, l1g_ref, l1b_ref,
                wr_ref, br_ref, xmid_ref, rt_ref, rw_ref, cnt_ref, carry_s, *, alpha):
    first = (pl.program_id(0) == 0) & (pl.program_id(1) == 0)

    @pl.when(first)
    def _():
        carry_s[...] = jnp.zeros_like(carry_s)

    tm = x_ref.shape[1]
    h = hf_ref[0].astype(F32) + hb_ref[0].astype(F32)
    parts = []
    for hd in range(ML_HEADS):
        hh = h[:, hd * LANES:(hd + 1) * LANES]
        parts.append(hh * lax.rsqrt(jnp.mean(hh * hh, axis=-1, keepdims=True) + LN_EPS))
    hn = jnp.concatenate(parts, axis=1)
    ml = (hn * mg_ref[...] * jax.nn.sigmoid(o_ref[0].astype(F32))).astype(BF16)
    mix = (jnp.dot(na_ref[0], wo_ref[0:NA_WIDTH, :], preferred_element_type=F32)
           + jnp.dot(ml, wo_ref[NA_WIDTH:, :], preferred_element_type=F32))
    xmid = _ln_rows(alpha * x_ref[0] + g1_ref[0] * mix) * l1g_ref[...] + l1b_ref[...]
    xmid_ref[0] = xmid

    xt = (_ln_rows(xmid) * (1.0 + sc2_ref[0]) + sh2_ref[0]).astype(BF16)
    logits = jnp.dot(xt, wr_ref[...], preferred_element_type=F32) + br_ref[...]
    lane = lax.broadcasted_iota(jnp.int32, (tm, LANES), 1)
    is_g = lane < N_GROUPS
    gl = jnp.where(is_g, logits, NEG)
    gmax = jnp.max(gl, axis=1, keepdims=True)
    grp = jnp.min(jnp.where(gl == gmax, lane, LANES), axis=1, keepdims=True)
    gsum = jnp.sum(jnp.where(is_g, jnp.exp(gl - gmax), 0.0), axis=1, keepdims=True)
    grp_w = 1.0 / gsum
    lo = N_GROUPS + EXPERTS_PER_GROUP * grp
    el = jnp.where((lane >= lo) & (lane < lo + EXPERTS_PER_GROUP), logits, NEG)
    t1 = jnp.max(el, axis=1, keepdims=True)
    i1 = jnp.min(jnp.where(el == t1, lane, LANES), axis=1, keepdims=True)
    el2 = jnp.where(lane == i1, NEG, el)
    t2 = jnp.max(el2, axis=1, keepdims=True)
    i2 = jnp.min(jnp.where(el2 == t2, lane, LANES), axis=1, keepdims=True)
    e21 = jnp.exp(t2 - t1)
    w0 = grp_w / (1.0 + e21)
    w1 = grp_w * e21 / (1.0 + e21)

    hit1 = lane == i1
    hit2 = lane == i2
    onehot = (hit1 | hit2).astype(BF16)
    r_i = lax.broadcasted_iota(jnp.int32, (tm, tm), 0)
    c_i = lax.broadcasted_iota(jnp.int32, (tm, tm), 1)
    before = (r_i > c_i).astype(BF16)
    prefix = jnp.dot(before, onehot, preferred_element_type=F32) + carry_s[0:1, :]
    rank0 = jnp.sum(jnp.where(hit1, prefix, 0.0), axis=1, keepdims=True)
    rank1 = jnp.sum(jnp.where(hit2, prefix, 0.0), axis=1, keepdims=True)
    total = carry_s[0:1, :] + jnp.sum(onehot.astype(F32), axis=0, keepdims=True)
    carry_s[...] = jnp.broadcast_to(total, carry_s.shape)
    cnt_ref[...] = jnp.broadcast_to(total, cnt_ref.shape)

    rf = jnp.where(lane == 0, (i1 - N_GROUPS).astype(F32),
                   jnp.where(lane == 1, (i2 - N_GROUPS).astype(F32),
                             jnp.where(lane == 2, rank0, jnp.where(lane == 3, rank1, 0.0))))
    rt_ref[0] = rf.T[0:SUBLANES, :].astype(jnp.int32)
    rw_ref[0] = jnp.where(lane == 0, w0, jnp.where(lane == 1, w1, 0.0))


def _mix(na, hf, hb, zvo, x, g1, sh2, sc2, wo, mg, l1g, l1b, wr, br, alpha, tm):
    b, n, d = x.shape
    row = lambda w: pl.BlockSpec((1, tm, w), lambda i, j: (i, j, 0))
    vec = pl.BlockSpec((1, 1, d), lambda i, j: (i, 0, 0))
    full = lambda a: pl.BlockSpec(a.shape, lambda i, j: (0,) * a.ndim)
    ocol = 1
    return pl.pallas_call(
        functools.partial(_mix_kernel, alpha=alpha),
        grid=(b, n // tm),
        in_specs=[row(NA_WIDTH), row(ML_WIDTH), row(ML_WIDTH),
                  pl.BlockSpec((1, tm, ML_WIDTH), lambda i, j: (i, j, ocol)),
                  row(d), vec, vec, vec, full(wo), full(mg), full(l1g), full(l1b), full(wr), full(br)],
        out_specs=[row(d), pl.BlockSpec((1, SUBLANES, tm), lambda i, j: (i * (n // tm) + j, 0, 0)), row(LANES),
                   pl.BlockSpec((SUBLANES, LANES), lambda i, j: (0, 0))],
        out_shape=[jax.ShapeDtypeStruct((b, n, d), F32),
                   jax.ShapeDtypeStruct((b * (n // tm), SUBLANES, tm), jnp.int32),
                   jax.ShapeDtypeStruct((b, n, LANES), F32),
                   jax.ShapeDtypeStruct((SUBLANES, LANES), F32)],
        scratch_shapes=[pltpu.VMEM((SUBLANES, LANES), F32)],
        compiler_params=_cparams("arbitrary", "arbitrary"),
        name="mix",
    )(na, hf, hb, zvo, x, g1, sh2, sc2, wo, mg, l1g, l1b, wr, br)


def _zero_fill_padding(pad_base_ref, pad_len_ref, nused_ref, xs_ref, zero_s, sem, wait):
    zero_t, xs_t = _token_view(zero_s), _token_view(xs_ref)
    tb = zero_t.shape[0]

    def run(copy):
        copy.wait() if wait else copy.start()

    def fill(off, nrows):
        run(pltpu.make_async_copy(zero_t.at[pl.ds(0, nrows)], xs_t.at[pl.ds(off, nrows)], sem))

    def per_expert(e, _):
        plen = pad_len_ref[e]
        base = pad_base_ref[e]
        bit = tb // 2
        while bit >= 1:
            off = base + (plen & ~(2 * bit - 1))

            @pl.when((plen & bit) != 0)
            def _(bit=bit, off=off):
                fill(off, bit)

            bit //= 2
        return 0

    lax.fori_loop(0, N_EXPERTS, per_expert, 0)

    def per_block(i, _):
        fill(i * tb, tb)
        return 0

    lax.fori_loop(nused_ref[0], xs_t.shape[0] // tb, per_block, 0)


ROW_WORDS = 4
BF16_BITS = 16
HIGH_HALF = -(1 << BF16_BITS)


def _token_view(ref):
    return ref.reshape(ref.shape[0] // ROW_WORDS, ROW_WORDS, LANES)


def _bf16_bits(v):
    return lax.bitcast_convert_type(v.astype(BF16).astype(F32), jnp.int32)


def _store_token_rows(dst_ref, base, val):
    tm, d = val.shape
    assert d == 2 * ROW_WORDS * LANES
    for s in range(ROW_WORDS):
        lo = _bf16_bits(val[:, s * LANES:(s + 1) * LANES])
        hi = _bf16_bits(val[:, (s + ROW_WORDS) * LANES:(s + ROW_WORDS + 1) * LANES])
        dst_ref[pl.ds(base * ROW_WORDS + s, tm, stride=ROW_WORDS), :] = hi | lax.shift_right_logical(lo, BF16_BITS)


def _load_token_rows(src_ref, base, tm, dtype):
    words = [src_ref[pl.ds(base * ROW_WORDS + s, tm, stride=ROW_WORDS), :] for s in range(ROW_WORDS)]
    lo = [lax.bitcast_convert_type(w << BF16_BITS, F32).astype(dtype) for w in words]
    hi = [lax.bitcast_convert_type(w & HIGH_HALF, F32).astype(dtype) for w in words]
    return jnp.concatenate(lo + hi, axis=1)


def _dispatch_kernel(pad_base_ref, pad_len_ref, nused_ref, dest_ref, xmid_ref, sh2_ref, sc2_ref, xs_ref,
                     xt_s, zero_s, sem, zsem):
    tm = xmid_ref.shape[1]
    step = pl.program_id(0) * pl.num_programs(1) + pl.program_id(1)
    nsteps = pl.num_programs(0) * pl.num_programs(1)

    slot = step % 2

    def wait_slot_copies(s):
        for _ in range(2):
            pltpu.make_async_copy(_token_view(xt_s).at[pl.ds(s * tm, tm)], _token_view(xs_ref).at[pl.ds(0, tm)],
                                  sem.at[s]).wait()

    @pl.when(step == 0)
    def _():
        zero_s[...] = jnp.zeros_like(zero_s)
        _zero_fill_padding(pad_base_ref, pad_len_ref, nused_ref, xs_ref, zero_s, zsem, False)

    for c in range(tm // COMBINE_CHUNK):
        rows = slice(c * COMBINE_CHUNK, (c + 1) * COMBINE_CHUNK)
        xt = _ln_rows(xmid_ref[0, rows, :]) * (1.0 + sc2_ref[0]) + sh2_ref[0]
        _store_token_rows(xt_s, slot * tm + c * COMBINE_CHUNK, xt)
        for r in range(c * COMBINE_CHUNK, (c + 1) * COMBINE_CHUNK):
            for k in range(2):
                pltpu.make_async_copy(_token_view(xt_s).at[slot * tm + r], _token_view(xs_ref).at[dest_ref[0, k, r]],
                                      sem.at[slot]).start(priority=k)

    @pl.when(step == 0)
    def _():
        _zero_fill_padding(pad_base_ref, pad_len_ref, nused_ref, xs_ref, zero_s, zsem, True)

    @pl.when(step > 0)
    def _():
        wait_slot_copies(1 - slot)

    @pl.when(step == nsteps - 1)
    def _():
        wait_slot_copies(slot)


def _dispatch(xmid, sh2, sc2, dest, pad_base, pad_len, nused, cap, tm):
    b, n, d = xmid.shape
    assert d == 2 * ROW_WORDS * LANES
    nt = n // tm
    vec = pl.BlockSpec((1, 1, d), lambda i, j, *_: (i, 0, 0))
    return pl.pallas_call(
        _dispatch_kernel,
        grid_spec=pltpu.PrefetchScalarGridSpec(
            num_scalar_prefetch=3,
            grid=(b, nt),
            in_specs=[pl.BlockSpec((1, 2, tm), lambda i, j, *_: (i * nt + j, 0, 0), memory_space=pltpu.SMEM),
                      pl.BlockSpec((1, tm, d), lambda i, j, *_: (i, j, 0)), vec, vec],
            out_specs=pl.BlockSpec(memory_space=pl.ANY),
            scratch_shapes=[pltpu.VMEM((2 * tm * ROW_WORDS, LANES), ROW_DTYPE),
                            pltpu.VMEM((EXPERT_ROWS * ROW_WORDS, LANES), ROW_DTYPE),
                            pltpu.SemaphoreType.DMA((2,)), pltpu.SemaphoreType.DMA]),
        out_shape=jax.ShapeDtypeStruct((cap * ROW_WORDS, LANES), ROW_DTYPE),
        compiler_params=_cparams("arbitrary", "arbitrary"),
        name="dispatch",
    )(pad_base, pad_len, nused, dest, xmid, sh2, sc2)


def _expert_kernel(be_ref, nv_ref, xs_ref, w1_ref, w3_ref, w2_ref, ys_ref, w1b, w3b, w2b):
    tb = xs_ref.shape[0] // ROW_WORDS
    i = pl.program_id(0)
    e = be_ref[i]
    changed = (i == 0) | (be_ref[jnp.maximum(i - 1, 0)] != e)

    @pl.when(changed)
    def _():
        w1b[...] = w1_ref[0].astype(BF16)
        w3b[...] = w3_ref[0].astype(BF16)
        w2b[...] = w2_ref[0].astype(BF16)

    nv = nv_ref[i]

    @pl.when(nv > 0)
    def _():
        xb = _load_token_rows(xs_ref, 0, tb, BF16)
        h1 = jnp.dot(xb, w1b[...], preferred_element_type=F32)
        h3 = jnp.dot(xb, w3b[...], preferred_element_type=F32)
        a = (_silu(h1) * h3).astype(BF16)
        _store_token_rows(ys_ref, 0, jnp.dot(a, w2b[...], preferred_element_type=F32))

    @pl.when(nv == 0)
    def _():
        ys_ref[...] = jnp.zeros_like(ys_ref)


def _experts(xs, block_e, block_nv, w1, w3, w2):
    d, hid = w1.shape[1], w1.shape[2]
    tb = EXPERT_ROWS
    rows = pl.BlockSpec((tb * ROW_WORDS, LANES), lambda i, be, nv: (i, 0))
    return pl.pallas_call(
        _expert_kernel,
        grid_spec=pltpu.PrefetchScalarGridSpec(
            num_scalar_prefetch=2,
            grid=(xs.shape[0] // (tb * ROW_WORDS),),
            in_specs=[rows,
                      pl.BlockSpec((1, d, hid), lambda i, be, nv: (be[i], 0, 0)),
                      pl.BlockSpec((1, d, hid), lambda i, be, nv: (be[i], 0, 0)),
                      pl.BlockSpec((1, hid, d), lambda i, be, nv: (be[i], 0, 0))],
            out_specs=rows,
            scratch_shapes=[pltpu.VMEM((d, hid), BF16), pltpu.VMEM((d, hid), BF16), pltpu.VMEM((hid, d), BF16)]),
        out_shape=jax.ShapeDtypeStruct(xs.shape, xs.dtype),
        compiler_params=_cparams("arbitrary"),
        name="experts",
    )(block_e, block_nv, xs, w1, w3, w2)


def _combine_kernel(dcur_ref, dnext_ref, xmid_ref, rw_ref, g2_ref, l2g_ref, l2b_ref, ys_ref, o_ref,
                    y0_s, y1_s, sem, *, alpha):
    tm = xmid_ref.shape[1]
    step = pl.program_id(0) * pl.num_programs(1) + pl.program_id(1)
    nsteps = pl.num_programs(0) * pl.num_programs(1)
    slot = step % 2
    other = 1 - slot

    def start_row(dest_ref, into, r):
        for k, buf in ((0, y0_s), (1, y1_s)):
            pltpu.make_async_copy(_token_view(ys_ref).at[dest_ref[0, k, r]],
                                  _token_view(buf).at[into * tm + r], sem.at[into]).start(priority=k)

    def wait_slot(into):
        for buf in (y0_s, y1_s):
            pltpu.make_async_copy(_token_view(ys_ref).at[pl.ds(0, tm)], _token_view(buf).at[pl.ds(into * tm, tm)],
                                  sem.at[into]).wait()

    @pl.when(step == 0)
    def _():
        lax.fori_loop(0, tm, lambda r, c: (start_row(dcur_ref, 0, r), c)[1], 0, unroll=8)

    wait_slot(slot)
    for c in range(tm // COMBINE_CHUNK):
        for r in range(c * COMBINE_CHUNK, (c + 1) * COMBINE_CHUNK):
            start_row(dnext_ref, other, r)
        rows = slice(c * COMBINE_CHUNK, (c + 1) * COMBINE_CHUNK)
        base = slot * tm + c * COMBINE_CHUNK
        rw = rw_ref[0, rows, :]
        moe = (rw[:, 0:1] * _load_token_rows(y0_s, base, COMBINE_CHUNK, F32)
               + rw[:, 1:2] * _load_token_rows(y1_s, base, COMBINE_CHUNK, F32))
        o_ref[0, rows, :] = (_ln_rows(alpha * xmid_ref[0, rows, :] + g2_ref[0] * moe) * l2g_ref[...] + l2b_ref[...])

    @pl.when(step == nsteps - 1)
    def _():
        wait_slot(other)


def _combine(xmid, rw, g2, l2g, l2b, ys, dest, alpha, tm):
    b, n, d = xmid.shape
    nt = n // tm
    full = lambda a: pl.BlockSpec(a.shape, lambda i, j: (0,) * a.ndim)
    return pl.pallas_call(
        functools.partial(_combine_kernel, alpha=alpha),
        grid=(b, nt),
        in_specs=[pl.BlockSpec((1, 2, tm), lambda i, j: (i * nt + j, 0, 0), memory_space=pltpu.SMEM),
                  pl.BlockSpec((1, 2, tm), lambda i, j: (jnp.minimum(i * nt + j + 1, b * nt - 1), 0, 0),
                               memory_space=pltpu.SMEM),
                  pl.BlockSpec((1, tm, d), lambda i, j: (i, j, 0)),
                  pl.BlockSpec((1, tm, LANES), lambda i, j: (i, j, 0)),
                  pl.BlockSpec((1, 1, d), lambda i, j: (i, 0, 0)),
                  full(l2g), full(l2b),
                  pl.BlockSpec(memory_space=pl.ANY)],
        out_specs=pl.BlockSpec((1, tm, d), lambda i, j: (i, j, 0)),
        out_shape=jax.ShapeDtypeStruct((b, n, d), F32),
        scratch_shapes=[pltpu.VMEM((2 * tm * ROW_WORDS, LANES), ys.dtype),
                        pltpu.VMEM((2 * tm * ROW_WORDS, LANES), ys.dtype), pltpu.SemaphoreType.DMA((2,))],
        compiler_params=_cparams("arbitrary", "arbitrary"),
        name="combine",
    )(dest, dest, xmid, rw, g2, l2g, l2b, ys)


def _tile(n, want):
    t = min(n, want)
    assert n % t == 0, (n, t)
    return t


def kernel(x, c, ctx, c_ctx, w_ada, b_ada, w_in, conv_w, conv_b, gate_b, rpb, ml_norm_g, w_out, ln1_g, ln1_b,
           w_router_g, b_router_g, w_router_e, b_router_e, w1, w3, w2, ln2_g, ln2_b):
    B, N, D = x.shape
    T_CTX = ctx.shape[1]
    depth = w_ada.shape[0]
    rows = N // GRID_W
    assert depth == 1 and N % GRID_W == 0 and rows % NA_ROWS_PER_STEP == 0 and rows >= 3 * NA_ROWS_PER_STEP
    assert N % ML_CHUNK == 0 and T_CTX % ML_CHUNK == 0
    alpha = (2.0 * depth) ** 0.25
    l = 0

    pad_rows = -(B + 1) % SUBLANES
    cvec = jnp.concatenate([c, c_ctx[None], jnp.zeros((pad_rows, D), F32)], axis=0)
    ada = _ada(cvec, w_ada[l], b_ada[l])
    sh1, sc1, g1, sh2, sc2, g2 = [a[:, None, :] for a in jnp.split(ada[:B], 6, axis=-1)]
    csh1, csc1 = [jnp.broadcast_to(a[None], (B, 1, D)) for a in jnp.split(ada[B:B + 1], 6, axis=-1)[:2]]

    col_ml = 3 * NA_WIDTH
    col_g = col_ml + 4 * ML_WIDTH
    col_v = col_ml + 2 * ML_WIDTH
    wb = w_in[l].astype(BF16)
    wna, wqk, wvo = wb[:, :col_ml], wb[:, col_ml:col_v], wb[:, col_v:col_g]
    n_gate = 4 * ML_HEADS
    wg = jnp.pad(wb[:, col_g:], ((0, 0), (0, LANES - n_gate)))
    gb = jnp.pad(gate_b[l], (0, LANES - n_gate)).reshape(1, LANES)
    cos_t, sin_t = _rope_tables(N)
    zna, zvo, qk_l, gcol_l, grow_l = _inproj(x, sh1, sc1, wna, wqk, wvo, wg, gb, conv_w[l], conv_b[l], cos_t, sin_t,
                                             True, _tile(N, 512))
    zcna, zcvo, qk_c, gcol_c, grow_c = _inproj(ctx, csh1, csc1, wna, wqk, wvo, wg, gb, conv_w[l], conv_b[l],
                                               cos_t[:T_CTX], sin_t[:T_CTX], False, _tile(T_CTX, 256))

    c0 = jnp.zeros((B, 2, ML_HEADS, ML_HEAD_DIM, 2 * ML_HEAD_DIM), F32)
    m0 = jnp.zeros((B, 2, ML_HEADS, SUBLANES, LANES), F32)
    c_ctx_end, m_ctx_end = _mlstm(qk_c, zcvo, gcol_c, grow_c, c0, m0, False)
    hf, hb, _, _ = _mlstm(qk_l, zvo, gcol_l, grow_l, c_ctx_end, m_ctx_end, True)

    na = _na(zna, zcna, _na_bias_tables(rpb[l], rows), rows)

    wr = jnp.pad(jnp.concatenate([w_router_g[l], w_router_e[l]], axis=1),
                 ((0, 0), (0, LANES - N_GROUPS - N_EXPERTS))).astype(BF16)
    br = jnp.pad(jnp.concatenate([b_router_g[l], b_router_e[l]]), (0, LANES - N_GROUPS - N_EXPERTS)).reshape(1, LANES)
    tm = _tile(N, 512)
    xmid, rt, rw, counts = _mix(na, hf, hb, zvo, x, g1, sh2, sc2, w_out[l].astype(BF16),
                                ml_norm_g[l].reshape(1, ML_WIDTH), ln1_g[l].reshape(1, D), ln1_b[l].reshape(1, D),
                                wr, br, alpha, tm)

    tb = EXPERT_ROWS
    n_assign = 2 * B * N
    cap = -(-n_assign // tb) * tb + N_EXPERTS * tb
    sizes = counts[0, N_GROUPS:N_GROUPS + N_EXPERTS].astype(jnp.int32)
    padded = (sizes + tb - 1) // tb * tb
    pend = jnp.cumsum(padded)
    pstart = pend - padded
    experts = jnp.arange(N_EXPERTS, dtype=jnp.int32)
    first_row = jnp.sum(jnp.where(rt[:, 0:2, :, None] == experts, pstart, 0), axis=-1)
    dest = first_row + rt[:, 2:4, :]
    blk0 = jnp.arange(cap // tb, dtype=jnp.int32) * tb
    block_e = jnp.minimum(jnp.sum(pend[None, :] <= blk0[:, None], axis=1), N_EXPERTS - 1).astype(jnp.int32)
    is_e = block_e[:, None] == experts
    block_nv = jnp.clip(jnp.sum(jnp.where(is_e, pstart + sizes, 0), axis=1) - blk0, 0, tb).astype(jnp.int32)

    nused = (pend[-1:] // tb).astype(jnp.int32)
    xs = _dispatch(xmid, sh2, sc2, dest, pstart + sizes, padded - sizes, nused, cap, tm)
    ys = _experts(xs, block_e, block_nv, w1[l], w3[l], w2[l])
    return _combine(xmid, rw, g2, ln2_g[l].reshape(1, D), ln2_b[l].reshape(1, D), ys, dest, alpha, tm)
```

```python
import functools

import numpy as np
import jax
import jax.numpy as jnp
from jax import lax
from jax.experimental import pallas as pl
from jax.experimental.pallas import tpu as pltpu

F32 = jnp.float32
BF16 = jnp.bfloat16
ROW_DTYPE = jnp.int32
HIGHEST = lax.Precision.HIGHEST

GRID_W = 64
NA_HEADS = 8
NA_HEAD_DIM = 64
NA_WIDTH = NA_HEADS * NA_HEAD_DIM
NA_WIN_ROWS = 8
NA_WIN_COLS = 16
ML_HEADS = 4
ML_HEAD_DIM = 128
ML_WIDTH = ML_HEADS * ML_HEAD_DIM
ML_CHUNK = 128
CONV_K = 5
N_GROUPS = 8
EXPERTS_PER_GROUP = 8
N_EXPERTS = N_GROUPS * EXPERTS_PER_GROUP
ROPE_BASE = 10000.0
LN_EPS = 1e-5

LANES = 128
SUBLANES = 8
VMEM_LIMIT = 56 * 1024 * 1024

NA_ROWS_PER_STEP = 4
EXPERT_ROWS = 512
CONV_ROWS = 128
COMBINE_CHUNK = 64
MLSTM_CHUNKS_PER_STEP = 4
NEG = -1e30


def _cparams(*sem):
    return pltpu.CompilerParams(dimension_semantics=sem, vmem_limit_bytes=VMEM_LIMIT)


def _silu(v):
    return v * jax.nn.sigmoid(v)


def _ln_rows(v):
    mu = jnp.mean(v, axis=-1, keepdims=True)
    vc = v - mu
    var = jnp.mean(vc * vc, axis=-1, keepdims=True)
    return vc * lax.rsqrt(var + LN_EPS)


def _ada_kernel(c_ref, w_ref, b_ref, o_ref):
    o_ref[...] = jnp.dot(_silu(c_ref[...]), w_ref[...], preferred_element_type=F32,
                         precision=HIGHEST) + b_ref[...]


def _ada(cvec, w, b):
    rows, d = cvec.shape
    cols = w.shape[1]
    tn = d
    assert cols % tn == 0
    return pl.pallas_call(
        _ada_kernel,
        grid=(cols // tn,),
        in_specs=[pl.BlockSpec((rows, d), lambda j: (0, 0)),
                  pl.BlockSpec((d, tn), lambda j: (0, j)),
                  pl.BlockSpec((1, tn), lambda j: (0, j))],
        out_specs=pl.BlockSpec((rows, tn), lambda j: (0, j)),
        out_shape=jax.ShapeDtypeStruct((rows, cols), F32),
        compiler_params=_cparams("arbitrary"),
        name="ada",
    )(cvec, w, b.reshape(1, cols))


HALO = 16


def _log_sigmoid(v):
    return jnp.minimum(v, 0.0) - jnp.log1p(jnp.exp(-jnp.abs(v)))


GATE_RAW, GATE_PREFIX, GATE_SUFFIX = 0, 16, 32


def _pack_gates(g):
    L = g.shape[0]
    row = lax.broadcasted_iota(jnp.int32, (L, L), 0)
    col = lax.broadcasted_iota(jnp.int32, (L, L), 1)
    lane = lax.broadcasted_iota(jnp.int32, g.shape, 1)
    lf = _log_sigmoid(g)
    prefix = jnp.dot((row >= col).astype(F32), lf, preferred_element_type=F32, precision=HIGHEST)
    suffix = jnp.dot((row <= col).astype(F32), lf, preferred_element_type=F32, precision=HIGHEST)
    return jnp.where(lane < GATE_PREFIX, g,
                     jnp.where(lane < GATE_SUFFIX, pltpu.roll(prefix, GATE_PREFIX, 1),
                               jnp.where(lane < GATE_SUFFIX + GATE_PREFIX, pltpu.roll(suffix, GATE_SUFFIX, 1), 0.0)))


def _inproj_kernel(x_ref, xp_ref, xn_ref, sh_ref, sc_ref, wna_ref, wqk_ref, wvo_ref, wg_ref, gb_ref, cw_ref, cb_ref,
                   cos_ref, sin_ref, zna_ref, zvo_ref, o_ref, gcol_ref, grow_ref, *pad_refs, rope, tr):
    i = pl.program_id(1)
    last = pl.num_programs(1) - 1
    ya_s, pad_refs = pad_refs[0], pad_refs[1:]

    def normalise(rows, dst):
        ya_s[dst:dst + rows.shape[0], :] = (_ln_rows(rows) * (1.0 + sc_ref[0]) + sh_ref[0]).astype(BF16)

    normalise(xp_ref[0], 0)
    for r0 in range(0, tr, CONV_ROWS):
        normalise(x_ref[0, r0:r0 + CONV_ROWS, :], HALO + r0)
    normalise(xn_ref[0], HALO + tr)
    ya = ya_s[...]
    yb = ya_s[HALO:HALO + tr, :]
    pad = CONV_K // 2
    width = wqk_ref.shape[1]
    lane = lax.broadcasted_iota(jnp.int32, (CONV_ROWS, LANES), 1)
    first_half = (lane % (ML_HEAD_DIM // 2)) < (ML_HEAD_DIM // 4)
    kscale = ML_HEAD_DIM ** -0.5
    cb = 2 * LANES

    def project_qk(c):
        z = jnp.dot(ya, wqk_ref[:, c * cb:(c + 1) * cb], preferred_element_type=F32)
        pad_refs[c][HALO:HALO + tr, :] = z[HALO:HALO + tr]
        pad_refs[c][0:HALO, :] = jnp.where(i > 0, z[0:HALO], 0.0)
        pad_refs[c][HALO + tr:2 * HALO + tr, :] = jnp.where(i < last, z[HALO + tr:], 0.0)

    def conv_group(g):
        cols = slice(g * LANES, (g + 1) * LANES)
        pad_ref = pad_refs[g * LANES // cb]
        pc = slice(g * LANES % cb, g * LANES % cb + LANES)
        for r0 in range(0, tr, CONV_ROWS):
            top = HALO - pad + r0
            acc = cw_ref[0:1, cols] * pad_ref[top:top + CONV_ROWS, pc] + cb_ref[:, cols]
            for j in range(1, CONV_K):
                acc = acc + cw_ref[j:j + 1, cols] * pad_ref[top + j:top + j + CONV_ROWS, pc]
            ug = _silu(acc)
            if rope:
                partner = jnp.where(first_half, pltpu.roll(ug, LANES - ML_HEAD_DIM // 4, 1),
                                    pltpu.roll(ug, ML_HEAD_DIM // 4, 1))
                ug = ug * cos_ref[r0:r0 + CONV_ROWS, :] + partner * sin_ref[r0:r0 + CONV_ROWS, :]
            if g >= ML_HEADS:
                ug = ug * kscale
            o_ref[0, r0:r0 + CONV_ROWS, cols] = ug.astype(BF16)

    other = ([(zna_ref, wna_ref, c) for c in range(wna_ref.shape[1] // cb)]
             + [(zvo_ref, wvo_ref, c) for c in range(wvo_ref.shape[1] // cb)])

    def project_other(n):
        for _ in range(n):
            if other:
                dst, w, c = other.pop(0)
                cols = slice(c * cb, (c + 1) * cb)
                dst[0, :, cols] = jnp.dot(yb, w[:, cols], preferred_element_type=F32).astype(BF16)

    n_qk = width // cb
    per_round = -(-len(other) // n_qk)
    project_qk(0)
    for c in range(n_qk):
        if c + 1 < n_qk:
            project_qk(c + 1)
        project_other(per_round)
        for g in range(c * cb // LANES, (c + 1) * cb // LANES):
            conv_group(g)
    project_other(len(other))
    gates = jnp.dot(yb, wg_ref[...], preferred_element_type=F32) + gb_ref[...]
    for c in range(tr // ML_CHUNK):
        packed = _pack_gates(gates[c * ML_CHUNK:(c + 1) * ML_CHUNK, :])
        gcol_ref[0, c * ML_CHUNK:(c + 1) * ML_CHUNK, :] = packed
        grow_ref[0, c] = packed.T


def _inproj(x, shift, scale, wna, wqk, wvo, wg, gb, conv_w, conv_b, cos_t, sin_t, rope, tr):
    b, t, d = x.shape
    width = 2 * ML_WIDTH
    nh = tr // HALO
    nblk = t // HALO
    L = ML_CHUNK
    full = lambda a: pl.BlockSpec(a.shape, lambda i, j: (0,) * a.ndim)
    row = lambda w: pl.BlockSpec((1, tr, w), lambda i, j: (i, j, 0))
    vec = pl.BlockSpec((1, 1, d), lambda i, j: (i, 0, 0))
    return pl.pallas_call(
        functools.partial(_inproj_kernel, rope=rope, tr=tr),
        grid=(b, t // tr),
        in_specs=[row(d),
                  pl.BlockSpec((1, HALO, d), lambda i, j: (i, jnp.maximum(j * nh - 1, 0), 0)),
                  pl.BlockSpec((1, HALO, d), lambda i, j: (i, jnp.minimum((j + 1) * nh, nblk - 1), 0)),
                  vec, vec, full(wna), full(wqk), full(wvo), full(wg), full(gb),
                  pl.BlockSpec((CONV_K, width), lambda i, j: (0, 0)),
                  pl.BlockSpec((1, width), lambda i, j: (0, 0)),
                  pl.BlockSpec((tr, LANES), lambda i, j: (j, 0)),
                  pl.BlockSpec((tr, LANES), lambda i, j: (j, 0))],
        out_specs=[row(wna.shape[1]), row(wvo.shape[1]), row(width), row(LANES),
                   pl.BlockSpec((1, tr // L, LANES, L), lambda i, j: (i, j, 0, 0))],
        out_shape=[jax.ShapeDtypeStruct((b, t, wna.shape[1]), BF16),
                   jax.ShapeDtypeStruct((b, t, wvo.shape[1]), BF16),
                   jax.ShapeDtypeStruct((b, t, width), BF16),
                   jax.ShapeDtypeStruct((b, t, LANES), F32),
                   jax.ShapeDtypeStruct((b, t // L, LANES, L), F32)],
        scratch_shapes=([pltpu.VMEM((tr + 2 * HALO, d), BF16)]
                        + [pltpu.VMEM((tr + 2 * HALO, 2 * LANES), F32)] * (width // (2 * LANES))),
        compiler_params=_cparams("parallel", "parallel"),
        name="inproj" if rope else "inproj_ctx",
    )(x, x, x, shift, scale, wna, wqk, wvo, wg, gb, conv_w, conv_b.reshape(1, width), cos_t, sin_t)


def _rope_tables(n):
    pos = np.arange(n)
    half = ML_HEAD_DIM // 4
    inv = np.float32(ROPE_BASE) ** (-np.arange(half, dtype=np.float32) / np.float32(half))

    def axis_tables(p):
        ang = p.astype(np.float32)[:, None] * inv[None, :]
        c, s = np.cos(ang), np.sin(ang)
        return np.concatenate([c, c], axis=-1), np.concatenate([-s, s], axis=-1)

    cr, sr = axis_tables(pos // GRID_W)
    cc, sc = axis_tables(pos % GRID_W)
    return (jnp.asarray(np.concatenate([cr, cc], axis=-1), F32), jnp.asarray(np.concatenate([sr, sc], axis=-1), F32))


def _mlstm_direction(d, qk, v, gcol, grow, cn_s, m_s):
    L = ML_CHUNK
    row = lax.broadcasted_iota(jnp.int32, (L, L), 0)
    col = lax.broadcasted_iota(jnp.int32, (L, L), 1)
    fwd = d == 0
    keep = (row >= col) if fwd else (row <= col)
    ones = jnp.ones((L, ML_HEAD_DIM), BF16)
    end = L - 1 if fwd else 0
    outs = []
    for h in range(ML_HEADS):
        ci = GATE_RAW + (2 * d) * ML_HEADS + h
        cf = (GATE_PREFIX if fwd else GATE_SUFFIX) + (2 * d + 1) * ML_HEADS + h
        bc = jnp.broadcast_to(gcol[:, cf:cf + 1], (L, L))
        ic = jnp.broadcast_to(gcol[:, ci:ci + 1], (L, L))
        br = grow[cf:cf + 1, :]
        ir = grow[ci:ci + 1, :]
        m_prev = m_s[d, h][0:1, :]
        dlog = jnp.where(keep, bc - br + ir, NEG)
        m_t = jnp.maximum(bc + m_prev, jnp.max(dlog, axis=1, keepdims=True))
        dw = jnp.exp((dlog - m_t).astype(BF16))
        inter = jnp.exp(bc + m_prev - m_t)
        qh = qk[:, h * LANES:(h + 1) * LANES]
        kh = qk[:, ML_WIDTH + h * LANES:ML_WIDTH + (h + 1) * LANES]
        vp = jnp.concatenate([v[:, h * LANES:(h + 1) * LANES], ones], axis=1)
        s = lax.dot_general(qh, kh, (((1,), (1,)), ((), ())), preferred_element_type=F32).astype(BF16) * dw
        cn = cn_s[d, h]
        a1 = jnp.dot(s, vp, preferred_element_type=F32)
        a2 = jnp.dot(qh, cn.astype(BF16), preferred_element_type=F32)
        num = a1[:, :ML_HEAD_DIM] + inter * a2[:, :ML_HEAD_DIM]
        den = a1[:, ML_HEAD_DIM:] + inter * a2[:, ML_HEAD_DIM:]
        outs.append(num / jnp.maximum(jnp.abs(den), jnp.exp(-m_t)))
        b_end = jnp.broadcast_to(br[:, end:end + 1], (1, L))
        g_row = b_end - br + ir
        m_new = jnp.maximum(b_end + m_prev, jnp.max(g_row, axis=1, keepdims=True))
        decay = jnp.exp(b_end + m_prev - m_new)
        wgt = jnp.exp((b_end - bc + ic - m_new).astype(BF16))
        kw = kh * wgt
        upd = lax.dot_general(kw, vp, (((0,), (0,)), ((), ())), preferred_element_type=F32)
        cn_s[d, h] = jnp.concatenate([decay, decay], axis=1) * cn + upd
        m_s[d, h] = jnp.broadcast_to(m_new, m_s.shape[2:])
    return jnp.concatenate(outs, axis=1)


def _mlstm_kernel(qkf_ref, vf_ref, gcf_ref, grf_ref, qkb_ref, vb_ref, gcb_ref, grb_ref, c0_ref, m0_ref, *rest, emit_h):
    if emit_h:
        hf_ref, hb_ref, ct_ref, mt_ref, cn_s, m_s = rest
    else:
        ct_ref, mt_ref, cn_s, m_s = rest
        hf_ref = hb_ref = None
    c = pl.program_id(1)
    L = ML_CHUNK

    @pl.when(c == 0)
    def _():
        cn_s[...] = c0_ref[0]
        m_s[...] = m0_ref[0]

    for d, refs, h_ref in ((0, (qkf_ref, vf_ref, gcf_ref, grf_ref), hf_ref), (1, (qkb_ref, vb_ref, gcb_ref, grb_ref), hb_ref)):
        qk_ref, v_ref, gc_ref, gr_ref = refs
        n_sub = qk_ref.shape[1] // L
        for s in (range(n_sub) if d == 0 else reversed(range(n_sub))):
            rows = slice(s * L, (s + 1) * L)
            h = _mlstm_direction(d, qk_ref[0, rows, :], v_ref[0, rows, :], gc_ref[0, rows, :], gr_ref[0, s], cn_s, m_s)
            if h_ref is not None:
                h_ref[0, rows, :] = h.astype(h_ref.dtype)

    @pl.when(c == pl.num_programs(1) - 1)
    def _():
        ct_ref[0] = cn_s[...]
        mt_ref[0] = m_s[...]


def _mlstm(qk, zvo, gcol, grow, c0, m0, emit_h):
    b, t, _ = qk.shape
    n_sub = min(MLSTM_CHUNKS_PER_STEP, t // ML_CHUNK)
    R = n_sub * ML_CHUNK
    assert t % R == 0
    nc = t // R
    vcol = 0
    f_idx = lambda i, c: (i, c, 0)
    b_idx = lambda i, c: (i, nc - 1 - c, 0)
    st_c = pl.BlockSpec((1, 2, ML_HEADS, ML_HEAD_DIM, 2 * ML_HEAD_DIM), lambda i, c: (i, 0, 0, 0, 0))
    st_m = pl.BlockSpec((1, 2, ML_HEADS, SUBLANES, LANES), lambda i, c: (i, 0, 0, 0, 0))
    out_specs = [st_c, st_m]
    out_shape = [jax.ShapeDtypeStruct(c0.shape, F32), jax.ShapeDtypeStruct(m0.shape, F32)]
    if emit_h:
        out_specs = [pl.BlockSpec((1, R, ML_WIDTH), f_idx), pl.BlockSpec((1, R, ML_WIDTH), b_idx)] + out_specs
        out_shape = [jax.ShapeDtypeStruct((b, t, ML_WIDTH), BF16)] * 2 + out_shape
    return pl.pallas_call(
        functools.partial(_mlstm_kernel, emit_h=emit_h),
        grid=(b, nc),
        in_specs=[pl.BlockSpec((1, R, 2 * ML_WIDTH), f_idx),
                  pl.BlockSpec((1, R, ML_WIDTH), lambda i, c: (i, c, vcol)),
                  pl.BlockSpec((1, R, LANES), f_idx),
                  pl.BlockSpec((1, n_sub, LANES, ML_CHUNK), lambda i, c: (i, c, 0, 0)),
                  pl.BlockSpec((1, R, 2 * ML_WIDTH), b_idx),
                  pl.BlockSpec((1, R, ML_WIDTH), lambda i, c: (i, nc - 1 - c, vcol)),
                  pl.BlockSpec((1, R, LANES), b_idx),
                  pl.BlockSpec((1, n_sub, LANES, ML_CHUNK), lambda i, c: (i, nc - 1 - c, 0, 0)),
                  st_c, st_m],
        out_specs=out_specs,
        out_shape=out_shape,
        scratch_shapes=[pltpu.VMEM((2, ML_HEADS, ML_HEAD_DIM, 2 * ML_HEAD_DIM), F32),
                        pltpu.VMEM((2, ML_HEADS, SUBLANES, LANES), F32)],
        compiler_params=_cparams("parallel", "arbitrary"),
        name="mlstm" if emit_h else "mlstm_ctx",
    )(qk, zvo, gcol, grow, qk, zvo, gcol, grow, c0, m0)


def _lane_in(shape, start, width):
    lane = lax.broadcasted_iota(jnp.int32, shape, 1)
    return (lane >= start) & (lane < start + width)


def _na_kernel(q_ref, k0_ref, k1_ref, k2_ref, v0_ref, v1_ref, v2_ref, kc_ref, vc_ref, bias_ref, o_ref):
    nq = q_ref.shape[1]
    lane = lax.broadcasted_iota(jnp.int32, (nq, LANES), 1)
    nt = (((1,), (1,)), ((), ()))
    for p in range(NA_HEADS // 2):
        sl = slice(p * LANES, (p + 1) * LANES)
        q2 = q_ref[0, :, sl] * (NA_HEAD_DIM ** -0.5)
        kwin = jnp.concatenate([k0_ref[0, :, sl], k1_ref[0, :, sl], k2_ref[0, :, sl]], axis=0)
        vwin = jnp.concatenate([v0_ref[0, :, sl], v1_ref[0, :, sl], v2_ref[0, :, sl]], axis=0)
        kc = kc_ref[0, :, sl]
        vwin = jnp.concatenate([vwin, jnp.ones_like(vwin)], axis=1)
        vc = jnp.concatenate([vc_ref[0, :, sl], jnp.ones_like(kc)], axis=1)
        halves = []
        for a in range(2):
            in_head = (lane >= a * NA_HEAD_DIM) & (lane < (a + 1) * NA_HEAD_DIM)
            qm = jnp.where(in_head, q2, jnp.zeros_like(q2))
            s_win = lax.dot_general(qm, kwin, nt, preferred_element_type=F32) + bias_ref[0, 2 * p + a]
            s_ctx = lax.dot_general(qm, kc, nt, preferred_element_type=F32)
            m = jnp.maximum(jnp.max(s_win, axis=1, keepdims=True), jnp.max(s_ctx, axis=1, keepdims=True))
            p_win = jnp.exp((s_win - m).astype(BF16))
            p_ctx = jnp.exp((s_ctx - m).astype(BF16))
            o = jnp.dot(p_win, vwin, preferred_element_type=F32) + jnp.dot(p_ctx, vc, preferred_element_type=F32)
            halves.append(o[:, :LANES] / o[:, LANES:])
        o_ref[0, :, sl] = jnp.where(lane < NA_HEAD_DIM, halves[0], halves[1]).astype(o_ref.dtype)


def _na_bias_tables(rpb, rows):
    R = NA_ROWS_PER_STEP
    nblk = rows // R
    kr = NA_WIN_ROWS
    cq = np.arange(GRID_W)
    cstart = np.clip(cq - NA_WIN_COLS // 2, 0, GRID_W - NA_WIN_COLS)
    ck = np.arange(GRID_W)
    col_ok = (ck[None, :] >= cstart[:, None]) & (ck[None, :] < cstart[:, None] + NA_WIN_COLS)
    col_off = np.where(col_ok, ck[None, :] - cq[:, None] + NA_WIN_COLS - 1, 0)
    row_ok = np.zeros((3, R, 3 * R), bool)
    row_off = np.zeros((3, R, 3 * R), np.int64)
    for vi, j in enumerate((0, 1, nblk - 1)):
        for i in range(R):
            r = j * R + i
            r0 = min(max(r - kr // 2, 0), rows - kr)
            for t in range(3):
                jb = j - 1 + t
                if jb < 0 or jb >= nblk:
                    continue
                for rr in range(R):
                    krow = jb * R + rr
                    if r0 <= krow < r0 + kr:
                        row_ok[vi, i, t * R + rr] = True
                        row_off[vi, i, t * R + rr] = krow - r + NA_WIN_ROWS - 1
    n_ro = rpb.shape[1]
    col_sel = (np.arange(rpb.shape[2])[None, None, :] == col_off[:, :, None]) & col_ok[:, :, None]
    planes = jnp.einsum('hrc,qkc->hrqk', rpb, jnp.asarray(col_sel, F32), precision=HIGHEST)
    planes = jnp.where(jnp.asarray(col_ok), planes, NEG)
    planes = jnp.concatenate([planes, jnp.full((NA_HEADS, 1, GRID_W, GRID_W), NEG, F32)], axis=1)
    plane_of = np.where(row_ok, row_off, n_ro).astype(np.int32).reshape(-1)

    def assemble(sel_ref, planes_ref, o_ref):
        v = pl.program_id(0)
        for i in range(R):
            for x in range(3 * R):
                o_ref[0, 0, i * GRID_W:(i + 1) * GRID_W, x * GRID_W:(x + 1) * GRID_W] = (
                    planes_ref[0, sel_ref[(v * R + i) * 3 * R + x]])

    return pl.pallas_call(
        assemble,
        grid_spec=pltpu.PrefetchScalarGridSpec(
            num_scalar_prefetch=1,
            grid=(3, NA_HEADS),
            in_specs=[pl.BlockSpec((1, n_ro + 1, GRID_W, GRID_W), lambda v, h, sel: (h, 0, 0, 0))],
            out_specs=pl.BlockSpec((1, 1, R * GRID_W, 3 * R * GRID_W), lambda v, h, sel: (v, h, 0, 0))),
        out_shape=jax.ShapeDtypeStruct((3, NA_HEADS, R * GRID_W, 3 * R * GRID_W), F32),
        compiler_params=_cparams("arbitrary", "arbitrary"),
        name="na_bias",
    )(jnp.asarray(plane_of), planes)


def _na(zna, zcna, bias, rows):
    b, n, _ = zna.shape
    ctx = zcna.shape[1]
    R = NA_ROWS_PER_STEP
    nq = R * GRID_W
    nblk = rows // R
    kb = lambda col, off: pl.BlockSpec(
        (1, nq, NA_WIDTH), lambda i, j: (i, jnp.clip(j + off, 0, nblk - 1), col))
    variant = lambda i, j: (jnp.where(j == 0, 0, jnp.where(j == nblk - 1, 2, 1)), 0, 0, 0)
    return pl.pallas_call(
        _na_kernel,
        grid=(b, nblk),
        in_specs=[pl.BlockSpec((1, nq, NA_WIDTH), lambda i, j: (i, j, 0)),
                  kb(1, -1), kb(1, 0), kb(1, 1), kb(2, -1), kb(2, 0), kb(2, 1),
                  pl.BlockSpec((1, ctx, NA_WIDTH), lambda i, j: (i, 0, 1)),
                  pl.BlockSpec((1, ctx, NA_WIDTH), lambda i, j: (i, 0, 2)),
                  pl.BlockSpec((1, NA_HEADS, nq, 3 * nq), variant)],
        out_specs=pl.BlockSpec((1, nq, NA_WIDTH), lambda i, j: (i, j, 0)),
        out_shape=jax.ShapeDtypeStruct((b, n, NA_WIDTH), BF16),
        compiler_params=_cparams("parallel", "parallel"),
        name="na",
    )(zna, zna, zna, zna, zna, zna, zna, zcna, zcna, bias)


def _mix_kernel(na_ref, hf_ref, hb_ref, o_ref, x_ref, g1_ref, sh2_ref, sc2_ref, wo_ref, mg_ref, l1g_ref, l1b_ref,
                wr_ref, br_ref, xmid_ref, rt_ref, rw_ref, cnt_ref, carry_s, *, alpha):
    first = (pl.program_id(0) == 0) & (pl.program_id(1) == 0)

    @pl.when(first)
    def _():
        carry_s[...] = jnp.zeros_like(carry_s)

    tm = x_ref.shape[1]
    h = hf_ref[0].astype(F32) + hb_ref[0].astype(F32)
    parts = []
    for hd in range(ML_HEADS):
        hh = h[:, hd * LANES:(hd + 1) * LANES]
        parts.append(hh * lax.rsqrt(jnp.mean(hh * hh, axis=-1, keepdims=True) + LN_EPS))
    hn = jnp.concatenate(parts, axis=1)
    ml = (hn * mg_ref[...] * jax.nn.sigmoid(o_ref[0].astype(F32))).astype(BF16)
    mix = (jnp.dot(na_ref[0], wo_ref[0:NA_WIDTH, :], preferred_element_type=F32)
           + jnp.dot(ml, wo_ref[NA_WIDTH:, :], preferred_element_type=F32))
    xmid = _ln_rows(alpha * x_ref[0] + g1_ref[0] * mix) * l1g_ref[...] + l1b_ref[...]
    xmid_ref[0] = xmid

    xt = (_ln_rows(xmid) * (1.0 + sc2_ref[0]) + sh2_ref[0]).astype(BF16)
    logits = jnp.dot(xt, wr_ref[...], preferred_element_type=F32) + br_ref[...]
    lane = lax.broadcasted_iota(jnp.int32, (tm, LANES), 1)
    is_g = lane < N_GROUPS
    gl = jnp.where(is_g, logits, NEG)
    gmax = jnp.max(gl, axis=1, keepdims=True)
    grp = jnp.min(jnp.where(gl == gmax, lane, LANES), axis=1, keepdims=True)
    gsum = jnp.sum(jnp.where(is_g, jnp.exp(gl - gmax), 0.0), axis=1, keepdims=True)
    grp_w = 1.0 / gsum
    lo = N_GROUPS + EXPERTS_PER_GROUP * grp
    el = jnp.where((lane >= lo) & (lane < lo + EXPERTS_PER_GROUP), logits, NEG)
    t1 = jnp.max(el, axis=1, keepdims=True)
    i1 = jnp.min(jnp.where(el == t1, lane, LANES), axis=1, keepdims=True)
    el2 = jnp.where(lane == i1, NEG, el)
    t2 = jnp.max(el2, axis=1, keepdims=True)
    i2 = jnp.min(jnp.where(el2 == t2, lane, LANES), axis=1, keepdims=True)
    e21 = jnp.exp(t2 - t1)
    w0 = grp_w / (1.0 + e21)
    w1 = grp_w * e21 / (1.0 + e21)

    hit1 = lane == i1
    hit2 = lane == i2
    onehot = (hit1 | hit2).astype(BF16)
    r_i = lax.broadcasted_iota(jnp.int32, (tm, tm), 0)
    c_i = lax.broadcasted_iota(jnp.int32, (tm, tm), 1)
    before = (r_i > c_i).astype(BF16)
    prefix = jnp.dot(before, onehot, preferred_element_type=F32) + carry_s[0:1, :]
    rank0 = jnp.sum(jnp.where(hit1, prefix, 0.0), axis=1, keepdims=True)
    rank1 = jnp.sum(jnp.where(hit2, prefix, 0.0), axis=1, keepdims=True)
    total = carry_s[0:1, :] + jnp.sum(onehot.astype(F32), axis=0, keepdims=True)
    carry_s[...] = jnp.broadcast_to(total, carry_s.shape)
    cnt_ref[...] = jnp.broadcast_to(total, cnt_ref.shape)

    rf = jnp.where(lane == 0, (i1 - N_GROUPS).astype(F32),
                   jnp.where(lane == 1, (i2 - N_GROUPS).astype(F32),
                             jnp.where(lane == 2, rank0, jnp.where(lane == 3, rank1, 0.0))))
    rt_ref[0] = rf.T[0:SUBLANES, :].astype(jnp.int32)
    rw_ref[0] = jnp.where(lane == 0, w0, jnp.where(lane == 1, w1, 0.0))


def _mix(na, hf, hb, zvo, x, g1, sh2, sc2, wo, mg, l1g, l1b, wr, br, alpha, tm):
    b, n, d = x.shape
    row = lambda w: pl.BlockSpec((1, tm, w), lambda i, j: (i, j, 0))
    vec = pl.BlockSpec((1, 1, d), lambda i, j: (i, 0, 0))
    full = lambda a: pl.BlockSpec(a.shape, lambda i, j: (0,) * a.ndim)
    ocol = 1
    return pl.pallas_call(
        functools.partial(_mix_kernel, alpha=alpha),
        grid=(b, n // tm),
        in_specs=[row(NA_WIDTH), row(ML_WIDTH), row(ML_WIDTH),
                  pl.BlockSpec((1, tm, ML_WIDTH), lambda i, j: (i, j, ocol)),
                  row(d), vec, vec, vec, full(wo), full(mg), full(l1g), full(l1b), full(wr), full(br)],
        out_specs=[row(d), pl.BlockSpec((1, SUBLANES, tm), lambda i, j: (i * (n // tm) + j, 0, 0)), row(LANES),
                   pl.BlockSpec((SUBLANES, LANES), lambda i, j: (0, 0))],
        out_shape=[jax.ShapeDtypeStruct((b, n, d), F32),
                   jax.ShapeDtypeStruct((b * (n // tm), SUBLANES, tm), jnp.int32),
                   jax.ShapeDtypeStruct((b, n, LANES), F32),
                   jax.ShapeDtypeStruct((SUBLANES, LANES), F32)],
        scratch_shapes=[pltpu.VMEM((SUBLANES, LANES), F32)],
        compiler_params=_cparams("arbitrary", "arbitrary"),
        name="mix",
    )(na, hf, hb, zvo, x, g1, sh2, sc2, wo, mg, l1g, l1b, wr, br)


def _zero_fill_padding(pad_base_ref, pad_len_ref, nused_ref, xs_ref, zero_s, sem, wait):
    zero_t, xs_t = _token_view(zero_s), _token_view(xs_ref)
    tb = zero_t.shape[0]

    def run(copy):
        copy.wait() if wait else copy.start()

    def fill(off, nrows):
        run(pltpu.make_async_copy(zero_t.at[pl.ds(0, nrows)], xs_t.at[pl.ds(off, nrows)], sem))

    def per_expert(e, _):
        plen = pad_len_ref[e]
        base = pad_base_ref[e]
        bit = tb // 2
        while bit >= 1:
            off = base + (plen & ~(2 * bit - 1))

            @pl.when((plen & bit) != 0)
            def _(bit=bit, off=off):
                fill(off, bit)

            bit //= 2
        return 0

    lax.fori_loop(0, N_EXPERTS, per_expert, 0)

    def per_block(i, _):
        fill(i * tb, tb)
        return 0

    lax.fori_loop(nused_ref[0], xs_t.shape[0] // tb, per_block, 0)


ROW_WORDS = 4
BF16_BITS = 16
HIGH_HALF = -(1 << BF16_BITS)


def _token_view(ref):
    return ref.reshape(ref.shape[0] // ROW_WORDS, ROW_WORDS, LANES)


def _bf16_bits(v):
    return lax.bitcast_convert_type(v.astype(BF16).astype(F32), jnp.int32)


def _store_token_rows(dst_ref, base, val):
    tm, d = val.shape
    assert d == 2 * ROW_WORDS * LANES
    for s in range(ROW_WORDS):
        lo = _bf16_bits(val[:, s * LANES:(s + 1) * LANES])
        hi = _bf16_bits(val[:, (s + ROW_WORDS) * LANES:(s + ROW_WORDS + 1) * LANES])
        dst_ref[pl.ds(base * ROW_WORDS + s, tm, stride=ROW_WORDS), :] = hi | lax.shift_right_logical(lo, BF16_BITS)


def _load_token_rows(src_ref, base, tm, dtype):
    words = [src_ref[pl.ds(base * ROW_WORDS + s, tm, stride=ROW_WORDS), :] for s in range(ROW_WORDS)]
    lo = [lax.bitcast_convert_type(w << BF16_BITS, F32).astype(dtype) for w in words]
    hi = [lax.bitcast_convert_type(w & HIGH_HALF, F32).astype(dtype) for w in words]
    return jnp.concatenate(lo + hi, axis=1)


def _dispatch_kernel(pad_base_ref, pad_len_ref, nused_ref, dest_ref, xmid_ref, sh2_ref, sc2_ref, xs_ref,
                     xt_s, zero_s, sem, zsem):
    tm = xmid_ref.shape[1]
    step = pl.program_id(0) * pl.num_programs(1) + pl.program_id(1)
    nsteps = pl.num_programs(0) * pl.num_programs(1)

    slot = step % 2

    def wait_slot_copies(s):
        for _ in range(2):
            pltpu.make_async_copy(_token_view(xt_s).at[pl.ds(s * tm, tm)], _token_view(xs_ref).at[pl.ds(0, tm)],
                                  sem.at[s]).wait()

    @pl.when(step == 0)
    def _():
        zero_s[...] = jnp.zeros_like(zero_s)
        _zero_fill_padding(pad_base_ref, pad_len_ref, nused_ref, xs_ref, zero_s, zsem, False)

    for c in range(tm // COMBINE_CHUNK):
        rows = slice(c * COMBINE_CHUNK, (c + 1) * COMBINE_CHUNK)
        xt = _ln_rows(xmid_ref[0, rows, :]) * (1.0 + sc2_ref[0]) + sh2_ref[0]
        _store_token_rows(xt_s, slot * tm + c * COMBINE_CHUNK, xt)
        for r in range(c * COMBINE_CHUNK, (c + 1) * COMBINE_CHUNK):
            for k in range(2):
                pltpu.make_async_copy(_token_view(xt_s).at[slot * tm + r], _token_view(xs_ref).at[dest_ref[0, k, r]],
                                      sem.at[slot]).start(priority=k)

    @pl.when(step == 0)
    def _():
        _zero_fill_padding(pad_base_ref, pad_len_ref, nused_ref, xs_ref, zero_s, zsem, True)

    @pl.when(step > 0)
    def _():
        wait_slot_copies(1 - slot)

    @pl.when(step == nsteps - 1)
    def _():
        wait_slot_copies(slot)


def _dispatch(xmid, sh2, sc2, dest, pad_base, pad_len, nused, cap, tm):
    b, n, d = xmid.shape
    assert d == 2 * ROW_WORDS * LANES
    nt = n // tm
    vec = pl.BlockSpec((1, 1, d), lambda i, j, *_: (i, 0, 0))
    return pl.pallas_call(
        _dispatch_kernel,
        grid_spec=pltpu.PrefetchScalarGridSpec(
            num_scalar_prefetch=3,
            grid=(b, nt),
            in_specs=[pl.BlockSpec((1, 2, tm), lambda i, j, *_: (i * nt + j, 0, 0), memory_space=pltpu.SMEM),
                      pl.BlockSpec((1, tm, d), lambda i, j, *_: (i, j, 0)), vec, vec],
            out_specs=pl.BlockSpec(memory_space=pl.ANY),
            scratch_shapes=[pltpu.VMEM((2 * tm * ROW_WORDS, LANES), ROW_DTYPE),
                            pltpu.VMEM((EXPERT_ROWS * ROW_WORDS, LANES), ROW_DTYPE),
                            pltpu.SemaphoreType.DMA((2,)), pltpu.SemaphoreType.DMA]),
        out_shape=jax.ShapeDtypeStruct((cap * ROW_WORDS, LANES), ROW_DTYPE),
        compiler_params=_cparams("arbitrary", "arbitrary"),
        name="dispatch",
    )(pad_base, pad_len, nused, dest, xmid, sh2, sc2)


def _expert_kernel(be_ref, nv_ref, xs_ref, w1_ref, w3_ref, w2_ref, ys_ref, w1b, w3b, w2b):
    tb = xs_ref.shape[0] // ROW_WORDS
    i = pl.program_id(0)
    e = be_ref[i]
    changed = (i == 0) | (be_ref[jnp.maximum(i - 1, 0)] != e)

    @pl.when(changed)
    def _():
        w1b[...] = w1_ref[0].astype(BF16)
        w3b[...] = w3_ref[0].astype(BF16)
        w2b[...] = w2_ref[0].astype(BF16)

    nv = nv_ref[i]

    @pl.when(nv > 0)
    def _():
        xb = _load_token_rows(xs_ref, 0, tb, BF16)
        h1 = jnp.dot(xb, w1b[...], preferred_element_type=F32)
        h3 = jnp.dot(xb, w3b[...], preferred_element_type=F32)
        a = (_silu(h1) * h3).astype(BF16)
        _store_token_rows(ys_ref, 0, jnp.dot(a, w2b[...], preferred_element_type=F32))

    @pl.when(nv == 0)
    def _():
        ys_ref[...] = jnp.zeros_like(ys_ref)


def _experts(xs, block_e, block_nv, w1, w3, w2):
    d, hid = w1.shape[1], w1.shape[2]
    tb = EXPERT_ROWS
    rows = pl.BlockSpec((tb * ROW_WORDS, LANES), lambda i, be, nv: (i, 0))
    return pl.pallas_call(
        _expert_kernel,
        grid_spec=pltpu.PrefetchScalarGridSpec(
            num_scalar_prefetch=2,
            grid=(xs.shape[0] // (tb * ROW_WORDS),),
            in_specs=[rows,
                      pl.BlockSpec((1, d, hid), lambda i, be, nv: (be[i], 0, 0)),
                      pl.BlockSpec((1, d, hid), lambda i, be, nv: (be[i], 0, 0)),
                      pl.BlockSpec((1, hid, d), lambda i, be, nv: (be[i], 0, 0))],
            out_specs=rows,
            scratch_shapes=[pltpu.VMEM((d, hid), BF16), pltpu.VMEM((d, hid), BF16), pltpu.VMEM((hid, d), BF16)]),
        out_shape=jax.ShapeDtypeStruct(xs.shape, xs.dtype),
        compiler_params=_cparams("arbitrary"),
        name="experts",
    )(block_e, block_nv, xs, w1, w3, w2)


def _combine_kernel(dcur_ref, dnext_ref, xmid_ref, rw_ref, g2_ref, l2g_ref, l2b_ref, ys_ref, o_ref,
                    y0_s, y1_s, sem, *, alpha):
    tm = xmid_ref.shape[1]
    step = pl.program_id(0) * pl.num_programs(1) + pl.program_id(1)
    nsteps = pl.num_programs(0) * pl.num_programs(1)
    slot = step % 2
    other = 1 - slot

    def start_row(dest_ref, into, r):
        for k, buf in ((0, y0_s), (1, y1_s)):
            pltpu.make_async_copy(_token_view(ys_ref).at[dest_ref[0, k, r]],
                                  _token_view(buf).at[into * tm + r], sem.at[into]).start(priority=k)

    def wait_slot(into):
        for buf in (y0_s, y1_s):
            pltpu.make_async_copy(_token_view(ys_ref).at[pl.ds(0, tm)], _token_view(buf).at[pl.ds(into * tm, tm)],
                                  sem.at[into]).wait()

    @pl.when(step == 0)
    def _():
        lax.fori_loop(0, tm, lambda r, c: (start_row(dcur_ref, 0, r), c)[1], 0, unroll=8)

    wait_slot(slot)
    for c in range(tm // COMBINE_CHUNK):
        for r in range(c * COMBINE_CHUNK, (c + 1) * COMBINE_CHUNK):
            start_row(dnext_ref, other, r)
        rows = slice(c * COMBINE_CHUNK, (c + 1) * COMBINE_CHUNK)
        base = slot * tm + c * COMBINE_CHUNK
        rw = rw_ref[0, rows, :]
        moe = (rw[:, 0:1] * _load_token_rows(y0_s, base, COMBINE_CHUNK, F32)
               + rw[:, 1:2] * _load_token_rows(y1_s, base, COMBINE_CHUNK, F32))
        o_ref[0, rows, :] = (_ln_rows(alpha * xmid_ref[0, rows, :] + g2_ref[0] * moe) * l2g_ref[...] + l2b_ref[...])

    @pl.when(step == nsteps - 1)
    def _():
        wait_slot(other)


def _combine(xmid, rw, g2, l2g, l2b, ys, dest, alpha, tm):
    b, n, d = xmid.shape
    nt = n // tm
    full = lambda a: pl.BlockSpec(a.shape, lambda i, j: (0,) * a.ndim)
    return pl.pallas_call(
        functools.partial(_combine_kernel, alpha=alpha),
        grid=(b, nt),
        in_specs=[pl.BlockSpec((1, 2, tm), lambda i, j: (i * nt + j, 0, 0), memory_space=pltpu.SMEM),
                  pl.BlockSpec((1, 2, tm), lambda i, j: (jnp.minimum(i * nt + j + 1, b * nt - 1), 0, 0),
                               memory_space=pltpu.SMEM),
                  pl.BlockSpec((1, tm, d), lambda i, j: (i, j, 0)),
                  pl.BlockSpec((1, tm, LANES), lambda i, j: (i, j, 0)),
                  pl.BlockSpec((1, 1, d), lambda i, j: (i, 0, 0)),
                  full(l2g), full(l2b),
                  pl.BlockSpec(memory_space=pl.ANY)],
        out_specs=pl.BlockSpec((1, tm, d), lambda i, j: (i, j, 0)),
        out_shape=jax.ShapeDtypeStruct((b, n, d), F32),
        scratch_shapes=[pltpu.VMEM((2 * tm * ROW_WORDS, LANES), ys.dtype),
                        pltpu.VMEM((2 * tm * ROW_WORDS, LANES), ys.dtype), pltpu.SemaphoreType.DMA((2,))],
        compiler_params=_cparams("arbitrary", "arbitrary"),
        name="combine",
    )(dest, dest, xmid, rw, g2, l2g, l2b, ys)


def _tile(n, want):
    t = min(n, want)
    assert n % t == 0, (n, t)
    return t


def kernel(x, c, ctx, c_ctx, w_ada, b_ada, w_in, conv_w, conv_b, gate_b, rpb, ml_norm_g, w_out, ln1_g, ln1_b,
           w_router_g, b_router_g, w_router_e, b_router_e, w1, w3, w2, ln2_g, ln2_b):
    B, N, D = x.shape
    T_CTX = ctx.shape[1]
    depth = w_ada.shape[0]
    rows = N // GRID_W
    assert depth == 1 and N % GRID_W == 0 and rows % NA_ROWS_PER_STEP == 0 and rows >= 3 * NA_ROWS_PER_STEP
    assert N % ML_CHUNK == 0 and T_CTX % ML_CHUNK == 0
    alpha = (2.0 * depth) ** 0.25
    l = 0

    pad_rows = -(B + 1) % SUBLANES
    cvec = jnp.concatenate([c, c_ctx[None], jnp.zeros((pad_rows, D), F32)], axis=0)
    ada = _ada(cvec, w_ada[l], b_ada[l])
    sh1, sc1, g1, sh2, sc2, g2 = [a[:, None, :] for a in jnp.split(ada[:B], 6, axis=-1)]
    csh1, csc1 = [jnp.broadcast_to(a[None], (B, 1, D)) for a in jnp.split(ada[B:B + 1], 6, axis=-1)[:2]]

    col_ml = 3 * NA_WIDTH
    col_g = col_ml + 4 * ML_WIDTH
    col_v = col_ml + 2 * ML_WIDTH
    wb = w_in[l].astype(BF16)
    wna, wqk, wvo = wb[:, :col_ml], wb[:, col_ml:col_v], wb[:, col_v:col_g]
    n_gate = 4 * ML_HEADS
    wg = jnp.pad(wb[:, col_g:], ((0, 0), (0, LANES - n_gate)))
    gb = jnp.pad(gate_b[l], (0, LANES - n_gate)).reshape(1, LANES)
    cos_t, sin_t = _rope_tables(N)
    zna, zvo, qk_l, gcol_l, grow_l = _inproj(x, sh1, sc1, wna, wqk, wvo, wg, gb, conv_w[l], conv_b[l], cos_t, sin_t,
                                             True, _tile(N, 512))
    zcna, zcvo, qk_c, gcol_c, grow_c = _inproj(ctx, csh1, csc1, wna, wqk, wvo, wg, gb, conv_w[l], conv_b[l],
                                               cos_t[:T_CTX], sin_t[:T_CTX], False, _tile(T_CTX, 256))

    c0 = jnp.zeros((B, 2, ML_HEADS, ML_HEAD_DIM, 2 * ML_HEAD_DIM), F32)
    m0 = jnp.zeros((B, 2, ML_HEADS, SUBLANES, LANES), F32)
    c_ctx_end, m_ctx_end = _mlstm(qk_c, zcvo, gcol_c, grow_c, c0, m0, False)
    hf, hb, _, _ = _mlstm(qk_l, zvo, gcol_l, grow_l, c_ctx_end, m_ctx_end, True)

    na = _na(zna, zcna, _na_bias_tables(rpb[l], rows), rows)

    wr = jnp.pad(jnp.concatenate([w_router_g[l], w_router_e[l]], axis=1),
                 ((0, 0), (0, LANES - N_GROUPS - N_EXPERTS))).astype(BF16)
    br = jnp.pad(jnp.concatenate([b_router_g[l], b_router_e[l]]), (0, LANES - N_GROUPS - N_EXPERTS)).reshape(1, LANES)
    tm = _tile(N, 512)
    xmid, rt, rw, counts = _mix(na, hf, hb, zvo, x, g1, sh2, sc2, w_out[l].astype(BF16),
                                ml_norm_g[l].reshape(1, ML_WIDTH), ln1_g[l].reshape(1, D), ln1_b[l].reshape(1, D),
                                wr, br, alpha, tm)

    tb = EXPERT_ROWS
    n_assign = 2 * B * N
    cap = -(-n_assign // tb) * tb + N_EXPERTS * tb
    sizes = counts[0, N_GROUPS:N_GROUPS + N_EXPERTS].astype(jnp.int32)
    padded = (sizes + tb - 1) // tb * tb
    pend = jnp.cumsum(padded)
    pstart = pend - padded
    experts = jnp.arange(N_EXPERTS, dtype=jnp.int32)
    first_row = jnp.sum(jnp.where(rt[:, 0:2, :, None] == experts, pstart, 0), axis=-1)
    dest = first_row + rt[:, 2:4, :]
    blk0 = jnp.arange(cap // tb, dtype=jnp.int32) * tb
    block_e = jnp.minimum(jnp.sum(pend[None, :] <= blk0[:, None], axis=1), N_EXPERTS - 1).astype(jnp.int32)
    is_e = block_e[:, None] == experts
    block_nv = jnp.clip(jnp.sum(jnp.where(is_e, pstart + sizes, 0), axis=1) - blk0, 0, tb).astype(jnp.int32)

    nused = (pend[-1:] // tb).astype(jnp.int32)
    xs = _dispatch(xmid, sh2, sc2, dest, pstart + sizes, padded - sizes, nused, cap, tm)
    ys = _experts(xs, block_e, block_nv, w1[l], w3[l], w2[l])
    return _combine(xmid, rw, g2, ln2_g[l].reshape(1, D), ln2_b[l].reshape(1, D), ys, dest, alpha, tm)
```
